```python
import math
import jax
import jax.numpy as jnp
from jax import lax
import numpy as np

D_MODEL = 1024
BATCH = 4
SEQ = 4096
DEPTH = 2

PLE_DIM = 256
N_EVEN = (DEPTH + 1) // 2
N_ODD = DEPTH // 2

DA_HEADS = 4
DA_HEAD_DIM = 64
DA_V_DIM = 2 * DA_HEAD_DIM
DA_QK_WIDTH = DA_HEADS * 2 * DA_HEAD_DIM
DA_WIDTH = DA_HEADS * DA_V_DIM
ROPE_THETA = 500000.0
ROT_DIM = DA_HEAD_DIM // 4
Q_BLOCK = 128

SC_WIDTH = 512
SC_WIDTH_CONV = 3
EVEN_IN_WIDTH = 3 * DA_QK_WIDTH // 2 * 2 // 2 * 0 + DA_QK_WIDTH * 2 + DA_WIDTH + 3 * SC_WIDTH

GM_WIDTH = D_MODEL
GM_GROUPS = 8
GM_GROUP_CH = GM_WIDTH // GM_GROUPS
GM_CHUNK = 128

D_FF = 2816
N_EXPERTS = 8
TOP_K = 2
MOE_BLOCK = 256

RMS_EPS = 1e-6
LN_EPS = 1e-5

kernel_name = "hybrid_diffattn_shortconv_gmlp_moe_ple"


def rms_norm(x, g):
    xf = x.astype(jnp.float32)
    y = xf * lax.rsqrt(jnp.mean(xf * xf, axis=-1, keepdims=True) + RMS_EPS)
    return (y * g.astype(jnp.float32)).astype(x.dtype)


def layer_norm(x, g, b):
    xf = x.astype(jnp.float32)
    mu = jnp.mean(xf, axis=-1, keepdims=True)
    var = jnp.mean(jnp.square(xf - mu), axis=-1, keepdims=True)
    y = (xf - mu) * lax.rsqrt(var + LN_EPS)
    return (y * g.astype(jnp.float32) + b.astype(jnp.float32)).astype(x.dtype)


def rope_partial(t, positions):
    half = ROT_DIM // 2
    inv_freq = ROPE_THETA ** (-jnp.arange(0, ROT_DIM, 2, dtype=jnp.float32) / ROT_DIM)
    ang = positions.astype(jnp.float32)[:, :, None] * inv_freq
    cos = jnp.cos(ang)[:, :, None, None, :].astype(t.dtype)
    sin = jnp.sin(ang)[:, :, None, None, :].astype(t.dtype)
    t1 = t[..., :half]
    t2 = t[..., half:ROT_DIM]
    return jnp.concatenate([t1 * cos - t2 * sin, t2 * cos + t1 * sin, t[..., ROT_DIM:]], axis=-1)


def diff_attention(q, k, v, lam, lambda_init, subln_g):
    bsz, seq = q.shape[0], q.shape[1]
    n_blocks = seq // Q_BLOCK
    q = q * (DA_HEAD_DIM ** -0.5)
    kpos = jnp.arange(seq)

    def one_block(i):
        qb = lax.dynamic_slice_in_dim(q, i * Q_BLOCK, Q_BLOCK, axis=1)
        s = jnp.einsum('bqhcd,bkhcd->bhcqk', qb, k, preferred_element_type=jnp.float32)
        qpos = i * Q_BLOCK + jnp.arange(Q_BLOCK)
        mask = kpos[None, :] <= qpos[:, None]
        pr = jax.nn.softmax(jnp.where(mask, s, -jnp.inf), axis=-1)
        w = pr[:, :, 0] - lam * pr[:, :, 1]
        return jnp.einsum('bhqk,bkhe->bqhe', w.astype(v.dtype), v)

    out = lax.map(one_block, jnp.arange(n_blocks))
    out = jnp.moveaxis(out, 0, 1).reshape(bsz, seq, DA_HEADS, DA_V_DIM)
    out = rms_norm(out, subln_g) * (1.0 - lambda_init)
    return out.reshape(bsz, seq, DA_WIDTH)


def causal_short_conv(z, w):
    ch = z.shape[-1]
    rhs = jnp.transpose(w).astype(z.dtype)[:, None, :]
    return lax.conv_general_dilated(z, rhs, window_strides=(1,), padding=[(SC_WIDTH_CONV - 1, 0)],
                                    dimension_numbers=('NWC', 'WIO', 'NWC'), feature_group_count=ch)


def spatial_gating(u, v, w_s, b_s):
    bsz, seq, _ = v.shape
    vr = v.reshape(bsz, seq // GM_CHUNK, GM_CHUNK, GM_GROUPS, GM_GROUP_CH)
    tril = jnp.tril(jnp.ones((GM_CHUNK, GM_CHUNK), dtype=w_s.dtype))
    vs = jnp.einsum('gts,bnsgc->bntgc', w_s * tril, vr)
    vs = vs + jnp.transpose(b_s)[None, None, :, :, None]
    return u * vs.reshape(bsz, seq, GM_WIDTH)


def swiglu(h, w1, w3, w2):
    return (jax.nn.silu(h @ w1) * (h @ w3)) @ w2


def moe_swiglu(h, router_w, w1, w3, w2):
    bsz, seq, dm = h.shape
    n_tok = bsz * seq
    n_assign = n_tok * TOP_K
    n_blocks = -(-(n_assign + N_EXPERTS * (MOE_BLOCK - 1)) // MOE_BLOCK)
    xt = h.reshape(n_tok, dm)
    logits = jnp.einsum('nd,de->ne', xt, router_w, preferred_element_type=jnp.float32)
    top_val, top_idx = lax.top_k(logits, TOP_K)
    gates = jax.nn.softmax(top_val, axis=-1)
    expert_flat = top_idx.reshape(-1)
    token_flat = jnp.arange(n_assign) // TOP_K
    gate_flat = gates.reshape(-1)
    order = jnp.argsort(expert_flat)
    sorted_expert = expert_flat[order]
    sorted_token = token_flat[order]
    counts = jnp.bincount(expert_flat, length=N_EXPERTS)
    starts = jnp.cumsum(counts) - counts
    padded = (counts + MOE_BLOCK - 1) // MOE_BLOCK * MOE_BLOCK
    pad_ends = jnp.cumsum(padded)
    pad_starts = pad_ends - padded
    dest = pad_starts[sorted_expert] + (jnp.arange(n_assign) - starts[sorted_expert])
    rows = jnp.zeros((n_blocks * MOE_BLOCK, dm), h.dtype).at[dest].set(xt[sorted_token])
    block_start = jnp.arange(n_blocks) * MOE_BLOCK
    block_expert = jnp.minimum(jnp.searchsorted(pad_ends, block_start, side='right'), N_EXPERTS - 1)

    def expert_block(args):
        xb, e = args
        return (jax.nn.silu(xb @ w1[e]) * (xb @ w3[e])) @ w2[e]

    y_rows = lax.map(expert_block, (rows.reshape(n_blocks, MOE_BLOCK, dm), block_expert))
    y_sorted = y_rows.reshape(n_blocks * MOE_BLOCK, dm)[dest]
    contrib = y_sorted * gate_flat[order][:, None].astype(h.dtype)
    out = jax.ops.segment_sum(contrib, sorted_token, num_segments=n_tok)
    return out.reshape(bsz, seq, dm)


def setup_inputs(seed: int = 0) -> dict:
    key = jax.random.key(seed)
    ks = list(jax.random.split(key, 40))
    cnt = [0]

    def nk():
        cnt[0] += 1
        return ks[cnt[0] - 1]

    def nrm(shape, scale):
        return scale * jax.random.normal(nk(), shape, jnp.float32)

    def gain(shape):
        return 1.0 + nrm(shape, 0.05)

    x = nrm((BATCH, SEQ, D_MODEL), 1.0)
    p = nrm((DEPTH, BATCH, SEQ, PLE_DIM), 1.0)
    positions = (jax.random.randint(nk(), (BATCH, 1), 0, 1024, jnp.int32)
                 + jnp.arange(SEQ, dtype=jnp.int32)[None, :])
    dsc = D_MODEL ** -0.5
    return {
        "x": x,
        "p": p,
        "positions": positions,
        "ln_mix": gain((DEPTH, D_MODEL)),
        "ln_ffn": gain((DEPTH, D_MODEL)),
        "ln_ple": gain((DEPTH, D_MODEL)),
        "a_w_in": nrm((N_EVEN, D_MODEL, EVEN_IN_WIDTH), dsc),
        "a_lambda": nrm((N_EVEN, 4, DA_HEAD_DIM), 0.1),
        "a_subln": gain((N_EVEN, DA_V_DIM)),
        "a_conv_w": nrm((N_EVEN, SC_WIDTH, SC_WIDTH_CONV), SC_WIDTH_CONV ** -0.5),
        "a_w_out": nrm((N_EVEN, DA_WIDTH + SC_WIDTH, D_MODEL), (DA_WIDTH + SC_WIDTH) ** -0.5),
        "ffn_w1": nrm((N_EVEN, D_MODEL, D_FF), dsc),
        "ffn_w3": nrm((N_EVEN, D_MODEL, D_FF), dsc),
        "ffn_w2": nrm((N_EVEN, D_FF, D_MODEL), D_FF ** -0.5),
        "c_w_in": nrm((N_ODD, D_MODEL, 2 * GM_WIDTH), dsc),
        "c_ln_g": gain((N_ODD, GM_WIDTH)),
        "c_ln_b": nrm((N_ODD, GM_WIDTH), 0.02),
        "c_w_s": nrm((N_ODD, GM_GROUPS, GM_CHUNK, GM_CHUNK), GM_CHUNK ** -0.5),
        "c_b_s": 1.0 + nrm((N_ODD, GM_GROUPS, GM_CHUNK), 0.1),
        "c_w_out": nrm((N_ODD, GM_WIDTH, D_MODEL), GM_WIDTH ** -0.5),
        "router_w": nrm((N_ODD, D_MODEL, N_EXPERTS), dsc),
        "moe_w1": nrm((N_ODD, N_EXPERTS, D_MODEL, D_FF), dsc),
        "moe_w3": nrm((N_ODD, N_EXPERTS, D_MODEL, D_FF), dsc),
        "moe_w2": nrm((N_ODD, N_EXPERTS, D_FF, D_MODEL), D_FF ** -0.5),
        "ple_gate": nrm((DEPTH, D_MODEL, D_MODEL), dsc),
        "ple_proj": nrm((DEPTH, PLE_DIM, D_MODEL), PLE_DIM ** -0.5),
        "final_norm": gain((D_MODEL,)),
    }


def reference(x, p, positions, ln_mix, ln_ffn, ln_ple, a_w_in, a_lambda, a_subln, a_conv_w,
              a_w_out, ffn_w1, ffn_w3, ffn_w2, c_w_in, c_ln_g, c_ln_b, c_w_s, c_b_s, c_w_out,
              router_w, moe_w1, moe_w3, moe_w2, ple_gate, ple_proj, final_norm):
    h = x
    bsz, seq, _ = x.shape
    splits = [DA_QK_WIDTH, 2 * DA_QK_WIDTH, 2 * DA_QK_WIDTH + DA_WIDTH,
              2 * DA_QK_WIDTH + DA_WIDTH + SC_WIDTH, 2 * DA_QK_WIDTH + DA_WIDTH + 2 * SC_WIDTH]
    for i in range(DEPTH):
        j = i // 2
        hn = rms_norm(h, ln_mix[i])
        if i % 2 == 0:
            z = hn @ a_w_in[j]
            q, k, v, b_gate, c_gate, hc = jnp.split(z, splits, axis=-1)
            q = rope_partial(q.reshape(bsz, seq, DA_HEADS, 2, DA_HEAD_DIM), positions)
            k = rope_partial(k.reshape(bsz, seq, DA_HEADS, 2, DA_HEAD_DIM), positions)
            v = v.reshape(bsz, seq, DA_HEADS, DA_V_DIM)
            lambda_init = 0.8 - 0.6 * math.exp(-0.3 * i)
            lp = a_lambda[j].astype(jnp.float32)
            lam = jnp.exp(jnp.sum(lp[0] * lp[1])) - jnp.exp(jnp.sum(lp[2] * lp[3])) + lambda_init
            attn = diff_attention(q, k, v, lam, lambda_init, a_subln[j])
            conv = b_gate * causal_short_conv(c_gate * hc, a_conv_w[j])
            h = h + jnp.concatenate([attn, conv], axis=-1) @ a_w_out[j]
            h = h + swiglu(rms_norm(h, ln_ffn[i]), ffn_w1[j], ffn_w3[j], ffn_w2[j])
        else:
            z = jax.nn.gelu(hn @ c_w_in[j], approximate=False)
            u, vv = jnp.split(z, 2, axis=-1)
            vv = layer_norm(vv, c_ln_g[j], c_ln_b[j])
            h = h + spatial_gating(u, vv, c_w_s[j], c_b_s[j]) @ c_w_out[j]
            h = h + moe_swiglu(rms_norm(h, ln_ffn[i]), router_w[j], moe_w1[j], moe_w3[j], moe_w2[j])
        gate = jax.nn.sigmoid(rms_norm(h, ln_ple[i]) @ ple_gate[i])
        h = h + gate * (p[i] @ ple_proj[i])
    return rms_norm(h, final_norm)
```

```python
import functools
import math

import jax
import jax.numpy as jnp
from jax import lax
from jax.experimental import pallas as pl
from jax.experimental.pallas import tpu as pltpu

F32 = jnp.float32
BF16 = jnp.bfloat16

D_MODEL = 1024
PLE_DIM = 256
DA_HEADS = 4
DA_HEAD_DIM = 64
DA_V_DIM = 2 * DA_HEAD_DIM
DA_WIDTH = DA_HEADS * DA_V_DIM
ROPE_THETA = 500000.0
ROT_DIM = DA_HEAD_DIM // 4
SC_WIDTH = 512
GM_WIDTH = D_MODEL
GM_GROUPS = 8
GM_CHUNK = 128
D_FF = 2816
N_EXPERTS = 8
TOP_K = 2
RMS_EPS = 1e-6
LN_EPS = 1e-5

LANES = 128
FF_CHUNK = 256
N_FF_CHUNKS = D_FF // FF_CHUNK
ROW_TILE = 512
ATTN_TILE = 256
MOE_TILE = 512
CONV_HALO = 8
VMEM_LIMIT = 56 * 2**20


def _resident(shape):
    nd = len(shape)
    return pl.BlockSpec(shape, lambda *_: (0,) * nd, pipeline_mode=pl.Buffered(1))


def _rows(tile, width):
    return pl.BlockSpec((tile, width), lambda i, *_: (i, 0))


def _rms(x, g):
    return x * lax.rsqrt(jnp.mean(x * x, axis=-1, keepdims=True) + RMS_EPS) * g


def _dot(a, b):
    return jnp.dot(a, b, preferred_element_type=F32)


def _swiglu_chunks(hn_ref, w13_ref, w2_ref, acc_ref):
    def chunk(c, carry):
        gu = _dot(hn_ref[...], w13_ref[c])
        g = gu[:, :FF_CHUNK]
        u = gu[:, FF_CHUNK:]
        a = (g * jax.nn.sigmoid(g) * u).astype(BF16)
        acc_ref[...] += _dot(a, w2_ref[c])
        return carry

    lax.fori_loop(0, N_FF_CHUNKS, chunk, 0)


def _ple(h, p_ref, gple_ref, wg_ref, wp_ref):
    gate = jax.nn.sigmoid(_dot(_rms(h, gple_ref[...]).astype(BF16), wg_ref[...]))
    return h + gate * _dot(p_ref[...].astype(BF16), wp_ref[...])


def _inproj_kernel(x_ref, pos_ref, g_ref, freq_ref, cw_ref, w_ref,
                   q_ref, k_ref, v_ref, conv_ref, cbuf, *, tiles_per_seq):
    i = pl.program_id(0)
    tm = x_ref.shape[0]
    hn = _rms(x_ref[...], g_ref[...]).astype(BF16)

    ang = pos_ref[...].astype(F32) * freq_ref[...]
    lane = lax.broadcasted_iota(jnp.int32, (1, LANES), 1) % DA_HEAD_DIM
    half = ROT_DIM // 2
    cos = jnp.where(lane < ROT_DIM, jnp.cos(ang), 1.0)
    sin = jnp.sin(ang)
    sin_up = jnp.where(lane < half, -sin, 0.0)
    sin_dn = jnp.where((lane >= half) & (lane < ROT_DIM), sin, 0.0)

    def rope(t):
        return (t * cos + pltpu.roll(t, LANES - half, 1) * sin_up
                + pltpu.roll(t, half, 1) * sin_dn)

    zq = _dot(hn, w_ref[:, 0:DA_WIDTH])
    zk = _dot(hn, w_ref[:, DA_WIDTH:2 * DA_WIDTH])
    for hd in range(DA_HEADS):
        sl = slice(hd * LANES, (hd + 1) * LANES)
        q_ref[:, sl] = (rope(zq[:, sl]) * (DA_HEAD_DIM ** -0.5)).astype(BF16)
        k_ref[:, sl] = rope(zk[:, sl]).astype(BF16)
    v_ref[...] = _dot(hn, w_ref[:, 2 * DA_WIDTH:3 * DA_WIDTH]).astype(BF16)

    off = 3 * DA_WIDTH
    b_gate = _dot(hn, w_ref[:, off:off + SC_WIDTH])
    c_gate = _dot(hn, w_ref[:, off + SC_WIDTH:off + 2 * SC_WIDTH])
    hc = _dot(hn, w_ref[:, off + 2 * SC_WIDTH:off + 3 * SC_WIDTH])

    @pl.when(i % tiles_per_seq == 0)
    def _():
        cbuf[0:CONV_HALO, :] = jnp.zeros((CONV_HALO, SC_WIDTH), F32)

    @pl.when(i % tiles_per_seq != 0)
    def _():
        cbuf[0:CONV_HALO, :] = cbuf[tm:tm + CONV_HALO, :]

    ch = c_gate * hc
    cbuf[CONV_HALO:CONV_HALO + tm, :] = ch
    ch1 = cbuf[CONV_HALO - 1:CONV_HALO - 1 + tm, :]
    ch2 = cbuf[CONV_HALO - 2:CONV_HALO - 2 + tm, :]
    conv = b_gate * (cw_ref[2:3, :] * ch + cw_ref[1:2, :] * ch1 + cw_ref[0:1, :] * ch2)
    conv_ref[...] = conv.astype(BF16)


def _inproj(x2, pos2, g, freq, cw, w, seq):
    n = x2.shape[0]
    tm = ROW_TILE
    out = jax.ShapeDtypeStruct((n, DA_WIDTH), BF16)
    return pl.pallas_call(
        functools.partial(_inproj_kernel, tiles_per_seq=seq // tm),
        grid=(n // tm,),
        in_specs=[_rows(tm, D_MODEL), _rows(tm, 1), _resident(g.shape), _resident(freq.shape),
                  _resident(cw.shape), _resident(w.shape)],
        out_specs=[_rows(tm, DA_WIDTH)] * 4,
        out_shape=[out] * 4,
        scratch_shapes=[pltpu.VMEM((tm + CONV_HALO, SC_WIDTH), F32)],
        compiler_params=pltpu.CompilerParams(dimension_semantics=("arbitrary",),
                                             vmem_limit_bytes=VMEM_LIMIT),
        name="l0_inproj",
    )(x2, pos2, g, freq, cw, w)


def _attn_kernel(lam_ref, sg_ref, q_ref, k_ref, v_ref, o_ref, m_sc, l_sc, acc_sc, *, lambda_init):
    qi = pl.program_id(2)
    t = q_ref.shape[0]
    q = q_ref[...]
    lane = lax.broadcasted_iota(jnp.int32, (1, LANES), 1)
    zero = jnp.zeros_like(q)
    qc = (jnp.where(lane < DA_HEAD_DIM, q, zero), jnp.where(lane >= DA_HEAD_DIM, q, zero))

    m_sc[...] = jnp.full(m_sc.shape, -jnp.inf, F32)
    l_sc[...] = jnp.zeros(l_sc.shape, F32)
    acc_sc[...] = jnp.zeros(acc_sc.shape, F32)

    def step(j, diagonal):
        ks = pl.multiple_of(j * t, t)
        kb = k_ref[pl.ds(ks, t), :]
        vb = v_ref[pl.ds(ks, t), :]
        for c in range(2):
            s = lax.dot_general(qc[c], kb, (((1,), (1,)), ((), ())), preferred_element_type=F32)
            if diagonal:
                row = lax.broadcasted_iota(jnp.int32, (t, t), 0)
                col = lax.broadcasted_iota(jnp.int32, (t, t), 1)
                s = jnp.where(col <= row, s, -jnp.inf)
            m_prev = m_sc[c]
            m_new = jnp.maximum(m_prev, jnp.max(s, axis=-1, keepdims=True))
            p = jnp.exp(s - m_new)
            alpha = jnp.exp(m_prev - m_new)
            l_sc[c] = alpha * l_sc[c] + jnp.sum(p, axis=-1, keepdims=True)
            acc_sc[c] = alpha * acc_sc[c] + _dot(p.astype(BF16), vb)
            m_sc[c] = m_new

    def full_step(j, carry):
        step(j, False)
        return carry

    lax.fori_loop(0, qi, full_step, 0)
    step(qi, True)

    lp = lam_ref[...]
    lam = (jnp.exp(jnp.sum(lp[0:1] * lp[1:2], axis=-1, keepdims=True))
           - jnp.exp(jnp.sum(lp[2:3] * lp[3:4], axis=-1, keepdims=True)) + lambda_init)
    o = acc_sc[0] / l_sc[0] - lam * (acc_sc[1] / l_sc[1])
    o_ref[...] = (_rms(o, sg_ref[...]) * (1.0 - lambda_init)).astype(o_ref.dtype)


def _diff_attention(q, k, v, lam_p, subln_g, lambda_init):
    bsz, seq, _ = q.shape
    t = ATTN_TILE
    qo_spec = pl.BlockSpec((None, t, LANES), lambda b, h, i: (b, i, h))
    kv_spec = pl.BlockSpec((None, seq, LANES), lambda b, h, i: (b, 0, h))
    return pl.pallas_call(
        functools.partial(_attn_kernel, lambda_init=lambda_init),
        grid=(bsz, DA_HEADS, seq // t),
        in_specs=[pl.BlockSpec(lam_p.shape, lambda b, h, i: (0, 0)),
                  pl.BlockSpec(subln_g.shape, lambda b, h, i: (0, 0)),
                  qo_spec, kv_spec, kv_spec],
        out_specs=qo_spec,
        out_shape=jax.ShapeDtypeStruct(q.shape, BF16),
        scratch_shapes=[pltpu.VMEM((2, t, 1), F32), pltpu.VMEM((2, t, 1), F32),
                        pltpu.VMEM((2, t, LANES), F32)],
        compiler_params=pltpu.CompilerParams(
            dimension_semantics=("arbitrary", "arbitrary", "arbitrary"),
            vmem_limit_bytes=VMEM_LIMIT),
        name="l0_diff_attention",
    )(lam_p, subln_g, q, k, v)


def _l0_post_kernel(x_ref, attn_ref, conv_ref, p_ref, wo_ref, gffn_ref, w13_ref, w2_ref,
                    gple_ref, wg_ref, wp_ref, o_ref, acc_sc, hn_sc):
    h = (x_ref[...] + _dot(attn_ref[...], wo_ref[0:DA_WIDTH, :])
         + _dot(conv_ref[...], wo_ref[DA_WIDTH:DA_WIDTH + SC_WIDTH, :]))
    hn_sc[...] = _rms(h, gffn_ref[...]).astype(BF16)
    acc_sc[...] = h
    _swiglu_chunks(hn_sc, w13_ref, w2_ref, acc_sc)
    o_ref[...] = _ple(acc_sc[...], p_ref, gple_ref, wg_ref, wp_ref)


def _l0_post(x2, attn, conv, p0, wo, gffn, w13, w2, gple, wg, wp):
    n = x2.shape[0]
    tm = ROW_TILE
    return pl.pallas_call(
        _l0_post_kernel,
        grid=(n // tm,),
        in_specs=[_rows(tm, D_MODEL), _rows(tm, DA_WIDTH), _rows(tm, SC_WIDTH), _rows(tm, PLE_DIM),
                  _resident(wo.shape), _resident(gffn.shape), _resident(w13.shape),
                  _resident(w2.shape), _resident(gple.shape), _resident(wg.shape),
                  _resident(wp.shape)],
        out_specs=_rows(tm, D_MODEL),
        out_shape=jax.ShapeDtypeStruct((n, D_MODEL), F32),
        scratch_shapes=[pltpu.VMEM((tm, D_MODEL), F32), pltpu.VMEM((tm, D_MODEL), BF16)],
        compiler_params=pltpu.CompilerParams(dimension_semantics=("arbitrary",),
                                             vmem_limit_bytes=VMEM_LIMIT),
        name="l0_outproj_swiglu_ple",
    )(x2, attn, conv, p0, wo, gffn, w13, w2, gple, wg, wp)


def _split_bf16(a):
    hi = a.astype(BF16)
    return hi, (a - hi.astype(F32)).astype(BF16)


def _l1_mix_kernel(h_ref, gmix_ref, win_ref, lng_ref, lnb_ref, ws_ref, bs_ref, wout_ref,
                   gffn_ref, wr_hi_ref, wr_lo_ref,
                   h1_ref, hn2_ref, ridx_ref, rgate_ref, gated_sc):
    tm = h_ref.shape[0]
    h = h_ref[...]
    hn = _rms(h, gmix_ref[...]).astype(BF16)

    def gelu(z):
        return 0.5 * z * (1.0 + lax.erf(z * (2.0 ** -0.5)))

    u = gelu(_dot(hn, win_ref[:, 0:GM_WIDTH]))
    vv = gelu(_dot(hn, win_ref[:, GM_WIDTH:2 * GM_WIDTH]))
    mu = jnp.mean(vv, axis=-1, keepdims=True)
    vc = vv - mu
    var = jnp.mean(vc * vc, axis=-1, keepdims=True)
    vv = (vc * lax.rsqrt(var + LN_EPS) * lng_ref[...] + lnb_ref[...]).astype(BF16)

    n_chunks = tm // GM_CHUNK
    row = lax.broadcasted_iota(jnp.int32, (GM_CHUNK, GM_CHUNK), 0)
    col = lax.broadcasted_iota(jnp.int32, (GM_CHUNK, GM_CHUNK), 1)
    for g in range(GM_GROUPS):
        gs = slice(g * LANES, (g + 1) * LANES)
        wm = jnp.where(col <= row, ws_ref[g], 0.0).astype(BF16)
        rhs = jnp.concatenate(
            [vv[r * GM_CHUNK:(r + 1) * GM_CHUNK, gs] for r in range(n_chunks)], axis=1)
        vs = _dot(wm, rhs) + bs_ref[:, g:g + 1]
        for r in range(n_chunks):
            rs = slice(r * GM_CHUNK, (r + 1) * GM_CHUNK)
            gated_sc[rs, gs] = (u[rs, gs] * vs[:, r * LANES:(r + 1) * LANES]).astype(BF16)

    h1 = h + _dot(gated_sc[...], wout_ref[...])
    h1_ref[...] = h1

    hn2 = _rms(h1, gffn_ref[...])
    hn2_ref[...] = hn2
    x_hi, x_lo = _split_bf16(hn2)
    logits = (_dot(x_hi, wr_hi_ref[...]) + _dot(x_hi, wr_lo_ref[...])
              + _dot(x_lo, wr_hi_ref[...]))
    lane = lax.broadcasted_iota(jnp.int32, logits.shape, 1)
    logits = jnp.where(lane < N_EXPERTS, logits, -jnp.inf)
    v1 = jnp.max(logits, axis=-1, keepdims=True)
    i1 = jnp.min(jnp.where(logits == v1, lane, LANES), axis=-1, keepdims=True)
    rest = jnp.where(lane == i1, -jnp.inf, logits)
    v2 = jnp.max(rest, axis=-1, keepdims=True)
    i2 = jnp.min(jnp.where(rest == v2, lane, LANES), axis=-1, keepdims=True)
    e2 = jnp.exp(v2 - v1)
    ridx_ref[:, 0:1] = i1
    ridx_ref[:, 1:2] = i2
    rgate_ref[:, 0:1] = 1.0 / (1.0 + e2)
    rgate_ref[:, 1:2] = e2 / (1.0 + e2)


def _l1_mix(h, gmix, win, lng, lnb, ws, bs_t, wout, gffn, wr_hi, wr_lo):
    n = h.shape[0]
    tm = ROW_TILE
    return pl.pallas_call(
        _l1_mix_kernel,
        grid=(n // tm,),
        in_specs=[_rows(tm, D_MODEL)] + [_resident(a.shape) for a in
                                         (gmix, win, lng, lnb, ws, bs_t, wout, gffn, wr_hi, wr_lo)],
        out_specs=[_rows(tm, D_MODEL), _rows(tm, D_MODEL), _rows(tm, TOP_K), _rows(tm, TOP_K)],
        out_shape=[jax.ShapeDtypeStruct((n, D_MODEL), F32), jax.ShapeDtypeStruct((n, D_MODEL), F32),
                   jax.ShapeDtypeStruct((n, TOP_K), jnp.int32),
                   jax.ShapeDtypeStruct((n, TOP_K), F32)],
        scratch_shapes=[pltpu.VMEM((tm, GM_WIDTH), BF16)],
        compiler_params=pltpu.CompilerParams(dimension_semantics=("arbitrary",),
                                             vmem_limit_bytes=VMEM_LIMIT),
        name="l1_gmlp_router",
    )(h, gmix, win, lng, lnb, ws, bs_t, wout, gffn, wr_hi, wr_lo)


def _row_gather(idx_ref, base, src_hbm, dst, sem, n_rows, stride=1):
    def issue(r, carry):
        pltpu.make_async_copy(src_hbm.at[pl.ds(idx_ref[base + r * stride], 1)],
                              dst.at[pl.ds(r, 1)], sem).start()
        return carry

    lax.fori_loop(0, n_rows, issue, 0)


def _row_gather_wait(src_hbm, dst, sem, n_rows):
    pltpu.make_async_copy(src_hbm.at[pl.ds(0, n_rows)], dst, sem).wait()


def _moe_kernel(bexp_ref, bvalid_ref, tok_ref, hn_hbm, w13_ref, w2_ref, y_ref,
                xbuf, xb_sc, acc_sc, sem):
    i = pl.program_id(0)
    tm = y_ref.shape[0]

    @pl.when(bvalid_ref[i] == 0)
    def _():
        y_ref[...] = jnp.zeros(y_ref.shape, y_ref.dtype)

    @pl.when(bvalid_ref[i] != 0)
    def _():
        _row_gather(tok_ref, i * tm, hn_hbm, xbuf, sem, tm)
        _row_gather_wait(hn_hbm, xbuf, sem, tm)
        xb_sc[...] = xbuf[...].astype(BF16)
        acc_sc[...] = jnp.zeros(acc_sc.shape, F32)
        _swiglu_chunks(xb_sc, w13_ref, w2_ref, acc_sc)
        y_ref[...] = acc_sc[...]


def _moe_experts(block_expert, block_valid, row_token, hn2, w13, w2):
    n_blocks = block_expert.shape[0]
    tm = MOE_TILE
    w13_spec = pl.BlockSpec((None,) + w13.shape[1:], lambda i, be, bv, tok: (be[i], 0, 0, 0))
    w2_spec = pl.BlockSpec((None,) + w2.shape[1:], lambda i, be, bv, tok: (be[i], 0, 0, 0))
    return pl.pallas_call(
        _moe_kernel,
        grid_spec=pltpu.PrefetchScalarGridSpec(
            num_scalar_prefetch=3,
            grid=(n_blocks,),
            in_specs=[pl.BlockSpec(memory_space=pl.ANY), w13_spec, w2_spec],
            out_specs=pl.BlockSpec((tm, D_MODEL), lambda i, be, bv, tok: (i, 0)),
            scratch_shapes=[pltpu.VMEM((tm, D_MODEL), F32), pltpu.VMEM((tm, D_MODEL), BF16),
                            pltpu.VMEM((tm, D_MODEL), F32), pltpu.SemaphoreType.DMA(())],
        ),
        out_shape=jax.ShapeDtypeStruct((n_blocks * tm, D_MODEL), F32),
        compiler_params=pltpu.CompilerParams(dimension_semantics=("arbitrary",),
                                             vmem_limit_bytes=VMEM_LIMIT),
        name="l1_expert_swiglu",
    )(block_expert, block_valid, row_token, hn2, w13, w2)


def _combine_kernel(dest_ref, h_ref, gate_ref, p_ref, y_hbm, gple_ref, wg_ref, wp_ref, gfin_ref,
                    o_ref, ybuf0, ybuf1, sem):
    i = pl.program_id(0)
    tm = h_ref.shape[0]
    _row_gather(dest_ref, i * tm * TOP_K, y_hbm, ybuf0, sem.at[0], tm, stride=TOP_K)
    _row_gather(dest_ref, i * tm * TOP_K + 1, y_hbm, ybuf1, sem.at[1], tm, stride=TOP_K)
    _row_gather_wait(y_hbm, ybuf0, sem.at[0], tm)
    _row_gather_wait(y_hbm, ybuf1, sem.at[1], tm)
    gate = gate_ref[...]
    h = h_ref[...] + (ybuf0[...] * gate[:, 0:1] + ybuf1[...] * gate[:, 1:2])
    h = _ple(h, p_ref, gple_ref, wg_ref, wp_ref)
    o_ref[...] = _rms(h, gfin_ref[...])


def _combine(dest, h1, rgate, p1, y_rows, gple, wg, wp, gfin):
    n = h1.shape[0]
    tm = ROW_TILE
    rows = lambda w: pl.BlockSpec((tm, w), lambda i, d: (i, 0))
    return pl.pallas_call(
        _combine_kernel,
        grid_spec=pltpu.PrefetchScalarGridSpec(
            num_scalar_prefetch=1,
            grid=(n // tm,),
            in_specs=[rows(D_MODEL), rows(TOP_K), rows(PLE_DIM), pl.BlockSpec(memory_space=pl.ANY),
                      _resident(gple.shape), _resident(wg.shape), _resident(wp.shape),
                      _resident(gfin.shape)],
            out_specs=rows(D_MODEL),
            scratch_shapes=[pltpu.VMEM((tm, D_MODEL), F32), pltpu.VMEM((tm, D_MODEL), F32),
                            pltpu.SemaphoreType.DMA((2,))],
        ),
        out_shape=jax.ShapeDtypeStruct((n, D_MODEL), F32),
        compiler_params=pltpu.CompilerParams(dimension_semantics=("arbitrary",),
                                             vmem_limit_bytes=VMEM_LIMIT),
        name="l1_combine_ple_norm",
    )(dest, h1, rgate, p1, y_rows, gple, wg, wp, gfin)


def _swiglu_weights(w1, w3, w2):
    lead = w1.shape[:-2]
    nl = len(lead)
    w1c = w1.astype(BF16).reshape(lead + (D_MODEL, N_FF_CHUNKS, FF_CHUNK))
    w3c = w3.astype(BF16).reshape(lead + (D_MODEL, N_FF_CHUNKS, FF_CHUNK))
    w13 = jnp.concatenate([w1c, w3c], axis=-1)
    w13 = jnp.moveaxis(w13, nl + 1, nl)
    w2c = w2.astype(BF16).reshape(lead + (N_FF_CHUNKS, FF_CHUNK, D_MODEL))
    return w13, w2c


def _routing_tables(ridx, n_blocks):
    n_assign = ridx.size
    e_flat = ridx.reshape(-1)
    onehot = (e_flat[:, None] == jnp.arange(N_EXPERTS, dtype=jnp.int32)[None, :]).astype(jnp.int32)
    csum = jnp.cumsum(onehot, axis=0)
    counts = csum[-1]
    rank = jnp.sum(csum * onehot, axis=1) - 1
    padded = (counts + MOE_TILE - 1) // MOE_TILE * MOE_TILE
    pad_ends = jnp.cumsum(padded)
    pad_starts = pad_ends - padded
    dest = (jnp.sum(pad_starts[None, :] * onehot, axis=1) + rank).astype(jnp.int32)
    row_token = jnp.zeros((n_blocks * MOE_TILE,), jnp.int32).at[dest].set(
        jnp.arange(n_assign, dtype=jnp.int32) // TOP_K)
    block_start = jnp.arange(n_blocks, dtype=jnp.int32) * MOE_TILE
    block_expert = jnp.minimum(
        jnp.sum((block_start[:, None] >= pad_ends[None, :]).astype(jnp.int32), axis=1),
        N_EXPERTS - 1).astype(jnp.int32)
    block_valid = (block_start < pad_ends[-1]).astype(jnp.int32)
    return dest, row_token, block_expert, block_valid


def kernel(x, p, positions, ln_mix, ln_ffn, ln_ple, a_w_in, a_lambda, a_subln, a_conv_w, a_w_out,
           ffn_w1, ffn_w3, ffn_w2, c_w_in, c_ln_g, c_ln_b, c_w_s, c_b_s, c_w_out, router_w,
           moe_w1, moe_w3, moe_w2, ple_gate, ple_proj, final_norm):
    bsz, seq, _ = x.shape
    n = bsz * seq
    assert seq % ROW_TILE == 0 and seq % ATTN_TILE == 0 and ROW_TILE % GM_CHUNK == 0
    x2 = x.reshape(n, D_MODEL)
    pos2 = positions.reshape(n, 1)
    row = lambda a: a.reshape(1, -1)

    inv_freq = ROPE_THETA ** (-jnp.arange(0, ROT_DIM, 2, dtype=F32) / ROT_DIM)
    lane = jnp.arange(LANES) % DA_HEAD_DIM
    freq = jnp.where(lane < ROT_DIM, inv_freq[lane % (ROT_DIM // 2)], 0.0).reshape(1, LANES)
    q, k, v, conv = _inproj(x2, pos2, row(ln_mix[0]), freq, a_conv_w[0].T, a_w_in[0].astype(BF16), seq)
    lambda_init = 0.8 - 0.6 * math.exp(-0.3 * 0)
    shp = (bsz, seq, DA_WIDTH)
    attn = _diff_attention(q.reshape(shp), k.reshape(shp), v.reshape(shp), a_lambda[0],
                           row(a_subln[0]), lambda_init).reshape(n, DA_WIDTH)
    w13, w2 = _swiglu_weights(ffn_w1[0], ffn_w3[0], ffn_w2[0])
    h = _l0_post(x2, attn, conv, p[0].reshape(n, PLE_DIM), a_w_out[0].astype(BF16), row(ln_ffn[0]),
                 w13, w2, row(ln_ple[0]), ple_gate[0].astype(BF16), ple_proj[0].astype(BF16))

    wr = jnp.pad(router_w[0], ((0, 0), (0, LANES - N_EXPERTS)))
    wr_hi, wr_lo = _split_bf16(wr)
    h1, hn2, ridx, rgate = _l1_mix(h, row(ln_mix[1]), c_w_in[0].astype(BF16), row(c_ln_g[0]),
                                   row(c_ln_b[0]), c_w_s[0], c_b_s[0].T, c_w_out[0].astype(BF16),
                                   row(ln_ffn[1]), wr_hi, wr_lo)
    n_blocks = -(-(n * TOP_K + N_EXPERTS * (MOE_TILE - 1)) // MOE_TILE)
    dest, row_token, block_expert, block_valid = _routing_tables(ridx, n_blocks)
    mw13, mw2 = _swiglu_weights(moe_w1[0], moe_w3[0], moe_w2[0])
    y_rows = _moe_experts(block_expert, block_valid, row_token, hn2, mw13, mw2)
    out = _combine(dest, h1, rgate, p[1].reshape(n, PLE_DIM), y_rows, row(ln_ple[1]),
                   ple_gate[1].astype(BF16), ple_proj[1].astype(BF16), row(final_norm))
    return out.reshape(bsz, seq, D_MODEL)
```

```python
import functools
import math

import jax
import jax.numpy as jnp
from jax import lax
from jax.experimental import pallas as pl
from jax.experimental.pallas import tpu as pltpu

F32 = jnp.float32
BF16 = jnp.bfloat16

D_MODEL = 1024
PLE_DIM = 256
DA_HEADS = 4
DA_HEAD_DIM = 64
DA_V_DIM = 2 * DA_HEAD_DIM
DA_WIDTH = DA_HEADS * DA_V_DIM
ROPE_THETA = 500000.0
ROT_DIM = DA_HEAD_DIM // 4
SC_WIDTH = 512
GM_WIDTH = D_MODEL
GM_GROUPS = 8
GM_CHUNK = 128
D_FF = 2816
N_EXPERTS = 8
TOP_K = 2
RMS_EPS = 1e-6
LN_EPS = 1e-5

LANES = 128
FF_CHUNK = 256
N_FF_CHUNKS = D_FF // FF_CHUNK
ROW_TILE = 512
ATTN_TILE = 256
VT_ROWS = DA_V_DIM + 16
MOE_TILE = 512
CONV_HALO = 8
VMEM_LIMIT = 56 * 2**20


def _resident(shape):
    nd = len(shape)
    return pl.BlockSpec(shape, lambda *_: (0,) * nd, pipeline_mode=pl.Buffered(1))


def _rows(tile, width):
    return pl.BlockSpec((tile, width), lambda i, *_: (i, 0))


def _rms(x, g):
    return x * lax.rsqrt(jnp.mean(x * x, axis=-1, keepdims=True) + RMS_EPS) * g


def _dot(a, b):
    return jnp.dot(a, b, preferred_element_type=F32)


def _swiglu_chunks(hn_ref, w13_ref, w2_ref, acc_ref):
    def chunk(c, carry):
        gu = _dot(hn_ref[...], w13_ref[c])
        g = gu[:, :FF_CHUNK]
        u = gu[:, FF_CHUNK:]
        a = (g * jax.nn.sigmoid(g) * u).astype(BF16)
        acc_ref[...] += _dot(a, w2_ref[c])
        return carry

    lax.fori_loop(0, N_FF_CHUNKS, chunk, 0)


def _ple(h, p_ref, gple_ref, wg_ref, wp_ref):
    gate = jax.nn.sigmoid(_dot(_rms(h, gple_ref[...]).astype(BF16), wg_ref[...]))
    return h + gate * _dot(p_ref[...].astype(BF16), wp_ref[...])


def _inproj_kernel(x_ref, pos_ref, g_ref, freq_ref, cw_ref, w_ref,
                   qt_ref, k_ref, vt_ref, conv_ref, cbuf, *, tiles_per_seq):
    i = pl.program_id(0)
    tm = x_ref.shape[0]
    hn = _rms(x_ref[...], g_ref[...]).astype(BF16)

    ang = pos_ref[...].astype(F32) * freq_ref[...]
    lane = lax.broadcasted_iota(jnp.int32, (1, LANES), 1) % DA_HEAD_DIM
    half = ROT_DIM // 2
    cos = jnp.where(lane < ROT_DIM, jnp.cos(ang), 1.0)
    sin = jnp.sin(ang)
    sin_up = jnp.where(lane < half, -sin, 0.0)
    sin_dn = jnp.where((lane >= half) & (lane < ROT_DIM), sin, 0.0)

    def rope(t):
        return (t * cos + pltpu.roll(t, LANES - half, 1) * sin_up
                + pltpu.roll(t, half, 1) * sin_dn)

    zq = _dot(hn, w_ref[:, 0:DA_WIDTH])
    zk = _dot(hn, w_ref[:, DA_WIDTH:2 * DA_WIDTH])
    zv = _dot(hn, w_ref[:, 2 * DA_WIDTH:3 * DA_WIDTH])
    ones = jnp.ones((VT_ROWS - DA_V_DIM, ATTN_TILE), BF16)
    for hd in range(DA_HEADS):
        sl = slice(hd * LANES, (hd + 1) * LANES)
        qt_ref[sl, :] = (rope(zq[:, sl]) * (DA_HEAD_DIM ** -0.5)).T.astype(BF16)
        k_ref[:, sl] = rope(zk[:, sl]).astype(BF16)
        for u in range(tm // ATTN_TILE):
            vt_ref[hd, u, 0:DA_V_DIM, :] = zv[u * ATTN_TILE:(u + 1) * ATTN_TILE, sl].T.astype(BF16)
            vt_ref[hd, u, DA_V_DIM:VT_ROWS, :] = ones

    off = 3 * DA_WIDTH
    b_gate = _dot(hn, w_ref[:, off:off + SC_WIDTH])
    c_gate = _dot(hn, w_ref[:, off + SC_WIDTH:off + 2 * SC_WIDTH])
    hc = _dot(hn, w_ref[:, off + 2 * SC_WIDTH:off + 3 * SC_WIDTH])

    @pl.when(i % tiles_per_seq == 0)
    def _():
        cbuf[0:CONV_HALO, :] = jnp.zeros((CONV_HALO, SC_WIDTH), F32)

    @pl.when(i % tiles_per_seq != 0)
    def _():
        cbuf[0:CONV_HALO, :] = cbuf[tm:tm + CONV_HALO, :]

    ch = c_gate * hc
    cbuf[CONV_HALO:CONV_HALO + tm, :] = ch
    ch1 = cbuf[CONV_HALO - 1:CONV_HALO - 1 + tm, :]
    ch2 = cbuf[CONV_HALO - 2:CONV_HALO - 2 + tm, :]
    conv = b_gate * (cw_ref[2:3, :] * ch + cw_ref[1:2, :] * ch1 + cw_ref[0:1, :] * ch2)
    conv_ref[...] = conv.astype(BF16)


def _inproj(x2, pos2, g, freq, cw, w, seq):
    n = x2.shape[0]
    tm = ROW_TILE
    out = jax.ShapeDtypeStruct((n, DA_WIDTH), BF16)
    sub = tm // ATTN_TILE
    return pl.pallas_call(
        functools.partial(_inproj_kernel, tiles_per_seq=seq // tm),
        grid=(n // tm,),
        in_specs=[_rows(tm, D_MODEL), _rows(tm, 1), _resident(g.shape), _resident(freq.shape),
                  _resident(cw.shape), _resident(w.shape)],
        out_specs=[pl.BlockSpec((DA_WIDTH, tm), lambda i: (0, i)), _rows(tm, DA_WIDTH),
                   pl.BlockSpec((DA_HEADS, sub, VT_ROWS, ATTN_TILE), lambda i: (0, i, 0, 0)),
                   _rows(tm, SC_WIDTH)],
        out_shape=[jax.ShapeDtypeStruct((DA_WIDTH, n), BF16), out,
                   jax.ShapeDtypeStruct((DA_HEADS, n // ATTN_TILE, VT_ROWS, ATTN_TILE), BF16), out],
        scratch_shapes=[pltpu.VMEM((tm + CONV_HALO, SC_WIDTH), F32)],
        compiler_params=pltpu.CompilerParams(dimension_semantics=("arbitrary",),
                                             vmem_limit_bytes=VMEM_LIMIT),
        name="l0_inproj",
    )(x2, pos2, g, freq, cw, w)


def _attn_kernel(lam_ref, sg_ref, qt_ref, k_ref, vt_ref, o_ref, qc_sc, m_sc, acc_sc, *, lambda_init):
    qi = pl.program_id(2)
    t = ATTN_TILE
    qt = qt_ref[...]
    dim = lax.broadcasted_iota(jnp.int32, (LANES, 1), 0)
    zero = jnp.zeros_like(qt)
    qc_sc[0] = jnp.where(dim < DA_HEAD_DIM, qt, zero)
    qc_sc[1] = jnp.where(dim >= DA_HEAD_DIM, qt, zero)
    m_sc[...] = jnp.full(m_sc.shape, -jnp.inf, F32)
    acc_sc[...] = jnp.zeros(acc_sc.shape, F32)

    def step(first, n_sub, diagonal):
        for c in range(2):
            s = []
            for u in range(n_sub):
                ks = pl.multiple_of((first + u) * t, t)
                su = _dot(k_ref[pl.ds(ks, t), :], qc_sc[c])
                if diagonal:
                    key = lax.broadcasted_iota(jnp.int32, (t, t), 0)
                    qry = lax.broadcasted_iota(jnp.int32, (t, t), 1)
                    su = jnp.where(key <= qry, su, -jnp.inf)
                s.append(su)
            m_prev = m_sc[c]
            m_new = m_prev
            for su in s:
                m_new = jnp.maximum(m_new, jnp.max(su, axis=0, keepdims=True))
            pv = None
            for u, su in enumerate(s):
                d = _dot(vt_ref[first + u], jnp.exp(su - m_new).astype(BF16))
                pv = d if pv is None else pv + d
            acc_sc[c] = jnp.exp(m_prev - m_new) * acc_sc[c] + pv
            m_sc[c] = m_new

    def pair_step(j, carry):
        step(2 * j, 2, False)
        return carry

    lax.fori_loop(0, qi // 2, pair_step, 0)

    @pl.when(qi % 2 == 1)
    def _():
        step(qi - 1, 1, False)

    step(qi, 1, True)

    lp = lam_ref[...]
    lam = (jnp.exp(jnp.sum(lp[0:1] * lp[1:2], axis=-1, keepdims=True))
           - jnp.exp(jnp.sum(lp[2:3] * lp[3:4], axis=-1, keepdims=True)) + lambda_init)
    a0 = acc_sc[0]
    a1 = acc_sc[1]
    o = (a0[0:DA_V_DIM] / a0[DA_V_DIM:DA_V_DIM + 1]
         - lam * (a1[0:DA_V_DIM] / a1[DA_V_DIM:DA_V_DIM + 1]))
    o = o * lax.rsqrt(jnp.mean(o * o, axis=0, keepdims=True) + RMS_EPS) * sg_ref[...]
    o_ref[...] = (o * (1.0 - lambda_init)).T.astype(o_ref.dtype)


def _diff_attention(qt, k, vt, lam_p, subln_col, lambda_init, bsz, seq):
    t = ATTN_TILE
    nq = seq // t
    return pl.pallas_call(
        functools.partial(_attn_kernel, lambda_init=lambda_init),
        grid=(bsz, DA_HEADS, nq),
        in_specs=[pl.BlockSpec(lam_p.shape, lambda b, h, i: (0, 0)),
                  pl.BlockSpec(subln_col.shape, lambda b, h, i: (0, 0)),
                  pl.BlockSpec((LANES, t), lambda b, h, i: (h, b * nq + i)),
                  pl.BlockSpec((seq, LANES), lambda b, h, i: (b, h)),
                  pl.BlockSpec((None, nq, VT_ROWS, t), lambda b, h, i: (h, b, 0, 0))],
        out_specs=pl.BlockSpec((t, LANES), lambda b, h, i: (b * nq + i, h)),
        out_shape=jax.ShapeDtypeStruct(k.shape, BF16),
        scratch_shapes=[pltpu.VMEM((2, LANES, t), BF16), pltpu.VMEM((2, 1, t), F32),
                        pltpu.VMEM((2, VT_ROWS, t), F32)],
        compiler_params=pltpu.CompilerParams(
            dimension_semantics=("arbitrary", "arbitrary", "arbitrary"),
            vmem_limit_bytes=VMEM_LIMIT),
        name="l0_diff_attention",
    )(lam_p, subln_col, qt, k, vt)


def _l0_post_kernel(x_ref, attn_ref, conv_ref, p_ref, wo_ref, gffn_ref, w13_ref, w2_ref,
                    gple_ref, wg_ref, wp_ref, o_ref, acc_sc, hn_sc):
    h = (x_ref[...] + _dot(attn_ref[...], wo_ref[0:DA_WIDTH, :])
         + _dot(conv_ref[...], wo_ref[DA_WIDTH:DA_WIDTH + SC_WIDTH, :]))
    hn_sc[...] = _rms(h, gffn_ref[...]).astype(BF16)
    acc_sc[...] = h
    _swiglu_chunks(hn_sc, w13_ref, w2_ref, acc_sc)
    o_ref[...] = _ple(acc_sc[...], p_ref, gple_ref, wg_ref, wp_ref)


def _l0_post(x2, attn, conv, p0, wo, gffn, w13, w2, gple, wg, wp):
    n = x2.shape[0]
    tm = ROW_TILE
    return pl.pallas_call(
        _l0_post_kernel,
        grid=(n // tm,),
        in_specs=[_rows(tm, D_MODEL), _rows(tm, DA_WIDTH), _rows(tm, SC_WIDTH), _rows(tm, PLE_DIM),
                  _resident(wo.shape), _resident(gffn.shape), _resident(w13.shape),
                  _resident(w2.shape), _resident(gple.shape), _resident(wg.shape),
                  _resident(wp.shape)],
        out_specs=_rows(tm, D_MODEL),
        out_shape=jax.ShapeDtypeStruct((n, D_MODEL), F32),
        scratch_shapes=[pltpu.VMEM((tm, D_MODEL), F32), pltpu.VMEM((tm, D_MODEL), BF16)],
        compiler_params=pltpu.CompilerParams(dimension_semantics=("arbitrary",),
                                             vmem_limit_bytes=VMEM_LIMIT),
        name="l0_outproj_swiglu_ple",
    )(x2, attn, conv, p0, wo, gffn, w13, w2, gple, wg, wp)


def _split_bf16(a):
    hi = a.astype(BF16)
    return hi, (a - hi.astype(F32)).astype(BF16)


def _l1_mix_kernel(h_ref, gmix_ref, win_ref, lng_ref, lnb_ref, ws_ref, bs_ref, wout_ref,
                   gffn_ref, wr_hi_ref, wr_lo_ref,
                   h1_ref, hn2_ref, ridx_ref, rgate_ref, gated_sc):
    tm = h_ref.shape[0]
    h = h_ref[...]
    hn = _rms(h, gmix_ref[...]).astype(BF16)

    def gelu(z):
        return 0.5 * z * (1.0 + lax.erf(z * (2.0 ** -0.5)))

    u = gelu(_dot(hn, win_ref[:, 0:GM_WIDTH]))
    vv = gelu(_dot(hn, win_ref[:, GM_WIDTH:2 * GM_WIDTH]))
    mu = jnp.mean(vv, axis=-1, keepdims=True)
    vc = vv - mu
    var = jnp.mean(vc * vc, axis=-1, keepdims=True)
    vv = (vc * lax.rsqrt(var + LN_EPS) * lng_ref[...] + lnb_ref[...]).astype(BF16)

    n_chunks = tm // GM_CHUNK
    row = lax.broadcasted_iota(jnp.int32, (GM_CHUNK, GM_CHUNK), 0)
    col = lax.broadcasted_iota(jnp.int32, (GM_CHUNK, GM_CHUNK), 1)
    for g in range(GM_GROUPS):
        gs = slice(g * LANES, (g + 1) * LANES)
        wm = jnp.where(col <= row, ws_ref[g], 0.0).astype(BF16)
        rhs = jnp.concatenate(
            [vv[r * GM_CHUNK:(r + 1) * GM_CHUNK, gs] for r in range(n_chunks)], axis=1)
        vs = _dot(wm, rhs) + bs_ref[:, g:g + 1]
        for r in range(n_chunks):
            rs = slice(r * GM_CHUNK, (r + 1) * GM_CHUNK)
            gated_sc[rs, gs] = (u[rs, gs] * vs[:, r * LANES:(r + 1) * LANES]).astype(BF16)

    h1 = h + _dot(gated_sc[...], wout_ref[...])
    h1_ref[...] = h1

    hn2 = _rms(h1, gffn_ref[...])
    hn2_ref[...] = hn2
    x_hi, x_lo = _split_bf16(hn2)
    logits = (_dot(x_hi, wr_hi_ref[...]) + _dot(x_hi, wr_lo_ref[...])
              + _dot(x_lo, wr_hi_ref[...]))
    lane = lax.broadcasted_iota(jnp.int32, logits.shape, 1)
    logits = jnp.where(lane < N_EXPERTS, logits, -jnp.inf)
    v1 = jnp.max(logits, axis=-1, keepdims=True)
    i1 = jnp.min(jnp.where(logits == v1, lane, LANES), axis=-1, keepdims=True)
    rest = jnp.where(lane == i1, -jnp.inf, logits)
    v2 = jnp.max(rest, axis=-1, keepdims=True)
    i2 = jnp.min(jnp.where(rest == v2, lane, LANES), axis=-1, keepdims=True)
    e2 = jnp.exp(v2 - v1)
    ridx_ref[:, 0:1] = i1
    ridx_ref[:, 1:2] = i2
    rgate_ref[:, 0:1] = 1.0 / (1.0 + e2)
    rgate_ref[:, 1:2] = e2 / (1.0 + e2)


def _l1_mix(h, gmix, win, lng, lnb, ws, bs_t, wout, gffn, wr_hi, wr_lo):
    n = h.shape[0]
    tm = ROW_TILE
    return pl.pallas_call(
        _l1_mix_kernel,
        grid=(n // tm,),
        in_specs=[_rows(tm, D_MODEL)] + [_resident(a.shape) for a in
                                         (gmix, win, lng, lnb, ws, bs_t, wout, gffn, wr_hi, wr_lo)],
        out_specs=[_rows(tm, D_MODEL), _rows(tm, D_MODEL), _rows(tm, TOP_K), _rows(tm, TOP_K)],
        out_shape=[jax.ShapeDtypeStruct((n, D_MODEL), F32), jax.ShapeDtypeStruct((n, D_MODEL), F32),
                   jax.ShapeDtypeStruct((n, TOP_K), jnp.int32),
                   jax.ShapeDtypeStruct((n, TOP_K), F32)],
        scratch_shapes=[pltpu.VMEM((tm, GM_WIDTH), BF16)],
        compiler_params=pltpu.CompilerParams(dimension_semantics=("arbitrary",),
                                             vmem_limit_bytes=VMEM_LIMIT),
        name="l1_gmlp_router",
    )(h, gmix, win, lng, lnb, ws, bs_t, wout, gffn, wr_hi, wr_lo)


def _row_gather(idx_ref, base, src_hbm, dst, sem, n_rows, stride=1):
    def issue(r, carry):
        pltpu.make_async_copy(src_hbm.at[pl.ds(idx_ref[base + r * stride], 1)],
                              dst.at[pl.ds(r, 1)], sem).start()
        return carry

    lax.fori_loop(0, n_rows, issue, 0)


def _row_gather_wait(src_hbm, dst, sem, n_rows):
    pltpu.make_async_copy(src_hbm.at[pl.ds(0, n_rows)], dst, sem).wait()


def _moe_kernel(bexp_ref, bvalid_ref, tok_ref, hn_hbm, w13_ref, w2_ref, y_ref,
                xbuf, xb_sc, acc_sc, sem):
    i = pl.program_id(0)
    tm = y_ref.shape[0]

    @pl.when(bvalid_ref[i] == 0)
    def _():
        y_ref[...] = jnp.zeros(y_ref.shape, y_ref.dtype)

    @pl.when(bvalid_ref[i] != 0)
    def _():
        _row_gather(tok_ref, i * tm, hn_hbm, xbuf, sem, tm)
        _row_gather_wait(hn_hbm, xbuf, sem, tm)
        xb_sc[...] = xbuf[...].astype(BF16)
        acc_sc[...] = jnp.zeros(acc_sc.shape, F32)
        _swiglu_chunks(xb_sc, w13_ref, w2_ref, acc_sc)
        y_ref[...] = acc_sc[...]


def _moe_experts(block_expert, block_valid, row_token, hn2, w13, w2):
    n_blocks = block_expert.shape[0]
    tm = MOE_TILE
    w13_spec = pl.BlockSpec((None,) + w13.shape[1:], lambda i, be, bv, tok: (be[i], 0, 0, 0))
    w2_spec = pl.BlockSpec((None,) + w2.shape[1:], lambda i, be, bv, tok: (be[i], 0, 0, 0))
    return pl.pallas_call(
        _moe_kernel,
        grid_spec=pltpu.PrefetchScalarGridSpec(
            num_scalar_prefetch=3,
            grid=(n_blocks,),
            in_specs=[pl.BlockSpec(memory_space=pl.ANY), w13_spec, w2_spec],
            out_specs=pl.BlockSpec((tm, D_MODEL), lambda i, be, bv, tok: (i, 0)),
            scratch_shapes=[pltpu.VMEM((tm, D_MODEL), F32), pltpu.VMEM((tm, D_MODEL), BF16),
                            pltpu.VMEM((tm, D_MODEL), F32), pltpu.SemaphoreType.DMA(())],
        ),
        out_shape=jax.ShapeDtypeStruct((n_blocks * tm, D_MODEL), F32),
        compiler_params=pltpu.CompilerParams(dimension_semantics=("arbitrary",),
                                             vmem_limit_bytes=VMEM_LIMIT),
        name="l1_expert_swiglu",
    )(block_expert, block_valid, row_token, hn2, w13, w2)


def _combine_kernel(dest_ref, h_ref, gate_ref, p_ref, y_hbm, gple_ref, wg_ref, wp_ref, gfin_ref,
                    o_ref, ybuf0, ybuf1, sem):
    i = pl.program_id(0)
    tm = h_ref.shape[0]
    _row_gather(dest_ref, i * tm * TOP_K, y_hbm, ybuf0, sem.at[0], tm, stride=TOP_K)
    _row_gather(dest_ref, i * tm * TOP_K + 1, y_hbm, ybuf1, sem.at[1], tm, stride=TOP_K)
    _row_gather_wait(y_hbm, ybuf0, sem.at[0], tm)
    _row_gather_wait(y_hbm, ybuf1, sem.at[1], tm)
    gate = gate_ref[...]
    h = h_ref[...] + (ybuf0[...] * gate[:, 0:1] + ybuf1[...] * gate[:, 1:2])
    h = _ple(h, p_ref, gple_ref, wg_ref, wp_ref)
    o_ref[...] = _rms(h, gfin_ref[...])


def _combine(dest, h1, rgate, p1, y_rows, gple, wg, wp, gfin):
    n = h1.shape[0]
    tm = ROW_TILE
    rows = lambda w: pl.BlockSpec((tm, w), lambda i, d: (i, 0))
    return pl.pallas_call(
        _combine_kernel,
        grid_spec=pltpu.PrefetchScalarGridSpec(
            num_scalar_prefetch=1,
            grid=(n // tm,),
            in_specs=[rows(D_MODEL), rows(TOP_K), rows(PLE_DIM), pl.BlockSpec(memory_space=pl.ANY),
                      _resident(gple.shape), _resident(wg.shape), _resident(wp.shape),
                      _resident(gfin.shape)],
            out_specs=rows(D_MODEL),
            scratch_shapes=[pltpu.VMEM((tm, D_MODEL), F32), pltpu.VMEM((tm, D_MODEL), F32),
                            pltpu.SemaphoreType.DMA((2,))],
        ),
        out_shape=jax.ShapeDtypeStruct((n, D_MODEL), F32),
        compiler_params=pltpu.CompilerParams(dimension_semantics=("arbitrary",),
                                             vmem_limit_bytes=VMEM_LIMIT),
        name="l1_combine_ple_norm",
    )(dest, h1, rgate, p1, y_rows, gple, wg, wp, gfin)


def _swiglu_weights(w1, w3, w2):
    lead = w1.shape[:-2]
    nl = len(lead)
    w1c = w1.astype(BF16).reshape(lead + (D_MODEL, N_FF_CHUNKS, FF_CHUNK))
    w3c = w3.astype(BF16).reshape(lead + (D_MODEL, N_FF_CHUNKS, FF_CHUNK))
    w13 = jnp.concatenate([w1c, w3c], axis=-1)
    w13 = jnp.moveaxis(w13, nl + 1, nl)
    w2c = w2.astype(BF16).reshape(lead + (N_FF_CHUNKS, FF_CHUNK, D_MODEL))
    return w13, w2c


def _routing_tables(ridx, n_blocks):
    n_assign = ridx.size
    e_flat = ridx.reshape(-1)
    onehot = (e_flat[:, None] == jnp.arange(N_EXPERTS, dtype=jnp.int32)[None, :]).astype(jnp.int32)
    csum = jnp.cumsum(onehot, axis=0)
    counts = csum[-1]
    rank = jnp.sum(csum * onehot, axis=1) - 1
    padded = (counts + MOE_TILE - 1) // MOE_TILE * MOE_TILE
    pad_ends = jnp.cumsum(padded)
    pad_starts = pad_ends - padded
    dest = (jnp.sum(pad_starts[None, :] * onehot, axis=1) + rank).astype(jnp.int32)
    row_token = jnp.zeros((n_blocks * MOE_TILE,), jnp.int32).at[dest].set(
        jnp.arange(n_assign, dtype=jnp.int32) // TOP_K)
    block_start = jnp.arange(n_blocks, dtype=jnp.int32) * MOE_TILE
    block_expert = jnp.minimum(
        jnp.sum((block_start[:, None] >= pad_ends[None, :]).astype(jnp.int32), axis=1),
        N_EXPERTS - 1).astype(jnp.int32)
    block_valid = (block_start < pad_ends[-1]).astype(jnp.int32)
    return dest, row_token, block_expert, block_valid


def kernel(x, p, positions, ln_mix, ln_ffn, ln_ple, a_w_in, a_lambda, a_subln, a_conv_w, a_w_out,
           ffn_w1, ffn_w3, ffn_w2, c_w_in, c_ln_g, c_ln_b, c_w_s, c_b_s, c_w_out, router_w,
           moe_w1, moe_w3, moe_w2, ple_gate, ple_proj, final_norm):
    bsz, seq, _ = x.shape
    n = bsz * seq
    assert seq % ROW_TILE == 0 and seq % ATTN_TILE == 0 and ROW_TILE % GM_CHUNK == 0
    x2 = x.reshape(n, D_MODEL)
    pos2 = positions.reshape(n, 1)
    row = lambda a: a.reshape(1, -1)

    inv_freq = ROPE_THETA ** (-jnp.arange(0, ROT_DIM, 2, dtype=F32) / ROT_DIM)
    lane = jnp.arange(LANES) % DA_HEAD_DIM
    freq = jnp.where(lane < ROT_DIM, inv_freq[lane % (ROT_DIM // 2)], 0.0).reshape(1, LANES)
    qt, k, vt, conv = _inproj(x2, pos2, row(ln_mix[0]), freq, a_conv_w[0].T, a_w_in[0].astype(BF16), seq)
    lambda_init = 0.8 - 0.6 * math.exp(-0.3 * 0)
    attn = _diff_attention(qt, k, vt, a_lambda[0], a_subln[0].reshape(-1, 1), lambda_init, bsz, seq)
    w13, w2 = _swiglu_weights(ffn_w1[0], ffn_w3[0], ffn_w2[0])
    h = _l0_post(x2, attn, conv, p[0].reshape(n, PLE_DIM), a_w_out[0].astype(BF16), row(ln_ffn[0]),
                 w13, w2, row(ln_ple[0]), ple_gate[0].astype(BF16), ple_proj[0].astype(BF16))

    wr = jnp.pad(router_w[0], ((0, 0), (0, LANES - N_EXPERTS)))
    wr_hi, wr_lo = _split_bf16(wr)
    h1, hn2, ridx, rgate = _l1_mix(h, row(ln_mix[1]), c_w_in[0].astype(BF16), row(c_ln_g[0]),
                                   row(c_ln_b[0]), c_w_s[0], c_b_s[0].T, c_w_out[0].astype(BF16),
                                   row(ln_ffn[1]), wr_hi, wr_lo)
    n_blocks = -(-(n * TOP_K + N_EXPERTS * (MOE_TILE - 1)) // MOE_TILE)
    dest, row_token, block_expert, block_valid = _routing_tables(ridx, n_blocks)
    mw13, mw2 = _swiglu_weights(moe_w1[0], moe_w3[0], moe_w2[0])
    y_rows = _moe_experts(block_expert, block_valid, row_token, hn2, mw13, mw2)
    out = _combine(dest, h1, rgate, p[1].reshape(n, PLE_DIM), y_rows, row(ln_ple[1]),
                   ple_gate[1].astype(BF16), ple_proj[1].astype(BF16), row(final_norm))
    return out.reshape(bsz, seq, D_MODEL)
```

```python
import functools
import math

import jax
import jax.numpy as jnp
from jax import lax
from jax.experimental import pallas as pl
from jax.experimental.pallas import tpu as pltpu

F32 = jnp.float32
BF16 = jnp.bfloat16

D_MODEL = 1024
PLE_DIM = 256
DA_HEADS = 4
DA_HEAD_DIM = 64
DA_V_DIM = 2 * DA_HEAD_DIM
DA_WIDTH = DA_HEADS * DA_V_DIM
ROPE_THETA = 500000.0
ROT_DIM = DA_HEAD_DIM // 4
SC_WIDTH = 512
GM_WIDTH = D_MODEL
GM_GROUPS = 8
GM_CHUNK = 128
D_FF = 2816
N_EXPERTS = 8
TOP_K = 2
RMS_EPS = 1e-6
LN_EPS = 1e-5

LANES = 128
FF_CHUNK = 256
N_FF_CHUNKS = D_FF // FF_CHUNK
ROW_TILE = 512
ATTN_TILE = 256
VT_ROWS = DA_V_DIM + 16
MOE_TILE = 512
MOE_FENCE_CHUNKS = (3, 6, 9)
MOE_COPY_GROUP = -(-MOE_TILE // (MOE_FENCE_CHUNKS[-1] + 1))
CONV_HALO = 8
VMEM_LIMIT = 56 * 2**20


def _resident(shape):
    nd = len(shape)
    return pl.BlockSpec(shape, lambda *_: (0,) * nd, pipeline_mode=pl.Buffered(1))


def _rows(tile, width):
    return pl.BlockSpec((tile, width), lambda i, *_: (i, 0))


def _rms(x, g):
    return x * lax.rsqrt(jnp.mean(x * x, axis=-1, keepdims=True) + RMS_EPS) * g


def _dot(a, b):
    return jnp.dot(a, b, preferred_element_type=F32)


def _swiglu_chunks(hn_ref, w13_ref, w2_ref, acc_ref, per_chunk=None):
    def chunk(c, carry):
        gu = _dot(hn_ref[...], w13_ref[c])
        g = gu[:, :FF_CHUNK]
        u = gu[:, FF_CHUNK:]
        a = (g * jax.nn.sigmoid(g) * u).astype(BF16)
        acc_ref[...] += _dot(a, w2_ref[c])
        if per_chunk is not None:
            per_chunk(c)
        return carry

    if per_chunk is None:
        lax.fori_loop(0, N_FF_CHUNKS, chunk, 0)
    else:
        for c in range(N_FF_CHUNKS):
            chunk(c, 0)


def _ple(h, p_ref, gple_ref, wg_ref, wp_ref):
    gate = jax.nn.sigmoid(_dot(_rms(h, gple_ref[...]).astype(BF16), wg_ref[...]))
    return h + gate * _dot(p_ref[...].astype(BF16), wp_ref[...])


def _inproj_kernel(x_ref, pos_ref, g_ref, freq_ref, cw_ref, w_ref,
                   qt_ref, k_ref, vt_ref, conv_ref, cbuf, *, tiles_per_seq):
    i = pl.program_id(0)
    tm = x_ref.shape[0]
    hn = _rms(x_ref[...], g_ref[...]).astype(BF16)

    ang = pos_ref[...].astype(F32) * freq_ref[...]
    lane = lax.broadcasted_iota(jnp.int32, (1, LANES), 1) % DA_HEAD_DIM
    half = ROT_DIM // 2
    cos = jnp.where(lane < ROT_DIM, jnp.cos(ang), 1.0)
    sin = jnp.sin(ang)
    sin_up = jnp.where(lane < half, -sin, 0.0)
    sin_dn = jnp.where((lane >= half) & (lane < ROT_DIM), sin, 0.0)

    def rope(t):
        return (t * cos + pltpu.roll(t, LANES - half, 1) * sin_up
                + pltpu.roll(t, half, 1) * sin_dn)

    zq = _dot(hn, w_ref[:, 0:DA_WIDTH])
    zk = _dot(hn, w_ref[:, DA_WIDTH:2 * DA_WIDTH])
    zv = _dot(hn, w_ref[:, 2 * DA_WIDTH:3 * DA_WIDTH])
    ones = jnp.ones((VT_ROWS - DA_V_DIM, ATTN_TILE), BF16)
    for hd in range(DA_HEADS):
        sl = slice(hd * LANES, (hd + 1) * LANES)
        qt_ref[sl, :] = (rope(zq[:, sl]) * (DA_HEAD_DIM ** -0.5)).T.astype(BF16)
        k_ref[:, sl] = rope(zk[:, sl]).astype(BF16)
        for u in range(tm // ATTN_TILE):
            vt_ref[hd, u, 0:DA_V_DIM, :] = zv[u * ATTN_TILE:(u + 1) * ATTN_TILE, sl].T.astype(BF16)
            vt_ref[hd, u, DA_V_DIM:VT_ROWS, :] = ones

    off = 3 * DA_WIDTH
    b_gate = _dot(hn, w_ref[:, off:off + SC_WIDTH])
    c_gate = _dot(hn, w_ref[:, off + SC_WIDTH:off + 2 * SC_WIDTH])
    hc = _dot(hn, w_ref[:, off + 2 * SC_WIDTH:off + 3 * SC_WIDTH])

    @pl.when(i % tiles_per_seq == 0)
    def _():
        cbuf[0:CONV_HALO, :] = jnp.zeros((CONV_HALO, SC_WIDTH), F32)

    @pl.when(i % tiles_per_seq != 0)
    def _():
        cbuf[0:CONV_HALO, :] = cbuf[tm:tm + CONV_HALO, :]

    ch = c_gate * hc
    cbuf[CONV_HALO:CONV_HALO + tm, :] = ch
    ch1 = cbuf[CONV_HALO - 1:CONV_HALO - 1 + tm, :]
    ch2 = cbuf[CONV_HALO - 2:CONV_HALO - 2 + tm, :]
    conv = b_gate * (cw_ref[2:3, :] * ch + cw_ref[1:2, :] * ch1 + cw_ref[0:1, :] * ch2)
    conv_ref[...] = conv.astype(BF16)


def _inproj(x2, pos2, g, freq, cw, w, seq):
    n = x2.shape[0]
    tm = ROW_TILE
    out = jax.ShapeDtypeStruct((n, DA_WIDTH), BF16)
    sub = tm // ATTN_TILE
    return pl.pallas_call(
        functools.partial(_inproj_kernel, tiles_per_seq=seq // tm),
        grid=(n // tm,),
        in_specs=[_rows(tm, D_MODEL), _rows(tm, 1), _resident(g.shape), _resident(freq.shape),
                  _resident(cw.shape), _resident(w.shape)],
        out_specs=[pl.BlockSpec((DA_WIDTH, tm), lambda i: (0, i)), _rows(tm, DA_WIDTH),
                   pl.BlockSpec((DA_HEADS, sub, VT_ROWS, ATTN_TILE), lambda i: (0, i, 0, 0)),
                   _rows(tm, SC_WIDTH)],
        out_shape=[jax.ShapeDtypeStruct((DA_WIDTH, n), BF16), out,
                   jax.ShapeDtypeStruct((DA_HEADS, n // ATTN_TILE, VT_ROWS, ATTN_TILE), BF16), out],
        scratch_shapes=[pltpu.VMEM((tm + CONV_HALO, SC_WIDTH), F32)],
        compiler_params=pltpu.CompilerParams(dimension_semantics=("arbitrary",),
                                             vmem_limit_bytes=VMEM_LIMIT),
        name="l0_inproj",
    )(x2, pos2, g, freq, cw, w)


def _attn_kernel(lam_ref, sg_ref, qt_ref, k_ref, vt_ref, o_ref, qc_sc, m_sc, acc_sc, *, lambda_init):
    qi = pl.program_id(2)
    t = ATTN_TILE
    qt = qt_ref[...]
    dim = lax.broadcasted_iota(jnp.int32, (LANES, 1), 0)
    zero = jnp.zeros_like(qt)
    qc_sc[0] = jnp.where(dim < DA_HEAD_DIM, qt, zero)
    qc_sc[1] = jnp.where(dim >= DA_HEAD_DIM, qt, zero)
    m_sc[...] = jnp.full(m_sc.shape, -jnp.inf, F32)
    acc_sc[...] = jnp.zeros(acc_sc.shape, F32)

    def step(first, n_sub, diagonal):
        for c in range(2):
            s = []
            for u in range(n_sub):
                ks = pl.multiple_of((first + u) * t, t)
                su = _dot(k_ref[pl.ds(ks, t), :], qc_sc[c])
                if diagonal:
                    key = lax.broadcasted_iota(jnp.int32, (t, t), 0)
                    qry = lax.broadcasted_iota(jnp.int32, (t, t), 1)
                    su = jnp.where(key <= qry, su, -jnp.inf)
                s.append(su)
            m_prev = m_sc[c]
            m_new = m_prev
            for su in s:
                m_new = jnp.maximum(m_new, jnp.max(su, axis=0, keepdims=True))
            pv = None
            for u, su in enumerate(s):
                d = _dot(vt_ref[first + u], jnp.exp(su - m_new).astype(BF16))
                pv = d if pv is None else pv + d
            acc_sc[c] = jnp.exp(m_prev - m_new) * acc_sc[c] + pv
            m_sc[c] = m_new

    def pair_step(j, carry):
        step(2 * j, 2, False)
        return carry

    lax.fori_loop(0, qi // 2, pair_step, 0)

    @pl.when(qi % 2 == 1)
    def _():
        step(qi - 1, 1, False)

    step(qi, 1, True)

    lp = lam_ref[...]
    lam = (jnp.exp(jnp.sum(lp[0:1] * lp[1:2], axis=-1, keepdims=True))
           - jnp.exp(jnp.sum(lp[2:3] * lp[3:4], axis=-1, keepdims=True)) + lambda_init)
    a0 = acc_sc[0]
    a1 = acc_sc[1]
    o = (a0[0:DA_V_DIM] / a0[DA_V_DIM:DA_V_DIM + 1]
         - lam * (a1[0:DA_V_DIM] / a1[DA_V_DIM:DA_V_DIM + 1]))
    o = o * lax.rsqrt(jnp.mean(o * o, axis=0, keepdims=True) + RMS_EPS) * sg_ref[...]
    o_ref[...] = (o * (1.0 - lambda_init)).T.astype(o_ref.dtype)


def _diff_attention(qt, k, vt, lam_p, subln_col, lambda_init, bsz, seq):
    t = ATTN_TILE
    nq = seq // t
    return pl.pallas_call(
        functools.partial(_attn_kernel, lambda_init=lambda_init),
        grid=(bsz, DA_HEADS, nq),
        in_specs=[pl.BlockSpec(lam_p.shape, lambda b, h, i: (0, 0)),
                  pl.BlockSpec(subln_col.shape, lambda b, h, i: (0, 0)),
                  pl.BlockSpec((LANES, t), lambda b, h, i: (h, b * nq + i)),
                  pl.BlockSpec((seq, LANES), lambda b, h, i: (b, h)),
                  pl.BlockSpec((None, nq, VT_ROWS, t), lambda b, h, i: (h, b, 0, 0))],
        out_specs=pl.BlockSpec((t, LANES), lambda b, h, i: (b * nq + i, h)),
        out_shape=jax.ShapeDtypeStruct(k.shape, BF16),
        scratch_shapes=[pltpu.VMEM((2, LANES, t), BF16), pltpu.VMEM((2, 1, t), F32),
                        pltpu.VMEM((2, VT_ROWS, t), F32)],
        compiler_params=pltpu.CompilerParams(
            dimension_semantics=("arbitrary", "arbitrary", "arbitrary"),
            vmem_limit_bytes=VMEM_LIMIT),
        name="l0_diff_attention",
    )(lam_p, subln_col, qt, k, vt)


def _l0_post_kernel(x_ref, attn_ref, conv_ref, p_ref, wo_ref, gffn_ref, w13_ref, w2_ref,
                    gple_ref, wg_ref, wp_ref, o_ref, acc_sc, hn_sc):
    h = (x_ref[...] + _dot(attn_ref[...], wo_ref[0:DA_WIDTH, :])
         + _dot(conv_ref[...], wo_ref[DA_WIDTH:DA_WIDTH + SC_WIDTH, :]))
    hn_sc[...] = _rms(h, gffn_ref[...]).astype(BF16)
    acc_sc[...] = h
    _swiglu_chunks(hn_sc, w13_ref, w2_ref, acc_sc)
    o_ref[...] = _ple(acc_sc[...], p_ref, gple_ref, wg_ref, wp_ref)


def _l0_post(x2, attn, conv, p0, wo, gffn, w13, w2, gple, wg, wp):
    n = x2.shape[0]
    tm = ROW_TILE
    return pl.pallas_call(
        _l0_post_kernel,
        grid=(n // tm,),
        in_specs=[_rows(tm, D_MODEL), _rows(tm, DA_WIDTH), _rows(tm, SC_WIDTH), _rows(tm, PLE_DIM),
                  _resident(wo.shape), _resident(gffn.shape), _resident(w13.shape),
                  _resident(w2.shape), _resident(gple.shape), _resident(wg.shape),
                  _resident(wp.shape)],
        out_specs=_rows(tm, D_MODEL),
        out_shape=jax.ShapeDtypeStruct((n, D_MODEL), F32),
        scratch_shapes=[pltpu.VMEM((tm, D_MODEL), F32), pltpu.VMEM((tm, D_MODEL), BF16)],
        compiler_params=pltpu.CompilerParams(dimension_semantics=("arbitrary",),
                                             vmem_limit_bytes=VMEM_LIMIT),
        name="l0_outproj_swiglu_ple",
    )(x2, attn, conv, p0, wo, gffn, w13, w2, gple, wg, wp)


def _split_bf16(a):
    hi = a.astype(BF16)
    return hi, (a - hi.astype(F32)).astype(BF16)


def _l1_mix_kernel(h_ref, gmix_ref, win_ref, lng_ref, lnb_ref, ws_ref, bs_ref, wout_ref,
                   gffn_ref, wr_hi_ref, wr_lo_ref,
                   h1_ref, hn2_ref, ridx_ref, rgate_ref, gated_sc):
    tm = h_ref.shape[0]
    h = h_ref[...]
    hn = _rms(h, gmix_ref[...]).astype(BF16)

    def gelu(z):
        return 0.5 * z * (1.0 + lax.erf(z * (2.0 ** -0.5)))

    u = gelu(_dot(hn, win_ref[:, 0:GM_WIDTH]))
    vv = gelu(_dot(hn, win_ref[:, GM_WIDTH:2 * GM_WIDTH]))
    mu = jnp.mean(vv, axis=-1, keepdims=True)
    vc = vv - mu
    var = jnp.mean(vc * vc, axis=-1, keepdims=True)
    vv = (vc * lax.rsqrt(var + LN_EPS) * lng_ref[...] + lnb_ref[...]).astype(BF16)

    n_chunks = tm // GM_CHUNK
    row = lax.broadcasted_iota(jnp.int32, (GM_CHUNK, GM_CHUNK), 0)
    col = lax.broadcasted_iota(jnp.int32, (GM_CHUNK, GM_CHUNK), 1)
    for g in range(GM_GROUPS):
        gs = slice(g * LANES, (g + 1) * LANES)
        wm = jnp.where(col <= row, ws_ref[g], 0.0).astype(BF16)
        rhs = jnp.concatenate(
            [vv[r * GM_CHUNK:(r + 1) * GM_CHUNK, gs] for r in range(n_chunks)], axis=1)
        vs = _dot(wm, rhs) + bs_ref[:, g:g + 1]
        for r in range(n_chunks):
            rs = slice(r * GM_CHUNK, (r + 1) * GM_CHUNK)
            gated_sc[rs, gs] = (u[rs, gs] * vs[:, r * LANES:(r + 1) * LANES]).astype(BF16)

    h1 = h + _dot(gated_sc[...], wout_ref[...])
    h1_ref[...] = h1

    hn2 = _rms(h1, gffn_ref[...])
    hn2_ref[...] = hn2
    x_hi, x_lo = _split_bf16(hn2)
    logits = (_dot(x_hi, wr_hi_ref[...]) + _dot(x_hi, wr_lo_ref[...])
              + _dot(x_lo, wr_hi_ref[...]))
    lane = lax.broadcasted_iota(jnp.int32, logits.shape, 1)
    logits = jnp.where(lane < N_EXPERTS, logits, -jnp.inf)
    v1 = jnp.max(logits, axis=-1, keepdims=True)
    i1 = jnp.min(jnp.where(logits == v1, lane, LANES), axis=-1, keepdims=True)
    rest = jnp.where(lane == i1, -jnp.inf, logits)
    v2 = jnp.max(rest, axis=-1, keepdims=True)
    i2 = jnp.min(jnp.where(rest == v2, lane, LANES), axis=-1, keepdims=True)
    e2 = jnp.exp(v2 - v1)
    ridx_ref[:, 0:1] = i1
    ridx_ref[:, 1:2] = i2
    rgate_ref[:, 0:1] = 1.0 / (1.0 + e2)
    rgate_ref[:, 1:2] = e2 / (1.0 + e2)


def _l1_mix(h, gmix, win, lng, lnb, ws, bs_t, wout, gffn, wr_hi, wr_lo):
    n = h.shape[0]
    tm = ROW_TILE
    return pl.pallas_call(
        _l1_mix_kernel,
        grid=(n // tm,),
        in_specs=[_rows(tm, D_MODEL)] + [_resident(a.shape) for a in
                                         (gmix, win, lng, lnb, ws, bs_t, wout, gffn, wr_hi, wr_lo)],
        out_specs=[_rows(tm, D_MODEL), _rows(tm, D_MODEL), _rows(tm, TOP_K), _rows(tm, TOP_K)],
        out_shape=[jax.ShapeDtypeStruct((n, D_MODEL), F32), jax.ShapeDtypeStruct((n, D_MODEL), F32),
                   jax.ShapeDtypeStruct((n, TOP_K), jnp.int32),
                   jax.ShapeDtypeStruct((n, TOP_K), F32)],
        scratch_shapes=[pltpu.VMEM((tm, GM_WIDTH), BF16)],
        compiler_params=pltpu.CompilerParams(dimension_semantics=("arbitrary",),
                                             vmem_limit_bytes=VMEM_LIMIT),
        name="l1_gmlp_router",
    )(h, gmix, win, lng, lnb, ws, bs_t, wout, gffn, wr_hi, wr_lo)


def _moe_kernel(bexp_ref, nvalid_ref, tok_ref, dst_ref, hn_hbm, w13_ref, w2_ref, y_hbm,
                xbuf, xb_sc, acc_sc, ybuf, gsem, ssem, fence_sem, fence_sink):
    i = pl.program_id(0)
    n_valid = nvalid_ref[0]
    tm = MOE_TILE
    prev = jnp.maximum(i - 1, 0)

    def gather_row(blk, r):
        return pltpu.make_async_copy(hn_hbm.at[pl.ds(tok_ref[blk * tm + r], 1)],
                                     xbuf.at[pl.ds(r, 1)], gsem)

    def scatter_row(blk, r):
        return pltpu.make_async_copy(ybuf.at[pl.ds(r, 1)],
                                     y_hbm.at[pl.ds(dst_ref[blk * tm + r], 1)], ssem)

    def gather_wait():
        pltpu.make_async_copy(hn_hbm.at[pl.ds(0, tm)], xbuf, gsem).wait()

    def scatter_wait():
        pltpu.make_async_copy(ybuf, y_hbm.at[pl.ds(0, tm)], ssem).wait()

    @pl.when(i < n_valid)
    def _():
        @pl.when(i == 0)
        def _():
            def issue(r, carry):
                gather_row(0, r).start()
                return carry
            lax.fori_loop(0, tm, issue, 0)
            ybuf[...] = jnp.zeros(ybuf.shape, F32)
            for b in range(2):
                spare = pltpu.make_async_copy(
                    ybuf, y_hbm.at[pl.ds(y_hbm.shape[0] - (b + 1) * tm, tm)], ssem)
                spare.start()
                spare.wait()

        gather_wait()
        xb_sc[...] = xbuf[...].astype(BF16)
        acc_sc[...] = jnp.zeros(acc_sc.shape, F32)
        has_next = i + 1 < n_valid
        has_prev = i >= 1

        def copies(c):
            for r in range(c * MOE_COPY_GROUP, min((c + 1) * MOE_COPY_GROUP, tm)):
                @pl.when(has_next)
                def _():
                    gather_row(i + 1, r).start()

                @pl.when(has_prev)
                def _():
                    scatter_row(prev, r).start()
            if c in MOE_FENCE_CHUNKS:
                fence_sink[0] = pl.semaphore_read(fence_sem)

        _swiglu_chunks(xb_sc, w13_ref, w2_ref, acc_sc, per_chunk=copies)

        @pl.when(has_prev)
        def _():
            scatter_wait()

        ybuf[...] = acc_sc[...]

    @pl.when(i == n_valid)
    def _():
        def issue(r, carry):
            scatter_row(prev, r).start()
            return carry
        lax.fori_loop(0, tm, issue, 0)
        scatter_wait()


def _moe_experts(block_expert, n_valid, row_token, row_dst, hn2, w13, w2, n_out_rows):
    n_blocks = block_expert.shape[0]
    tm = MOE_TILE
    w13_spec = pl.BlockSpec((None,) + w13.shape[1:], lambda i, be, *_: (be[i], 0, 0, 0))
    w2_spec = pl.BlockSpec((None,) + w2.shape[1:], lambda i, be, *_: (be[i], 0, 0, 0))
    return pl.pallas_call(
        _moe_kernel,
        grid_spec=pltpu.PrefetchScalarGridSpec(
            num_scalar_prefetch=4,
            grid=(n_blocks,),
            in_specs=[pl.BlockSpec(memory_space=pl.ANY), w13_spec, w2_spec],
            out_specs=pl.BlockSpec(memory_space=pl.ANY),
            scratch_shapes=[pltpu.VMEM((tm, D_MODEL), F32), pltpu.VMEM((tm, D_MODEL), BF16),
                            pltpu.VMEM((tm, D_MODEL), F32), pltpu.VMEM((tm, D_MODEL), F32),
                            pltpu.SemaphoreType.DMA(()), pltpu.SemaphoreType.DMA(()),
                            pltpu.SemaphoreType.REGULAR(()), pltpu.SMEM((1,), jnp.int32)],
        ),
        out_shape=jax.ShapeDtypeStruct((n_out_rows, D_MODEL), F32),
        compiler_params=pltpu.CompilerParams(dimension_semantics=("arbitrary",),
                                             vmem_limit_bytes=VMEM_LIMIT),
        name="l1_expert_swiglu",
    )(block_expert, n_valid, row_token, row_dst, hn2, w13, w2)


def _combine_kernel(h_ref, y_ref, gate_ref, p_ref, gple_ref, wg_ref, wp_ref, gfin_ref, o_ref):
    gate = gate_ref[...]
    h = h_ref[...] + (y_ref[:, 0:D_MODEL] * gate[:, 0:1]
                      + y_ref[:, D_MODEL:2 * D_MODEL] * gate[:, 1:2])
    h = _ple(h, p_ref, gple_ref, wg_ref, wp_ref)
    o_ref[...] = _rms(h, gfin_ref[...])


def _combine(h1, y_pairs, rgate, p1, gple, wg, wp, gfin):
    n = h1.shape[0]
    tm = ROW_TILE
    return pl.pallas_call(
        _combine_kernel,
        grid=(n // tm,),
        in_specs=[_rows(tm, D_MODEL), _rows(tm, TOP_K * D_MODEL), _rows(tm, TOP_K),
                  _rows(tm, PLE_DIM), _resident(gple.shape), _resident(wg.shape),
                  _resident(wp.shape), _resident(gfin.shape)],
        out_specs=_rows(tm, D_MODEL),
        out_shape=jax.ShapeDtypeStruct((n, D_MODEL), F32),
        compiler_params=pltpu.CompilerParams(dimension_semantics=("arbitrary",),
                                             vmem_limit_bytes=VMEM_LIMIT),
        name="l1_combine_ple_norm",
    )(h1, y_pairs, rgate, p1, gple, wg, wp, gfin)


def _swiglu_weights(w1, w3, w2):
    lead = w1.shape[:-2]
    nl = len(lead)
    w1c = w1.astype(BF16).reshape(lead + (D_MODEL, N_FF_CHUNKS, FF_CHUNK))
    w3c = w3.astype(BF16).reshape(lead + (D_MODEL, N_FF_CHUNKS, FF_CHUNK))
    w13 = jnp.concatenate([w1c, w3c], axis=-1)
    w13 = jnp.moveaxis(w13, nl + 1, nl)
    w2c = w2.astype(BF16).reshape(lead + (N_FF_CHUNKS, FF_CHUNK, D_MODEL))
    return w13, w2c


def _routing_tables(ridx, n_blocks):
    n_assign = ridx.size
    e_flat = ridx.reshape(-1)
    onehot = (e_flat[:, None] == jnp.arange(N_EXPERTS, dtype=jnp.int32)[None, :]).astype(jnp.int32)
    csum = jnp.cumsum(onehot, axis=0)
    counts = csum[-1]
    rank = jnp.sum(csum * onehot, axis=1) - 1
    padded = (counts + MOE_TILE - 1) // MOE_TILE * MOE_TILE
    pad_ends = jnp.cumsum(padded)
    pad_starts = pad_ends - padded
    dest = (jnp.sum(pad_starts[None, :] * onehot, axis=1) + rank).astype(jnp.int32)
    n_rows = (n_blocks + 2) * MOE_TILE
    assign = jnp.full((n_rows,), -1, jnp.int32).at[dest].set(jnp.arange(n_assign, dtype=jnp.int32))
    row_token = jnp.maximum(assign, 0) // TOP_K
    r = jnp.arange(n_rows, dtype=jnp.int32)
    spare = n_assign + (r // MOE_TILE % 2) * MOE_TILE + r % MOE_TILE
    row_dst = jnp.where(assign >= 0, assign, spare)
    block_start = jnp.arange(n_blocks, dtype=jnp.int32) * MOE_TILE
    block_expert = jnp.minimum(
        jnp.sum((block_start[:, None] >= pad_ends[None, :]).astype(jnp.int32), axis=1),
        N_EXPERTS - 1).astype(jnp.int32)
    n_valid = (pad_ends[-1:] // MOE_TILE).astype(jnp.int32)
    return row_token, row_dst, block_expert, n_valid


def kernel(x, p, positions, ln_mix, ln_ffn, ln_ple, a_w_in, a_lambda, a_subln, a_conv_w, a_w_out,
           ffn_w1, ffn_w3, ffn_w2, c_w_in, c_ln_g, c_ln_b, c_w_s, c_b_s, c_w_out, router_w,
           moe_w1, moe_w3, moe_w2, ple_gate, ple_proj, final_norm):
    bsz, seq, _ = x.shape
    n = bsz * seq
    assert seq % ROW_TILE == 0 and seq % ATTN_TILE == 0 and ROW_TILE % GM_CHUNK == 0
    x2 = x.reshape(n, D_MODEL)
    pos2 = positions.reshape(n, 1)
    row = lambda a: a.reshape(1, -1)

    inv_freq = ROPE_THETA ** (-jnp.arange(0, ROT_DIM, 2, dtype=F32) / ROT_DIM)
    lane = jnp.arange(LANES) % DA_HEAD_DIM
    freq = jnp.where(lane < ROT_DIM, inv_freq[lane % (ROT_DIM // 2)], 0.0).reshape(1, LANES)
    qt, k, vt, conv = _inproj(x2, pos2, row(ln_mix[0]), freq, a_conv_w[0].T, a_w_in[0].astype(BF16), seq)
    lambda_init = 0.8 - 0.6 * math.exp(-0.3 * 0)
    attn = _diff_attention(qt, k, vt, a_lambda[0], a_subln[0].reshape(-1, 1), lambda_init, bsz, seq)
    w13, w2 = _swiglu_weights(ffn_w1[0], ffn_w3[0], ffn_w2[0])
    h = _l0_post(x2, attn, conv, p[0].reshape(n, PLE_DIM), a_w_out[0].astype(BF16), row(ln_ffn[0]),
                 w13, w2, row(ln_ple[0]), ple_gate[0].astype(BF16), ple_proj[0].astype(BF16))

    wr = jnp.pad(router_w[0], ((0, 0), (0, LANES - N_EXPERTS)))
    wr_hi, wr_lo = _split_bf16(wr)
    h1, hn2, ridx, rgate = _l1_mix(h, row(ln_mix[1]), c_w_in[0].astype(BF16), row(c_ln_g[0]),
                                   row(c_ln_b[0]), c_w_s[0], c_b_s[0].T, c_w_out[0].astype(BF16),
                                   row(ln_ffn[1]), wr_hi, wr_lo)
    n_blocks = -(-(n * TOP_K + N_EXPERTS * (MOE_TILE - 1)) // MOE_TILE)
    row_token, row_dst, block_expert, n_valid = _routing_tables(ridx, n_blocks)
    mw13, mw2 = _swiglu_weights(moe_w1[0], moe_w3[0], moe_w2[0])
    y_rows = _moe_experts(block_expert, n_valid, row_token, row_dst, hn2, mw13, mw2,
                          n * TOP_K + 2 * MOE_TILE)
    y_pairs = y_rows.reshape(-1, TOP_K * D_MODEL)
    out = _combine(h1, y_pairs, rgate, p[1].reshape(n, PLE_DIM), row(ln_ple[1]),
                   ple_gate[1].astype(BF16), ple_proj[1].astype(BF16), row(final_norm))
    return out.reshape(bsz, seq, D_MODEL)
```

```python
import functools
import math

import jax
import jax.numpy as jnp
from jax import lax
from jax.experimental import pallas as pl
from jax.experimental.pallas import tpu as pltpu

F32 = jnp.float32
BF16 = jnp.bfloat16

D_MODEL = 1024
PLE_DIM = 256
DA_HEADS = 4
DA_HEAD_DIM = 64
DA_V_DIM = 2 * DA_HEAD_DIM
DA_WIDTH = DA_HEADS * DA_V_DIM
ROPE_THETA = 500000.0
ROT_DIM = DA_HEAD_DIM // 4
SC_WIDTH = 512
GM_WIDTH = D_MODEL
GM_GROUPS = 8
GM_CHUNK = 128
D_FF = 2816
N_EXPERTS = 8
TOP_K = 2
RMS_EPS = 1e-6
LN_EPS = 1e-5

LANES = 128
FF_CHUNK = 256
N_FF_CHUNKS = D_FF // FF_CHUNK
ROW_TILE = 512
ATTN_TILE = 256
VT_ROWS = DA_V_DIM + 16
MOE_TILE = 512
MOE_FENCE_CHUNKS = (0, 3, 6, 8, 10)
MOE_COPY_ROWS = (0, 168, 336, 504, 512)
CONV_HALO = 8
VMEM_LIMIT = 56 * 2**20


def _resident(shape):
    nd = len(shape)
    return pl.BlockSpec(shape, lambda *_: (0,) * nd, pipeline_mode=pl.Buffered(1))


def _rows(tile, width):
    return pl.BlockSpec((tile, width), lambda i, *_: (i, 0))


def _rms(x, g):
    return x * lax.rsqrt(jnp.mean(x * x, axis=-1, keepdims=True) + RMS_EPS) * g


def _dot(a, b):
    return jnp.dot(a, b, preferred_element_type=F32)


def _swiglu_chunks(hn_ref, w1_ref, w3_ref, w2_ref, acc_ref, per_chunk=None):
    for c in range(N_FF_CHUNKS):
        cols = slice(c * FF_CHUNK, (c + 1) * FF_CHUNK)
        g = _dot(hn_ref[...], w1_ref[:, cols])
        u = _dot(hn_ref[...], w3_ref[:, cols])
        a = (g * jax.nn.sigmoid(g) * u).astype(BF16)
        acc_ref[...] += _dot(a, w2_ref[cols, :])
        if per_chunk is not None:
            per_chunk(c)


def _ple(h, p_ref, gple_ref, wg_ref, wp_ref):
    gate = jax.nn.sigmoid(_dot(_rms(h, gple_ref[...]).astype(BF16), wg_ref[...]))
    return h + gate * _dot(p_ref[...].astype(BF16), wp_ref[...])


def _inproj_kernel(x_ref, pos_ref, g_ref, freq_ref, cw_ref, w_ref,
                   qt_ref, k_ref, vt_ref, conv_ref, cbuf, *, tiles_per_seq):
    i = pl.program_id(0)
    tm = x_ref.shape[0]
    hn = _rms(x_ref[...], g_ref[...]).astype(BF16)

    ang = pos_ref[...].astype(F32) * freq_ref[...]
    lane = lax.broadcasted_iota(jnp.int32, (1, LANES), 1) % DA_HEAD_DIM
    half = ROT_DIM // 2
    cos = jnp.where(lane < ROT_DIM, jnp.cos(ang), 1.0)
    sin = jnp.sin(ang)
    sin_up = jnp.where(lane < half, -sin, 0.0)
    sin_dn = jnp.where((lane >= half) & (lane < ROT_DIM), sin, 0.0)

    def rope(t):
        return (t * cos + pltpu.roll(t, LANES - half, 1) * sin_up
                + pltpu.roll(t, half, 1) * sin_dn)

    zq = _dot(hn, w_ref[:, 0:DA_WIDTH])
    zk = _dot(hn, w_ref[:, DA_WIDTH:2 * DA_WIDTH])
    zv = _dot(hn, w_ref[:, 2 * DA_WIDTH:3 * DA_WIDTH])
    ones = jnp.ones((VT_ROWS - DA_V_DIM, ATTN_TILE), BF16)
    for hd in range(DA_HEADS):
        sl = slice(hd * LANES, (hd + 1) * LANES)
        qt_ref[sl, :] = (rope(zq[:, sl]) * (DA_HEAD_DIM ** -0.5)).T.astype(BF16)
        k_ref[:, sl] = rope(zk[:, sl]).astype(BF16)
        for u in range(tm // ATTN_TILE):
            vt_ref[hd, u, 0:DA_V_DIM, :] = zv[u * ATTN_TILE:(u + 1) * ATTN_TILE, sl].T.astype(BF16)
            vt_ref[hd, u, DA_V_DIM:VT_ROWS, :] = ones

    off = 3 * DA_WIDTH
    b_gate = _dot(hn, w_ref[:, off:off + SC_WIDTH])
    c_gate = _dot(hn, w_ref[:, off + SC_WIDTH:off + 2 * SC_WIDTH])
    hc = _dot(hn, w_ref[:, off + 2 * SC_WIDTH:off + 3 * SC_WIDTH])

    @pl.when(i % tiles_per_seq == 0)
    def _():
        cbuf[0:CONV_HALO, :] = jnp.zeros((CONV_HALO, SC_WIDTH), F32)

    @pl.when(i % tiles_per_seq != 0)
    def _():
        cbuf[0:CONV_HALO, :] = cbuf[tm:tm + CONV_HALO, :]

    ch = c_gate * hc
    cbuf[CONV_HALO:CONV_HALO + tm, :] = ch
    ch1 = cbuf[CONV_HALO - 1:CONV_HALO - 1 + tm, :]
    ch2 = cbuf[CONV_HALO - 2:CONV_HALO - 2 + tm, :]
    conv = b_gate * (cw_ref[2:3, :] * ch + cw_ref[1:2, :] * ch1 + cw_ref[0:1, :] * ch2)
    conv_ref[...] = conv.astype(BF16)


def _inproj(x2, pos2, g, freq, cw, w, seq):
    n = x2.shape[0]
    tm = ROW_TILE
    out = jax.ShapeDtypeStruct((n, DA_WIDTH), BF16)
    sub = tm // ATTN_TILE
    return pl.pallas_call(
        functools.partial(_inproj_kernel, tiles_per_seq=seq // tm),
        grid=(n // tm,),
        in_specs=[_rows(tm, D_MODEL), _rows(tm, 1), _resident(g.shape), _resident(freq.shape),
                  _resident(cw.shape), _resident(w.shape)],
        out_specs=[pl.BlockSpec((DA_WIDTH, tm), lambda i: (0, i)), _rows(tm, DA_WIDTH),
                   pl.BlockSpec((DA_HEADS, sub, VT_ROWS, ATTN_TILE), lambda i: (0, i, 0, 0)),
                   _rows(tm, SC_WIDTH)],
        out_shape=[jax.ShapeDtypeStruct((DA_WIDTH, n), BF16), out,
                   jax.ShapeDtypeStruct((DA_HEADS, n // ATTN_TILE, VT_ROWS, ATTN_TILE), BF16), out],
        scratch_shapes=[pltpu.VMEM((tm + CONV_HALO, SC_WIDTH), F32)],
        compiler_params=pltpu.CompilerParams(dimension_semantics=("arbitrary",),
                                             vmem_limit_bytes=VMEM_LIMIT),
        name="l0_inproj",
    )(x2, pos2, g, freq, cw, w)


def _attn_kernel(lam_ref, sg_ref, qt_ref, k_ref, vt_ref, o_ref, qc_sc, m_sc, acc_sc, *, lambda_init):
    qi = pl.program_id(2)
    t = ATTN_TILE
    qt = qt_ref[...]
    dim = lax.broadcasted_iota(jnp.int32, (LANES, 1), 0)
    zero = jnp.zeros_like(qt)
    qc_sc[0] = jnp.where(dim < DA_HEAD_DIM, qt, zero)
    qc_sc[1] = jnp.where(dim >= DA_HEAD_DIM, qt, zero)
    m_sc[...] = jnp.full(m_sc.shape, -jnp.inf, F32)
    acc_sc[...] = jnp.zeros(acc_sc.shape, F32)

    def step(first, n_sub, diagonal):
        for c in range(2):
            s = []
            for u in range(n_sub):
                ks = pl.multiple_of((first + u) * t, t)
                su = _dot(k_ref[pl.ds(ks, t), :], qc_sc[c])
                if diagonal:
                    key = lax.broadcasted_iota(jnp.int32, (t, t), 0)
                    qry = lax.broadcasted_iota(jnp.int32, (t, t), 1)
                    su = jnp.where(key <= qry, su, -jnp.inf)
                s.append(su)
            m_prev = m_sc[c]
            m_new = m_prev
            for su in s:
                m_new = jnp.maximum(m_new, jnp.max(su, axis=0, keepdims=True))
            pv = None
            for u, su in enumerate(s):
                d = _dot(vt_ref[first + u], jnp.exp(su - m_new).astype(BF16))
                pv = d if pv is None else pv + d
            acc_sc[c] = jnp.exp(m_prev - m_new) * acc_sc[c] + pv
            m_sc[c] = m_new

    def pair_step(j, carry):
        step(2 * j, 2, False)
        return carry

    lax.fori_loop(0, qi // 2, pair_step, 0)

    @pl.when(qi % 2 == 1)
    def _():
        step(qi - 1, 1, False)

    step(qi, 1, True)

    lp = lam_ref[...]
    lam = (jnp.exp(jnp.sum(lp[0:1] * lp[1:2], axis=-1, keepdims=True))
           - jnp.exp(jnp.sum(lp[2:3] * lp[3:4], axis=-1, keepdims=True)) + lambda_init)
    a0 = acc_sc[0]
    a1 = acc_sc[1]
    o = (a0[0:DA_V_DIM] / a0[DA_V_DIM:DA_V_DIM + 1]
         - lam * (a1[0:DA_V_DIM] / a1[DA_V_DIM:DA_V_DIM + 1]))
    o = o * lax.rsqrt(jnp.mean(o * o, axis=0, keepdims=True) + RMS_EPS) * sg_ref[...]
    o_ref[...] = (o * (1.0 - lambda_init)).T.astype(o_ref.dtype)


def _diff_attention(qt, k, vt, lam_p, subln_col, lambda_init, bsz, seq):
    t = ATTN_TILE
    nq = seq // t
    return pl.pallas_call(
        functools.partial(_attn_kernel, lambda_init=lambda_init),
        grid=(bsz, DA_HEADS, nq),
        in_specs=[pl.BlockSpec(lam_p.shape, lambda b, h, i: (0, 0)),
                  pl.BlockSpec(subln_col.shape, lambda b, h, i: (0, 0)),
                  pl.BlockSpec((LANES, t), lambda b, h, i: (h, b * nq + i)),
                  pl.BlockSpec((seq, LANES), lambda b, h, i: (b, h)),
                  pl.BlockSpec((None, nq, VT_ROWS, t), lambda b, h, i: (h, b, 0, 0))],
        out_specs=pl.BlockSpec((t, LANES), lambda b, h, i: (b * nq + i, h)),
        out_shape=jax.ShapeDtypeStruct(k.shape, BF16),
        scratch_shapes=[pltpu.VMEM((2, LANES, t), BF16), pltpu.VMEM((2, 1, t), F32),
                        pltpu.VMEM((2, VT_ROWS, t), F32)],
        compiler_params=pltpu.CompilerParams(
            dimension_semantics=("arbitrary", "arbitrary", "arbitrary"),
            vmem_limit_bytes=VMEM_LIMIT),
        name="l0_diff_attention",
    )(lam_p, subln_col, qt, k, vt)


def _l0_post_kernel(x_ref, attn_ref, conv_ref, p_ref, wo_ref, gffn_ref, w1_ref, w3_ref, w2_ref,
                    gple_ref, wg_ref, wp_ref, o_ref, acc_sc, hn_sc, h_sc):
    h = (x_ref[...] + _dot(attn_ref[...], wo_ref[0:DA_WIDTH, :])
         + _dot(conv_ref[...], wo_ref[DA_WIDTH:DA_WIDTH + SC_WIDTH, :]))
    hn_sc[...] = _rms(h, gffn_ref[...]).astype(BF16)
    h_sc[...] = h
    acc_sc[...] = jnp.zeros(acc_sc.shape, F32)
    _swiglu_chunks(hn_sc, w1_ref, w3_ref, w2_ref, acc_sc)
    o_ref[...] = _ple(h_sc[...] + acc_sc[...], p_ref, gple_ref, wg_ref, wp_ref)


def _l0_post(x2, attn, conv, p0, wo, gffn, w1, w3, w2, gple, wg, wp):
    n = x2.shape[0]
    tm = ROW_TILE
    return pl.pallas_call(
        _l0_post_kernel,
        grid=(n // tm,),
        in_specs=[_rows(tm, D_MODEL), _rows(tm, DA_WIDTH), _rows(tm, SC_WIDTH), _rows(tm, PLE_DIM),
                  _resident(wo.shape), _resident(gffn.shape), _resident(w1.shape),
                  _resident(w3.shape), _resident(w2.shape), _resident(gple.shape),
                  _resident(wg.shape), _resident(wp.shape)],
        out_specs=_rows(tm, D_MODEL),
        out_shape=jax.ShapeDtypeStruct((n, D_MODEL), F32),
        scratch_shapes=[pltpu.VMEM((tm, D_MODEL), F32), pltpu.VMEM((tm, D_MODEL), BF16),
                        pltpu.VMEM((tm, D_MODEL), F32)],
        compiler_params=pltpu.CompilerParams(dimension_semantics=("arbitrary",),
                                             vmem_limit_bytes=VMEM_LIMIT),
        name="l0_outproj_swiglu_ple",
    )(x2, attn, conv, p0, wo, gffn, w1, w3, w2, gple, wg, wp)


def _split_bf16(a):
    hi = a.astype(BF16)
    return hi, (a - hi.astype(F32)).astype(BF16)


def _l1_mix_kernel(h_ref, gmix_ref, win_ref, lng_ref, lnb_ref, ws_ref, bs_ref, wout_ref,
                   gffn_ref, wr_hi_ref, wr_lo_ref,
                   h1_ref, hn2_ref, ridx_ref, rgate_ref, gated_sc):
    tm = h_ref.shape[0]
    h = h_ref[...]
    hn = _rms(h, gmix_ref[...]).astype(BF16)

    def gelu(z):
        return 0.5 * z * (1.0 + lax.erf(z * (2.0 ** -0.5)))

    u = gelu(_dot(hn, win_ref[:, 0:GM_WIDTH]))
    vv = gelu(_dot(hn, win_ref[:, GM_WIDTH:2 * GM_WIDTH]))
    mu = jnp.mean(vv, axis=-1, keepdims=True)
    vc = vv - mu
    var = jnp.mean(vc * vc, axis=-1, keepdims=True)
    vv = (vc * lax.rsqrt(var + LN_EPS) * lng_ref[...] + lnb_ref[...]).astype(BF16)

    n_chunks = tm // GM_CHUNK
    row = lax.broadcasted_iota(jnp.int32, (GM_CHUNK, GM_CHUNK), 0)
    col = lax.broadcasted_iota(jnp.int32, (GM_CHUNK, GM_CHUNK), 1)
    for g in range(GM_GROUPS):
        gs = slice(g * LANES, (g + 1) * LANES)
        wm = jnp.where(col <= row, ws_ref[g], 0.0).astype(BF16)
        rhs = jnp.concatenate(
            [vv[r * GM_CHUNK:(r + 1) * GM_CHUNK, gs] for r in range(n_chunks)], axis=1)
        vs = _dot(wm, rhs) + bs_ref[:, g:g + 1]
        for r in range(n_chunks):
            rs = slice(r * GM_CHUNK, (r + 1) * GM_CHUNK)
            gated_sc[rs, gs] = (u[rs, gs] * vs[:, r * LANES:(r + 1) * LANES]).astype(BF16)

    h1 = h + _dot(gated_sc[...], wout_ref[...])
    h1_ref[...] = h1

    hn2 = _rms(h1, gffn_ref[...])
    hn2_ref[...] = hn2
    x_hi, x_lo = _split_bf16(hn2)
    logits = (_dot(x_hi, wr_hi_ref[...]) + _dot(x_hi, wr_lo_ref[...])
              + _dot(x_lo, wr_hi_ref[...]))
    lane = lax.broadcasted_iota(jnp.int32, logits.shape, 1)
    logits = jnp.where(lane < N_EXPERTS, logits, -jnp.inf)
    v1 = jnp.max(logits, axis=-1, keepdims=True)
    i1 = jnp.min(jnp.where(logits == v1, lane, LANES), axis=-1, keepdims=True)
    rest = jnp.where(lane == i1, -jnp.inf, logits)
    v2 = jnp.max(rest, axis=-1, keepdims=True)
    i2 = jnp.min(jnp.where(rest == v2, lane, LANES), axis=-1, keepdims=True)
    e2 = jnp.exp(v2 - v1)
    ridx_ref[:, 0:1] = i1
    ridx_ref[:, 1:2] = i2
    rgate_ref[:, 0:1] = 1.0 / (1.0 + e2)
    rgate_ref[:, 1:2] = e2 / (1.0 + e2)


def _l1_mix(h, gmix, win, lng, lnb, ws, bs_t, wout, gffn, wr_hi, wr_lo):
    n = h.shape[0]
    tm = ROW_TILE
    return pl.pallas_call(
        _l1_mix_kernel,
        grid=(n // tm,),
        in_specs=[_rows(tm, D_MODEL)] + [_resident(a.shape) for a in
                                         (gmix, win, lng, lnb, ws, bs_t, wout, gffn, wr_hi, wr_lo)],
        out_specs=[_rows(tm, D_MODEL), _rows(tm, D_MODEL), _rows(tm, TOP_K), _rows(tm, TOP_K)],
        out_shape=[jax.ShapeDtypeStruct((n, D_MODEL), F32), jax.ShapeDtypeStruct((n, D_MODEL), F32),
                   jax.ShapeDtypeStruct((n, TOP_K), jnp.int32),
                   jax.ShapeDtypeStruct((n, TOP_K), F32)],
        scratch_shapes=[pltpu.VMEM((tm, GM_WIDTH), BF16)],
        compiler_params=pltpu.CompilerParams(dimension_semantics=("arbitrary",),
                                             vmem_limit_bytes=VMEM_LIMIT),
        name="l1_gmlp_router",
    )(h, gmix, win, lng, lnb, ws, bs_t, wout, gffn, wr_hi, wr_lo)


def _moe_kernel(bexp_ref, nvalid_ref, tok_ref, dst_ref, hn_hbm, w1_ref, w3_ref, w2_ref, y_hbm,
                xbuf, xb_sc, acc_sc, ybuf, gsem, ssem, fence_sem, fence_sink):
    i = pl.program_id(0)
    n_valid = nvalid_ref[0]
    tm = MOE_TILE
    prev = jnp.maximum(i - 1, 0)

    def gather_row(blk, r):
        return pltpu.make_async_copy(hn_hbm.at[pl.ds(tok_ref[blk * tm + r], 1)],
                                     xbuf.at[pl.ds(r, 1)], gsem)

    def scatter_row(blk, r):
        return pltpu.make_async_copy(ybuf.at[pl.ds(r, 1)],
                                     y_hbm.at[pl.ds(dst_ref[blk * tm + r], 1)], ssem)

    def gather_wait():
        pltpu.make_async_copy(hn_hbm.at[pl.ds(0, tm)], xbuf, gsem).wait()

    def scatter_wait():
        pltpu.make_async_copy(ybuf, y_hbm.at[pl.ds(0, tm)], ssem).wait()

    @pl.when(i < n_valid)
    def _():
        @pl.when(i == 0)
        def _():
            def issue(r, carry):
                gather_row(0, r).start()
                return carry
            lax.fori_loop(0, tm, issue, 0)

        gather_wait()
        xb_sc[...] = xbuf[...].astype(BF16)
        acc_sc[...] = jnp.zeros(acc_sc.shape, F32)
        has_next = i + 1 < n_valid
        has_prev = i >= 1

        def copies(c):
            if c not in MOE_FENCE_CHUNKS:
                return
            g = MOE_FENCE_CHUNKS.index(c)
            if g > 0:
                fence_sink[0] = pl.semaphore_read(fence_sem)
            if g == len(MOE_FENCE_CHUNKS) - 1:
                return
            for r in range(MOE_COPY_ROWS[g], MOE_COPY_ROWS[g + 1]):
                @pl.when(has_next)
                def _():
                    gather_row(i + 1, r).start()

                @pl.when(has_prev)
                def _():
                    scatter_row(prev, r).start()

        _swiglu_chunks(xb_sc, w1_ref, w3_ref, w2_ref, acc_sc, per_chunk=copies)

        @pl.when(has_prev)
        def _():
            scatter_wait()

        ybuf[...] = acc_sc[...]

    @pl.when(i == n_valid)
    def _():
        def issue(r, carry):
            scatter_row(prev, r).start()
            return carry
        lax.fori_loop(0, tm, issue, 0)
        scatter_wait()
        ybuf[...] = jnp.zeros(ybuf.shape, F32)
        plane = y_hbm.shape[0] // TOP_K
        for b in range(TOP_K):
            spare = pltpu.make_async_copy(ybuf, y_hbm.at[pl.ds((b + 1) * plane - tm, tm)], ssem)
            spare.start()
            spare.wait()


def _moe_experts(block_expert, n_valid, row_token, row_dst, hn2, w1, w3, w2, n_out_rows):
    n_blocks = block_expert.shape[0]
    tm = MOE_TILE
    per_expert = lambda w: pl.BlockSpec((None,) + w.shape[1:], lambda i, be, *_: (be[i], 0, 0))
    return pl.pallas_call(
        _moe_kernel,
        grid_spec=pltpu.PrefetchScalarGridSpec(
            num_scalar_prefetch=4,
            grid=(n_blocks,),
            in_specs=[pl.BlockSpec(memory_space=pl.ANY), per_expert(w1), per_expert(w3),
                      per_expert(w2)],
            out_specs=pl.BlockSpec(memory_space=pl.ANY),
            scratch_shapes=[pltpu.VMEM((tm, D_MODEL), F32), pltpu.VMEM((tm, D_MODEL), BF16),
                            pltpu.VMEM((tm, D_MODEL), F32), pltpu.VMEM((tm, D_MODEL), F32),
                            pltpu.SemaphoreType.DMA(()), pltpu.SemaphoreType.DMA(()),
                            pltpu.SemaphoreType.REGULAR(()), pltpu.SMEM((1,), jnp.int32)],
        ),
        out_shape=jax.ShapeDtypeStruct((n_out_rows, D_MODEL), F32),
        compiler_params=pltpu.CompilerParams(dimension_semantics=("arbitrary",),
                                             vmem_limit_bytes=VMEM_LIMIT),
        name="l1_expert_swiglu",
    )(block_expert, n_valid, row_token, row_dst, hn2, w1, w3, w2)


def _combine_kernel(h_ref, y0_ref, y1_ref, gate_ref, p_ref, gple_ref, wg_ref, wp_ref, gfin_ref,
                    o_ref):
    gate = gate_ref[...]
    h = h_ref[...] + (y0_ref[...] * gate[:, 0:1] + y1_ref[...] * gate[:, 1:2])
    h = _ple(h, p_ref, gple_ref, wg_ref, wp_ref)
    o_ref[...] = _rms(h, gfin_ref[...])


def _combine(h1, y_rows, rgate, p1, gple, wg, wp, gfin):
    n = h1.shape[0]
    tm = ROW_TILE
    plane_tiles = y_rows.shape[0] // TOP_K // tm
    return pl.pallas_call(
        _combine_kernel,
        grid=(n // tm,),
        in_specs=[_rows(tm, D_MODEL), _rows(tm, D_MODEL),
                  pl.BlockSpec((tm, D_MODEL), lambda i: (plane_tiles + i, 0)), _rows(tm, TOP_K),
                  _rows(tm, PLE_DIM), _resident(gple.shape), _resident(wg.shape),
                  _resident(wp.shape), _resident(gfin.shape)],
        out_specs=_rows(tm, D_MODEL),
        out_shape=jax.ShapeDtypeStruct((n, D_MODEL), F32),
        compiler_params=pltpu.CompilerParams(dimension_semantics=("arbitrary",),
                                             vmem_limit_bytes=VMEM_LIMIT),
        name="l1_combine_ple_norm",
    )(h1, y_rows, y_rows, rgate, p1, gple, wg, wp, gfin)


def _routing_tables(ridx, n_blocks):
    n_assign = ridx.size
    e_flat = ridx.reshape(-1)
    onehot = (e_flat[:, None] == jnp.arange(N_EXPERTS, dtype=jnp.int32)[None, :]).astype(jnp.int32)
    csum = jnp.cumsum(onehot, axis=0)
    counts = csum[-1]
    rank = jnp.sum(csum * onehot, axis=1) - 1
    padded = (counts + MOE_TILE - 1) // MOE_TILE * MOE_TILE
    pad_ends = jnp.cumsum(padded)
    pad_starts = pad_ends - padded
    dest = (jnp.sum(pad_starts[None, :] * onehot, axis=1) + rank).astype(jnp.int32)
    n_rows = (n_blocks + 2) * MOE_TILE
    assign = jnp.full((n_rows,), -1, jnp.int32).at[dest].set(jnp.arange(n_assign, dtype=jnp.int32))
    row_token = jnp.maximum(assign, 0) // TOP_K
    plane = n_assign // TOP_K + MOE_TILE
    r = jnp.arange(n_rows, dtype=jnp.int32)
    spare = (r // MOE_TILE % 2) * plane + n_assign // TOP_K + r % MOE_TILE
    row_dst = jnp.where(assign >= 0, (assign % TOP_K) * plane + assign // TOP_K, spare)
    block_start = jnp.arange(n_blocks, dtype=jnp.int32) * MOE_TILE
    block_expert = jnp.minimum(
        jnp.sum((block_start[:, None] >= pad_ends[None, :]).astype(jnp.int32), axis=1),
        N_EXPERTS - 1).astype(jnp.int32)
    n_valid = (pad_ends[-1:] // MOE_TILE).astype(jnp.int32)
    return row_token, row_dst, block_expert, n_valid


def kernel(x, p, positions, ln_mix, ln_ffn, ln_ple, a_w_in, a_lambda, a_subln, a_conv_w, a_w_out,
           ffn_w1, ffn_w3, ffn_w2, c_w_in, c_ln_g, c_ln_b, c_w_s, c_b_s, c_w_out, router_w,
           moe_w1, moe_w3, moe_w2, ple_gate, ple_proj, final_norm):
    bsz, seq, _ = x.shape
    n = bsz * seq
    assert seq % ROW_TILE == 0 and seq % ATTN_TILE == 0 and ROW_TILE % GM_CHUNK == 0
    x2 = x.reshape(n, D_MODEL)
    pos2 = positions.reshape(n, 1)
    row = lambda a: a.reshape(1, -1)

    inv_freq = ROPE_THETA ** (-jnp.arange(0, ROT_DIM, 2, dtype=F32) / ROT_DIM)
    lane = jnp.arange(LANES) % DA_HEAD_DIM
    freq = jnp.where(lane < ROT_DIM, inv_freq[lane % (ROT_DIM // 2)], 0.0).reshape(1, LANES)
    qt, k, vt, conv = _inproj(x2, pos2, row(ln_mix[0]), freq, a_conv_w[0].T, a_w_in[0].astype(BF16), seq)
    lambda_init = 0.8 - 0.6 * math.exp(-0.3 * 0)
    attn = _diff_attention(qt, k, vt, a_lambda[0], a_subln[0].reshape(-1, 1), lambda_init, bsz, seq)
    h = _l0_post(x2, attn, conv, p[0].reshape(n, PLE_DIM), a_w_out[0].astype(BF16), row(ln_ffn[0]),
                 ffn_w1[0].astype(BF16), ffn_w3[0].astype(BF16), ffn_w2[0].astype(BF16),
                 row(ln_ple[0]), ple_gate[0].astype(BF16), ple_proj[0].astype(BF16))

    wr = jnp.pad(router_w[0], ((0, 0), (0, LANES - N_EXPERTS)))
    wr_hi, wr_lo = _split_bf16(wr)
    h1, hn2, ridx, rgate = _l1_mix(h, row(ln_mix[1]), c_w_in[0].astype(BF16), row(c_ln_g[0]),
                                   row(c_ln_b[0]), c_w_s[0], c_b_s[0].T, c_w_out[0].astype(BF16),
                                   row(ln_ffn[1]), wr_hi, wr_lo)
    n_blocks = -(-(n * TOP_K + N_EXPERTS * (MOE_TILE - 1)) // MOE_TILE)
    row_token, row_dst, block_expert, n_valid = _routing_tables(ridx, n_blocks)
    y_rows = _moe_experts(block_expert, n_valid, row_token, row_dst, hn2, moe_w1[0].astype(BF16),
                          moe_w3[0].astype(BF16), moe_w2[0].astype(BF16), TOP_K * (n + MOE_TILE))
    out = _combine(h1, y_rows, rgate, p[1].reshape(n, PLE_DIM), row(ln_ple[1]),
                   ple_gate[1].astype(BF16), ple_proj[1].astype(BF16), row(final_norm))
    return out.reshape(bsz, seq, D_MODEL)
```

```python
import functools
import math

import jax
import jax.numpy as jnp
from jax import lax
from jax.experimental import pallas as pl
from jax.experimental.pallas import tpu as pltpu

F32 = jnp.float32
BF16 = jnp.bfloat16

D_MODEL = 1024
PLE_DIM = 256
DA_HEADS = 4
DA_HEAD_DIM = 64
DA_V_DIM = 2 * DA_HEAD_DIM
DA_WIDTH = DA_HEADS * DA_V_DIM
ROPE_THETA = 500000.0
ROT_DIM = DA_HEAD_DIM // 4
SC_WIDTH = 512
GM_WIDTH = D_MODEL
GM_GROUPS = 8
GM_CHUNK = 128
D_FF = 2816
N_EXPERTS = 8
TOP_K = 2
RMS_EPS = 1e-6
LN_EPS = 1e-5

LANES = 128
FF_CHUNK = 256
N_FF_CHUNKS = D_FF // FF_CHUNK
ROW_TILE = 512
ATTN_TILE = 256
ATTN_Q_TILE = 512
VT_ROWS = DA_V_DIM + 16
MOE_TILE = 512
MOE_FENCE_CHUNKS = (0, 3, 6, 8, 10)
MOE_COPY_ROWS = (0, 168, 336, 504, 512)
CONV_HALO = 8
VMEM_LIMIT = 56 * 2**20


def _resident(shape):
    nd = len(shape)
    return pl.BlockSpec(shape, lambda *_: (0,) * nd, pipeline_mode=pl.Buffered(1))


def _rows(tile, width):
    return pl.BlockSpec((tile, width), lambda i, *_: (i, 0))


def _rms(x, g):
    return x * lax.rsqrt(jnp.mean(x * x, axis=-1, keepdims=True) + RMS_EPS) * g


def _dot(a, b):
    return jnp.dot(a, b, preferred_element_type=F32)


def _swiglu_chunks(hn_ref, w1_ref, w3_ref, w2_ref, acc_ref, per_chunk=None):
    for c in range(N_FF_CHUNKS):
        cols = slice(c * FF_CHUNK, (c + 1) * FF_CHUNK)
        g = _dot(hn_ref[...], w1_ref[:, cols])
        u = _dot(hn_ref[...], w3_ref[:, cols])
        a = (g * jax.nn.sigmoid(g) * u).astype(BF16)
        acc_ref[...] += _dot(a, w2_ref[cols, :])
        if per_chunk is not None:
            per_chunk(c)


def _ple(h, p_ref, gple_ref, wg_ref, wp_ref):
    gate = jax.nn.sigmoid(_dot(_rms(h, gple_ref[...]).astype(BF16), wg_ref[...]))
    return h + gate * _dot(p_ref[...].astype(BF16), wp_ref[...])


def _inproj_kernel(x_ref, pos_ref, g_ref, freq_ref, cw_ref, w_ref,
                   qt_ref, k_ref, vt_ref, conv_ref, cbuf, *, tiles_per_seq):
    i = pl.program_id(0)
    tm = x_ref.shape[0]
    hn = _rms(x_ref[...], g_ref[...]).astype(BF16)

    ang = pos_ref[...].astype(F32) * freq_ref[...]
    lane = lax.broadcasted_iota(jnp.int32, (1, LANES), 1) % DA_HEAD_DIM
    half = ROT_DIM // 2
    cos = jnp.where(lane < ROT_DIM, jnp.cos(ang), 1.0)
    sin = jnp.sin(ang)
    sin_up = jnp.where(lane < half, -sin, 0.0)
    sin_dn = jnp.where((lane >= half) & (lane < ROT_DIM), sin, 0.0)

    def rope(t):
        return (t * cos + pltpu.roll(t, LANES - half, 1) * sin_up
                + pltpu.roll(t, half, 1) * sin_dn)

    zq = _dot(hn, w_ref[:, 0:DA_WIDTH])
    zk = _dot(hn, w_ref[:, DA_WIDTH:2 * DA_WIDTH])
    zv = _dot(hn, w_ref[:, 2 * DA_WIDTH:3 * DA_WIDTH])
    ones = jnp.ones((VT_ROWS - DA_V_DIM, ATTN_TILE), BF16)
    for hd in range(DA_HEADS):
        sl = slice(hd * LANES, (hd + 1) * LANES)
        qt_ref[sl, :] = (rope(zq[:, sl]) * (DA_HEAD_DIM ** -0.5)).T.astype(BF16)
        k_ref[:, sl] = rope(zk[:, sl]).astype(BF16)
        for u in range(tm // ATTN_TILE):
            vt_ref[hd, u, 0:DA_V_DIM, :] = zv[u * ATTN_TILE:(u + 1) * ATTN_TILE, sl].T.astype(BF16)
            vt_ref[hd, u, DA_V_DIM:VT_ROWS, :] = ones

    off = 3 * DA_WIDTH
    b_gate = _dot(hn, w_ref[:, off:off + SC_WIDTH])
    c_gate = _dot(hn, w_ref[:, off + SC_WIDTH:off + 2 * SC_WIDTH])
    hc = _dot(hn, w_ref[:, off + 2 * SC_WIDTH:off + 3 * SC_WIDTH])

    @pl.when(i % tiles_per_seq == 0)
    def _():
        cbuf[0:CONV_HALO, :] = jnp.zeros((CONV_HALO, SC_WIDTH), F32)

    @pl.when(i % tiles_per_seq != 0)
    def _():
        cbuf[0:CONV_HALO, :] = cbuf[tm:tm + CONV_HALO, :]

    ch = c_gate * hc
    cbuf[CONV_HALO:CONV_HALO + tm, :] = ch
    ch1 = cbuf[CONV_HALO - 1:CONV_HALO - 1 + tm, :]
    ch2 = cbuf[CONV_HALO - 2:CONV_HALO - 2 + tm, :]
    conv = b_gate * (cw_ref[2:3, :] * ch + cw_ref[1:2, :] * ch1 + cw_ref[0:1, :] * ch2)
    conv_ref[...] = conv.astype(BF16)


def _inproj(x2, pos2, g, freq, cw, w, seq):
    n = x2.shape[0]
    tm = ROW_TILE
    out = jax.ShapeDtypeStruct((n, DA_WIDTH), BF16)
    sub = tm // ATTN_TILE
    return pl.pallas_call(
        functools.partial(_inproj_kernel, tiles_per_seq=seq // tm),
        grid=(n // tm,),
        in_specs=[_rows(tm, D_MODEL), _rows(tm, 1), _resident(g.shape), _resident(freq.shape),
                  _resident(cw.shape), _resident(w.shape)],
        out_specs=[pl.BlockSpec((DA_WIDTH, tm), lambda i: (0, i)), _rows(tm, DA_WIDTH),
                   pl.BlockSpec((DA_HEADS, sub, VT_ROWS, ATTN_TILE), lambda i: (0, i, 0, 0)),
                   _rows(tm, SC_WIDTH)],
        out_shape=[jax.ShapeDtypeStruct((DA_WIDTH, n), BF16), out,
                   jax.ShapeDtypeStruct((DA_HEADS, n // ATTN_TILE, VT_ROWS, ATTN_TILE), BF16), out],
        scratch_shapes=[pltpu.VMEM((tm + CONV_HALO, SC_WIDTH), F32)],
        compiler_params=pltpu.CompilerParams(dimension_semantics=("arbitrary",),
                                             vmem_limit_bytes=VMEM_LIMIT),
        name="l0_inproj",
    )(x2, pos2, g, freq, cw, w)


def _attn_kernel(lam_ref, sg_ref, qt_ref, k_ref, vt_ref, o_ref, qc_sc, m_sc, acc_sc, *, lambda_init):
    qi = pl.program_id(2)
    tq = ATTN_Q_TILE
    tk = ATTN_TILE
    n_sub = tq // tk
    qt = qt_ref[...]
    dim = lax.broadcasted_iota(jnp.int32, (LANES, 1), 0)
    zero = jnp.zeros_like(qt)
    qc_sc[0] = jnp.where(dim < DA_HEAD_DIM, qt, zero)
    qc_sc[1] = jnp.where(dim >= DA_HEAD_DIM, qt, zero)
    m_sc[...] = jnp.full(m_sc.shape, -jnp.inf, F32)
    acc_sc[...] = jnp.zeros(acc_sc.shape, F32)

    def step(j, diagonal):
        first = j * n_sub
        for c in range(2):
            s = []
            for u in range(n_sub):
                ks = pl.multiple_of((first + u) * tk, tk)
                su = _dot(k_ref[pl.ds(ks, tk), :], qc_sc[c])
                if diagonal:
                    key = lax.broadcasted_iota(jnp.int32, (tk, tq), 0) + u * tk
                    qry = lax.broadcasted_iota(jnp.int32, (tk, tq), 1)
                    su = jnp.where(key <= qry, su, -jnp.inf)
                s.append(su)
            m_prev = m_sc[c]
            m_new = m_prev
            for su in s:
                m_new = jnp.maximum(m_new, jnp.max(su, axis=0, keepdims=True))
            pv = None
            for u, su in enumerate(s):
                d = _dot(vt_ref[first + u], jnp.exp(su - m_new).astype(BF16))
                pv = d if pv is None else pv + d
            acc_sc[c] = jnp.exp(m_prev - m_new) * acc_sc[c] + pv
            m_sc[c] = m_new

    def full_step(j, carry):
        step(j, False)
        return carry

    lax.fori_loop(0, qi, full_step, 0)
    step(qi, True)

    lp = lam_ref[...]
    lam = (jnp.exp(jnp.sum(lp[0:1] * lp[1:2], axis=-1, keepdims=True))
           - jnp.exp(jnp.sum(lp[2:3] * lp[3:4], axis=-1, keepdims=True)) + lambda_init)
    a0 = acc_sc[0]
    a1 = acc_sc[1]
    o = (a0[0:DA_V_DIM] / a0[DA_V_DIM:DA_V_DIM + 1]
         - lam * (a1[0:DA_V_DIM] / a1[DA_V_DIM:DA_V_DIM + 1]))
    o = o * lax.rsqrt(jnp.mean(o * o, axis=0, keepdims=True) + RMS_EPS) * sg_ref[...]
    o_ref[...] = (o * (1.0 - lambda_init)).T.astype(o_ref.dtype)


def _diff_attention(qt, k, vt, lam_p, subln_col, lambda_init, bsz, seq):
    tq = ATTN_Q_TILE
    tk = ATTN_TILE
    nq = seq // tq
    nk = seq // tk
    return pl.pallas_call(
        functools.partial(_attn_kernel, lambda_init=lambda_init),
        grid=(bsz, DA_HEADS, nq),
        in_specs=[pl.BlockSpec(lam_p.shape, lambda b, h, i: (0, 0)),
                  pl.BlockSpec(subln_col.shape, lambda b, h, i: (0, 0)),
                  pl.BlockSpec((LANES, tq), lambda b, h, i: (h, b * nq + i)),
                  pl.BlockSpec((seq, LANES), lambda b, h, i: (b, h)),
                  pl.BlockSpec((None, nk, VT_ROWS, tk), lambda b, h, i: (h, b, 0, 0))],
        out_specs=pl.BlockSpec((tq, LANES), lambda b, h, i: (b * nq + i, h)),
        out_shape=jax.ShapeDtypeStruct(k.shape, BF16),
        scratch_shapes=[pltpu.VMEM((2, LANES, tq), BF16), pltpu.VMEM((2, 1, tq), F32),
                        pltpu.VMEM((2, VT_ROWS, tq), F32)],
        compiler_params=pltpu.CompilerParams(
            dimension_semantics=("arbitrary", "arbitrary", "arbitrary"),
            vmem_limit_bytes=VMEM_LIMIT),
        name="l0_diff_attention",
    )(lam_p, subln_col, qt, k, vt)


def _l0_post_kernel(x_ref, attn_ref, conv_ref, p_ref, wo_ref, gffn_ref, w1_ref, w3_ref, w2_ref,
                    gple_ref, wg_ref, wp_ref, o_ref, acc_sc, hn_sc, h_sc):
    h = (x_ref[...] + _dot(attn_ref[...], wo_ref[0:DA_WIDTH, :])
         + _dot(conv_ref[...], wo_ref[DA_WIDTH:DA_WIDTH + SC_WIDTH, :]))
    hn_sc[...] = _rms(h, gffn_ref[...]).astype(BF16)
    h_sc[...] = h
    acc_sc[...] = jnp.zeros(acc_sc.shape, F32)
    _swiglu_chunks(hn_sc, w1_ref, w3_ref, w2_ref, acc_sc)
    o_ref[...] = _ple(h_sc[...] + acc_sc[...], p_ref, gple_ref, wg_ref, wp_ref)


def _l0_post(x2, attn, conv, p0, wo, gffn, w1, w3, w2, gple, wg, wp):
    n = x2.shape[0]
    tm = ROW_TILE
    return pl.pallas_call(
        _l0_post_kernel,
        grid=(n // tm,),
        in_specs=[_rows(tm, D_MODEL), _rows(tm, DA_WIDTH), _rows(tm, SC_WIDTH), _rows(tm, PLE_DIM),
                  _resident(wo.shape), _resident(gffn.shape), _resident(w1.shape),
                  _resident(w3.shape), _resident(w2.shape), _resident(gple.shape),
                  _resident(wg.shape), _resident(wp.shape)],
        out_specs=_rows(tm, D_MODEL),
        out_shape=jax.ShapeDtypeStruct((n, D_MODEL), F32),
        scratch_shapes=[pltpu.VMEM((tm, D_MODEL), F32), pltpu.VMEM((tm, D_MODEL), BF16),
                        pltpu.VMEM((tm, D_MODEL), F32)],
        compiler_params=pltpu.CompilerParams(dimension_semantics=("arbitrary",),
                                             vmem_limit_bytes=VMEM_LIMIT),
        name="l0_outproj_swiglu_ple",
    )(x2, attn, conv, p0, wo, gffn, w1, w3, w2, gple, wg, wp)


def _split_bf16(a):
    hi = a.astype(BF16)
    return hi, (a - hi.astype(F32)).astype(BF16)


def _l1_mix_kernel(h_ref, gmix_ref, win_ref, lng_ref, lnb_ref, ws_ref, bs_ref, wout_ref,
                   gffn_ref, wr_hi_ref, wr_lo_ref,
                   h1_ref, hn2_ref, ridx_ref, rgate_ref, gated_sc):
    tm = h_ref.shape[0]
    h = h_ref[...]
    hn = _rms(h, gmix_ref[...]).astype(BF16)

    def gelu(z):
        return 0.5 * z * (1.0 + lax.erf(z * (2.0 ** -0.5)))

    u = gelu(_dot(hn, win_ref[:, 0:GM_WIDTH]))
    vv = gelu(_dot(hn, win_ref[:, GM_WIDTH:2 * GM_WIDTH]))
    mu = jnp.mean(vv, axis=-1, keepdims=True)
    vc = vv - mu
    var = jnp.mean(vc * vc, axis=-1, keepdims=True)
    vv = (vc * lax.rsqrt(var + LN_EPS) * lng_ref[...] + lnb_ref[...]).astype(BF16)

    n_chunks = tm // GM_CHUNK
    row = lax.broadcasted_iota(jnp.int32, (GM_CHUNK, GM_CHUNK), 0)
    col = lax.broadcasted_iota(jnp.int32, (GM_CHUNK, GM_CHUNK), 1)
    for g in range(GM_GROUPS):
        gs = slice(g * LANES, (g + 1) * LANES)
        wm = jnp.where(col <= row, ws_ref[g], 0.0).astype(BF16)
        rhs = jnp.concatenate(
            [vv[r * GM_CHUNK:(r + 1) * GM_CHUNK, gs] for r in range(n_chunks)], axis=1)
        vs = _dot(wm, rhs) + bs_ref[:, g:g + 1]
        for r in range(n_chunks):
            rs = slice(r * GM_CHUNK, (r + 1) * GM_CHUNK)
            gated_sc[rs, gs] = (u[rs, gs] * vs[:, r * LANES:(r + 1) * LANES]).astype(BF16)

    h1 = h + _dot(gated_sc[...], wout_ref[...])
    h1_ref[...] = h1

    hn2 = _rms(h1, gffn_ref[...])
    hn2_ref[...] = hn2
    x_hi, x_lo = _split_bf16(hn2)
    logits = (_dot(x_hi, wr_hi_ref[...]) + _dot(x_hi, wr_lo_ref[...])
              + _dot(x_lo, wr_hi_ref[...]))
    lane = lax.broadcasted_iota(jnp.int32, logits.shape, 1)
    logits = jnp.where(lane < N_EXPERTS, logits, -jnp.inf)
    v1 = jnp.max(logits, axis=-1, keepdims=True)
    i1 = jnp.min(jnp.where(logits == v1, lane, LANES), axis=-1, keepdims=True)
    rest = jnp.where(lane == i1, -jnp.inf, logits)
    v2 = jnp.max(rest, axis=-1, keepdims=True)
    i2 = jnp.min(jnp.where(rest == v2, lane, LANES), axis=-1, keepdims=True)
    e2 = jnp.exp(v2 - v1)
    ridx_ref[:, 0:1] = i1
    ridx_ref[:, 1:2] = i2
    rgate_ref[:, 0:1] = 1.0 / (1.0 + e2)
    rgate_ref[:, 1:2] = e2 / (1.0 + e2)


def _l1_mix(h, gmix, win, lng, lnb, ws, bs_t, wout, gffn, wr_hi, wr_lo):
    n = h.shape[0]
    tm = ROW_TILE
    return pl.pallas_call(
        _l1_mix_kernel,
        grid=(n // tm,),
        in_specs=[_rows(tm, D_MODEL)] + [_resident(a.shape) for a in
                                         (gmix, win, lng, lnb, ws, bs_t, wout, gffn, wr_hi, wr_lo)],
        out_specs=[_rows(tm, D_MODEL), _rows(tm, D_MODEL), _rows(tm, TOP_K), _rows(tm, TOP_K)],
        out_shape=[jax.ShapeDtypeStruct((n, D_MODEL), F32), jax.ShapeDtypeStruct((n, D_MODEL), F32),
                   jax.ShapeDtypeStruct((n, TOP_K), jnp.int32),
                   jax.ShapeDtypeStruct((n, TOP_K), F32)],
        scratch_shapes=[pltpu.VMEM((tm, GM_WIDTH), BF16)],
        compiler_params=pltpu.CompilerParams(dimension_semantics=("arbitrary",),
                                             vmem_limit_bytes=VMEM_LIMIT),
        name="l1_gmlp_router",
    )(h, gmix, win, lng, lnb, ws, bs_t, wout, gffn, wr_hi, wr_lo)


def _moe_kernel(bexp_ref, nvalid_ref, tok_ref, dst_ref, hn_hbm, w1_ref, w3_ref, w2_ref, y_hbm,
                xbuf, xb_sc, acc_sc, ybuf, gsem, ssem, fence_sem, fence_sink):
    i = pl.program_id(0)
    n_valid = nvalid_ref[0]
    tm = MOE_TILE
    prev = jnp.maximum(i - 1, 0)

    def gather_row(blk, r):
        return pltpu.make_async_copy(hn_hbm.at[pl.ds(tok_ref[blk * tm + r], 1)],
                                     xbuf.at[pl.ds(r, 1)], gsem)

    def scatter_row(blk, r):
        return pltpu.make_async_copy(ybuf.at[pl.ds(r, 1)],
                                     y_hbm.at[pl.ds(dst_ref[blk * tm + r], 1)], ssem)

    def gather_wait():
        pltpu.make_async_copy(hn_hbm.at[pl.ds(0, tm)], xbuf, gsem).wait()

    def scatter_wait():
        pltpu.make_async_copy(ybuf, y_hbm.at[pl.ds(0, tm)], ssem).wait()

    @pl.when(i < n_valid)
    def _():
        @pl.when(i == 0)
        def _():
            def issue(r, carry):
                gather_row(0, r).start()
                return carry
            lax.fori_loop(0, tm, issue, 0)

        gather_wait()
        xb_sc[...] = xbuf[...].astype(BF16)
        acc_sc[...] = jnp.zeros(acc_sc.shape, F32)
        has_next = i + 1 < n_valid
        has_prev = i >= 1

        def copies(c):
            if c not in MOE_FENCE_CHUNKS:
                return
            g = MOE_FENCE_CHUNKS.index(c)
            if g > 0:
                fence_sink[0] = pl.semaphore_read(fence_sem)
            if g == len(MOE_FENCE_CHUNKS) - 1:
                return
            for r in range(MOE_COPY_ROWS[g], MOE_COPY_ROWS[g + 1]):
                @pl.when(has_next)
                def _():
                    gather_row(i + 1, r).start()

                @pl.when(has_prev)
                def _():
                    scatter_row(prev, r).start()

        _swiglu_chunks(xb_sc, w1_ref, w3_ref, w2_ref, acc_sc, per_chunk=copies)

        @pl.when(has_prev)
        def _():
            scatter_wait()

        ybuf[...] = acc_sc[...]

    @pl.when(i == n_valid)
    def _():
        def issue(r, carry):
            scatter_row(prev, r).start()
            return carry
        lax.fori_loop(0, tm, issue, 0)
        scatter_wait()
        ybuf[...] = jnp.zeros(ybuf.shape, F32)
        plane = y_hbm.shape[0] // TOP_K
        for b in range(TOP_K):
            spare = pltpu.make_async_copy(ybuf, y_hbm.at[pl.ds((b + 1) * plane - tm, tm)], ssem)
            spare.start()
            spare.wait()


def _moe_experts(block_expert, n_valid, row_token, row_dst, hn2, w1, w3, w2, n_out_rows):
    n_blocks = block_expert.shape[0]
    tm = MOE_TILE
    per_expert = lambda w: pl.BlockSpec((None,) + w.shape[1:], lambda i, be, *_: (be[i], 0, 0))
    return pl.pallas_call(
        _moe_kernel,
        grid_spec=pltpu.PrefetchScalarGridSpec(
            num_scalar_prefetch=4,
            grid=(n_blocks,),
            in_specs=[pl.BlockSpec(memory_space=pl.ANY), per_expert(w1), per_expert(w3),
                      per_expert(w2)],
            out_specs=pl.BlockSpec(memory_space=pl.ANY),
            scratch_shapes=[pltpu.VMEM((tm, D_MODEL), F32), pltpu.VMEM((tm, D_MODEL), BF16),
                            pltpu.VMEM((tm, D_MODEL), F32), pltpu.VMEM((tm, D_MODEL), F32),
                            pltpu.SemaphoreType.DMA(()), pltpu.SemaphoreType.DMA(()),
                            pltpu.SemaphoreType.REGULAR(()), pltpu.SMEM((1,), jnp.int32)],
        ),
        out_shape=jax.ShapeDtypeStruct((n_out_rows, D_MODEL), F32),
        compiler_params=pltpu.CompilerParams(dimension_semantics=("arbitrary",),
                                             vmem_limit_bytes=VMEM_LIMIT),
        name="l1_expert_swiglu",
    )(block_expert, n_valid, row_token, row_dst, hn2, w1, w3, w2)


def _combine_kernel(h_ref, y0_ref, y1_ref, gate_ref, p_ref, gple_ref, wg_ref, wp_ref, gfin_ref,
                    o_ref):
    gate = gate_ref[...]
    h = h_ref[...] + (y0_ref[...] * gate[:, 0:1] + y1_ref[...] * gate[:, 1:2])
    h = _ple(h, p_ref, gple_ref, wg_ref, wp_ref)
    o_ref[...] = _rms(h, gfin_ref[...])


def _combine(h1, y_rows, rgate, p1, gple, wg, wp, gfin):
    n = h1.shape[0]
    tm = ROW_TILE
    plane_tiles = y_rows.shape[0] // TOP_K // tm
    return pl.pallas_call(
        _combine_kernel,
        grid=(n // tm,),
        in_specs=[_rows(tm, D_MODEL), _rows(tm, D_MODEL),
                  pl.BlockSpec((tm, D_MODEL), lambda i: (plane_tiles + i, 0)), _rows(tm, TOP_K),
                  _rows(tm, PLE_DIM), _resident(gple.shape), _resident(wg.shape),
                  _resident(wp.shape), _resident(gfin.shape)],
        out_specs=_rows(tm, D_MODEL),
        out_shape=jax.ShapeDtypeStruct((n, D_MODEL), F32),
        compiler_params=pltpu.CompilerParams(dimension_semantics=("arbitrary",),
                                             vmem_limit_bytes=VMEM_LIMIT),
        name="l1_combine_ple_norm",
    )(h1, y_rows, y_rows, rgate, p1, gple, wg, wp, gfin)


def _routing_tables(ridx, n_blocks):
    n_assign = ridx.size
    e_flat = ridx.reshape(-1)
    onehot = (e_flat[:, None] == jnp.arange(N_EXPERTS, dtype=jnp.int32)[None, :]).astype(jnp.int32)
    csum = jnp.cumsum(onehot, axis=0)
    counts = csum[-1]
    rank = jnp.sum(csum * onehot, axis=1) - 1
    padded = (counts + MOE_TILE - 1) // MOE_TILE * MOE_TILE
    pad_ends = jnp.cumsum(padded)
    pad_starts = pad_ends - padded
    dest = (jnp.sum(pad_starts[None, :] * onehot, axis=1) + rank).astype(jnp.int32)
    n_rows = (n_blocks + 2) * MOE_TILE
    assign = jnp.full((n_rows,), -1, jnp.int32).at[dest].set(jnp.arange(n_assign, dtype=jnp.int32))
    row_token = jnp.maximum(assign, 0) // TOP_K
    plane = n_assign // TOP_K + MOE_TILE
    r = jnp.arange(n_rows, dtype=jnp.int32)
    spare = (r // MOE_TILE % 2) * plane + n_assign // TOP_K + r % MOE_TILE
    row_dst = jnp.where(assign >= 0, (assign % TOP_K) * plane + assign // TOP_K, spare)
    block_start = jnp.arange(n_blocks, dtype=jnp.int32) * MOE_TILE
    block_expert = jnp.minimum(
        jnp.sum((block_start[:, None] >= pad_ends[None, :]).astype(jnp.int32), axis=1),
        N_EXPERTS - 1).astype(jnp.int32)
    n_valid = (pad_ends[-1:] // MOE_TILE).astype(jnp.int32)
    return row_token, row_dst, block_expert, n_valid


def kernel(x, p, positions, ln_mix, ln_ffn, ln_ple, a_w_in, a_lambda, a_subln, a_conv_w, a_w_out,
           ffn_w1, ffn_w3, ffn_w2, c_w_in, c_ln_g, c_ln_b, c_w_s, c_b_s, c_w_out, router_w,
           moe_w1, moe_w3, moe_w2, ple_gate, ple_proj, final_norm):
    bsz, seq, _ = x.shape
    n = bsz * seq
    assert seq % ROW_TILE == 0 and seq % ATTN_TILE == 0 and ROW_TILE % GM_CHUNK == 0
    x2 = x.reshape(n, D_MODEL)
    pos2 = positions.reshape(n, 1)
    row = lambda a: a.reshape(1, -1)

    inv_freq = ROPE_THETA ** (-jnp.arange(0, ROT_DIM, 2, dtype=F32) / ROT_DIM)
    lane = jnp.arange(LANES) % DA_HEAD_DIM
    freq = jnp.where(lane < ROT_DIM, inv_freq[lane % (ROT_DIM // 2)], 0.0).reshape(1, LANES)
    qt, k, vt, conv = _inproj(x2, pos2, row(ln_mix[0]), freq, a_conv_w[0].T, a_w_in[0].astype(BF16), seq)
    lambda_init = 0.8 - 0.6 * math.exp(-0.3 * 0)
    attn = _diff_attention(qt, k, vt, a_lambda[0], a_subln[0].reshape(-1, 1), lambda_init, bsz, seq)
    h = _l0_post(x2, attn, conv, p[0].reshape(n, PLE_DIM), a_w_out[0].astype(BF16), row(ln_ffn[0]),
                 ffn_w1[0].astype(BF16), ffn_w3[0].astype(BF16), ffn_w2[0].astype(BF16),
                 row(ln_ple[0]), ple_gate[0].astype(BF16), ple_proj[0].astype(BF16))

    wr = jnp.pad(router_w[0], ((0, 0), (0, LANES - N_EXPERTS)))
    wr_hi, wr_lo = _split_bf16(wr)
    h1, hn2, ridx, rgate = _l1_mix(h, row(ln_mix[1]), c_w_in[0].astype(BF16), row(c_ln_g[0]),
                                   row(c_ln_b[0]), c_w_s[0], c_b_s[0].T, c_w_out[0].astype(BF16),
                                   row(ln_ffn[1]), wr_hi, wr_lo)
    n_blocks = -(-(n * TOP_K + N_EXPERTS * (MOE_TILE - 1)) // MOE_TILE)
    row_token, row_dst, block_expert, n_valid = _routing_tables(ridx, n_blocks)
    y_rows = _moe_experts(block_expert, n_valid, row_token, row_dst, hn2, moe_w1[0].astype(BF16),
                          moe_w3[0].astype(BF16), moe_w2[0].astype(BF16), TOP_K * (n + MOE_TILE))
    out = _combine(h1, y_rows, rgate, p[1].reshape(n, PLE_DIM), row(ln_ple[1]),
                   ple_gate[1].astype(BF16), ple_proj[1].astype(BF16), row(final_norm))
    return out.reshape(bsz, seq, D_MODEL)
```

```python
import functools
import math

import jax
import jax.numpy as jnp
from jax import lax
from jax.experimental import pallas as pl
from jax.experimental.pallas import tpu as pltpu

F32 = jnp.float32
BF16 = jnp.bfloat16

D_MODEL = 1024
PLE_DIM = 256
DA_HEADS = 4
DA_HEAD_DIM = 64
DA_V_DIM = 2 * DA_HEAD_DIM
DA_WIDTH = DA_HEADS * DA_V_DIM
ROPE_THETA = 500000.0
ROT_DIM = DA_HEAD_DIM // 4
SC_WIDTH = 512
GM_WIDTH = D_MODEL
GM_GROUPS = 8
GM_CHUNK = 128
D_FF = 2816
N_EXPERTS = 8
TOP_K = 2
RMS_EPS = 1e-6
LN_EPS = 1e-5

LANES = 128
FF_CHUNK = 256
N_FF_CHUNKS = D_FF // FF_CHUNK
ROW_TILE = 512
ATTN_TILE = 256
ATTN_Q_TILE = 512
VT_ROWS = DA_V_DIM + 16
MOE_TILE = 512
MOE_FENCE_CHUNKS = (0, 3, 6, 8, 10)
MOE_COPY_ROWS = (0, 168, 336, 504, 512)
CONV_HALO = 8
VMEM_LIMIT = 56 * 2**20


def _resident(shape):
    nd = len(shape)
    return pl.BlockSpec(shape, lambda *_: (0,) * nd, pipeline_mode=pl.Buffered(1))


def _rows(tile, width):
    return pl.BlockSpec((tile, width), lambda i, *_: (i, 0))


def _rms(x, g):
    return x * lax.rsqrt(jnp.mean(x * x, axis=-1, keepdims=True) + RMS_EPS) * g


def _dot(a, b):
    return jnp.dot(a, b, preferred_element_type=F32)


def _swiglu_chunks(hn_ref, w1_ref, w3_ref, w2_ref, acc_ref, per_chunk=None):
    for c in range(N_FF_CHUNKS):
        cols = slice(c * FF_CHUNK, (c + 1) * FF_CHUNK)
        g = _dot(hn_ref[...], w1_ref[:, cols])
        u = _dot(hn_ref[...], w3_ref[:, cols])
        a = (g * jax.nn.sigmoid(g) * u).astype(BF16)
        acc_ref[...] += _dot(a, w2_ref[cols, :])
        if per_chunk is not None:
            per_chunk(c)


def _ple(h, p_ref, gple_ref, wg_ref, wp_ref):
    gate = jax.nn.sigmoid(_dot(_rms(h, gple_ref[...]).astype(BF16), wg_ref[...]))
    return h + gate * _dot(p_ref[...].astype(BF16), wp_ref[...])


def _inproj_kernel(x_ref, pos_ref, g_ref, freq_ref, cw_ref, w_ref,
                   qt_ref, k_ref, vt_ref, conv_ref, cbuf, *, tiles_per_seq):
    i = pl.program_id(0)
    tm = x_ref.shape[0]
    hn = _rms(x_ref[...], g_ref[...]).astype(BF16)

    ang = pos_ref[...].astype(F32) * freq_ref[...]
    lane = lax.broadcasted_iota(jnp.int32, (1, LANES), 1) % DA_HEAD_DIM
    half = ROT_DIM // 2
    cos = jnp.where(lane < ROT_DIM, jnp.cos(ang), 1.0)
    sin = jnp.sin(ang)
    sin_up = jnp.where(lane < half, -sin, 0.0)
    sin_dn = jnp.where((lane >= half) & (lane < ROT_DIM), sin, 0.0)

    def rope(t):
        return (t * cos + pltpu.roll(t, LANES - half, 1) * sin_up
                + pltpu.roll(t, half, 1) * sin_dn)

    zq = _dot(hn, w_ref[:, 0:DA_WIDTH])
    zk = _dot(hn, w_ref[:, DA_WIDTH:2 * DA_WIDTH])
    zv = _dot(hn, w_ref[:, 2 * DA_WIDTH:3 * DA_WIDTH])
    ones = jnp.ones((VT_ROWS - DA_V_DIM, ATTN_TILE), BF16)
    for hd in range(DA_HEADS):
        sl = slice(hd * LANES, (hd + 1) * LANES)
        qt_ref[sl, :] = (rope(zq[:, sl]) * (DA_HEAD_DIM ** -0.5)).T.astype(BF16)
        k_ref[:, sl] = rope(zk[:, sl]).astype(BF16)
        for u in range(tm // ATTN_TILE):
            vt_ref[hd, u, 0:DA_V_DIM, :] = zv[u * ATTN_TILE:(u + 1) * ATTN_TILE, sl].T.astype(BF16)
            vt_ref[hd, u, DA_V_DIM:VT_ROWS, :] = ones

    off = 3 * DA_WIDTH
    b_gate = _dot(hn, w_ref[:, off:off + SC_WIDTH])
    c_gate = _dot(hn, w_ref[:, off + SC_WIDTH:off + 2 * SC_WIDTH])
    hc = _dot(hn, w_ref[:, off + 2 * SC_WIDTH:off + 3 * SC_WIDTH])

    @pl.when(i % tiles_per_seq == 0)
    def _():
        cbuf[0:CONV_HALO, :] = jnp.zeros((CONV_HALO, SC_WIDTH), F32)

    @pl.when(i % tiles_per_seq != 0)
    def _():
        cbuf[0:CONV_HALO, :] = cbuf[tm:tm + CONV_HALO, :]

    ch = c_gate * hc
    cbuf[CONV_HALO:CONV_HALO + tm, :] = ch
    ch1 = cbuf[CONV_HALO - 1:CONV_HALO - 1 + tm, :]
    ch2 = cbuf[CONV_HALO - 2:CONV_HALO - 2 + tm, :]
    conv = b_gate * (cw_ref[2:3, :] * ch + cw_ref[1:2, :] * ch1 + cw_ref[0:1, :] * ch2)
    conv_ref[...] = conv.astype(BF16)


def _inproj(x2, pos2, g, freq, cw, w, seq):
    n = x2.shape[0]
    tm = ROW_TILE
    out = jax.ShapeDtypeStruct((n, DA_WIDTH), BF16)
    sub = tm // ATTN_TILE
    return pl.pallas_call(
        functools.partial(_inproj_kernel, tiles_per_seq=seq // tm),
        grid=(n // tm,),
        in_specs=[_rows(tm, D_MODEL), _rows(tm, 1), _resident(g.shape), _resident(freq.shape),
                  _resident(cw.shape), _resident(w.shape)],
        out_specs=[pl.BlockSpec((DA_WIDTH, tm), lambda i: (0, i)), _rows(tm, DA_WIDTH),
                   pl.BlockSpec((DA_HEADS, sub, VT_ROWS, ATTN_TILE), lambda i: (0, i, 0, 0)),
                   _rows(tm, SC_WIDTH)],
        out_shape=[jax.ShapeDtypeStruct((DA_WIDTH, n), BF16), out,
                   jax.ShapeDtypeStruct((DA_HEADS, n // ATTN_TILE, VT_ROWS, ATTN_TILE), BF16), out],
        scratch_shapes=[pltpu.VMEM((tm + CONV_HALO, SC_WIDTH), F32)],
        compiler_params=pltpu.CompilerParams(dimension_semantics=("arbitrary",),
                                             vmem_limit_bytes=VMEM_LIMIT),
        name="l0_inproj",
    )(x2, pos2, g, freq, cw, w)


def _attn_kernel(lam_ref, sg_ref, qt_ref, k_ref, vt_ref, o_ref,
                 qc_sc, m_sc, alpha_sc, acc_sc, s_sc, p_sc, cap_sc, *, lambda_init):
    first_step = (pl.program_id(0) == 0) & (pl.program_id(1) == 0) & (pl.program_id(2) == 0)
    qi = pl.program_id(2)
    tq = ATTN_Q_TILE
    tk = ATTN_TILE
    n_sub = tq // tk

    @pl.when(first_step)
    def _():
        key = lax.broadcasted_iota(jnp.int32, (tq, tq), 0)
        qry = lax.broadcasted_iota(jnp.int32, (tq, tq), 1)
        cap_sc[0] = jnp.full((tq, tq), jnp.inf, F32)
        cap_sc[1] = jnp.where(key <= qry, jnp.inf, -jnp.inf).astype(F32)

    qt = qt_ref[...]
    dim = lax.broadcasted_iota(jnp.int32, (LANES, 1), 0)
    zero = jnp.zeros_like(qt)
    qc_sc[0] = jnp.where(dim < DA_HEAD_DIM, qt, zero)
    qc_sc[1] = jnp.where(dim >= DA_HEAD_DIM, qt, zero)
    m_sc[...] = jnp.full(m_sc.shape, -jnp.inf, F32)
    acc_sc[...] = jnp.zeros(acc_sc.shape, F32)
    alpha_sc[...] = jnp.ones(alpha_sc.shape, F32)
    p_sc[...] = jnp.zeros(p_sc.shape, BF16)

    def scores(j):
        diag = (j == qi).astype(jnp.int32)
        for c in range(2):
            for u in range(n_sub):
                rows = slice(u * tk, (u + 1) * tk)
                ks = pl.multiple_of((j * n_sub + u) * tk, tk)
                su = _dot(k_ref[pl.ds(ks, tk), :], qc_sc[c])
                s_sc[c, rows, :] = jnp.minimum(su, cap_sc[diag, rows, :])

    def softmax():
        for c in range(2):
            s = s_sc[c]
            m_prev = m_sc[c]
            m_new = jnp.maximum(m_prev, jnp.max(s, axis=0, keepdims=True))
            alpha_sc[c] = jnp.exp(m_prev - m_new)
            m_sc[c] = m_new
            p_sc[c] = jnp.exp(s - m_new).astype(BF16)

    def fold(j):
        for c in range(2):
            pv = None
            for u in range(n_sub):
                d = _dot(vt_ref[j * n_sub + u], p_sc[c, u * tk:(u + 1) * tk, :])
                pv = d if pv is None else pv + d
            acc_sc[c] = alpha_sc[c] * acc_sc[c] + pv

    def iteration(t, carry):
        fold(jnp.maximum(t - 1, 0))
        softmax()
        scores(jnp.minimum(t + 1, qi))
        return carry

    scores(0)
    lax.fori_loop(0, qi + 1, iteration, 0)
    fold(qi)

    lp = lam_ref[...]
    lam = (jnp.exp(jnp.sum(lp[0:1] * lp[1:2], axis=-1, keepdims=True))
           - jnp.exp(jnp.sum(lp[2:3] * lp[3:4], axis=-1, keepdims=True)) + lambda_init)
    a0 = acc_sc[0]
    a1 = acc_sc[1]
    o = (a0[0:DA_V_DIM] / a0[DA_V_DIM:DA_V_DIM + 1]
         - lam * (a1[0:DA_V_DIM] / a1[DA_V_DIM:DA_V_DIM + 1]))
    o = o * lax.rsqrt(jnp.mean(o * o, axis=0, keepdims=True) + RMS_EPS) * sg_ref[...]
    o_ref[...] = (o * (1.0 - lambda_init)).T.astype(o_ref.dtype)


def _diff_attention(qt, k, vt, lam_p, subln_col, lambda_init, bsz, seq):
    tq = ATTN_Q_TILE
    tk = ATTN_TILE
    nq = seq // tq
    nk = seq // tk
    return pl.pallas_call(
        functools.partial(_attn_kernel, lambda_init=lambda_init),
        grid=(bsz, DA_HEADS, nq),
        in_specs=[pl.BlockSpec(lam_p.shape, lambda b, h, i: (0, 0)),
                  pl.BlockSpec(subln_col.shape, lambda b, h, i: (0, 0)),
                  pl.BlockSpec((LANES, tq), lambda b, h, i: (h, b * nq + i)),
                  pl.BlockSpec((seq, LANES), lambda b, h, i: (b, h)),
                  pl.BlockSpec((None, nk, VT_ROWS, tk), lambda b, h, i: (h, b, 0, 0))],
        out_specs=pl.BlockSpec((tq, LANES), lambda b, h, i: (b * nq + i, h)),
        out_shape=jax.ShapeDtypeStruct(k.shape, BF16),
        scratch_shapes=[pltpu.VMEM((2, LANES, tq), BF16), pltpu.VMEM((2, 1, tq), F32),
                        pltpu.VMEM((2, 1, tq), F32), pltpu.VMEM((2, VT_ROWS, tq), F32),
                        pltpu.VMEM((2, tq, tq), F32), pltpu.VMEM((2, tq, tq), BF16),
                        pltpu.VMEM((2, tq, tq), F32)],
        compiler_params=pltpu.CompilerParams(
            dimension_semantics=("arbitrary", "arbitrary", "arbitrary"),
            vmem_limit_bytes=VMEM_LIMIT),
        name="l0_diff_attention",
    )(lam_p, subln_col, qt, k, vt)


def _l0_post_kernel(x_ref, attn_ref, conv_ref, p_ref, wo_ref, gffn_ref, w1_ref, w3_ref, w2_ref,
                    gple_ref, wg_ref, wp_ref, o_ref, acc_sc, hn_sc, h_sc):
    h = (x_ref[...] + _dot(attn_ref[...], wo_ref[0:DA_WIDTH, :])
         + _dot(conv_ref[...], wo_ref[DA_WIDTH:DA_WIDTH + SC_WIDTH, :]))
    hn_sc[...] = _rms(h, gffn_ref[...]).astype(BF16)
    h_sc[...] = h
    acc_sc[...] = jnp.zeros(acc_sc.shape, F32)
    _swiglu_chunks(hn_sc, w1_ref, w3_ref, w2_ref, acc_sc)
    o_ref[...] = _ple(h_sc[...] + acc_sc[...], p_ref, gple_ref, wg_ref, wp_ref)


def _l0_post(x2, attn, conv, p0, wo, gffn, w1, w3, w2, gple, wg, wp):
    n = x2.shape[0]
    tm = ROW_TILE
    return pl.pallas_call(
        _l0_post_kernel,
        grid=(n // tm,),
        in_specs=[_rows(tm, D_MODEL), _rows(tm, DA_WIDTH), _rows(tm, SC_WIDTH), _rows(tm, PLE_DIM),
                  _resident(wo.shape), _resident(gffn.shape), _resident(w1.shape),
                  _resident(w3.shape), _resident(w2.shape), _resident(gple.shape),
                  _resident(wg.shape), _resident(wp.shape)],
        out_specs=_rows(tm, D_MODEL),
        out_shape=jax.ShapeDtypeStruct((n, D_MODEL), F32),
        scratch_shapes=[pltpu.VMEM((tm, D_MODEL), F32), pltpu.VMEM((tm, D_MODEL), BF16),
                        pltpu.VMEM((tm, D_MODEL), F32)],
        compiler_params=pltpu.CompilerParams(dimension_semantics=("arbitrary",),
                                             vmem_limit_bytes=VMEM_LIMIT),
        name="l0_outproj_swiglu_ple",
    )(x2, attn, conv, p0, wo, gffn, w1, w3, w2, gple, wg, wp)


def _split_bf16(a):
    hi = a.astype(BF16)
    return hi, (a - hi.astype(F32)).astype(BF16)


def _l1_mix_kernel(h_ref, gmix_ref, win_ref, lng_ref, lnb_ref, ws_ref, bs_ref, wout_ref,
                   gffn_ref, wr_hi_ref, wr_lo_ref,
                   h1_ref, hn2_ref, ridx_ref, rgate_ref, gated_sc):
    tm = h_ref.shape[0]
    h = h_ref[...]
    hn = _rms(h, gmix_ref[...]).astype(BF16)

    def gelu(z):
        return 0.5 * z * (1.0 + lax.erf(z * (2.0 ** -0.5)))

    u = gelu(_dot(hn, win_ref[:, 0:GM_WIDTH]))
    vv = gelu(_dot(hn, win_ref[:, GM_WIDTH:2 * GM_WIDTH]))
    mu = jnp.mean(vv, axis=-1, keepdims=True)
    vc = vv - mu
    var = jnp.mean(vc * vc, axis=-1, keepdims=True)
    vv = (vc * lax.rsqrt(var + LN_EPS) * lng_ref[...] + lnb_ref[...]).astype(BF16)

    n_chunks = tm // GM_CHUNK
    row = lax.broadcasted_iota(jnp.int32, (GM_CHUNK, GM_CHUNK), 0)
    col = lax.broadcasted_iota(jnp.int32, (GM_CHUNK, GM_CHUNK), 1)
    for g in range(GM_GROUPS):
        gs = slice(g * LANES, (g + 1) * LANES)
        wm = jnp.where(col <= row, ws_ref[g], 0.0).astype(BF16)
        rhs = jnp.concatenate(
            [vv[r * GM_CHUNK:(r + 1) * GM_CHUNK, gs] for r in range(n_chunks)], axis=1)
        vs = _dot(wm, rhs) + bs_ref[:, g:g + 1]
        for r in range(n_chunks):
            rs = slice(r * GM_CHUNK, (r + 1) * GM_CHUNK)
            gated_sc[rs, gs] = (u[rs, gs] * vs[:, r * LANES:(r + 1) * LANES]).astype(BF16)

    h1 = h + _dot(gated_sc[...], wout_ref[...])
    h1_ref[...] = h1

    hn2 = _rms(h1, gffn_ref[...])
    hn2_ref[...] = hn2
    x_hi, x_lo = _split_bf16(hn2)
    logits = (_dot(x_hi, wr_hi_ref[...]) + _dot(x_hi, wr_lo_ref[...])
              + _dot(x_lo, wr_hi_ref[...]))
    lane = lax.broadcasted_iota(jnp.int32, logits.shape, 1)
    logits = jnp.where(lane < N_EXPERTS, logits, -jnp.inf)
    v1 = jnp.max(logits, axis=-1, keepdims=True)
    i1 = jnp.min(jnp.where(logits == v1, lane, LANES), axis=-1, keepdims=True)
    rest = jnp.where(lane == i1, -jnp.inf, logits)
    v2 = jnp.max(rest, axis=-1, keepdims=True)
    i2 = jnp.min(jnp.where(rest == v2, lane, LANES), axis=-1, keepdims=True)
    e2 = jnp.exp(v2 - v1)
    ridx_ref[:, 0:1] = i1
    ridx_ref[:, 1:2] = i2
    rgate_ref[:, 0:1] = 1.0 / (1.0 + e2)
    rgate_ref[:, 1:2] = e2 / (1.0 + e2)


def _l1_mix(h, gmix, win, lng, lnb, ws, bs_t, wout, gffn, wr_hi, wr_lo):
    n = h.shape[0]
    tm = ROW_TILE
    return pl.pallas_call(
        _l1_mix_kernel,
        grid=(n // tm,),
        in_specs=[_rows(tm, D_MODEL)] + [_resident(a.shape) for a in
                                         (gmix, win, lng, lnb, ws, bs_t, wout, gffn, wr_hi, wr_lo)],
        out_specs=[_rows(tm, D_MODEL), _rows(tm, D_MODEL), _rows(tm, TOP_K), _rows(tm, TOP_K)],
        out_shape=[jax.ShapeDtypeStruct((n, D_MODEL), F32), jax.ShapeDtypeStruct((n, D_MODEL), F32),
                   jax.ShapeDtypeStruct((n, TOP_K), jnp.int32),
                   jax.ShapeDtypeStruct((n, TOP_K), F32)],
        scratch_shapes=[pltpu.VMEM((tm, GM_WIDTH), BF16)],
        compiler_params=pltpu.CompilerParams(dimension_semantics=("arbitrary",),
                                             vmem_limit_bytes=VMEM_LIMIT),
        name="l1_gmlp_router",
    )(h, gmix, win, lng, lnb, ws, bs_t, wout, gffn, wr_hi, wr_lo)


def _moe_kernel(bexp_ref, nvalid_ref, tok_ref, dst_ref, hn_hbm, w1_ref, w3_ref, w2_ref, y_hbm,
                xbuf, xb_sc, acc_sc, ybuf, gsem, ssem, fence_sem, fence_sink):
    i = pl.program_id(0)
    n_valid = nvalid_ref[0]
    tm = MOE_TILE
    prev = jnp.maximum(i - 1, 0)

    def gather_row(blk, r):
        return pltpu.make_async_copy(hn_hbm.at[pl.ds(tok_ref[blk * tm + r], 1)],
                                     xbuf.at[pl.ds(r, 1)], gsem)

    def scatter_row(blk, r):
        return pltpu.make_async_copy(ybuf.at[pl.ds(r, 1)],
                                     y_hbm.at[pl.ds(dst_ref[blk * tm + r], 1)], ssem)

    def gather_wait():
        pltpu.make_async_copy(hn_hbm.at[pl.ds(0, tm)], xbuf, gsem).wait()

    def scatter_wait():
        pltpu.make_async_copy(ybuf, y_hbm.at[pl.ds(0, tm)], ssem).wait()

    @pl.when(i < n_valid)
    def _():
        @pl.when(i == 0)
        def _():
            def issue(r, carry):
                gather_row(0, r).start()
                return carry
            lax.fori_loop(0, tm, issue, 0)

        gather_wait()
        xb_sc[...] = xbuf[...].astype(BF16)
        acc_sc[...] = jnp.zeros(acc_sc.shape, F32)
        has_next = i + 1 < n_valid
        has_prev = i >= 1

        def copies(c):
            if c not in MOE_FENCE_CHUNKS:
                return
            g = MOE_FENCE_CHUNKS.index(c)
            if g > 0:
                fence_sink[0] = pl.semaphore_read(fence_sem)
            if g == len(MOE_FENCE_CHUNKS) - 1:
                return
            for r in range(MOE_COPY_ROWS[g], MOE_COPY_ROWS[g + 1]):
                @pl.when(has_next)
                def _():
                    gather_row(i + 1, r).start()

                @pl.when(has_prev)
                def _():
                    scatter_row(prev, r).start()

        _swiglu_chunks(xb_sc, w1_ref, w3_ref, w2_ref, acc_sc, per_chunk=copies)

        @pl.when(has_prev)
        def _():
            scatter_wait()

        ybuf[...] = acc_sc[...]

    @pl.when(i == n_valid)
    def _():
        def issue(r, carry):
            scatter_row(prev, r).start()
            return carry
        lax.fori_loop(0, tm, issue, 0)
        scatter_wait()
        ybuf[...] = jnp.zeros(ybuf.shape, F32)
        plane = y_hbm.shape[0] // TOP_K
        for b in range(TOP_K):
            spare = pltpu.make_async_copy(ybuf, y_hbm.at[pl.ds((b + 1) * plane - tm, tm)], ssem)
            spare.start()
            spare.wait()


def _moe_experts(block_expert, n_valid, row_token, row_dst, hn2, w1, w3, w2, n_out_rows):
    n_blocks = block_expert.shape[0]
    tm = MOE_TILE
    per_expert = lambda w: pl.BlockSpec((None,) + w.shape[1:], lambda i, be, *_: (be[i], 0, 0))
    return pl.pallas_call(
        _moe_kernel,
        grid_spec=pltpu.PrefetchScalarGridSpec(
            num_scalar_prefetch=4,
            grid=(n_blocks,),
            in_specs=[pl.BlockSpec(memory_space=pl.ANY), per_expert(w1), per_expert(w3),
                      per_expert(w2)],
            out_specs=pl.BlockSpec(memory_space=pl.ANY),
            scratch_shapes=[pltpu.VMEM((tm, D_MODEL), F32), pltpu.VMEM((tm, D_MODEL), BF16),
                            pltpu.VMEM((tm, D_MODEL), F32), pltpu.VMEM((tm, D_MODEL), F32),
                            pltpu.SemaphoreType.DMA(()), pltpu.SemaphoreType.DMA(()),
                            pltpu.SemaphoreType.REGULAR(()), pltpu.SMEM((1,), jnp.int32)],
        ),
        out_shape=jax.ShapeDtypeStruct((n_out_rows, D_MODEL), F32),
        compiler_params=pltpu.CompilerParams(dimension_semantics=("arbitrary",),
                                             vmem_limit_bytes=VMEM_LIMIT),
        name="l1_expert_swiglu",
    )(block_expert, n_valid, row_token, row_dst, hn2, w1, w3, w2)


def _combine_kernel(h_ref, y0_ref, y1_ref, gate_ref, p_ref, gple_ref, wg_ref, wp_ref, gfin_ref,
                    o_ref):
    gate = gate_ref[...]
    h = h_ref[...] + (y0_ref[...] * gate[:, 0:1] + y1_ref[...] * gate[:, 1:2])
    h = _ple(h, p_ref, gple_ref, wg_ref, wp_ref)
    o_ref[...] = _rms(h, gfin_ref[...])


def _combine(h1, y_rows, rgate, p1, gple, wg, wp, gfin):
    n = h1.shape[0]
    tm = ROW_TILE
    plane_tiles = y_rows.shape[0] // TOP_K // tm
    return pl.pallas_call(
        _combine_kernel,
        grid=(n // tm,),
        in_specs=[_rows(tm, D_MODEL), _rows(tm, D_MODEL),
                  pl.BlockSpec((tm, D_MODEL), lambda i: (plane_tiles + i, 0)), _rows(tm, TOP_K),
                  _rows(tm, PLE_DIM), _resident(gple.shape), _resident(wg.shape),
                  _resident(wp.shape), _resident(gfin.shape)],
        out_specs=_rows(tm, D_MODEL),
        out_shape=jax.ShapeDtypeStruct((n, D_MODEL), F32),
        compiler_params=pltpu.CompilerParams(dimension_semantics=("arbitrary",),
                                             vmem_limit_bytes=VMEM_LIMIT),
        name="l1_combine_ple_norm",
    )(h1, y_rows, y_rows, rgate, p1, gple, wg, wp, gfin)


def _routing_tables(ridx, n_blocks):
    n_assign = ridx.size
    e_flat = ridx.reshape(-1)
    onehot = (e_flat[:, None] == jnp.arange(N_EXPERTS, dtype=jnp.int32)[None, :]).astype(jnp.int32)
    csum = jnp.cumsum(onehot, axis=0)
    counts = csum[-1]
    rank = jnp.sum(csum * onehot, axis=1) - 1
    padded = (counts + MOE_TILE - 1) // MOE_TILE * MOE_TILE
    pad_ends = jnp.cumsum(padded)
    pad_starts = pad_ends - padded
    dest = (jnp.sum(pad_starts[None, :] * onehot, axis=1) + rank).astype(jnp.int32)
    n_rows = (n_blocks + 2) * MOE_TILE
    assign = jnp.full((n_rows,), -1, jnp.int32).at[dest].set(jnp.arange(n_assign, dtype=jnp.int32))
    row_token = jnp.maximum(assign, 0) // TOP_K
    plane = n_assign // TOP_K + MOE_TILE
    r = jnp.arange(n_rows, dtype=jnp.int32)
    spare = (r // MOE_TILE % 2) * plane + n_assign // TOP_K + r % MOE_TILE
    row_dst = jnp.where(assign >= 0, (assign % TOP_K) * plane + assign // TOP_K, spare)
    block_start = jnp.arange(n_blocks, dtype=jnp.int32) * MOE_TILE
    block_expert = jnp.minimum(
        jnp.sum((block_start[:, None] >= pad_ends[None, :]).astype(jnp.int32), axis=1),
        N_EXPERTS - 1).astype(jnp.int32)
    n_valid = (pad_ends[-1:] // MOE_TILE).astype(jnp.int32)
    return row_token, row_dst, block_expert, n_valid


def kernel(x, p, positions, ln_mix, ln_ffn, ln_ple, a_w_in, a_lambda, a_subln, a_conv_w, a_w_out,
           ffn_w1, ffn_w3, ffn_w2, c_w_in, c_ln_g, c_ln_b, c_w_s, c_b_s, c_w_out, router_w,
           moe_w1, moe_w3, moe_w2, ple_gate, ple_proj, final_norm):
    bsz, seq, _ = x.shape
    n = bsz * seq
    assert seq % ROW_TILE == 0 and seq % ATTN_TILE == 0 and ROW_TILE % GM_CHUNK == 0
    x2 = x.reshape(n, D_MODEL)
    pos2 = positions.reshape(n, 1)
    row = lambda a: a.reshape(1, -1)

    inv_freq = ROPE_THETA ** (-jnp.arange(0, ROT_DIM, 2, dtype=F32) / ROT_DIM)
    lane = jnp.arange(LANES) % DA_HEAD_DIM
    freq = jnp.where(lane < ROT_DIM, inv_freq[lane % (ROT_DIM // 2)], 0.0).reshape(1, LANES)
    qt, k, vt, conv = _inproj(x2, pos2, row(ln_mix[0]), freq, a_conv_w[0].T, a_w_in[0].astype(BF16), seq)
    lambda_init = 0.8 - 0.6 * math.exp(-0.3 * 0)
    attn = _diff_attention(qt, k, vt, a_lambda[0], a_subln[0].reshape(-1, 1), lambda_init, bsz, seq)
    h = _l0_post(x2, attn, conv, p[0].reshape(n, PLE_DIM), a_w_out[0].astype(BF16), row(ln_ffn[0]),
                 ffn_w1[0].astype(BF16), ffn_w3[0].astype(BF16), ffn_w2[0].astype(BF16),
                 row(ln_ple[0]), ple_gate[0].astype(BF16), ple_proj[0].astype(BF16))

    wr = jnp.pad(router_w[0], ((0, 0), (0, LANES - N_EXPERTS)))
    wr_hi, wr_lo = _split_bf16(wr)
    h1, hn2, ridx, rgate = _l1_mix(h, row(ln_mix[1]), c_w_in[0].astype(BF16), row(c_ln_g[0]),
                                   row(c_ln_b[0]), c_w_s[0], c_b_s[0].T, c_w_out[0].astype(BF16),
                                   row(ln_ffn[1]), wr_hi, wr_lo)
    n_blocks = -(-(n * TOP_K + N_EXPERTS * (MOE_TILE - 1)) // MOE_TILE)
    row_token, row_dst, block_expert, n_valid = _routing_tables(ridx, n_blocks)
    y_rows = _moe_experts(block_expert, n_valid, row_token, row_dst, hn2, moe_w1[0].astype(BF16),
                          moe_w3[0].astype(BF16), moe_w2[0].astype(BF16), TOP_K * (n + MOE_TILE))
    out = _combine(h1, y_rows, rgate, p[1].reshape(n, PLE_DIM), row(ln_ple[1]),
                   ple_gate[1].astype(BF16), ple_proj[1].astype(BF16), row(final_norm))
    return out.reshape(bsz, seq, D_MODEL)
```

```python
import functools
import math

import jax
import jax.numpy as jnp
from jax import lax
from jax.experimental import pallas as pl
from jax.experimental.pallas import tpu as pltpu

F32 = jnp.float32
BF16 = jnp.bfloat16

D_MODEL = 1024
PLE_DIM = 256
DA_HEADS = 4
DA_HEAD_DIM = 64
DA_V_DIM = 2 * DA_HEAD_DIM
DA_WIDTH = DA_HEADS * DA_V_DIM
ROPE_THETA = 500000.0
ROT_DIM = DA_HEAD_DIM // 4
SC_WIDTH = 512
GM_WIDTH = D_MODEL
GM_GROUPS = 8
GM_CHUNK = 128
D_FF = 2816
N_EXPERTS = 8
TOP_K = 2
RMS_EPS = 1e-6
LN_EPS = 1e-5

LANES = 128
FF_CHUNK = 256
N_FF_CHUNKS = D_FF // FF_CHUNK
ROW_TILE = 512
ATTN_TILE = 256
ATTN_Q_TILE = 512
Q_SCALE = DA_HEAD_DIM ** -0.5 * math.log2(math.e)
VT_ROWS = DA_V_DIM + 16
MOE_TILE = 512
MOE_FENCE_CHUNKS = (0, 3, 6, 8, 10)
MOE_COPY_ROWS = (0, 168, 336, 504, 512)
CONV_HALO = 8
VMEM_LIMIT = 56 * 2**20


def _resident(shape):
    nd = len(shape)
    return pl.BlockSpec(shape, lambda *_: (0,) * nd, pipeline_mode=pl.Buffered(1))


def _rows(tile, width):
    return pl.BlockSpec((tile, width), lambda i, *_: (i, 0))


def _rms(x, g):
    return x * lax.rsqrt(jnp.mean(x * x, axis=-1, keepdims=True) + RMS_EPS) * g


def _dot(a, b):
    return jnp.dot(a, b, preferred_element_type=F32)


def _swiglu_chunks(hn_ref, w1_ref, w3_ref, w2_ref, acc_ref, per_chunk=None):
    for c in range(N_FF_CHUNKS):
        cols = slice(c * FF_CHUNK, (c + 1) * FF_CHUNK)
        g = _dot(hn_ref[...], w1_ref[:, cols])
        u = _dot(hn_ref[...], w3_ref[:, cols])
        a = (g * jax.nn.sigmoid(g) * u).astype(BF16)
        acc_ref[...] += _dot(a, w2_ref[cols, :])
        if per_chunk is not None:
            per_chunk(c)


def _ple(h, p_ref, gple_ref, wg_ref, wp_ref):
    gate = jax.nn.sigmoid(_dot(_rms(h, gple_ref[...]).astype(BF16), wg_ref[...]))
    return h + gate * _dot(p_ref[...].astype(BF16), wp_ref[...])


def _inproj_kernel(x_ref, pos_ref, g_ref, freq_ref, cw_ref, w_ref,
                   qt_ref, k_ref, vt_ref, conv_ref, cbuf, *, tiles_per_seq):
    i = pl.program_id(0)
    tm = x_ref.shape[0]
    hn = _rms(x_ref[...], g_ref[...]).astype(BF16)

    ang = pos_ref[...].astype(F32) * freq_ref[...]
    lane = lax.broadcasted_iota(jnp.int32, (1, LANES), 1) % DA_HEAD_DIM
    half = ROT_DIM // 2
    cos = jnp.where(lane < ROT_DIM, jnp.cos(ang), 1.0)
    sin = jnp.sin(ang)
    sin_up = jnp.where(lane < half, -sin, 0.0)
    sin_dn = jnp.where((lane >= half) & (lane < ROT_DIM), sin, 0.0)

    def rope(t):
        return (t * cos + pltpu.roll(t, LANES - half, 1) * sin_up
                + pltpu.roll(t, half, 1) * sin_dn)

    zq = _dot(hn, w_ref[:, 0:DA_WIDTH])
    zk = _dot(hn, w_ref[:, DA_WIDTH:2 * DA_WIDTH])
    zv = _dot(hn, w_ref[:, 2 * DA_WIDTH:3 * DA_WIDTH])
    ones = jnp.ones((VT_ROWS - DA_V_DIM, ATTN_TILE), BF16)
    for hd in range(DA_HEADS):
        sl = slice(hd * LANES, (hd + 1) * LANES)
        qt_ref[sl, :] = (rope(zq[:, sl]) * Q_SCALE).T.astype(BF16)
        k_ref[:, sl] = rope(zk[:, sl]).astype(BF16)
        for u in range(tm // ATTN_TILE):
            vt_ref[hd, u, 0:DA_V_DIM, :] = zv[u * ATTN_TILE:(u + 1) * ATTN_TILE, sl].T.astype(BF16)
            vt_ref[hd, u, DA_V_DIM:VT_ROWS, :] = ones

    off = 3 * DA_WIDTH
    b_gate = _dot(hn, w_ref[:, off:off + SC_WIDTH])
    c_gate = _dot(hn, w_ref[:, off + SC_WIDTH:off + 2 * SC_WIDTH])
    hc = _dot(hn, w_ref[:, off + 2 * SC_WIDTH:off + 3 * SC_WIDTH])

    @pl.when(i % tiles_per_seq == 0)
    def _():
        cbuf[0:CONV_HALO, :] = jnp.zeros((CONV_HALO, SC_WIDTH), F32)

    @pl.when(i % tiles_per_seq != 0)
    def _():
        cbuf[0:CONV_HALO, :] = cbuf[tm:tm + CONV_HALO, :]

    ch = c_gate * hc
    cbuf[CONV_HALO:CONV_HALO + tm, :] = ch
    ch1 = cbuf[CONV_HALO - 1:CONV_HALO - 1 + tm, :]
    ch2 = cbuf[CONV_HALO - 2:CONV_HALO - 2 + tm, :]
    conv = b_gate * (cw_ref[2:3, :] * ch + cw_ref[1:2, :] * ch1 + cw_ref[0:1, :] * ch2)
    conv_ref[...] = conv.astype(BF16)


def _inproj(x2, pos2, g, freq, cw, w, seq):
    n = x2.shape[0]
    tm = ROW_TILE
    out = jax.ShapeDtypeStruct((n, DA_WIDTH), BF16)
    sub = tm // ATTN_TILE
    return pl.pallas_call(
        functools.partial(_inproj_kernel, tiles_per_seq=seq // tm),
        grid=(n // tm,),
        in_specs=[_rows(tm, D_MODEL), _rows(tm, 1), _resident(g.shape), _resident(freq.shape),
                  _resident(cw.shape), _resident(w.shape)],
        out_specs=[pl.BlockSpec((DA_WIDTH, tm), lambda i: (0, i)), _rows(tm, DA_WIDTH),
                   pl.BlockSpec((DA_HEADS, sub, VT_ROWS, ATTN_TILE), lambda i: (0, i, 0, 0)),
                   _rows(tm, SC_WIDTH)],
        out_shape=[jax.ShapeDtypeStruct((DA_WIDTH, n), BF16), out,
                   jax.ShapeDtypeStruct((DA_HEADS, n // ATTN_TILE, VT_ROWS, ATTN_TILE), BF16), out],
        scratch_shapes=[pltpu.VMEM((tm + CONV_HALO, SC_WIDTH), F32)],
        compiler_params=pltpu.CompilerParams(dimension_semantics=("arbitrary",),
                                             vmem_limit_bytes=VMEM_LIMIT),
        name="l0_inproj",
    )(x2, pos2, g, freq, cw, w)


def _attn_kernel(lam_ref, sg_ref, qt_ref, k_ref, vt_ref, o_ref,
                 qc_sc, m_sc, alpha_sc, acc_sc, s_sc, p_sc, *, lambda_init):
    qi = pl.program_id(2)
    tq = ATTN_Q_TILE
    tk = ATTN_TILE
    n_sub = tq // tk

    qt = qt_ref[...]
    dim = lax.broadcasted_iota(jnp.int32, (LANES, 1), 0)
    zero = jnp.zeros_like(qt)
    qc_sc[0] = jnp.where(dim < DA_HEAD_DIM, qt, zero)
    qc_sc[1] = jnp.where(dim >= DA_HEAD_DIM, qt, zero)
    m_sc[...] = jnp.full(m_sc.shape, -jnp.inf, F32)
    acc_sc[...] = jnp.zeros(acc_sc.shape, F32)
    alpha_sc[...] = jnp.ones(alpha_sc.shape, F32)
    p_sc[...] = jnp.zeros(p_sc.shape, BF16)

    def scores(j, diagonal):
        for c in range(2):
            for u in range(n_sub):
                ks = pl.multiple_of((j * n_sub + u) * tk, tk)
                su = _dot(k_ref[pl.ds(ks, tk), :], qc_sc[c])
                if diagonal:
                    key = lax.broadcasted_iota(jnp.int32, (tk, tq), 0) + u * tk
                    qry = lax.broadcasted_iota(jnp.int32, (tk, tq), 1)
                    su = jnp.where(key <= qry, su, -jnp.inf)
                s_sc[c, u * tk:(u + 1) * tk, :] = su

    def softmax():
        for c in range(2):
            s = s_sc[c]
            m_prev = m_sc[c]
            m_new = jnp.maximum(m_prev, jnp.max(s, axis=0, keepdims=True))
            alpha_sc[c] = jnp.exp2(m_prev - m_new)
            m_sc[c] = m_new
            p_sc[c] = jnp.exp2(s - m_new).astype(BF16)

    def fold(j):
        for c in range(2):
            pv = None
            for u in range(n_sub):
                d = _dot(vt_ref[j * n_sub + u], p_sc[c, u * tk:(u + 1) * tk, :])
                pv = d if pv is None else pv + d
            acc_sc[c] = alpha_sc[c] * acc_sc[c] + pv

    def iteration(t, next_scores):
        fold(jnp.maximum(t - 1, 0))
        softmax()
        if next_scores is not None:
            scores(t + 1, diagonal=next_scores == "diagonal")

    def steady(t, carry):
        iteration(t, "full")
        return carry

    @pl.when(qi == 0)
    def _():
        scores(0, diagonal=True)

    @pl.when(qi > 0)
    def _():
        scores(0, diagonal=False)

    lax.fori_loop(0, qi - 1, steady, 0)

    @pl.when(qi > 0)
    def _():
        iteration(qi - 1, "diagonal")

    iteration(qi, None)
    fold(qi)

    lp = lam_ref[...]
    lam = (jnp.exp(jnp.sum(lp[0:1] * lp[1:2], axis=-1, keepdims=True))
           - jnp.exp(jnp.sum(lp[2:3] * lp[3:4], axis=-1, keepdims=True)) + lambda_init)
    a0 = acc_sc[0]
    a1 = acc_sc[1]
    o = (a0[0:DA_V_DIM] / a0[DA_V_DIM:DA_V_DIM + 1]
         - lam * (a1[0:DA_V_DIM] / a1[DA_V_DIM:DA_V_DIM + 1]))
    o = o * lax.rsqrt(jnp.mean(o * o, axis=0, keepdims=True) + RMS_EPS) * sg_ref[...]
    o_ref[...] = (o * (1.0 - lambda_init)).T.astype(o_ref.dtype)


def _diff_attention(qt, k, vt, lam_p, subln_col, lambda_init, bsz, seq):
    tq = ATTN_Q_TILE
    tk = ATTN_TILE
    nq = seq // tq
    nk = seq // tk
    return pl.pallas_call(
        functools.partial(_attn_kernel, lambda_init=lambda_init),
        grid=(bsz, DA_HEADS, nq),
        in_specs=[pl.BlockSpec(lam_p.shape, lambda b, h, i: (0, 0)),
                  pl.BlockSpec(subln_col.shape, lambda b, h, i: (0, 0)),
                  pl.BlockSpec((LANES, tq), lambda b, h, i: (h, b * nq + i)),
                  pl.BlockSpec((seq, LANES), lambda b, h, i: (b, h)),
                  pl.BlockSpec((None, nk, VT_ROWS, tk), lambda b, h, i: (h, b, 0, 0))],
        out_specs=pl.BlockSpec((tq, LANES), lambda b, h, i: (b * nq + i, h)),
        out_shape=jax.ShapeDtypeStruct(k.shape, BF16),
        scratch_shapes=[pltpu.VMEM((2, LANES, tq), BF16), pltpu.VMEM((2, 1, tq), F32),
                        pltpu.VMEM((2, 1, tq), F32), pltpu.VMEM((2, VT_ROWS, tq), F32),
                        pltpu.VMEM((2, tq, tq), F32), pltpu.VMEM((2, tq, tq), BF16)],
        compiler_params=pltpu.CompilerParams(
            dimension_semantics=("arbitrary", "arbitrary", "arbitrary"),
            vmem_limit_bytes=VMEM_LIMIT),
        name="l0_diff_attention",
    )(lam_p, subln_col, qt, k, vt)


def _l0_post_kernel(x_ref, attn_ref, conv_ref, p_ref, wo_ref, gffn_ref, w1_ref, w3_ref, w2_ref,
                    gple_ref, wg_ref, wp_ref, o_ref, acc_sc, hn_sc, h_sc):
    h = (x_ref[...] + _dot(attn_ref[...], wo_ref[0:DA_WIDTH, :])
         + _dot(conv_ref[...], wo_ref[DA_WIDTH:DA_WIDTH + SC_WIDTH, :]))
    hn_sc[...] = _rms(h, gffn_ref[...]).astype(BF16)
    h_sc[...] = h
    acc_sc[...] = jnp.zeros(acc_sc.shape, F32)
    _swiglu_chunks(hn_sc, w1_ref, w3_ref, w2_ref, acc_sc)
    o_ref[...] = _ple(h_sc[...] + acc_sc[...], p_ref, gple_ref, wg_ref, wp_ref)


def _l0_post(x2, attn, conv, p0, wo, gffn, w1, w3, w2, gple, wg, wp):
    n = x2.shape[0]
    tm = ROW_TILE
    return pl.pallas_call(
        _l0_post_kernel,
        grid=(n // tm,),
        in_specs=[_rows(tm, D_MODEL), _rows(tm, DA_WIDTH), _rows(tm, SC_WIDTH), _rows(tm, PLE_DIM),
                  _resident(wo.shape), _resident(gffn.shape), _resident(w1.shape),
                  _resident(w3.shape), _resident(w2.shape), _resident(gple.shape),
                  _resident(wg.shape), _resident(wp.shape)],
        out_specs=_rows(tm, D_MODEL),
        out_shape=jax.ShapeDtypeStruct((n, D_MODEL), F32),
        scratch_shapes=[pltpu.VMEM((tm, D_MODEL), F32), pltpu.VMEM((tm, D_MODEL), BF16),
                        pltpu.VMEM((tm, D_MODEL), F32)],
        compiler_params=pltpu.CompilerParams(dimension_semantics=("arbitrary",),
                                             vmem_limit_bytes=VMEM_LIMIT),
        name="l0_outproj_swiglu_ple",
    )(x2, attn, conv, p0, wo, gffn, w1, w3, w2, gple, wg, wp)


def _split_bf16(a):
    hi = a.astype(BF16)
    return hi, (a - hi.astype(F32)).astype(BF16)


def _l1_mix_kernel(h_ref, gmix_ref, win_ref, lng_ref, lnb_ref, ws_ref, bs_ref, wout_ref,
                   gffn_ref, wr_hi_ref, wr_lo_ref,
                   h1_ref, hn2_ref, ridx_ref, rgate_ref, gated_sc):
    tm = h_ref.shape[0]
    h = h_ref[...]
    hn = _rms(h, gmix_ref[...]).astype(BF16)

    def gelu(z):
        return 0.5 * z * (1.0 + lax.erf(z * (2.0 ** -0.5)))

    u = gelu(_dot(hn, win_ref[:, 0:GM_WIDTH]))
    vv = gelu(_dot(hn, win_ref[:, GM_WIDTH:2 * GM_WIDTH]))
    mu = jnp.mean(vv, axis=-1, keepdims=True)
    vc = vv - mu
    var = jnp.mean(vc * vc, axis=-1, keepdims=True)
    vv = (vc * lax.rsqrt(var + LN_EPS) * lng_ref[...] + lnb_ref[...]).astype(BF16)

    n_chunks = tm // GM_CHUNK
    row = lax.broadcasted_iota(jnp.int32, (GM_CHUNK, GM_CHUNK), 0)
    col = lax.broadcasted_iota(jnp.int32, (GM_CHUNK, GM_CHUNK), 1)
    for g in range(GM_GROUPS):
        gs = slice(g * LANES, (g + 1) * LANES)
        wm = jnp.where(col <= row, ws_ref[g], 0.0).astype(BF16)
        rhs = jnp.concatenate(
            [vv[r * GM_CHUNK:(r + 1) * GM_CHUNK, gs] for r in range(n_chunks)], axis=1)
        vs = _dot(wm, rhs) + bs_ref[:, g:g + 1]
        for r in range(n_chunks):
            rs = slice(r * GM_CHUNK, (r + 1) * GM_CHUNK)
            gated_sc[rs, gs] = (u[rs, gs] * vs[:, r * LANES:(r + 1) * LANES]).astype(BF16)

    h1 = h + _dot(gated_sc[...], wout_ref[...])
    h1_ref[...] = h1

    hn2 = _rms(h1, gffn_ref[...])
    hn2_ref[...] = hn2
    x_hi, x_lo = _split_bf16(hn2)
    logits = (_dot(x_hi, wr_hi_ref[...]) + _dot(x_hi, wr_lo_ref[...])
              + _dot(x_lo, wr_hi_ref[...]))
    lane = lax.broadcasted_iota(jnp.int32, logits.shape, 1)
    logits = jnp.where(lane < N_EXPERTS, logits, -jnp.inf)
    v1 = jnp.max(logits, axis=-1, keepdims=True)
    i1 = jnp.min(jnp.where(logits == v1, lane, LANES), axis=-1, keepdims=True)
    rest = jnp.where(lane == i1, -jnp.inf, logits)
    v2 = jnp.max(rest, axis=-1, keepdims=True)
    i2 = jnp.min(jnp.where(rest == v2, lane, LANES), axis=-1, keepdims=True)
    e2 = jnp.exp(v2 - v1)
    ridx_ref[:, 0:1] = i1
    ridx_ref[:, 1:2] = i2
    rgate_ref[:, 0:1] = 1.0 / (1.0 + e2)
    rgate_ref[:, 1:2] = e2 / (1.0 + e2)


def _l1_mix(h, gmix, win, lng, lnb, ws, bs_t, wout, gffn, wr_hi, wr_lo):
    n = h.shape[0]
    tm = ROW_TILE
    return pl.pallas_call(
        _l1_mix_kernel,
        grid=(n // tm,),
        in_specs=[_rows(tm, D_MODEL)] + [_resident(a.shape) for a in
                                         (gmix, win, lng, lnb, ws, bs_t, wout, gffn, wr_hi, wr_lo)],
        out_specs=[_rows(tm, D_MODEL), _rows(tm, D_MODEL), _rows(tm, TOP_K), _rows(tm, TOP_K)],
        out_shape=[jax.ShapeDtypeStruct((n, D_MODEL), F32), jax.ShapeDtypeStruct((n, D_MODEL), F32),
                   jax.ShapeDtypeStruct((n, TOP_K), jnp.int32),
                   jax.ShapeDtypeStruct((n, TOP_K), F32)],
        scratch_shapes=[pltpu.VMEM((tm, GM_WIDTH), BF16)],
        compiler_params=pltpu.CompilerParams(dimension_semantics=("arbitrary",),
                                             vmem_limit_bytes=VMEM_LIMIT),
        name="l1_gmlp_router",
    )(h, gmix, win, lng, lnb, ws, bs_t, wout, gffn, wr_hi, wr_lo)


def _moe_kernel(bexp_ref, nvalid_ref, tok_ref, dst_ref, hn_hbm, w1_ref, w3_ref, w2_ref, y_hbm,
                xbuf, xb_sc, acc_sc, ybuf, gsem, ssem, fence_sem, fence_sink):
    i = pl.program_id(0)
    n_valid = nvalid_ref[0]
    tm = MOE_TILE
    prev = jnp.maximum(i - 1, 0)

    def gather_row(blk, r):
        return pltpu.make_async_copy(hn_hbm.at[pl.ds(tok_ref[blk * tm + r], 1)],
                                     xbuf.at[pl.ds(r, 1)], gsem)

    def scatter_row(blk, r):
        return pltpu.make_async_copy(ybuf.at[pl.ds(r, 1)],
                                     y_hbm.at[pl.ds(dst_ref[blk * tm + r], 1)], ssem)

    def gather_wait():
        pltpu.make_async_copy(hn_hbm.at[pl.ds(0, tm)], xbuf, gsem).wait()

    def scatter_wait():
        pltpu.make_async_copy(ybuf, y_hbm.at[pl.ds(0, tm)], ssem).wait()

    @pl.when(i < n_valid)
    def _():
        @pl.when(i == 0)
        def _():
            def issue(r, carry):
                gather_row(0, r).start()
                return carry
            lax.fori_loop(0, tm, issue, 0)

        gather_wait()
        xb_sc[...] = xbuf[...].astype(BF16)
        acc_sc[...] = jnp.zeros(acc_sc.shape, F32)
        has_next = i + 1 < n_valid
        has_prev = i >= 1

        def copies(c):
            if c not in MOE_FENCE_CHUNKS:
                return
            g = MOE_FENCE_CHUNKS.index(c)
            if g > 0:
                fence_sink[0] = pl.semaphore_read(fence_sem)
            if g == len(MOE_FENCE_CHUNKS) - 1:
                return
            for r in range(MOE_COPY_ROWS[g], MOE_COPY_ROWS[g + 1]):
                @pl.when(has_next)
                def _():
                    gather_row(i + 1, r).start()

                @pl.when(has_prev)
                def _():
                    scatter_row(prev, r).start()

        _swiglu_chunks(xb_sc, w1_ref, w3_ref, w2_ref, acc_sc, per_chunk=copies)

        @pl.when(has_prev)
        def _():
            scatter_wait()

        ybuf[...] = acc_sc[...]

    @pl.when(i == n_valid)
    def _():
        def issue(r, carry):
            scatter_row(prev, r).start()
            return carry
        lax.fori_loop(0, tm, issue, 0)
        scatter_wait()
        ybuf[...] = jnp.zeros(ybuf.shape, F32)
        plane = y_hbm.shape[0] // TOP_K
        for b in range(TOP_K):
            spare = pltpu.make_async_copy(ybuf, y_hbm.at[pl.ds((b + 1) * plane - tm, tm)], ssem)
            spare.start()
            spare.wait()


def _moe_experts(block_expert, n_valid, row_token, row_dst, hn2, w1, w3, w2, n_out_rows):
    n_blocks = block_expert.shape[0]
    tm = MOE_TILE
    per_expert = lambda w: pl.BlockSpec((None,) + w.shape[1:], lambda i, be, *_: (be[i], 0, 0))
    return pl.pallas_call(
        _moe_kernel,
        grid_spec=pltpu.PrefetchScalarGridSpec(
            num_scalar_prefetch=4,
            grid=(n_blocks,),
            in_specs=[pl.BlockSpec(memory_space=pl.ANY), per_expert(w1), per_expert(w3),
                      per_expert(w2)],
            out_specs=pl.BlockSpec(memory_space=pl.ANY),
            scratch_shapes=[pltpu.VMEM((tm, D_MODEL), F32), pltpu.VMEM((tm, D_MODEL), BF16),
                            pltpu.VMEM((tm, D_MODEL), F32), pltpu.VMEM((tm, D_MODEL), F32),
                            pltpu.SemaphoreType.DMA(()), pltpu.SemaphoreType.DMA(()),
                            pltpu.SemaphoreType.REGULAR(()), pltpu.SMEM((1,), jnp.int32)],
        ),
        out_shape=jax.ShapeDtypeStruct((n_out_rows, D_MODEL), F32),
        compiler_params=pltpu.CompilerParams(dimension_semantics=("arbitrary",),
                                             vmem_limit_bytes=VMEM_LIMIT),
        name="l1_expert_swiglu",
    )(block_expert, n_valid, row_token, row_dst, hn2, w1, w3, w2)


def _combine_kernel(h_ref, y0_ref, y1_ref, gate_ref, p_ref, gple_ref, wg_ref, wp_ref, gfin_ref,
                    o_ref):
    gate = gate_ref[...]
    h = h_ref[...] + (y0_ref[...] * gate[:, 0:1] + y1_ref[...] * gate[:, 1:2])
    h = _ple(h, p_ref, gple_ref, wg_ref, wp_ref)
    o_ref[...] = _rms(h, gfin_ref[...])


def _combine(h1, y_rows, rgate, p1, gple, wg, wp, gfin):
    n = h1.shape[0]
    tm = ROW_TILE
    plane_tiles = y_rows.shape[0] // TOP_K // tm
    return pl.pallas_call(
        _combine_kernel,
        grid=(n // tm,),
        in_specs=[_rows(tm, D_MODEL), _rows(tm, D_MODEL),
                  pl.BlockSpec((tm, D_MODEL), lambda i: (plane_tiles + i, 0)), _rows(tm, TOP_K),
                  _rows(tm, PLE_DIM), _resident(gple.shape), _resident(wg.shape),
                  _resident(wp.shape), _resident(gfin.shape)],
        out_specs=_rows(tm, D_MODEL),
        out_shape=jax.ShapeDtypeStruct((n, D_MODEL), F32),
        compiler_params=pltpu.CompilerParams(dimension_semantics=("arbitrary",),
                                             vmem_limit_bytes=VMEM_LIMIT),
        name="l1_combine_ple_norm",
    )(h1, y_rows, y_rows, rgate, p1, gple, wg, wp, gfin)


def _routing_tables(ridx, n_blocks):
    n_assign = ridx.size
    e_flat = ridx.reshape(-1)
    onehot = (e_flat[:, None] == jnp.arange(N_EXPERTS, dtype=jnp.int32)[None, :]).astype(jnp.int32)
    csum = jnp.cumsum(onehot, axis=0)
    counts = csum[-1]
    rank = jnp.sum(csum * onehot, axis=1) - 1
    padded = (counts + MOE_TILE - 1) // MOE_TILE * MOE_TILE
    pad_ends = jnp.cumsum(padded)
    pad_starts = pad_ends - padded
    dest = (jnp.sum(pad_starts[None, :] * onehot, axis=1) + rank).astype(jnp.int32)
    n_rows = (n_blocks + 2) * MOE_TILE
    assign = jnp.full((n_rows,), -1, jnp.int32).at[dest].set(jnp.arange(n_assign, dtype=jnp.int32))
    row_token = jnp.maximum(assign, 0) // TOP_K
    plane = n_assign // TOP_K + MOE_TILE
    r = jnp.arange(n_rows, dtype=jnp.int32)
    spare = (r // MOE_TILE % 2) * plane + n_assign // TOP_K + r % MOE_TILE
    row_dst = jnp.where(assign >= 0, (assign % TOP_K) * plane + assign // TOP_K, spare)
    block_start = jnp.arange(n_blocks, dtype=jnp.int32) * MOE_TILE
    block_expert = jnp.minimum(
        jnp.sum((block_start[:, None] >= pad_ends[None, :]).astype(jnp.int32), axis=1),
        N_EXPERTS - 1).astype(jnp.int32)
    n_valid = (pad_ends[-1:] // MOE_TILE).astype(jnp.int32)
    return row_token, row_dst, block_expert, n_valid


def kernel(x, p, positions, ln_mix, ln_ffn, ln_ple, a_w_in, a_lambda, a_subln, a_conv_w, a_w_out,
           ffn_w1, ffn_w3, ffn_w2, c_w_in, c_ln_g, c_ln_b, c_w_s, c_b_s, c_w_out, router_w,
           moe_w1, moe_w3, moe_w2, ple_gate, ple_proj, final_norm):
    bsz, seq, _ = x.shape
    n = bsz * seq
    assert seq % ROW_TILE == 0 and seq % ATTN_TILE == 0 and ROW_TILE % GM_CHUNK == 0
    x2 = x.reshape(n, D_MODEL)
    pos2 = positions.reshape(n, 1)
    row = lambda a: a.reshape(1, -1)

    inv_freq = ROPE_THETA ** (-jnp.arange(0, ROT_DIM, 2, dtype=F32) / ROT_DIM)
    lane = jnp.arange(LANES) % DA_HEAD_DIM
    freq = jnp.where(lane < ROT_DIM, inv_freq[lane % (ROT_DIM // 2)], 0.0).reshape(1, LANES)
    qt, k, vt, conv = _inproj(x2, pos2, row(ln_mix[0]), freq, a_conv_w[0].T, a_w_in[0].astype(BF16), seq)
    lambda_init = 0.8 - 0.6 * math.exp(-0.3 * 0)
    attn = _diff_attention(qt, k, vt, a_lambda[0], a_subln[0].reshape(-1, 1), lambda_init, bsz, seq)
    h = _l0_post(x2, attn, conv, p[0].reshape(n, PLE_DIM), a_w_out[0].astype(BF16), row(ln_ffn[0]),
                 ffn_w1[0].astype(BF16), ffn_w3[0].astype(BF16), ffn_w2[0].astype(BF16),
                 row(ln_ple[0]), ple_gate[0].astype(BF16), ple_proj[0].astype(BF16))

    wr = jnp.pad(router_w[0], ((0, 0), (0, LANES - N_EXPERTS)))
    wr_hi, wr_lo = _split_bf16(wr)
    h1, hn2, ridx, rgate = _l1_mix(h, row(ln_mix[1]), c_w_in[0].astype(BF16), row(c_ln_g[0]),
                                   row(c_ln_b[0]), c_w_s[0], c_b_s[0].T, c_w_out[0].astype(BF16),
                                   row(ln_ffn[1]), wr_hi, wr_lo)
    n_blocks = -(-(n * TOP_K + N_EXPERTS * (MOE_TILE - 1)) // MOE_TILE)
    row_token, row_dst, block_expert, n_valid = _routing_tables(ridx, n_blocks)
    y_rows = _moe_experts(block_expert, n_valid, row_token, row_dst, hn2, moe_w1[0].astype(BF16),
                          moe_w3[0].astype(BF16), moe_w2[0].astype(BF16), TOP_K * (n + MOE_TILE))
    out = _combine(h1, y_rows, rgate, p[1].reshape(n, PLE_DIM), row(ln_ple[1]),
                   ple_gate[1].astype(BF16), ple_proj[1].astype(BF16), row(final_norm))
    return out.reshape(bsz, seq, D_MODEL)
```

```python
import functools
import math

import jax
import jax.numpy as jnp
from jax import lax
from jax.experimental import pallas as pl
from jax.experimental.pallas import tpu as pltpu

F32 = jnp.float32
BF16 = jnp.bfloat16

D_MODEL = 1024
PLE_DIM = 256
DA_HEADS = 4
DA_HEAD_DIM = 64
DA_V_DIM = 2 * DA_HEAD_DIM
DA_WIDTH = DA_HEADS * DA_V_DIM
ROPE_THETA = 500000.0
ROT_DIM = DA_HEAD_DIM // 4
SC_WIDTH = 512
GM_WIDTH = D_MODEL
GM_GROUPS = 8
GM_CHUNK = 128
D_FF = 2816
N_EXPERTS = 8
TOP_K = 2
RMS_EPS = 1e-6
LN_EPS = 1e-5

LANES = 128
FF_CHUNK = 256
N_FF_CHUNKS = D_FF // FF_CHUNK
ROW_TILE = 512
ATTN_TILE = 256
ATTN_Q_TILE = 512
Q_SCALE = DA_HEAD_DIM ** -0.5 * math.log2(math.e)
VT_ROWS = DA_V_DIM + 16
MOE_TILE = 512
MOE_FENCE_CHUNKS = (0, 3, 6, 8, 10)
MOE_COPY_ROWS = (0, 168, 336, 504, 512)
CONV_HALO = 8
VMEM_LIMIT = 56 * 2**20


def _resident(shape):
    nd = len(shape)
    return pl.BlockSpec(shape, lambda *_: (0,) * nd, pipeline_mode=pl.Buffered(1))


def _rows(tile, width):
    return pl.BlockSpec((tile, width), lambda i, *_: (i, 0))


def _rms(x, g):
    return x * lax.rsqrt(jnp.mean(x * x, axis=-1, keepdims=True) + RMS_EPS) * g


def _dot(a, b):
    return jnp.dot(a, b, preferred_element_type=F32)


def _swiglu_chunks(hn_ref, w1_ref, w3_ref, w2_ref, acc_ref, per_chunk=None):
    for c in range(N_FF_CHUNKS):
        cols = slice(c * FF_CHUNK, (c + 1) * FF_CHUNK)
        g = _dot(hn_ref[...], w1_ref[:, cols])
        u = _dot(hn_ref[...], w3_ref[:, cols])
        a = (g * jax.nn.sigmoid(g) * u).astype(BF16)
        acc_ref[...] += _dot(a, w2_ref[cols, :])
        if per_chunk is not None:
            per_chunk(c)


def _ple(h, p_ref, gple_ref, wg_ref, wp_ref):
    gate = jax.nn.sigmoid(_dot(_rms(h, gple_ref[...]).astype(BF16), wg_ref[...]))
    return h + gate * _dot(p_ref[...].astype(BF16), wp_ref[...])


def _inproj_kernel(x_ref, pos_ref, g_ref, freq_ref, cw_ref, w_ref,
                   qt_ref, k_ref, vt_ref, conv_ref, cbuf, *, tiles_per_seq):
    i = pl.program_id(0)
    tm = x_ref.shape[0]
    hn = _rms(x_ref[...], g_ref[...]).astype(BF16)

    ang = pos_ref[...].astype(F32) * freq_ref[...]
    lane = lax.broadcasted_iota(jnp.int32, (1, LANES), 1) % DA_HEAD_DIM
    half = ROT_DIM // 2
    cos = jnp.where(lane < ROT_DIM, jnp.cos(ang), 1.0)
    sin = jnp.sin(ang)
    sin_up = jnp.where(lane < half, -sin, 0.0)
    sin_dn = jnp.where((lane >= half) & (lane < ROT_DIM), sin, 0.0)

    def rope(t):
        return (t * cos + pltpu.roll(t, LANES - half, 1) * sin_up
                + pltpu.roll(t, half, 1) * sin_dn)

    zq = _dot(hn, w_ref[:, 0:DA_WIDTH])
    zk = _dot(hn, w_ref[:, DA_WIDTH:2 * DA_WIDTH])
    zv = _dot(hn, w_ref[:, 2 * DA_WIDTH:3 * DA_WIDTH])
    ones = jnp.ones((VT_ROWS - DA_V_DIM, ATTN_TILE), BF16)
    for hd in range(DA_HEADS):
        sl = slice(hd * LANES, (hd + 1) * LANES)
        qt_ref[sl, :] = (rope(zq[:, sl]) * Q_SCALE).T.astype(BF16)
        k_ref[:, sl] = rope(zk[:, sl]).astype(BF16)
        for u in range(tm // ATTN_TILE):
            vt_ref[hd, u, 0:DA_V_DIM, :] = zv[u * ATTN_TILE:(u + 1) * ATTN_TILE, sl].T.astype(BF16)
            vt_ref[hd, u, DA_V_DIM:VT_ROWS, :] = ones

    off = 3 * DA_WIDTH
    b_gate = _dot(hn, w_ref[:, off:off + SC_WIDTH])
    c_gate = _dot(hn, w_ref[:, off + SC_WIDTH:off + 2 * SC_WIDTH])
    hc = _dot(hn, w_ref[:, off + 2 * SC_WIDTH:off + 3 * SC_WIDTH])

    @pl.when(i % tiles_per_seq == 0)
    def _():
        cbuf[0:CONV_HALO, :] = jnp.zeros((CONV_HALO, SC_WIDTH), F32)

    @pl.when(i % tiles_per_seq != 0)
    def _():
        cbuf[0:CONV_HALO, :] = cbuf[tm:tm + CONV_HALO, :]

    ch = c_gate * hc
    cbuf[CONV_HALO:CONV_HALO + tm, :] = ch
    ch1 = cbuf[CONV_HALO - 1:CONV_HALO - 1 + tm, :]
    ch2 = cbuf[CONV_HALO - 2:CONV_HALO - 2 + tm, :]
    conv = b_gate * (cw_ref[2:3, :] * ch + cw_ref[1:2, :] * ch1 + cw_ref[0:1, :] * ch2)
    conv_ref[...] = conv.astype(BF16)


def _inproj(x2, pos2, g, freq, cw, w, seq):
    n = x2.shape[0]
    tm = ROW_TILE
    out = jax.ShapeDtypeStruct((n, DA_WIDTH), BF16)
    sub = tm // ATTN_TILE
    return pl.pallas_call(
        functools.partial(_inproj_kernel, tiles_per_seq=seq // tm),
        grid=(n // tm,),
        in_specs=[_rows(tm, D_MODEL), _rows(tm, 1), _resident(g.shape), _resident(freq.shape),
                  _resident(cw.shape), _resident(w.shape)],
        out_specs=[pl.BlockSpec((DA_WIDTH, tm), lambda i: (0, i)), _rows(tm, DA_WIDTH),
                   pl.BlockSpec((DA_HEADS, sub, VT_ROWS, ATTN_TILE), lambda i: (0, i, 0, 0)),
                   _rows(tm, SC_WIDTH)],
        out_shape=[jax.ShapeDtypeStruct((DA_WIDTH, n), BF16), out,
                   jax.ShapeDtypeStruct((DA_HEADS, n // ATTN_TILE, VT_ROWS, ATTN_TILE), BF16), out],
        scratch_shapes=[pltpu.VMEM((tm + CONV_HALO, SC_WIDTH), F32)],
        compiler_params=pltpu.CompilerParams(dimension_semantics=("arbitrary",),
                                             vmem_limit_bytes=VMEM_LIMIT),
        name="l0_inproj",
    )(x2, pos2, g, freq, cw, w)


def _attn_kernel(lam_ref, sg_ref, qt_ref, k_ref, vt_ref, o_ref,
                 qc_sc, m_sc, alpha_sc, acc_sc, s_sc, p_sc, *, lambda_init):
    qi = pl.program_id(2)
    tq = ATTN_Q_TILE
    tk = ATTN_TILE
    n_sub = tq // tk

    qt = qt_ref[...]
    dim = lax.broadcasted_iota(jnp.int32, (LANES, 1), 0)
    zero = jnp.zeros_like(qt)
    qc_sc[0] = jnp.where(dim < DA_HEAD_DIM, qt, zero)
    qc_sc[1] = jnp.where(dim >= DA_HEAD_DIM, qt, zero)
    m_sc[...] = jnp.full(m_sc.shape, -jnp.inf, F32)
    acc_sc[...] = jnp.zeros(acc_sc.shape, F32)
    alpha_sc[...] = jnp.ones(alpha_sc.shape, F32)
    p_sc[...] = jnp.zeros(p_sc.shape, BF16)

    def scores(j, diagonal):
        for c in range(2):
            for u in range(n_sub):
                ks = pl.multiple_of((j * n_sub + u) * tk, tk)
                su = _dot(k_ref[pl.ds(ks, tk), :], qc_sc[c])
                if diagonal:
                    key = lax.broadcasted_iota(jnp.int32, (tk, tq), 0) + u * tk
                    qry = lax.broadcasted_iota(jnp.int32, (tk, tq), 1)
                    su = jnp.where(key <= qry, su, -jnp.inf)
                s_sc[c, u * tk:(u + 1) * tk, :] = su

    def softmax():
        for c in range(2):
            s = s_sc[c]
            m_prev = m_sc[c]
            m_new = jnp.maximum(m_prev, jnp.max(s, axis=0, keepdims=True))
            alpha_sc[c] = jnp.exp2(m_prev - m_new)
            m_sc[c] = m_new
            p_sc[c] = jnp.exp2(s - m_new).astype(BF16)

    def fold(j):
        for c in range(2):
            pv = None
            for u in range(n_sub):
                d = _dot(vt_ref[j * n_sub + u], p_sc[c, u * tk:(u + 1) * tk, :])
                pv = d if pv is None else pv + d
            acc_sc[c] = alpha_sc[c] * acc_sc[c] + pv

    def iteration(t, next_scores):
        fold(jnp.maximum(t - 1, 0))
        softmax()
        if next_scores is not None:
            scores(t + 1, diagonal=next_scores == "diagonal")

    def steady(t, carry):
        iteration(t, "full")
        return carry

    @pl.when(qi == 0)
    def _():
        scores(0, diagonal=True)

    @pl.when(qi > 0)
    def _():
        scores(0, diagonal=False)

    lax.fori_loop(0, qi - 1, steady, 0)

    @pl.when(qi > 0)
    def _():
        iteration(qi - 1, "diagonal")

    iteration(qi, None)
    fold(qi)

    lp = lam_ref[...]
    lam = (jnp.exp(jnp.sum(lp[0:1] * lp[1:2], axis=-1, keepdims=True))
           - jnp.exp(jnp.sum(lp[2:3] * lp[3:4], axis=-1, keepdims=True)) + lambda_init)
    a0 = acc_sc[0]
    a1 = acc_sc[1]
    o = (a0[0:DA_V_DIM] / a0[DA_V_DIM:DA_V_DIM + 1]
         - lam * (a1[0:DA_V_DIM] / a1[DA_V_DIM:DA_V_DIM + 1]))
    o = o * lax.rsqrt(jnp.mean(o * o, axis=0, keepdims=True) + RMS_EPS) * sg_ref[...]
    o_ref[...] = (o * (1.0 - lambda_init)).T.astype(o_ref.dtype)


def _diff_attention(qt, k, vt, lam_p, subln_col, lambda_init, bsz, seq):
    tq = ATTN_Q_TILE
    tk = ATTN_TILE
    nq = seq // tq
    nk = seq // tk
    return pl.pallas_call(
        functools.partial(_attn_kernel, lambda_init=lambda_init),
        grid=(bsz, DA_HEADS, nq),
        in_specs=[pl.BlockSpec(lam_p.shape, lambda b, h, i: (0, 0)),
                  pl.BlockSpec(subln_col.shape, lambda b, h, i: (0, 0)),
                  pl.BlockSpec((LANES, tq), lambda b, h, i: (h, b * nq + i)),
                  pl.BlockSpec((seq, LANES), lambda b, h, i: (b, h)),
                  pl.BlockSpec((None, nk, VT_ROWS, tk), lambda b, h, i: (h, b, 0, 0))],
        out_specs=pl.BlockSpec((tq, LANES), lambda b, h, i: (b * nq + i, h)),
        out_shape=jax.ShapeDtypeStruct(k.shape, BF16),
        scratch_shapes=[pltpu.VMEM((2, LANES, tq), BF16), pltpu.VMEM((2, 1, tq), F32),
                        pltpu.VMEM((2, 1, tq), F32), pltpu.VMEM((2, VT_ROWS, tq), F32),
                        pltpu.VMEM((2, tq, tq), F32), pltpu.VMEM((2, tq, tq), BF16)],
        compiler_params=pltpu.CompilerParams(
            dimension_semantics=("arbitrary", "arbitrary", "arbitrary"),
            vmem_limit_bytes=VMEM_LIMIT),
        name="l0_diff_attention",
    )(lam_p, subln_col, qt, k, vt)


def _l0_post_kernel(x_ref, attn_ref, conv_ref, p_ref, wo_ref, gffn_ref, w1_ref, w3_ref, w2_ref,
                    gple_ref, wg_ref, wp_ref, o_ref, acc_sc, hn_sc, h_sc):
    h = (x_ref[...] + _dot(attn_ref[...], wo_ref[0:DA_WIDTH, :])
         + _dot(conv_ref[...], wo_ref[DA_WIDTH:DA_WIDTH + SC_WIDTH, :]))
    hn_sc[...] = _rms(h, gffn_ref[...]).astype(BF16)
    h_sc[...] = h
    acc_sc[...] = jnp.zeros(acc_sc.shape, F32)
    _swiglu_chunks(hn_sc, w1_ref, w3_ref, w2_ref, acc_sc)
    o_ref[...] = _ple(h_sc[...] + acc_sc[...], p_ref, gple_ref, wg_ref, wp_ref)


def _l0_post(x2, attn, conv, p0, wo, gffn, w1, w3, w2, gple, wg, wp):
    n = x2.shape[0]
    tm = ROW_TILE
    return pl.pallas_call(
        _l0_post_kernel,
        grid=(n // tm,),
        in_specs=[_rows(tm, D_MODEL), _rows(tm, DA_WIDTH), _rows(tm, SC_WIDTH), _rows(tm, PLE_DIM),
                  _resident(wo.shape), _resident(gffn.shape), _resident(w1.shape),
                  _resident(w3.shape), _resident(w2.shape), _resident(gple.shape),
                  _resident(wg.shape), _resident(wp.shape)],
        out_specs=_rows(tm, D_MODEL),
        out_shape=jax.ShapeDtypeStruct((n, D_MODEL), F32),
        scratch_shapes=[pltpu.VMEM((tm, D_MODEL), F32), pltpu.VMEM((tm, D_MODEL), BF16),
                        pltpu.VMEM((tm, D_MODEL), F32)],
        compiler_params=pltpu.CompilerParams(dimension_semantics=("arbitrary",),
                                             vmem_limit_bytes=VMEM_LIMIT),
        name="l0_outproj_swiglu_ple",
    )(x2, attn, conv, p0, wo, gffn, w1, w3, w2, gple, wg, wp)


def _split_bf16(a):
    hi = a.astype(BF16)
    return hi, (a - hi.astype(F32)).astype(BF16)


def _l1_mix_kernel(h_ref, gmix_ref, win_ref, lng_ref, lnb_ref, ws_ref, bs_ref, wout_ref,
                   gffn_ref, wr_ref,
                   h1_ref, hn2_ref, ridx_ref, rgate_ref, hn_sc, u_sc, v_sc, vn_sc, gated_sc):
    tm = h_ref.shape[0]
    hn_sc[...] = _rms(h_ref[...], gmix_ref[...]).astype(BF16)

    def gelu(z):
        return 0.5 * z * (1.0 + lax.erf(z * (2.0 ** -0.5)))

    for dst, base in ((v_sc, GM_WIDTH), (u_sc, 0)):
        for c in range(GM_WIDTH // FF_CHUNK):
            cols = slice(c * FF_CHUNK, (c + 1) * FF_CHUNK)
            dst[:, cols] = gelu(_dot(hn_sc[...], win_ref[:, base + c * FF_CHUNK:base + (c + 1) * FF_CHUNK]))

    vv = v_sc[...]
    mu = jnp.mean(vv, axis=-1, keepdims=True)
    vc = vv - mu
    var = jnp.mean(vc * vc, axis=-1, keepdims=True)
    vn_sc[...] = (vc * lax.rsqrt(var + LN_EPS) * lng_ref[...] + lnb_ref[...]).astype(BF16)

    n_chunks = tm // GM_CHUNK
    row = lax.broadcasted_iota(jnp.int32, (GM_CHUNK, GM_CHUNK), 0)
    col = lax.broadcasted_iota(jnp.int32, (GM_CHUNK, GM_CHUNK), 1)
    for g in range(GM_GROUPS):
        gs = slice(g * LANES, (g + 1) * LANES)
        wm = jnp.where(col <= row, ws_ref[g], 0.0).astype(BF16)
        rhs = jnp.concatenate(
            [vn_sc[r * GM_CHUNK:(r + 1) * GM_CHUNK, gs] for r in range(n_chunks)], axis=1)
        vs = _dot(wm, rhs) + bs_ref[:, g:g + 1]
        for r in range(n_chunks):
            rs = slice(r * GM_CHUNK, (r + 1) * GM_CHUNK)
            gated_sc[rs, gs] = (u_sc[rs, gs] * vs[:, r * LANES:(r + 1) * LANES]).astype(BF16)

    h1 = h_ref[...] + _dot(gated_sc[...], wout_ref[...])
    h1_ref[...] = h1

    hn2 = _rms(h1, gffn_ref[...])
    hn2_ref[...] = hn2
    x_hi, x_lo = _split_bf16(hn2)
    a_hi = _dot(x_hi, wr_ref[...])
    logits = a_hi + pltpu.roll(a_hi, LANES - N_EXPERTS, 1) + _dot(x_lo, wr_ref[...])
    lane = lax.broadcasted_iota(jnp.int32, logits.shape, 1)
    logits = jnp.where(lane < N_EXPERTS, logits, -jnp.inf)
    v1 = jnp.max(logits, axis=-1, keepdims=True)
    i1 = jnp.min(jnp.where(logits == v1, lane, LANES), axis=-1, keepdims=True)
    rest = jnp.where(lane == i1, -jnp.inf, logits)
    v2 = jnp.max(rest, axis=-1, keepdims=True)
    i2 = jnp.min(jnp.where(rest == v2, lane, LANES), axis=-1, keepdims=True)
    e2 = jnp.exp(v2 - v1)
    ridx_ref[:, 0:1] = i1
    ridx_ref[:, 1:2] = i2
    rgate_ref[:, 0:1] = 1.0 / (1.0 + e2)
    rgate_ref[:, 1:2] = e2 / (1.0 + e2)


def _l1_mix(h, gmix, win, lng, lnb, ws, bs_t, wout, gffn, wr):
    n = h.shape[0]
    tm = ROW_TILE
    return pl.pallas_call(
        _l1_mix_kernel,
        grid=(n // tm,),
        in_specs=[_rows(tm, D_MODEL)] + [_resident(a.shape) for a in
                                         (gmix, win, lng, lnb, ws, bs_t, wout, gffn, wr)],
        out_specs=[_rows(tm, D_MODEL), _rows(tm, D_MODEL), _rows(tm, TOP_K), _rows(tm, TOP_K)],
        out_shape=[jax.ShapeDtypeStruct((n, D_MODEL), F32), jax.ShapeDtypeStruct((n, D_MODEL), F32),
                   jax.ShapeDtypeStruct((n, TOP_K), jnp.int32),
                   jax.ShapeDtypeStruct((n, TOP_K), F32)],
        scratch_shapes=[pltpu.VMEM((tm, D_MODEL), BF16), pltpu.VMEM((tm, GM_WIDTH), F32),
                        pltpu.VMEM((tm, GM_WIDTH), F32), pltpu.VMEM((tm, GM_WIDTH), BF16),
                        pltpu.VMEM((tm, GM_WIDTH), BF16)],
        compiler_params=pltpu.CompilerParams(dimension_semantics=("arbitrary",),
                                             vmem_limit_bytes=VMEM_LIMIT),
        name="l1_gmlp_router",
    )(h, gmix, win, lng, lnb, ws, bs_t, wout, gffn, wr)


def _moe_kernel(bexp_ref, nvalid_ref, tok_ref, dst_ref, hn_hbm, w1_ref, w3_ref, w2_ref, y_hbm,
                xbuf, xb_sc, acc_sc, ybuf, gsem, ssem, fence_sem, fence_sink):
    i = pl.program_id(0)
    n_valid = nvalid_ref[0]
    tm = MOE_TILE
    prev = jnp.maximum(i - 1, 0)

    def gather_row(blk, r):
        return pltpu.make_async_copy(hn_hbm.at[pl.ds(tok_ref[blk * tm + r], 1)],
                                     xbuf.at[pl.ds(r, 1)], gsem)

    def scatter_row(blk, r):
        return pltpu.make_async_copy(ybuf.at[pl.ds(r, 1)],
                                     y_hbm.at[pl.ds(dst_ref[blk * tm + r], 1)], ssem)

    def gather_wait():
        pltpu.make_async_copy(hn_hbm.at[pl.ds(0, tm)], xbuf, gsem).wait()

    def scatter_wait():
        pltpu.make_async_copy(ybuf, y_hbm.at[pl.ds(0, tm)], ssem).wait()

    @pl.when(i < n_valid)
    def _():
        @pl.when(i == 0)
        def _():
            def issue(r, carry):
                gather_row(0, r).start()
                return carry
            lax.fori_loop(0, tm, issue, 0)

        gather_wait()
        xb_sc[...] = xbuf[...].astype(BF16)
        acc_sc[...] = jnp.zeros(acc_sc.shape, F32)
        has_next = i + 1 < n_valid
        has_prev = i >= 1

        def copies(c):
            if c not in MOE_FENCE_CHUNKS:
                return
            g = MOE_FENCE_CHUNKS.index(c)
            if g > 0:
                fence_sink[0] = pl.semaphore_read(fence_sem)
            if g == len(MOE_FENCE_CHUNKS) - 1:
                return
            for r in range(MOE_COPY_ROWS[g], MOE_COPY_ROWS[g + 1]):
                @pl.when(has_next)
                def _():
                    gather_row(i + 1, r).start()

                @pl.when(has_prev)
                def _():
                    scatter_row(prev, r).start()

        _swiglu_chunks(xb_sc, w1_ref, w3_ref, w2_ref, acc_sc, per_chunk=copies)

        @pl.when(has_prev)
        def _():
            scatter_wait()

        ybuf[...] = acc_sc[...]

    @pl.when(i == n_valid)
    def _():
        def issue(r, carry):
            scatter_row(prev, r).start()
            return carry
        lax.fori_loop(0, tm, issue, 0)
        scatter_wait()
        ybuf[...] = jnp.zeros(ybuf.shape, F32)
        plane = y_hbm.shape[0] // TOP_K
        for b in range(TOP_K):
            spare = pltpu.make_async_copy(ybuf, y_hbm.at[pl.ds((b + 1) * plane - tm, tm)], ssem)
            spare.start()
            spare.wait()


def _moe_experts(block_expert, n_valid, row_token, row_dst, hn2, w1, w3, w2, n_out_rows):
    n_blocks = block_expert.shape[0]
    tm = MOE_TILE
    per_expert = lambda w: pl.BlockSpec((None,) + w.shape[1:], lambda i, be, *_: (be[i], 0, 0))
    return pl.pallas_call(
        _moe_kernel,
        grid_spec=pltpu.PrefetchScalarGridSpec(
            num_scalar_prefetch=4,
            grid=(n_blocks,),
            in_specs=[pl.BlockSpec(memory_space=pl.ANY), per_expert(w1), per_expert(w3),
                      per_expert(w2)],
            out_specs=pl.BlockSpec(memory_space=pl.ANY),
            scratch_shapes=[pltpu.VMEM((tm, D_MODEL), F32), pltpu.VMEM((tm, D_MODEL), BF16),
                            pltpu.VMEM((tm, D_MODEL), F32), pltpu.VMEM((tm, D_MODEL), F32),
                            pltpu.SemaphoreType.DMA(()), pltpu.SemaphoreType.DMA(()),
                            pltpu.SemaphoreType.REGULAR(()), pltpu.SMEM((1,), jnp.int32)],
        ),
        out_shape=jax.ShapeDtypeStruct((n_out_rows, D_MODEL), F32),
        compiler_params=pltpu.CompilerParams(dimension_semantics=("arbitrary",),
                                             vmem_limit_bytes=VMEM_LIMIT),
        name="l1_expert_swiglu",
    )(block_expert, n_valid, row_token, row_dst, hn2, w1, w3, w2)


def _combine_kernel(h_ref, y0_ref, y1_ref, gate_ref, p_ref, gple_ref, wg_ref, wp_ref, gfin_ref,
                    o_ref):
    gate = gate_ref[...]
    h = h_ref[...] + (y0_ref[...] * gate[:, 0:1] + y1_ref[...] * gate[:, 1:2])
    h = _ple(h, p_ref, gple_ref, wg_ref, wp_ref)
    o_ref[...] = _rms(h, gfin_ref[...])


def _combine(h1, y_rows, rgate, p1, gple, wg, wp, gfin):
    n = h1.shape[0]
    tm = ROW_TILE
    plane_tiles = y_rows.shape[0] // TOP_K // tm
    return pl.pallas_call(
        _combine_kernel,
        grid=(n // tm,),
        in_specs=[_rows(tm, D_MODEL), _rows(tm, D_MODEL),
                  pl.BlockSpec((tm, D_MODEL), lambda i: (plane_tiles + i, 0)), _rows(tm, TOP_K),
                  _rows(tm, PLE_DIM), _resident(gple.shape), _resident(wg.shape),
                  _resident(wp.shape), _resident(gfin.shape)],
        out_specs=_rows(tm, D_MODEL),
        out_shape=jax.ShapeDtypeStruct((n, D_MODEL), F32),
        compiler_params=pltpu.CompilerParams(dimension_semantics=("arbitrary",),
                                             vmem_limit_bytes=VMEM_LIMIT),
        name="l1_combine_ple_norm",
    )(h1, y_rows, y_rows, rgate, p1, gple, wg, wp, gfin)


def _routing_tables(ridx, n_blocks):
    n_assign = ridx.size
    e_flat = ridx.reshape(-1)
    onehot = (e_flat[:, None] == jnp.arange(N_EXPERTS, dtype=jnp.int32)[None, :]).astype(jnp.int32)
    csum = jnp.cumsum(onehot, axis=0)
    counts = csum[-1]
    rank = jnp.sum(csum * onehot, axis=1) - 1
    padded = (counts + MOE_TILE - 1) // MOE_TILE * MOE_TILE
    pad_ends = jnp.cumsum(padded)
    pad_starts = pad_ends - padded
    dest = (jnp.sum(pad_starts[None, :] * onehot, axis=1) + rank).astype(jnp.int32)
    n_rows = (n_blocks + 2) * MOE_TILE
    assign = jnp.full((n_rows,), -1, jnp.int32).at[dest].set(jnp.arange(n_assign, dtype=jnp.int32))
    row_token = jnp.maximum(assign, 0) // TOP_K
    plane = n_assign // TOP_K + MOE_TILE
    r = jnp.arange(n_rows, dtype=jnp.int32)
    spare = (r // MOE_TILE % 2) * plane + n_assign // TOP_K + r % MOE_TILE
    row_dst = jnp.where(assign >= 0, (assign % TOP_K) * plane + assign // TOP_K, spare)
    block_start = jnp.arange(n_blocks, dtype=jnp.int32) * MOE_TILE
    block_expert = jnp.minimum(
        jnp.sum((block_start[:, None] >= pad_ends[None, :]).astype(jnp.int32), axis=1),
        N_EXPERTS - 1).astype(jnp.int32)
    n_valid = (pad_ends[-1:] // MOE_TILE).astype(jnp.int32)
    return row_token, row_dst, block_expert, n_valid


def kernel(x, p, positions, ln_mix, ln_ffn, ln_ple, a_w_in, a_lambda, a_subln, a_conv_w, a_w_out,
           ffn_w1, ffn_w3, ffn_w2, c_w_in, c_ln_g, c_ln_b, c_w_s, c_b_s, c_w_out, router_w,
           moe_w1, moe_w3, moe_w2, ple_gate, ple_proj, final_norm):
    bsz, seq, _ = x.shape
    n = bsz * seq
    assert seq % ROW_TILE == 0 and seq % ATTN_TILE == 0 and ROW_TILE % GM_CHUNK == 0
    x2 = x.reshape(n, D_MODEL)
    pos2 = positions.reshape(n, 1)
    row = lambda a: a.reshape(1, -1)

    inv_freq = ROPE_THETA ** (-jnp.arange(0, ROT_DIM, 2, dtype=F32) / ROT_DIM)
    lane = jnp.arange(LANES) % DA_HEAD_DIM
    freq = jnp.where(lane < ROT_DIM, inv_freq[lane % (ROT_DIM // 2)], 0.0).reshape(1, LANES)
    qt, k, vt, conv = _inproj(x2, pos2, row(ln_mix[0]), freq, a_conv_w[0].T, a_w_in[0].astype(BF16), seq)
    lambda_init = 0.8 - 0.6 * math.exp(-0.3 * 0)
    attn = _diff_attention(qt, k, vt, a_lambda[0], a_subln[0].reshape(-1, 1), lambda_init, bsz, seq)
    h = _l0_post(x2, attn, conv, p[0].reshape(n, PLE_DIM), a_w_out[0].astype(BF16), row(ln_ffn[0]),
                 ffn_w1[0].astype(BF16), ffn_w3[0].astype(BF16), ffn_w2[0].astype(BF16),
                 row(ln_ple[0]), ple_gate[0].astype(BF16), ple_proj[0].astype(BF16))

    wr = jnp.pad(jnp.concatenate(_split_bf16(router_w[0]), axis=1),
                 ((0, 0), (0, LANES - 2 * N_EXPERTS)))
    h1, hn2, ridx, rgate = _l1_mix(h, row(ln_mix[1]), c_w_in[0].astype(BF16), row(c_ln_g[0]),
                                   row(c_ln_b[0]), c_w_s[0], c_b_s[0].T, c_w_out[0].astype(BF16),
                                   row(ln_ffn[1]), wr)
    n_blocks = -(-(n * TOP_K + N_EXPERTS * (MOE_TILE - 1)) // MOE_TILE)
    row_token, row_dst, block_expert, n_valid = _routing_tables(ridx, n_blocks)
    y_rows = _moe_experts(block_expert, n_valid, row_token, row_dst, hn2, moe_w1[0].astype(BF16),
                          moe_w3[0].astype(BF16), moe_w2[0].astype(BF16), TOP_K * (n + MOE_TILE))
    out = _combine(h1, y_rows, rgate, p[1].reshape(n, PLE_DIM), row(ln_ple[1]),
                   ple_gate[1].astype(BF16), ple_proj[1].astype(BF16), row(final_norm))
    return out.reshape(bsz, seq, D_MODEL)
```

```python
import functools
import math

import jax
import jax.numpy as jnp
from jax import lax
from jax.experimental import pallas as pl
from jax.experimental.pallas import tpu as pltpu

F32 = jnp.float32
BF16 = jnp.bfloat16

D_MODEL = 1024
PLE_DIM = 256
DA_HEADS = 4
DA_HEAD_DIM = 64
DA_V_DIM = 2 * DA_HEAD_DIM
DA_WIDTH = DA_HEADS * DA_V_DIM
ROPE_THETA = 500000.0
ROT_DIM = DA_HEAD_DIM // 4
SC_WIDTH = 512
GM_WIDTH = D_MODEL
GM_GROUPS = 8
GM_CHUNK = 128
D_FF = 2816
N_EXPERTS = 8
TOP_K = 2
RMS_EPS = 1e-6
LN_EPS = 1e-5

LANES = 128
FF_CHUNK = 256
N_FF_CHUNKS = D_FF // FF_CHUNK
ROW_TILE = 512
ATTN_TILE = 256
ATTN_Q_TILE = 512
Q_SCALE = DA_HEAD_DIM ** -0.5 * math.log2(math.e)
VT_ROWS = DA_V_DIM + 16
MOE_TILE = 512
MOE_FENCE_CHUNKS = (0, 3, 6, 8, 10)
MOE_COPY_ROWS = (0, 168, 336, 504, 512)
CONV_HALO = 8
VMEM_LIMIT = 56 * 2**20


def _resident(shape):
    nd = len(shape)
    return pl.BlockSpec(shape, lambda *_: (0,) * nd, pipeline_mode=pl.Buffered(1))


def _rows(tile, width):
    return pl.BlockSpec((tile, width), lambda i, *_: (i, 0))


def _rms(x, g):
    return x * lax.rsqrt(jnp.mean(x * x, axis=-1, keepdims=True) + RMS_EPS) * g


def _dot(a, b):
    return jnp.dot(a, b, preferred_element_type=F32)


def _swiglu_chunks(hn_ref, w1_ref, w3_ref, w2_ref, acc_ref, per_chunk=None):
    for c in range(N_FF_CHUNKS):
        cols = slice(c * FF_CHUNK, (c + 1) * FF_CHUNK)
        g = _dot(hn_ref[...], w1_ref[:, cols])
        u = _dot(hn_ref[...], w3_ref[:, cols])
        a = (g * jax.nn.sigmoid(g) * u).astype(BF16)
        acc_ref[...] += _dot(a, w2_ref[cols, :])
        if per_chunk is not None:
            per_chunk(c)


def _ple(h, p_ref, gple_ref, wg_ref, wp_ref):
    gate = jax.nn.sigmoid(_dot(_rms(h, gple_ref[...]).astype(BF16), wg_ref[...]))
    return h + gate * _dot(p_ref[...].astype(BF16), wp_ref[...])


def _inproj_kernel(x_ref, pos_ref, g_ref, freq_ref, cw_ref, w_ref,
                   qt_ref, k_ref, vt_ref, conv_ref, cbuf, *, tiles_per_seq):
    i = pl.program_id(0)
    tm = x_ref.shape[0]
    hn = _rms(x_ref[...], g_ref[...]).astype(BF16)

    half = ROT_DIM // 2
    ang = freq_ref[...] * pos_ref[...].astype(F32)
    cos_t = jnp.cos(ang)
    sin_t = jnp.sin(ang)
    zero_t = jnp.zeros_like(ang)
    rest = DA_HEAD_DIM - ROT_DIM
    pattern = lambda a, b, fill: jnp.concatenate(
        [a, b, jnp.full((rest, tm), fill, F32)] * (LANES // DA_HEAD_DIM), axis=0).T
    cos = pattern(cos_t, cos_t, 1.0)
    sin_up = pattern(-sin_t, zero_t, 0.0)
    sin_dn = pattern(zero_t, sin_t, 0.0)

    def rope(t):
        return (t * cos + pltpu.roll(t, LANES - half, 1) * sin_up
                + pltpu.roll(t, half, 1) * sin_dn)

    zq = _dot(hn, w_ref[:, 0:DA_WIDTH])
    zk = _dot(hn, w_ref[:, DA_WIDTH:2 * DA_WIDTH])
    zv = _dot(hn, w_ref[:, 2 * DA_WIDTH:3 * DA_WIDTH])
    ones = jnp.ones((VT_ROWS - DA_V_DIM, ATTN_TILE), BF16)
    for hd in range(DA_HEADS):
        sl = slice(hd * LANES, (hd + 1) * LANES)
        qt_ref[sl, :] = (rope(zq[:, sl]) * Q_SCALE).T.astype(BF16)
        k_ref[:, sl] = rope(zk[:, sl]).astype(BF16)
        for u in range(tm // ATTN_TILE):
            vt_ref[hd, u, 0:DA_V_DIM, :] = zv[u * ATTN_TILE:(u + 1) * ATTN_TILE, sl].T.astype(BF16)
            vt_ref[hd, u, DA_V_DIM:VT_ROWS, :] = ones

    off = 3 * DA_WIDTH
    b_gate = _dot(hn, w_ref[:, off:off + SC_WIDTH])
    c_gate = _dot(hn, w_ref[:, off + SC_WIDTH:off + 2 * SC_WIDTH])
    hc = _dot(hn, w_ref[:, off + 2 * SC_WIDTH:off + 3 * SC_WIDTH])

    @pl.when(i % tiles_per_seq == 0)
    def _():
        cbuf[0:CONV_HALO, :] = jnp.zeros((CONV_HALO, SC_WIDTH), F32)

    @pl.when(i % tiles_per_seq != 0)
    def _():
        cbuf[0:CONV_HALO, :] = cbuf[tm:tm + CONV_HALO, :]

    ch = c_gate * hc
    cbuf[CONV_HALO:CONV_HALO + tm, :] = ch
    ch1 = cbuf[CONV_HALO - 1:CONV_HALO - 1 + tm, :]
    ch2 = cbuf[CONV_HALO - 2:CONV_HALO - 2 + tm, :]
    conv = b_gate * (cw_ref[2:3, :] * ch + cw_ref[1:2, :] * ch1 + cw_ref[0:1, :] * ch2)
    conv_ref[...] = conv.astype(BF16)


def _inproj(x2, pos3, g, freq, cw, w, seq):
    n = x2.shape[0]
    tm = ROW_TILE
    out = jax.ShapeDtypeStruct((n, DA_WIDTH), BF16)
    sub = tm // ATTN_TILE
    return pl.pallas_call(
        functools.partial(_inproj_kernel, tiles_per_seq=seq // tm),
        grid=(n // tm,),
        in_specs=[_rows(tm, D_MODEL), pl.BlockSpec((None, 1, tm), lambda i: (i, 0, 0)),
                  _resident(g.shape), _resident(freq.shape),
                  _resident(cw.shape), _resident(w.shape)],
        out_specs=[pl.BlockSpec((DA_WIDTH, tm), lambda i: (0, i)), _rows(tm, DA_WIDTH),
                   pl.BlockSpec((DA_HEADS, sub, VT_ROWS, ATTN_TILE), lambda i: (0, i, 0, 0)),
                   _rows(tm, SC_WIDTH)],
        out_shape=[jax.ShapeDtypeStruct((DA_WIDTH, n), BF16), out,
                   jax.ShapeDtypeStruct((DA_HEADS, n // ATTN_TILE, VT_ROWS, ATTN_TILE), BF16), out],
        scratch_shapes=[pltpu.VMEM((tm + CONV_HALO, SC_WIDTH), F32)],
        compiler_params=pltpu.CompilerParams(dimension_semantics=("arbitrary",),
                                             vmem_limit_bytes=VMEM_LIMIT),
        name="l0_inproj",
    )(x2, pos3, g, freq, cw, w)


def _attn_kernel(lam_ref, sg_ref, qt_ref, k_ref, vt_ref, o_ref,
                 qc_sc, m_sc, alpha_sc, acc_sc, s_sc, p_sc, *, lambda_init):
    qi = pl.program_id(2)
    tq = ATTN_Q_TILE
    tk = ATTN_TILE
    n_sub = tq // tk

    qt = qt_ref[...]
    dim = lax.broadcasted_iota(jnp.int32, (LANES, 1), 0)
    zero = jnp.zeros_like(qt)
    qc_sc[0] = jnp.where(dim < DA_HEAD_DIM, qt, zero)
    qc_sc[1] = jnp.where(dim >= DA_HEAD_DIM, qt, zero)
    m_sc[...] = jnp.full(m_sc.shape, -jnp.inf, F32)
    acc_sc[...] = jnp.zeros(acc_sc.shape, F32)
    alpha_sc[...] = jnp.ones(alpha_sc.shape, F32)
    p_sc[...] = jnp.zeros(p_sc.shape, BF16)

    def scores(j, diagonal):
        for c in range(2):
            for u in range(n_sub):
                ks = pl.multiple_of((j * n_sub + u) * tk, tk)
                su = _dot(k_ref[pl.ds(ks, tk), :], qc_sc[c])
                if diagonal:
                    key = lax.broadcasted_iota(jnp.int32, (tk, tq), 0) + u * tk
                    qry = lax.broadcasted_iota(jnp.int32, (tk, tq), 1)
                    su = jnp.where(key <= qry, su, -jnp.inf)
                s_sc[c, u * tk:(u + 1) * tk, :] = su

    def softmax():
        for c in range(2):
            s = s_sc[c]
            m_prev = m_sc[c]
            m_new = jnp.maximum(m_prev, jnp.max(s, axis=0, keepdims=True))
            alpha_sc[c] = jnp.exp2(m_prev - m_new)
            m_sc[c] = m_new
            p_sc[c] = jnp.exp2(s - m_new).astype(BF16)

    def fold(j):
        for c in range(2):
            pv = None
            for u in range(n_sub):
                d = _dot(vt_ref[j * n_sub + u], p_sc[c, u * tk:(u + 1) * tk, :])
                pv = d if pv is None else pv + d
            acc_sc[c] = alpha_sc[c] * acc_sc[c] + pv

    def iteration(t, next_scores):
        fold(jnp.maximum(t - 1, 0))
        softmax()
        if next_scores is not None:
            scores(t + 1, diagonal=next_scores == "diagonal")

    def steady(t, carry):
        iteration(t, "full")
        return carry

    @pl.when(qi == 0)
    def _():
        scores(0, diagonal=True)

    @pl.when(qi > 0)
    def _():
        scores(0, diagonal=False)

    lax.fori_loop(0, qi - 1, steady, 0)

    @pl.when(qi > 0)
    def _():
        iteration(qi - 1, "diagonal")

    iteration(qi, None)
    fold(qi)

    lp = lam_ref[...]
    lam = (jnp.exp(jnp.sum(lp[0:1] * lp[1:2], axis=-1, keepdims=True))
           - jnp.exp(jnp.sum(lp[2:3] * lp[3:4], axis=-1, keepdims=True)) + lambda_init)
    a0 = acc_sc[0]
    a1 = acc_sc[1]
    o = (a0[0:DA_V_DIM] / a0[DA_V_DIM:DA_V_DIM + 1]
         - lam * (a1[0:DA_V_DIM] / a1[DA_V_DIM:DA_V_DIM + 1]))
    o = o * lax.rsqrt(jnp.mean(o * o, axis=0, keepdims=True) + RMS_EPS) * sg_ref[...]
    o_ref[...] = (o * (1.0 - lambda_init)).T.astype(o_ref.dtype)


def _diff_attention(qt, k, vt, lam_p, subln_col, lambda_init, bsz, seq):
    tq = ATTN_Q_TILE
    tk = ATTN_TILE
    nq = seq // tq
    nk = seq // tk
    return pl.pallas_call(
        functools.partial(_attn_kernel, lambda_init=lambda_init),
        grid=(bsz, DA_HEADS, nq),
        in_specs=[pl.BlockSpec(lam_p.shape, lambda b, h, i: (0, 0)),
                  pl.BlockSpec(subln_col.shape, lambda b, h, i: (0, 0)),
                  pl.BlockSpec((LANES, tq), lambda b, h, i: (h, b * nq + i)),
                  pl.BlockSpec((seq, LANES), lambda b, h, i: (b, h)),
                  pl.BlockSpec((None, nk, VT_ROWS, tk), lambda b, h, i: (h, b, 0, 0))],
        out_specs=pl.BlockSpec((tq, LANES), lambda b, h, i: (b * nq + i, h)),
        out_shape=jax.ShapeDtypeStruct(k.shape, BF16),
        scratch_shapes=[pltpu.VMEM((2, LANES, tq), BF16), pltpu.VMEM((2, 1, tq), F32),
                        pltpu.VMEM((2, 1, tq), F32), pltpu.VMEM((2, VT_ROWS, tq), F32),
                        pltpu.VMEM((2, tq, tq), F32), pltpu.VMEM((2, tq, tq), BF16)],
        compiler_params=pltpu.CompilerParams(
            dimension_semantics=("arbitrary", "arbitrary", "arbitrary"),
            vmem_limit_bytes=VMEM_LIMIT),
        name="l0_diff_attention",
    )(lam_p, subln_col, qt, k, vt)


def _l0_post_kernel(x_ref, attn_ref, conv_ref, p_ref, wo_ref, gffn_ref, w1_ref, w3_ref, w2_ref,
                    gple_ref, wg_ref, wp_ref, o_ref, acc_sc, hn_sc, h_sc):
    h = (x_ref[...] + _dot(attn_ref[...], wo_ref[0:DA_WIDTH, :])
         + _dot(conv_ref[...], wo_ref[DA_WIDTH:DA_WIDTH + SC_WIDTH, :]))
    hn_sc[...] = _rms(h, gffn_ref[...]).astype(BF16)
    h_sc[...] = h
    acc_sc[...] = jnp.zeros(acc_sc.shape, F32)
    _swiglu_chunks(hn_sc, w1_ref, w3_ref, w2_ref, acc_sc)
    o_ref[...] = _ple(h_sc[...] + acc_sc[...], p_ref, gple_ref, wg_ref, wp_ref)


def _l0_post(x2, attn, conv, p0, wo, gffn, w1, w3, w2, gple, wg, wp):
    n = x2.shape[0]
    tm = ROW_TILE
    return pl.pallas_call(
        _l0_post_kernel,
        grid=(n // tm,),
        in_specs=[_rows(tm, D_MODEL), _rows(tm, DA_WIDTH), _rows(tm, SC_WIDTH), _rows(tm, PLE_DIM),
                  _resident(wo.shape), _resident(gffn.shape), _resident(w1.shape),
                  _resident(w3.shape), _resident(w2.shape), _resident(gple.shape),
                  _resident(wg.shape), _resident(wp.shape)],
        out_specs=_rows(tm, D_MODEL),
        out_shape=jax.ShapeDtypeStruct((n, D_MODEL), F32),
        scratch_shapes=[pltpu.VMEM((tm, D_MODEL), F32), pltpu.VMEM((tm, D_MODEL), BF16),
                        pltpu.VMEM((tm, D_MODEL), F32)],
        compiler_params=pltpu.CompilerParams(dimension_semantics=("arbitrary",),
                                             vmem_limit_bytes=VMEM_LIMIT),
        name="l0_outproj_swiglu_ple",
    )(x2, attn, conv, p0, wo, gffn, w1, w3, w2, gple, wg, wp)


def _split_bf16(a):
    hi = a.astype(BF16)
    return hi, (a - hi.astype(F32)).astype(BF16)


def _l1_mix_kernel(h_ref, gmix_ref, win_ref, lng_ref, lnb_ref, ws_ref, bs_ref, wout_ref,
                   gffn_ref, wr_ref,
                   h1_ref, hn2_ref, ridx_ref, rgate_ref, hn_sc, u_sc, v_sc, vn_sc, gated_sc):
    tm = h_ref.shape[0]
    hn_sc[...] = _rms(h_ref[...], gmix_ref[...]).astype(BF16)

    def gelu(z):
        return 0.5 * z * (1.0 + lax.erf(z * (2.0 ** -0.5)))

    for dst, base in ((v_sc, GM_WIDTH), (u_sc, 0)):
        for c in range(GM_WIDTH // FF_CHUNK):
            cols = slice(c * FF_CHUNK, (c + 1) * FF_CHUNK)
            dst[:, cols] = gelu(_dot(hn_sc[...], win_ref[:, base + c * FF_CHUNK:base + (c + 1) * FF_CHUNK]))

    vv = v_sc[...]
    mu = jnp.mean(vv, axis=-1, keepdims=True)
    vc = vv - mu
    var = jnp.mean(vc * vc, axis=-1, keepdims=True)
    vn_sc[...] = (vc * lax.rsqrt(var + LN_EPS) * lng_ref[...] + lnb_ref[...]).astype(BF16)

    n_chunks = tm // GM_CHUNK
    row = lax.broadcasted_iota(jnp.int32, (GM_CHUNK, GM_CHUNK), 0)
    col = lax.broadcasted_iota(jnp.int32, (GM_CHUNK, GM_CHUNK), 1)
    for g in range(GM_GROUPS):
        gs = slice(g * LANES, (g + 1) * LANES)
        wm = jnp.where(col <= row, ws_ref[g], 0.0).astype(BF16)
        rhs = jnp.concatenate(
            [vn_sc[r * GM_CHUNK:(r + 1) * GM_CHUNK, gs] for r in range(n_chunks)], axis=1)
        vs = _dot(wm, rhs) + bs_ref[:, g:g + 1]
        for r in range(n_chunks):
            rs = slice(r * GM_CHUNK, (r + 1) * GM_CHUNK)
            gated_sc[rs, gs] = (u_sc[rs, gs] * vs[:, r * LANES:(r + 1) * LANES]).astype(BF16)

    h1 = h_ref[...] + _dot(gated_sc[...], wout_ref[...])
    h1_ref[...] = h1

    hn2 = _rms(h1, gffn_ref[...])
    hn2_ref[...] = hn2
    x_hi, x_lo = _split_bf16(hn2)
    a_hi = _dot(x_hi, wr_ref[...])
    logits = a_hi + pltpu.roll(a_hi, LANES - N_EXPERTS, 1) + _dot(x_lo, wr_ref[...])
    lane = lax.broadcasted_iota(jnp.int32, logits.shape, 1)
    logits = jnp.where(lane < N_EXPERTS, logits, -jnp.inf)
    v1 = jnp.max(logits, axis=-1, keepdims=True)
    i1 = jnp.min(jnp.where(logits == v1, lane, LANES), axis=-1, keepdims=True)
    rest = jnp.where(lane == i1, -jnp.inf, logits)
    v2 = jnp.max(rest, axis=-1, keepdims=True)
    i2 = jnp.min(jnp.where(rest == v2, lane, LANES), axis=-1, keepdims=True)
    e2 = jnp.exp(v2 - v1)
    ridx_ref[:, 0:1] = i1
    ridx_ref[:, 1:2] = i2
    rgate_ref[:, 0:1] = 1.0 / (1.0 + e2)
    rgate_ref[:, 1:2] = e2 / (1.0 + e2)


def _l1_mix(h, gmix, win, lng, lnb, ws, bs_t, wout, gffn, wr):
    n = h.shape[0]
    tm = ROW_TILE
    return pl.pallas_call(
        _l1_mix_kernel,
        grid=(n // tm,),
        in_specs=[_rows(tm, D_MODEL)] + [_resident(a.shape) for a in
                                         (gmix, win, lng, lnb, ws, bs_t, wout, gffn, wr)],
        out_specs=[_rows(tm, D_MODEL), _rows(tm, D_MODEL), _rows(tm, TOP_K), _rows(tm, TOP_K)],
        out_shape=[jax.ShapeDtypeStruct((n, D_MODEL), F32), jax.ShapeDtypeStruct((n, D_MODEL), F32),
                   jax.ShapeDtypeStruct((n, TOP_K), jnp.int32),
                   jax.ShapeDtypeStruct((n, TOP_K), F32)],
        scratch_shapes=[pltpu.VMEM((tm, D_MODEL), BF16), pltpu.VMEM((tm, GM_WIDTH), F32),
                        pltpu.VMEM((tm, GM_WIDTH), F32), pltpu.VMEM((tm, GM_WIDTH), BF16),
                        pltpu.VMEM((tm, GM_WIDTH), BF16)],
        compiler_params=pltpu.CompilerParams(dimension_semantics=("arbitrary",),
                                             vmem_limit_bytes=VMEM_LIMIT),
        name="l1_gmlp_router",
    )(h, gmix, win, lng, lnb, ws, bs_t, wout, gffn, wr)


def _moe_kernel(bexp_ref, nvalid_ref, tok_ref, dst_ref, hn_hbm, w1_ref, w3_ref, w2_ref, y_hbm,
                xbuf, xb_sc, acc_sc, ybuf, gsem, ssem, fence_sem, fence_sink):
    i = pl.program_id(0)
    n_valid = nvalid_ref[0]
    tm = MOE_TILE
    prev = jnp.maximum(i - 1, 0)

    def gather_row(blk, r):
        return pltpu.make_async_copy(hn_hbm.at[pl.ds(tok_ref[blk * tm + r], 1)],
                                     xbuf.at[pl.ds(r, 1)], gsem)

    def scatter_row(blk, r):
        return pltpu.make_async_copy(ybuf.at[pl.ds(r, 1)],
                                     y_hbm.at[pl.ds(dst_ref[blk * tm + r], 1)], ssem)

    def gather_wait():
        pltpu.make_async_copy(hn_hbm.at[pl.ds(0, tm)], xbuf, gsem).wait()

    def scatter_wait():
        pltpu.make_async_copy(ybuf, y_hbm.at[pl.ds(0, tm)], ssem).wait()

    @pl.when(i < n_valid)
    def _():
        @pl.when(i == 0)
        def _():
            def issue(r, carry):
                gather_row(0, r).start()
                return carry
            lax.fori_loop(0, tm, issue, 0)

        gather_wait()
        xb_sc[...] = xbuf[...].astype(BF16)
        acc_sc[...] = jnp.zeros(acc_sc.shape, F32)
        has_next = i + 1 < n_valid
        has_prev = i >= 1

        def copies(c):
            if c not in MOE_FENCE_CHUNKS:
                return
            g = MOE_FENCE_CHUNKS.index(c)
            if g > 0:
                fence_sink[0] = pl.semaphore_read(fence_sem)
            if g == len(MOE_FENCE_CHUNKS) - 1:
                return
            for r in range(MOE_COPY_ROWS[g], MOE_COPY_ROWS[g + 1]):
                @pl.when(has_next)
                def _():
                    gather_row(i + 1, r).start()

                @pl.when(has_prev)
                def _():
                    scatter_row(prev, r).start(priority=r % 2)

        _swiglu_chunks(xb_sc, w1_ref, w3_ref, w2_ref, acc_sc, per_chunk=copies)

        @pl.when(has_prev)
        def _():
            scatter_wait()

        ybuf[...] = acc_sc[...]

    @pl.when(i == n_valid)
    def _():
        def issue(r, carry):
            scatter_row(prev, r).start()
            return carry
        lax.fori_loop(0, tm, issue, 0)
        scatter_wait()
        ybuf[...] = jnp.zeros(ybuf.shape, F32)
        plane = y_hbm.shape[0] // TOP_K
        for b in range(TOP_K):
            spare = pltpu.make_async_copy(ybuf, y_hbm.at[pl.ds((b + 1) * plane - tm, tm)], ssem)
            spare.start()
            spare.wait()


def _moe_experts(block_expert, n_valid, row_token, row_dst, hn2, w1, w3, w2, n_out_rows):
    n_blocks = block_expert.shape[0]
    tm = MOE_TILE
    per_expert = lambda w: pl.BlockSpec((None,) + w.shape[1:], lambda i, be, *_: (be[i], 0, 0))
    return pl.pallas_call(
        _moe_kernel,
        grid_spec=pltpu.PrefetchScalarGridSpec(
            num_scalar_prefetch=4,
            grid=(n_blocks,),
            in_specs=[pl.BlockSpec(memory_space=pl.ANY), per_expert(w1), per_expert(w3),
                      per_expert(w2)],
            out_specs=pl.BlockSpec(memory_space=pl.ANY),
            scratch_shapes=[pltpu.VMEM((tm, D_MODEL), F32), pltpu.VMEM((tm, D_MODEL), BF16),
                            pltpu.VMEM((tm, D_MODEL), F32), pltpu.VMEM((tm, D_MODEL), F32),
                            pltpu.SemaphoreType.DMA(()), pltpu.SemaphoreType.DMA(()),
                            pltpu.SemaphoreType.REGULAR(()), pltpu.SMEM((1,), jnp.int32)],
        ),
        out_shape=jax.ShapeDtypeStruct((n_out_rows, D_MODEL), F32),
        compiler_params=pltpu.CompilerParams(dimension_semantics=("arbitrary",),
                                             vmem_limit_bytes=VMEM_LIMIT),
        name="l1_expert_swiglu",
    )(block_expert, n_valid, row_token, row_dst, hn2, w1, w3, w2)


def _combine_kernel(h_ref, y0_ref, y1_ref, gate_ref, p_ref, gple_ref, wg_ref, wp_ref, gfin_ref,
                    o_ref):
    gate = gate_ref[...]
    h = h_ref[...] + (y0_ref[...] * gate[:, 0:1] + y1_ref[...] * gate[:, 1:2])
    h = _ple(h, p_ref, gple_ref, wg_ref, wp_ref)
    o_ref[...] = _rms(h, gfin_ref[...])


def _combine(h1, y_rows, rgate, p1, gple, wg, wp, gfin):
    n = h1.shape[0]
    tm = ROW_TILE
    plane_tiles = y_rows.shape[0] // TOP_K // tm
    return pl.pallas_call(
        _combine_kernel,
        grid=(n // tm,),
        in_specs=[_rows(tm, D_MODEL), _rows(tm, D_MODEL),
                  pl.BlockSpec((tm, D_MODEL), lambda i: (plane_tiles + i, 0)), _rows(tm, TOP_K),
                  _rows(tm, PLE_DIM), _resident(gple.shape), _resident(wg.shape),
                  _resident(wp.shape), _resident(gfin.shape)],
        out_specs=_rows(tm, D_MODEL),
        out_shape=jax.ShapeDtypeStruct((n, D_MODEL), F32),
        compiler_params=pltpu.CompilerParams(dimension_semantics=("arbitrary",),
                                             vmem_limit_bytes=VMEM_LIMIT),
        name="l1_combine_ple_norm",
    )(h1, y_rows, y_rows, rgate, p1, gple, wg, wp, gfin)


def _routing_tables(ridx, n_blocks):
    n_assign = ridx.size
    e_flat = ridx.reshape(-1)
    onehot = (e_flat[:, None] == jnp.arange(N_EXPERTS, dtype=jnp.int32)[None, :]).astype(jnp.int32)
    csum = jnp.cumsum(onehot, axis=0)
    counts = csum[-1]
    rank = jnp.sum(csum * onehot, axis=1) - 1
    padded = (counts + MOE_TILE - 1) // MOE_TILE * MOE_TILE
    pad_ends = jnp.cumsum(padded)
    pad_starts = pad_ends - padded
    dest = (jnp.sum(pad_starts[None, :] * onehot, axis=1) + rank).astype(jnp.int32)
    n_rows = (n_blocks + 2) * MOE_TILE
    assign = jnp.full((n_rows,), -1, jnp.int32).at[dest].set(jnp.arange(n_assign, dtype=jnp.int32))
    row_token = jnp.maximum(assign, 0) // TOP_K
    plane = n_assign // TOP_K + MOE_TILE
    r = jnp.arange(n_rows, dtype=jnp.int32)
    spare = (r // MOE_TILE % 2) * plane + n_assign // TOP_K + r % MOE_TILE
    row_dst = jnp.where(assign >= 0, (assign % TOP_K) * plane + assign // TOP_K, spare)
    block_start = jnp.arange(n_blocks, dtype=jnp.int32) * MOE_TILE
    block_expert = jnp.minimum(
        jnp.sum((block_start[:, None] >= pad_ends[None, :]).astype(jnp.int32), axis=1),
        N_EXPERTS - 1).astype(jnp.int32)
    n_valid = (pad_ends[-1:] // MOE_TILE).astype(jnp.int32)
    return row_token, row_dst, block_expert, n_valid


def kernel(x, p, positions, ln_mix, ln_ffn, ln_ple, a_w_in, a_lambda, a_subln, a_conv_w, a_w_out,
           ffn_w1, ffn_w3, ffn_w2, c_w_in, c_ln_g, c_ln_b, c_w_s, c_b_s, c_w_out, router_w,
           moe_w1, moe_w3, moe_w2, ple_gate, ple_proj, final_norm):
    bsz, seq, _ = x.shape
    n = bsz * seq
    assert seq % ROW_TILE == 0 and seq % ATTN_TILE == 0 and ROW_TILE % GM_CHUNK == 0
    x2 = x.reshape(n, D_MODEL)
    pos3 = positions.reshape(n // ROW_TILE, 1, ROW_TILE)
    row = lambda a: a.reshape(1, -1)

    freq = (ROPE_THETA ** (-jnp.arange(0, ROT_DIM, 2, dtype=F32) / ROT_DIM)).reshape(-1, 1)
    qt, k, vt, conv = _inproj(x2, pos3, row(ln_mix[0]), freq, a_conv_w[0].T, a_w_in[0].astype(BF16), seq)
    lambda_init = 0.8 - 0.6 * math.exp(-0.3 * 0)
    attn = _diff_attention(qt, k, vt, a_lambda[0], a_subln[0].reshape(-1, 1), lambda_init, bsz, seq)
    h = _l0_post(x2, attn, conv, p[0].reshape(n, PLE_DIM), a_w_out[0].astype(BF16), row(ln_ffn[0]),
                 ffn_w1[0].astype(BF16), ffn_w3[0].astype(BF16), ffn_w2[0].astype(BF16),
                 row(ln_ple[0]), ple_gate[0].astype(BF16), ple_proj[0].astype(BF16))

    wr = jnp.pad(jnp.concatenate(_split_bf16(router_w[0]), axis=1),
                 ((0, 0), (0, LANES - 2 * N_EXPERTS)))
    h1, hn2, ridx, rgate = _l1_mix(h, row(ln_mix[1]), c_w_in[0].astype(BF16), row(c_ln_g[0]),
                                   row(c_ln_b[0]), c_w_s[0], c_b_s[0].T, c_w_out[0].astype(BF16),
                                   row(ln_ffn[1]), wr)
    n_blocks = -(-(n * TOP_K + N_EXPERTS * (MOE_TILE - 1)) // MOE_TILE)
    row_token, row_dst, block_expert, n_valid = _routing_tables(ridx, n_blocks)
    y_rows = _moe_experts(block_expert, n_valid, row_token, row_dst, hn2, moe_w1[0].astype(BF16),
                          moe_w3[0].astype(BF16), moe_w2[0].astype(BF16), TOP_K * (n + MOE_TILE))
    out = _combine(h1, y_rows, rgate, p[1].reshape(n, PLE_DIM), row(ln_ple[1]),
                   ple_gate[1].astype(BF16), ple_proj[1].astype(BF16), row(final_norm))
    return out.reshape(bsz, seq, D_MODEL)
```

```python
import functools
import math

import jax
import jax.numpy as jnp
from jax import lax
from jax.experimental import pallas as pl
from jax.experimental.pallas import tpu as pltpu

F32 = jnp.float32
BF16 = jnp.bfloat16

D_MODEL = 1024
PLE_DIM = 256
DA_HEADS = 4
DA_HEAD_DIM = 64
DA_V_DIM = 2 * DA_HEAD_DIM
DA_WIDTH = DA_HEADS * DA_V_DIM
ROPE_THETA = 500000.0
ROT_DIM = DA_HEAD_DIM // 4
SC_WIDTH = 512
GM_WIDTH = D_MODEL
GM_GROUPS = 8
GM_CHUNK = 128
D_FF = 2816
N_EXPERTS = 8
TOP_K = 2
RMS_EPS = 1e-6
LN_EPS = 1e-5

LANES = 128
FF_CHUNK = 256
N_FF_CHUNKS = D_FF // FF_CHUNK
ROW_TILE = 512
ATTN_TILE = 256
ATTN_Q_TILE = 512
Q_SCALE = DA_HEAD_DIM ** -0.5 * math.log2(math.e)
VT_ROWS = DA_V_DIM + 16
MOE_TILE = 512
MOE_FENCE_CHUNKS = (0, 3, 6, 8, 10)
MOE_COPY_ROWS = (0, 168, 336, 504, 512)
CONV_HALO = 8
VMEM_LIMIT = 56 * 2**20


def _resident(shape):
    nd = len(shape)
    return pl.BlockSpec(shape, lambda *_: (0,) * nd, pipeline_mode=pl.Buffered(1))


def _rows(tile, width):
    return pl.BlockSpec((tile, width), lambda i, *_: (i, 0))


def _rms(x, g):
    return x * lax.rsqrt(jnp.mean(x * x, axis=-1, keepdims=True) + RMS_EPS) * g


def _dot(a, b):
    return jnp.dot(a, b, preferred_element_type=F32)


def _swiglu_chunks(hn_ref, w1_ref, w3_ref, w2_ref, acc_ref, per_chunk=None):
    for c in range(N_FF_CHUNKS):
        cols = slice(c * FF_CHUNK, (c + 1) * FF_CHUNK)
        g = _dot(hn_ref[...], w1_ref[:, cols])
        u = _dot(hn_ref[...], w3_ref[:, cols])
        a = (g * jax.nn.sigmoid(g) * u).astype(BF16)
        acc_ref[...] += _dot(a, w2_ref[cols, :])
        if per_chunk is not None:
            per_chunk(c)


def _ple(h, p_ref, gple_ref, wg_ref, wp_ref):
    gate = jax.nn.sigmoid(_dot(_rms(h, gple_ref[...]).astype(BF16), wg_ref[...]))
    return h + gate * _dot(p_ref[...].astype(BF16), wp_ref[...])


def _inproj_kernel(x_ref, pos_ref, g_ref, freq_ref, cw_ref, w_ref,
                   qt_ref, k_ref, vt_ref, conv_ref, cbuf, *, tiles_per_seq):
    i = pl.program_id(0)
    tm = x_ref.shape[0]
    hn = _rms(x_ref[...], g_ref[...]).astype(BF16)

    half = ROT_DIM // 2
    ang = freq_ref[...] * pos_ref[...].astype(F32)
    cos_t = jnp.cos(ang)
    sin_t = jnp.sin(ang)
    zero_t = jnp.zeros_like(ang)
    rest = DA_HEAD_DIM - ROT_DIM
    pattern = lambda a, b, fill: jnp.concatenate(
        [a, b, jnp.full((rest, tm), fill, F32)] * (LANES // DA_HEAD_DIM), axis=0).T
    cos = pattern(cos_t, cos_t, 1.0)
    sin_up = pattern(-sin_t, zero_t, 0.0)
    sin_dn = pattern(zero_t, sin_t, 0.0)

    def rope(t):
        return (t * cos + pltpu.roll(t, LANES - half, 1) * sin_up
                + pltpu.roll(t, half, 1) * sin_dn)

    zq = _dot(hn, w_ref[:, 0:DA_WIDTH])
    zk = _dot(hn, w_ref[:, DA_WIDTH:2 * DA_WIDTH])
    zv = _dot(hn, w_ref[:, 2 * DA_WIDTH:3 * DA_WIDTH])
    ones = jnp.ones((VT_ROWS - DA_V_DIM, ATTN_TILE), BF16)
    for hd in range(DA_HEADS):
        sl = slice(hd * LANES, (hd + 1) * LANES)
        qt_ref[sl, :] = (rope(zq[:, sl]) * Q_SCALE).T.astype(BF16)
        k_ref[:, sl] = rope(zk[:, sl]).astype(BF16)
        for u in range(tm // ATTN_TILE):
            vt_ref[hd, u, 0:DA_V_DIM, :] = zv[u * ATTN_TILE:(u + 1) * ATTN_TILE, sl].T.astype(BF16)
            vt_ref[hd, u, DA_V_DIM:VT_ROWS, :] = ones

    off = 3 * DA_WIDTH
    b_gate = _dot(hn, w_ref[:, off:off + SC_WIDTH])
    c_gate = _dot(hn, w_ref[:, off + SC_WIDTH:off + 2 * SC_WIDTH])
    hc = _dot(hn, w_ref[:, off + 2 * SC_WIDTH:off + 3 * SC_WIDTH])

    @pl.when(i % tiles_per_seq == 0)
    def _():
        cbuf[0:CONV_HALO, :] = jnp.zeros((CONV_HALO, SC_WIDTH), F32)

    @pl.when(i % tiles_per_seq != 0)
    def _():
        cbuf[0:CONV_HALO, :] = cbuf[tm:tm + CONV_HALO, :]

    ch = c_gate * hc
    cbuf[CONV_HALO:CONV_HALO + tm, :] = ch
    ch1 = cbuf[CONV_HALO - 1:CONV_HALO - 1 + tm, :]
    ch2 = cbuf[CONV_HALO - 2:CONV_HALO - 2 + tm, :]
    conv = b_gate * (cw_ref[2:3, :] * ch + cw_ref[1:2, :] * ch1 + cw_ref[0:1, :] * ch2)
    conv_ref[...] = conv.astype(BF16)


def _inproj(x2, pos3, g, freq, cw, w, seq):
    n = x2.shape[0]
    tm = ROW_TILE
    out = jax.ShapeDtypeStruct((n, DA_WIDTH), BF16)
    sub = tm // ATTN_TILE
    return pl.pallas_call(
        functools.partial(_inproj_kernel, tiles_per_seq=seq // tm),
        grid=(n // tm,),
        in_specs=[_rows(tm, D_MODEL), pl.BlockSpec((None, 1, tm), lambda i: (i, 0, 0)),
                  _resident(g.shape), _resident(freq.shape),
                  _resident(cw.shape), _resident(w.shape)],
        out_specs=[pl.BlockSpec((DA_WIDTH, tm), lambda i: (0, i)), _rows(tm, DA_WIDTH),
                   pl.BlockSpec((DA_HEADS, sub, VT_ROWS, ATTN_TILE), lambda i: (0, i, 0, 0)),
                   _rows(tm, SC_WIDTH)],
        out_shape=[jax.ShapeDtypeStruct((DA_WIDTH, n), BF16), out,
                   jax.ShapeDtypeStruct((DA_HEADS, n // ATTN_TILE, VT_ROWS, ATTN_TILE), BF16), out],
        scratch_shapes=[pltpu.VMEM((tm + CONV_HALO, SC_WIDTH), F32)],
        compiler_params=pltpu.CompilerParams(dimension_semantics=("arbitrary",),
                                             vmem_limit_bytes=VMEM_LIMIT),
        name="l0_inproj",
    )(x2, pos3, g, freq, cw, w)


def _attn_kernel(lam_ref, sg_ref, qt_ref, k_ref, vt_ref, o_ref,
                 qc_sc, m_sc, alpha_sc, acc_sc, s_sc, p_sc, *, lambda_init):
    qi = pl.program_id(2)
    tq = ATTN_Q_TILE
    tk = ATTN_TILE
    n_sub = tq // tk

    qt = qt_ref[...]
    dim = lax.broadcasted_iota(jnp.int32, (LANES, 1), 0)
    zero = jnp.zeros_like(qt)
    qc_sc[0] = jnp.where(dim < DA_HEAD_DIM, qt, zero)
    qc_sc[1] = jnp.where(dim >= DA_HEAD_DIM, qt, zero)
    m_sc[...] = jnp.full(m_sc.shape, -jnp.inf, F32)
    acc_sc[...] = jnp.zeros(acc_sc.shape, F32)
    alpha_sc[...] = jnp.ones(alpha_sc.shape, F32)
    p_sc[...] = jnp.zeros(p_sc.shape, BF16)

    def scores(j, diagonal):
        for c in range(2):
            for u in range(n_sub):
                ks = pl.multiple_of((j * n_sub + u) * tk, tk)
                su = _dot(k_ref[pl.ds(ks, tk), :], qc_sc[c])
                if diagonal:
                    key = lax.broadcasted_iota(jnp.int32, (tk, tq), 0) + u * tk
                    qry = lax.broadcasted_iota(jnp.int32, (tk, tq), 1)
                    su = jnp.where(key <= qry, su, -jnp.inf)
                s_sc[c, u * tk:(u + 1) * tk, :] = su

    def softmax():
        for c in range(2):
            s = s_sc[c]
            m_prev = m_sc[c]
            m_new = jnp.maximum(m_prev, jnp.max(s, axis=0, keepdims=True))
            alpha_sc[c] = jnp.exp2(m_prev - m_new)
            m_sc[c] = m_new
            p_sc[c] = jnp.exp2(s - m_new).astype(BF16)

    def fold(j):
        for c in range(2):
            pv = None
            for u in range(n_sub):
                d = _dot(vt_ref[j * n_sub + u], p_sc[c, u * tk:(u + 1) * tk, :])
                pv = d if pv is None else pv + d
            acc_sc[c] = alpha_sc[c] * acc_sc[c] + pv

    def iteration(t, next_scores):
        fold(jnp.maximum(t - 1, 0))
        softmax()
        if next_scores is not None:
            scores(t + 1, diagonal=next_scores == "diagonal")

    def steady(t, carry):
        iteration(t, "full")
        return carry

    @pl.when(qi == 0)
    def _():
        scores(0, diagonal=True)

    @pl.when(qi > 0)
    def _():
        scores(0, diagonal=False)

    lax.fori_loop(0, qi - 1, steady, 0)

    @pl.when(qi > 0)
    def _():
        iteration(qi - 1, "diagonal")

    iteration(qi, None)
    fold(qi)

    lp = lam_ref[...]
    lam = (jnp.exp(jnp.sum(lp[0:1] * lp[1:2], axis=-1, keepdims=True))
           - jnp.exp(jnp.sum(lp[2:3] * lp[3:4], axis=-1, keepdims=True)) + lambda_init)
    a0 = acc_sc[0]
    a1 = acc_sc[1]
    o = (a0[0:DA_V_DIM] / a0[DA_V_DIM:DA_V_DIM + 1]
         - lam * (a1[0:DA_V_DIM] / a1[DA_V_DIM:DA_V_DIM + 1]))
    o = o * lax.rsqrt(jnp.mean(o * o, axis=0, keepdims=True) + RMS_EPS) * sg_ref[...]
    o_ref[...] = (o * (1.0 - lambda_init)).T.astype(o_ref.dtype)


def _diff_attention(qt, k, vt, lam_p, subln_col, lambda_init, bsz, seq):
    tq = ATTN_Q_TILE
    tk = ATTN_TILE
    nq = seq // tq
    nk = seq // tk
    return pl.pallas_call(
        functools.partial(_attn_kernel, lambda_init=lambda_init),
        grid=(bsz, DA_HEADS, nq),
        in_specs=[pl.BlockSpec(lam_p.shape, lambda b, h, i: (0, 0)),
                  pl.BlockSpec(subln_col.shape, lambda b, h, i: (0, 0)),
                  pl.BlockSpec((LANES, tq), lambda b, h, i: (h, b * nq + i)),
                  pl.BlockSpec((seq, LANES), lambda b, h, i: (b, h)),
                  pl.BlockSpec((None, nk, VT_ROWS, tk), lambda b, h, i: (h, b, 0, 0))],
        out_specs=pl.BlockSpec((tq, LANES), lambda b, h, i: (b * nq + i, h)),
        out_shape=jax.ShapeDtypeStruct(k.shape, BF16),
        scratch_shapes=[pltpu.VMEM((2, LANES, tq), BF16), pltpu.VMEM((2, 1, tq), F32),
                        pltpu.VMEM((2, 1, tq), F32), pltpu.VMEM((2, VT_ROWS, tq), F32),
                        pltpu.VMEM((2, tq, tq), F32), pltpu.VMEM((2, tq, tq), BF16)],
        compiler_params=pltpu.CompilerParams(
            dimension_semantics=("arbitrary", "arbitrary", "arbitrary"),
            vmem_limit_bytes=VMEM_LIMIT),
        name="l0_diff_attention",
    )(lam_p, subln_col, qt, k, vt)


def _l0_post_kernel(x_ref, attn_ref, conv_ref, p_ref, wo_ref, gffn_ref, w1_ref, w3_ref, w2_ref,
                    gple_ref, wg_ref, wp_ref, o_ref, acc_sc, hn_sc, h_sc):
    h = (x_ref[...] + _dot(attn_ref[...], wo_ref[0:DA_WIDTH, :])
         + _dot(conv_ref[...], wo_ref[DA_WIDTH:DA_WIDTH + SC_WIDTH, :]))
    hn_sc[...] = _rms(h, gffn_ref[...]).astype(BF16)
    h_sc[...] = h
    acc_sc[...] = jnp.zeros(acc_sc.shape, F32)
    _swiglu_chunks(hn_sc, w1_ref, w3_ref, w2_ref, acc_sc)
    o_ref[...] = _ple(h_sc[...] + acc_sc[...], p_ref, gple_ref, wg_ref, wp_ref)


def _l0_post(x2, attn, conv, p0, wo, gffn, w1, w3, w2, gple, wg, wp):
    n = x2.shape[0]
    tm = ROW_TILE
    return pl.pallas_call(
        _l0_post_kernel,
        grid=(n // tm,),
        in_specs=[_rows(tm, D_MODEL), _rows(tm, DA_WIDTH), _rows(tm, SC_WIDTH), _rows(tm, PLE_DIM),
                  _resident(wo.shape), _resident(gffn.shape), _resident(w1.shape),
                  _resident(w3.shape), _resident(w2.shape), _resident(gple.shape),
                  _resident(wg.shape), _resident(wp.shape)],
        out_specs=_rows(tm, D_MODEL),
        out_shape=jax.ShapeDtypeStruct((n, D_MODEL), F32),
        scratch_shapes=[pltpu.VMEM((tm, D_MODEL), F32), pltpu.VMEM((tm, D_MODEL), BF16),
                        pltpu.VMEM((tm, D_MODEL), F32)],
        compiler_params=pltpu.CompilerParams(dimension_semantics=("arbitrary",),
                                             vmem_limit_bytes=VMEM_LIMIT),
        name="l0_outproj_swiglu_ple",
    )(x2, attn, conv, p0, wo, gffn, w1, w3, w2, gple, wg, wp)


def _split_bf16(a):
    hi = a.astype(BF16)
    return hi, (a - hi.astype(F32)).astype(BF16)


def _l1_mix_kernel(h_ref, gmix_ref, win_ref, lng_ref, lnb_ref, ws_ref, bs_ref, wout_ref,
                   gffn_ref, wr_ref,
                   h1_ref, hn2_ref, ridx_ref, rgate_ref, hn_sc, u_sc, v_sc, vn_sc, gated_sc):
    tm = h_ref.shape[0]
    hn_sc[...] = _rms(h_ref[...], gmix_ref[...]).astype(BF16)

    def gelu(z):
        return 0.5 * z * (1.0 + lax.erf(z * (2.0 ** -0.5)))

    for dst, base in ((v_sc, GM_WIDTH), (u_sc, 0)):
        for c in range(GM_WIDTH // FF_CHUNK):
            cols = slice(c * FF_CHUNK, (c + 1) * FF_CHUNK)
            dst[:, cols] = gelu(_dot(hn_sc[...], win_ref[:, base + c * FF_CHUNK:base + (c + 1) * FF_CHUNK]))

    vv = v_sc[...]
    mu = jnp.mean(vv, axis=-1, keepdims=True)
    vc = vv - mu
    var = jnp.mean(vc * vc, axis=-1, keepdims=True)
    vn_sc[...] = (vc * lax.rsqrt(var + LN_EPS) * lng_ref[...] + lnb_ref[...]).astype(BF16)

    n_chunks = tm // GM_CHUNK
    row = lax.broadcasted_iota(jnp.int32, (GM_CHUNK, GM_CHUNK), 0)
    col = lax.broadcasted_iota(jnp.int32, (GM_CHUNK, GM_CHUNK), 1)
    for g in range(GM_GROUPS):
        gs = slice(g * LANES, (g + 1) * LANES)
        wm = jnp.where(col <= row, ws_ref[g], 0.0).astype(BF16)
        rhs = jnp.concatenate(
            [vn_sc[r * GM_CHUNK:(r + 1) * GM_CHUNK, gs] for r in range(n_chunks)], axis=1)
        vs = _dot(wm, rhs) + bs_ref[:, g:g + 1]
        for r in range(n_chunks):
            rs = slice(r * GM_CHUNK, (r + 1) * GM_CHUNK)
            gated_sc[rs, gs] = (u_sc[rs, gs] * vs[:, r * LANES:(r + 1) * LANES]).astype(BF16)

    h1 = h_ref[...] + _dot(gated_sc[...], wout_ref[...])
    h1_ref[...] = h1

    hn2 = _rms(h1, gffn_ref[...])
    hn2_ref[...] = hn2
    x_hi, x_lo = _split_bf16(hn2)
    a_hi = _dot(x_hi, wr_ref[...])
    logits = a_hi + pltpu.roll(a_hi, LANES - N_EXPERTS, 1) + _dot(x_lo, wr_ref[...])
    lane = lax.broadcasted_iota(jnp.int32, logits.shape, 1)
    logits = jnp.where(lane < N_EXPERTS, logits, -jnp.inf)
    v1 = jnp.max(logits, axis=-1, keepdims=True)
    i1 = jnp.min(jnp.where(logits == v1, lane, LANES), axis=-1, keepdims=True)
    rest = jnp.where(lane == i1, -jnp.inf, logits)
    v2 = jnp.max(rest, axis=-1, keepdims=True)
    i2 = jnp.min(jnp.where(rest == v2, lane, LANES), axis=-1, keepdims=True)
    e2 = jnp.exp(v2 - v1)
    ridx_ref[:, 0:1] = i1
    ridx_ref[:, 1:2] = i2
    rgate_ref[:, 0:1] = 1.0 / (1.0 + e2)
    rgate_ref[:, 1:2] = e2 / (1.0 + e2)


def _l1_mix(h, gmix, win, lng, lnb, ws, bs_t, wout, gffn, wr):
    n = h.shape[0]
    tm = ROW_TILE
    return pl.pallas_call(
        _l1_mix_kernel,
        grid=(n // tm,),
        in_specs=[_rows(tm, D_MODEL)] + [_resident(a.shape) for a in
                                         (gmix, win, lng, lnb, ws, bs_t, wout, gffn, wr)],
        out_specs=[_rows(tm, D_MODEL), _rows(tm, D_MODEL), _rows(tm, TOP_K), _rows(tm, TOP_K)],
        out_shape=[jax.ShapeDtypeStruct((n, D_MODEL), F32), jax.ShapeDtypeStruct((n, D_MODEL), F32),
                   jax.ShapeDtypeStruct((n, TOP_K), jnp.int32),
                   jax.ShapeDtypeStruct((n, TOP_K), F32)],
        scratch_shapes=[pltpu.VMEM((tm, D_MODEL), BF16), pltpu.VMEM((tm, GM_WIDTH), F32),
                        pltpu.VMEM((tm, GM_WIDTH), F32), pltpu.VMEM((tm, GM_WIDTH), BF16),
                        pltpu.VMEM((tm, GM_WIDTH), BF16)],
        compiler_params=pltpu.CompilerParams(dimension_semantics=("arbitrary",),
                                             vmem_limit_bytes=VMEM_LIMIT),
        name="l1_gmlp_router",
    )(h, gmix, win, lng, lnb, ws, bs_t, wout, gffn, wr)


def _moe_kernel(bexp_ref, nvalid_ref, tok_ref, dst_ref, hn_hbm, w1_ref, w3_ref, w2_ref, y_hbm,
                xbuf, xb_sc, acc_sc, ybuf, gsem, ssem, fence_sem, fence_sink):
    i = pl.program_id(0)
    n_valid = nvalid_ref[0]
    tm = MOE_TILE
    prev = jnp.maximum(i - 1, 0)

    def gather_row(blk, r):
        return pltpu.make_async_copy(hn_hbm.at[pl.ds(tok_ref[blk * tm + r], 1)],
                                     xbuf.at[pl.ds(r, 1)], gsem)

    def scatter_row(blk, r):
        return pltpu.make_async_copy(ybuf.at[pl.ds(r, 1)],
                                     y_hbm.at[pl.ds(dst_ref[blk * tm + r], 1)], ssem)

    def gather_wait():
        pltpu.make_async_copy(hn_hbm.at[pl.ds(0, tm)], xbuf, gsem).wait()

    def scatter_wait():
        pltpu.make_async_copy(ybuf, y_hbm.at[pl.ds(0, tm)], ssem).wait()

    @pl.when(i < n_valid)
    def _():
        @pl.when(i == 0)
        def _():
            def issue(r, carry):
                gather_row(0, r).start()
                return carry
            lax.fori_loop(0, tm, issue, 0)

        gather_wait()
        xb_sc[...] = xbuf[...].astype(BF16)
        acc_sc[...] = jnp.zeros(acc_sc.shape, F32)
        has_next = i + 1 < n_valid
        has_prev = i >= 1

        def copies(c):
            if c not in MOE_FENCE_CHUNKS:
                return
            g = MOE_FENCE_CHUNKS.index(c)
            if g > 0:
                fence_sink[0] = pl.semaphore_read(fence_sem)
            if g == len(MOE_FENCE_CHUNKS) - 1:
                return
            for r in range(MOE_COPY_ROWS[g], MOE_COPY_ROWS[g + 1]):
                @pl.when(has_next)
                def _():
                    gather_row(i + 1, r).start()

                @pl.when(has_prev)
                def _():
                    scatter_row(prev, r).start(priority=r % 2)

        _swiglu_chunks(xb_sc, w1_ref, w3_ref, w2_ref, acc_sc, per_chunk=copies)

        @pl.when(has_prev)
        def _():
            scatter_wait()

        ybuf[...] = acc_sc[...]

    @pl.when(i == n_valid)
    def _():
        def issue(r, carry):
            scatter_row(prev, r).start()
            return carry
        lax.fori_loop(0, tm, issue, 0)
        scatter_wait()
        ybuf[...] = jnp.zeros(ybuf.shape, F32)
        plane = y_hbm.shape[0] // TOP_K
        for b in range(TOP_K):
            spare = pltpu.make_async_copy(ybuf, y_hbm.at[pl.ds((b + 1) * plane - tm, tm)], ssem)
            spare.start()
            spare.wait()


def _moe_experts(block_expert, n_valid, row_token, row_dst, hn2, w1, w3, w2, n_out_rows):
    n_blocks = block_expert.shape[0]
    tm = MOE_TILE
    per_expert = lambda w: pl.BlockSpec((None,) + w.shape[1:], lambda i, be, *_: (be[i], 0, 0))
    return pl.pallas_call(
        _moe_kernel,
        grid_spec=pltpu.PrefetchScalarGridSpec(
            num_scalar_prefetch=4,
            grid=(n_blocks,),
            in_specs=[pl.BlockSpec(memory_space=pl.ANY), per_expert(w1), per_expert(w3),
                      per_expert(w2)],
            out_specs=pl.BlockSpec(memory_space=pl.ANY),
            scratch_shapes=[pltpu.VMEM((tm, D_MODEL), F32), pltpu.VMEM((tm, D_MODEL), BF16),
                            pltpu.VMEM((tm, D_MODEL), F32), pltpu.VMEM((tm, D_MODEL), F32),
                            pltpu.SemaphoreType.DMA(()), pltpu.SemaphoreType.DMA(()),
                            pltpu.SemaphoreType.REGULAR(()), pltpu.SMEM((1,), jnp.int32)],
        ),
        out_shape=jax.ShapeDtypeStruct((n_out_rows, D_MODEL), F32),
        compiler_params=pltpu.CompilerParams(dimension_semantics=("arbitrary",),
                                             vmem_limit_bytes=VMEM_LIMIT),
        name="l1_expert_swiglu",
    )(block_expert, n_valid, row_token, row_dst, hn2, w1, w3, w2)


def _combine_kernel(h_ref, y0_ref, y1_ref, gate_ref, p_ref, gple_ref, wg_ref, wp_ref, gfin_ref,
                    o_ref):
    gate = gate_ref[...]
    h = h_ref[...] + (y0_ref[...] * gate[:, 0:1] + y1_ref[...] * gate[:, 1:2])
    h = _ple(h, p_ref, gple_ref, wg_ref, wp_ref)
    o_ref[...] = _rms(h, gfin_ref[...])


def _combine(h1, y_rows, rgate, p1, gple, wg, wp, gfin):
    n = h1.shape[0]
    tm = ROW_TILE
    plane_tiles = y_rows.shape[0] // TOP_K // tm
    return pl.pallas_call(
        _combine_kernel,
        grid=(n // tm,),
        in_specs=[_rows(tm, D_MODEL), _rows(tm, D_MODEL),
                  pl.BlockSpec((tm, D_MODEL), lambda i: (plane_tiles + i, 0)), _rows(tm, TOP_K),
                  _rows(tm, PLE_DIM), _resident(gple.shape), _resident(wg.shape),
                  _resident(wp.shape), _resident(gfin.shape)],
        out_specs=_rows(tm, D_MODEL),
        out_shape=jax.ShapeDtypeStruct((n, D_MODEL), F32),
        compiler_params=pltpu.CompilerParams(dimension_semantics=("arbitrary",),
                                             vmem_limit_bytes=VMEM_LIMIT),
        name="l1_combine_ple_norm",
    )(h1, y_rows, y_rows, rgate, p1, gple, wg, wp, gfin)


def _routing_tables(ridx, n_blocks):
    n_assign = ridx.size
    e_flat = ridx.reshape(-1)
    experts = jnp.arange(N_EXPERTS, dtype=jnp.int32)[None, :]
    order = jnp.argsort(e_flat, stable=True).astype(jnp.int32)
    counts = jnp.sum((e_flat[:, None] == experts).astype(jnp.int32), axis=0)
    starts = jnp.cumsum(counts) - counts
    padded = (counts + MOE_TILE - 1) // MOE_TILE * MOE_TILE
    pad_ends = jnp.cumsum(padded)
    pad_starts = pad_ends - padded
    n_rows = (n_blocks + 2) * MOE_TILE
    r = jnp.arange(n_rows, dtype=jnp.int32)
    row_expert = jnp.minimum(jnp.sum((r[:, None] >= pad_ends[None, :]).astype(jnp.int32), axis=1),
                             N_EXPERTS - 1)
    of_row = lambda per_expert: jnp.sum(
        jnp.where(row_expert[:, None] == experts, per_expert[None, :], 0), axis=1)
    j = r - of_row(pad_starts)
    valid = (j < of_row(counts)) & (r < pad_ends[-1])
    src = jnp.clip(of_row(starts) + j, 0, n_assign - 1)
    assign = jnp.where(valid, order[src], -1)
    row_token = jnp.maximum(assign, 0) // TOP_K
    plane = n_assign // TOP_K + MOE_TILE
    spare =(r // MOE_TILE % 2) * plane + n_assign // TOP_K + r % MOE_TILE
    row_dst = jnp.where(assign >= 0, (assign % TOP_K) * plane + assign // TOP_K, spare)
    block_start = jnp.arange(n_blocks, dtype=jnp.int32) * MOE_TILE
    block_expert = jnp.minimum(
        jnp.sum((block_start[:, None] >= pad_ends[None, :]).astype(jnp.int32), axis=1),
        N_EXPERTS - 1).astype(jnp.int32)
    n_valid = (pad_ends[-1:] // MOE_TILE).astype(jnp.int32)
    return row_token, row_dst, block_expert, n_valid


def kernel(x, p, positions, ln_mix, ln_ffn, ln_ple, a_w_in, a_lambda, a_subln, a_conv_w, a_w_out,
           ffn_w1, ffn_w3, ffn_w2, c_w_in, c_ln_g, c_ln_b, c_w_s, c_b_s, c_w_out, router_w,
           moe_w1, moe_w3, moe_w2, ple_gate, ple_proj, final_norm):
    bsz, seq, _ = x.shape
    n = bsz * seq
    assert seq % ROW_TILE == 0 and seq % ATTN_TILE == 0 and ROW_TILE % GM_CHUNK == 0
    x2 = x.reshape(n, D_MODEL)
    pos3 = positions.reshape(n // ROW_TILE, 1, ROW_TILE)
    row = lambda a: a.reshape(1, -1)

    freq = (ROPE_THETA ** (-jnp.arange(0, ROT_DIM, 2, dtype=F32) / ROT_DIM)).reshape(-1, 1)
    qt, k, vt, conv = _inproj(x2, pos3, row(ln_mix[0]), freq, a_conv_w[0].T, a_w_in[0].astype(BF16), seq)
    lambda_init = 0.8 - 0.6 * math.exp(-0.3 * 0)
    attn = _diff_attention(qt, k, vt, a_lambda[0], a_subln[0].reshape(-1, 1), lambda_init, bsz, seq)
    h = _l0_post(x2, attn, conv, p[0].reshape(n, PLE_DIM), a_w_out[0].astype(BF16), row(ln_ffn[0]),
                 ffn_w1[0].astype(BF16), ffn_w3[0].astype(BF16), ffn_w2[0].astype(BF16),
                 row(ln_ple[0]), ple_gate[0].astype(BF16), ple_proj[0].astype(BF16))

    wr = jnp.pad(jnp.concatenate(_split_bf16(router_w[0]), axis=1),
                 ((0, 0), (0, LANES - 2 * N_EXPERTS)))
    h1, hn2, ridx, rgate = _l1_mix(h, row(ln_mix[1]), c_w_in[0].astype(BF16), row(c_ln_g[0]),
                                   row(c_ln_b[0]), c_w_s[0], c_b_s[0].T, c_w_out[0].astype(BF16),
                                   row(ln_ffn[1]), wr)
    n_blocks = -(-(n * TOP_K + N_EXPERTS * (MOE_TILE - 1)) // MOE_TILE)
    row_token, row_dst, block_expert, n_valid = _routing_tables(ridx, n_blocks)
    y_rows = _moe_experts(block_expert, n_valid, row_token, row_dst, hn2, moe_w1[0].astype(BF16),
                          moe_w3[0].astype(BF16), moe_w2[0].astype(BF16), TOP_K * (n + MOE_TILE))
    out = _combine(h1, y_rows, rgate, p[1].reshape(n, PLE_DIM), row(ln_ple[1]),
                   ple_gate[1].astype(BF16), ple_proj[1].astype(BF16), row(final_norm))
    return out.reshape(bsz, seq, D_MODEL)
```

```python
import functools
import math

import jax
import jax.numpy as jnp
from jax import lax
from jax.experimental import pallas as pl
from jax.experimental.pallas import tpu as pltpu

F32 = jnp.float32
BF16 = jnp.bfloat16

D_MODEL = 1024
PLE_DIM = 256
DA_HEADS = 4
DA_HEAD_DIM = 64
DA_V_DIM = 2 * DA_HEAD_DIM
DA_WIDTH = DA_HEADS * DA_V_DIM
ROPE_THETA = 500000.0
ROT_DIM = DA_HEAD_DIM // 4
SC_WIDTH = 512
GM_WIDTH = D_MODEL
GM_GROUPS = 8
GM_CHUNK = 128
D_FF = 2816
N_EXPERTS = 8
TOP_K = 2
RMS_EPS = 1e-6
LN_EPS = 1e-5

LANES = 128
FF_CHUNK = 256
N_FF_CHUNKS = D_FF // FF_CHUNK
ROW_TILE = 512
ATTN_TILE = 256
ATTN_Q_TILE = 512
Q_SCALE = DA_HEAD_DIM ** -0.5 * math.log2(math.e)
VT_ROWS = DA_V_DIM + 16
MOE_TILE = 512
MOE_FENCE_CHUNKS = (0, 3, 6, 8, 10)
MOE_COPY_ROWS = (0, 168, 336, 504, 512)
CONV_HALO = 8
VMEM_LIMIT = 56 * 2**20


def _resident(shape):
    nd = len(shape)
    return pl.BlockSpec(shape, lambda *_: (0,) * nd, pipeline_mode=pl.Buffered(1))


def _rows(tile, width):
    return pl.BlockSpec((tile, width), lambda i, *_: (i, 0))


def _rms(x, g):
    return x * lax.rsqrt(jnp.mean(x * x, axis=-1, keepdims=True) + RMS_EPS) * g


def _dot(a, b):
    return jnp.dot(a, b, preferred_element_type=F32)


def _swiglu_chunks(hn_ref, w1_ref, w3_ref, w2_ref, acc_ref, per_chunk=None):
    for c in range(N_FF_CHUNKS):
        cols = slice(c * FF_CHUNK, (c + 1) * FF_CHUNK)
        g = _dot(hn_ref[...], w1_ref[:, cols])
        u = _dot(hn_ref[...], w3_ref[:, cols])
        a = (g * jax.nn.sigmoid(g) * u).astype(BF16)
        acc_ref[...] += _dot(a, w2_ref[cols, :])
        if per_chunk is not None:
            per_chunk(c)


def _ple(h, p_ref, gple_ref, wg_ref, wp_ref):
    gate = jax.nn.sigmoid(_dot(_rms(h, gple_ref[...]).astype(BF16), wg_ref[...]))
    return h + gate * _dot(p_ref[...].astype(BF16), wp_ref[...])


def _inproj_kernel(x_ref, pos_ref, g_ref, freq_ref, cw_ref, w_ref,
                   qt_ref, k_ref, vt_ref, conv_ref, cbuf, *, tiles_per_seq):
    i = pl.program_id(0)
    tm = x_ref.shape[0]
    hn = _rms(x_ref[...], g_ref[...]).astype(BF16)

    half = ROT_DIM // 2
    ang = freq_ref[...] * pos_ref[...].astype(F32)
    cos_t = jnp.cos(ang)
    sin_t = jnp.sin(ang)
    zero_t = jnp.zeros_like(ang)
    rest = DA_HEAD_DIM - ROT_DIM
    pattern = lambda a, b, fill: jnp.concatenate(
        [a, b, jnp.full((rest, tm), fill, F32)] * (LANES // DA_HEAD_DIM), axis=0).T
    cos = pattern(cos_t, cos_t, 1.0)
    sin_up = pattern(-sin_t, zero_t, 0.0)
    sin_dn = pattern(zero_t, sin_t, 0.0)

    def rope(t):
        return (t * cos + pltpu.roll(t, LANES - half, 1) * sin_up
                + pltpu.roll(t, half, 1) * sin_dn)

    zq = _dot(hn, w_ref[:, 0:DA_WIDTH])
    zk = _dot(hn, w_ref[:, DA_WIDTH:2 * DA_WIDTH])
    zv = _dot(hn, w_ref[:, 2 * DA_WIDTH:3 * DA_WIDTH])
    ones = jnp.ones((VT_ROWS - DA_V_DIM, ATTN_TILE), BF16)
    for hd in range(DA_HEADS):
        sl = slice(hd * LANES, (hd + 1) * LANES)
        qt_ref[sl, :] = (rope(zq[:, sl]) * Q_SCALE).T.astype(BF16)
        k_ref[:, sl] = rope(zk[:, sl]).astype(BF16)
        for u in range(tm // ATTN_TILE):
            vt_ref[hd, u, 0:DA_V_DIM, :] = zv[u * ATTN_TILE:(u + 1) * ATTN_TILE, sl].T.astype(BF16)
            vt_ref[hd, u, DA_V_DIM:VT_ROWS, :] = ones

    off = 3 * DA_WIDTH
    b_gate = _dot(hn, w_ref[:, off:off + SC_WIDTH])
    c_gate = _dot(hn, w_ref[:, off + SC_WIDTH:off + 2 * SC_WIDTH])
    hc = _dot(hn, w_ref[:, off + 2 * SC_WIDTH:off + 3 * SC_WIDTH])

    @pl.when(i % tiles_per_seq == 0)
    def _():
        cbuf[0:CONV_HALO, :] = jnp.zeros((CONV_HALO, SC_WIDTH), F32)

    @pl.when(i % tiles_per_seq != 0)
    def _():
        cbuf[0:CONV_HALO, :] = cbuf[tm:tm + CONV_HALO, :]

    ch = c_gate * hc
    cbuf[CONV_HALO:CONV_HALO + tm, :] = ch
    ch1 = cbuf[CONV_HALO - 1:CONV_HALO - 1 + tm, :]
    ch2 = cbuf[CONV_HALO - 2:CONV_HALO - 2 + tm, :]
    conv = b_gate * (cw_ref[2:3, :] * ch + cw_ref[1:2, :] * ch1 + cw_ref[0:1, :] * ch2)
    conv_ref[...] = conv.astype(BF16)


def _inproj(x2, pos3, g, freq, cw, w, seq):
    n = x2.shape[0]
    tm = ROW_TILE
    out = jax.ShapeDtypeStruct((n, DA_WIDTH), BF16)
    sub = tm // ATTN_TILE
    return pl.pallas_call(
        functools.partial(_inproj_kernel, tiles_per_seq=seq // tm),
        grid=(n // tm,),
        in_specs=[_rows(tm, D_MODEL), pl.BlockSpec((None, 1, tm), lambda i: (i, 0, 0)),
                  _resident(g.shape), _resident(freq.shape),
                  _resident(cw.shape), _resident(w.shape)],
        out_specs=[pl.BlockSpec((DA_WIDTH, tm), lambda i: (0, i)), _rows(tm, DA_WIDTH),
                   pl.BlockSpec((DA_HEADS, sub, VT_ROWS, ATTN_TILE), lambda i: (0, i, 0, 0)),
                   _rows(tm, SC_WIDTH)],
        out_shape=[jax.ShapeDtypeStruct((DA_WIDTH, n), BF16), out,
                   jax.ShapeDtypeStruct((DA_HEADS, n // ATTN_TILE, VT_ROWS, ATTN_TILE), BF16), out],
        scratch_shapes=[pltpu.VMEM((tm + CONV_HALO, SC_WIDTH), F32)],
        compiler_params=pltpu.CompilerParams(dimension_semantics=("arbitrary",),
                                             vmem_limit_bytes=VMEM_LIMIT),
        name="l0_inproj",
    )(x2, pos3, g, freq, cw, w)


def _cast_slice(step, n_steps, src_refs, dst_refs, in_bufs, out_bufs, in_sem, out_sem):
    slot = step % 2
    other = 1 - slot

    def fetch(s, sl):
        return [pltpu.make_async_copy(src.at[pl.ds(s * buf.shape[1], buf.shape[1])], buf.at[sl],
                                      in_sem.at[sl]) for src, buf in zip(src_refs, in_bufs)]

    def write_back(s, sl):
        return [pltpu.make_async_copy(buf.at[sl], dst.at[pl.ds(s * buf.shape[1], buf.shape[1])],
                                      out_sem.at[sl]) for dst, buf in zip(dst_refs, out_bufs)]

    @pl.when(step == 0)
    def _():
        for c in fetch(0, 0):
            c.start()

    for c in fetch(step, slot):
        c.wait()

    @pl.when(step + 1 < n_steps)
    def _():
        for c in fetch(step + 1, other):
            c.start()

    @pl.when(step >= 2)
    def _():
        for c in write_back(step - 2, slot):
            c.wait()

    for src, dst in zip(in_bufs, out_bufs):
        dst[slot] = src[slot].astype(dst.dtype)
    for c in write_back(step, slot):
        c.start()

    @pl.when(step == n_steps - 1)
    def _():
        if n_steps > 1:
            for c in write_back(step - 1, other):
                c.wait()
        for c in write_back(step, slot):
            c.wait()


def _attn_kernel(lam_ref, sg_ref, qt_ref, k_ref, vt_ref, *rest, lambda_init, n_cast, n_steps):
    cast_src, rest = rest[:n_cast], rest[n_cast:]
    o_ref, rest = rest[0], rest[1:]
    cast_dst, rest = rest[:n_cast], rest[n_cast:]
    qc_sc, m_sc, alpha_sc, acc_sc, s_sc, p_sc = rest[:6]
    cast_in, cast_out = rest[6:6 + n_cast], rest[6 + n_cast:6 + 2 * n_cast]
    in_sem, out_sem = rest[6 + 2 * n_cast:]
    step = ((pl.program_id(0) * pl.num_programs(1) + pl.program_id(1)) * pl.num_programs(2)
            + pl.program_id(2))
    _cast_slice(step, n_steps, cast_src, cast_dst, cast_in, cast_out, in_sem, out_sem)

    qi = pl.program_id(2)
    tq = ATTN_Q_TILE
    tk = ATTN_TILE
    n_sub = tq // tk

    qt = qt_ref[...]
    dim = lax.broadcasted_iota(jnp.int32, (LANES, 1), 0)
    zero = jnp.zeros_like(qt)
    qc_sc[0] = jnp.where(dim < DA_HEAD_DIM, qt, zero)
    qc_sc[1] = jnp.where(dim >= DA_HEAD_DIM, qt, zero)
    m_sc[...] = jnp.full(m_sc.shape, -jnp.inf, F32)
    acc_sc[...] = jnp.zeros(acc_sc.shape, F32)
    alpha_sc[...] = jnp.ones(alpha_sc.shape, F32)
    p_sc[...] = jnp.zeros(p_sc.shape, BF16)

    def scores(j, diagonal):
        for c in range(2):
            for u in range(n_sub):
                ks = pl.multiple_of((j * n_sub + u) * tk, tk)
                su = _dot(k_ref[pl.ds(ks, tk), :], qc_sc[c])
                if diagonal:
                    key = lax.broadcasted_iota(jnp.int32, (tk, tq), 0) + u * tk
                    qry = lax.broadcasted_iota(jnp.int32, (tk, tq), 1)
                    su = jnp.where(key <= qry, su, -jnp.inf)
                s_sc[c, u * tk:(u + 1) * tk, :] = su

    def softmax():
        for c in range(2):
            s = s_sc[c]
            m_prev = m_sc[c]
            m_new = jnp.maximum(m_prev, jnp.max(s, axis=0, keepdims=True))
            alpha_sc[c] = jnp.exp2(m_prev - m_new)
            m_sc[c] = m_new
            p_sc[c] = jnp.exp2(s - m_new).astype(BF16)

    def fold(j):
        for c in range(2):
            pv = None
            for u in range(n_sub):
                d = _dot(vt_ref[j * n_sub + u], p_sc[c, u * tk:(u + 1) * tk, :])
                pv = d if pv is None else pv + d
            acc_sc[c] = alpha_sc[c] * acc_sc[c] + pv

    def iteration(t, next_scores):
        fold(jnp.maximum(t - 1, 0))
        softmax()
        if next_scores is not None:
            scores(t + 1, diagonal=next_scores == "diagonal")

    def steady(t, carry):
        iteration(t, "full")
        return carry

    @pl.when(qi == 0)
    def _():
        scores(0, diagonal=True)

    @pl.when(qi > 0)
    def _():
        scores(0, diagonal=False)

    lax.fori_loop(0, qi - 1, steady, 0)

    @pl.when(qi > 0)
    def _():
        iteration(qi - 1, "diagonal")

    iteration(qi, None)
    fold(qi)

    lp = lam_ref[...]
    lam = (jnp.exp(jnp.sum(lp[0:1] * lp[1:2], axis=-1, keepdims=True))
           - jnp.exp(jnp.sum(lp[2:3] * lp[3:4], axis=-1, keepdims=True)) + lambda_init)
    a0 = acc_sc[0]
    a1 = acc_sc[1]
    o = (a0[0:DA_V_DIM] / a0[DA_V_DIM:DA_V_DIM + 1]
         - lam * (a1[0:DA_V_DIM] / a1[DA_V_DIM:DA_V_DIM + 1]))
    o = o * lax.rsqrt(jnp.mean(o * o, axis=0, keepdims=True) + RMS_EPS) * sg_ref[...]
    o_ref[...] = (o * (1.0 - lambda_init)).T.astype(o_ref.dtype)


def _diff_attention(qt, k, vt, lam_p, subln_col, lambda_init, bsz, seq, cast_f32):
    tq = ATTN_Q_TILE
    tk = ATTN_TILE
    nq = seq // tq
    nk = seq // tk
    n_steps = bsz * DA_HEADS * nq
    n_cast = len(cast_f32)
    slices = [(w.shape[0] // n_steps, w.shape[1]) for w in cast_f32]
    assert all(w.shape[0] % n_steps == 0 and r % 16 == 0 for w, (r, _) in zip(cast_f32, slices))
    any_spec = pl.BlockSpec(memory_space=pl.ANY)
    out = pl.pallas_call(
        functools.partial(_attn_kernel, lambda_init=lambda_init, n_cast=n_cast, n_steps=n_steps),
        grid=(bsz, DA_HEADS, nq),
        in_specs=[pl.BlockSpec(lam_p.shape, lambda b, h, i: (0, 0)),
                  pl.BlockSpec(subln_col.shape, lambda b, h, i: (0, 0)),
                  pl.BlockSpec((LANES, tq), lambda b, h, i: (h, b * nq + i)),
                  pl.BlockSpec((seq, LANES), lambda b, h, i: (b, h)),
                  pl.BlockSpec((None, nk, VT_ROWS, tk), lambda b, h, i: (h, b, 0, 0))]
                 + [any_spec] * n_cast,
        out_specs=[pl.BlockSpec((tq, LANES), lambda b, h, i: (b * nq + i, h))] + [any_spec] * n_cast,
        out_shape=[jax.ShapeDtypeStruct(k.shape, BF16)]
                  + [jax.ShapeDtypeStruct(w.shape, BF16) for w in cast_f32],
        scratch_shapes=[pltpu.VMEM((2, LANES, tq), BF16), pltpu.VMEM((2, 1, tq), F32),
                        pltpu.VMEM((2, 1, tq), F32), pltpu.VMEM((2, VT_ROWS, tq), F32),
                        pltpu.VMEM((2, tq, tq), F32), pltpu.VMEM((2, tq, tq), BF16)]
                       + [pltpu.VMEM((2,) + s, F32) for s in slices]
                       + [pltpu.VMEM((2,) + s, BF16) for s in slices]
                       + [pltpu.SemaphoreType.DMA((2,)), pltpu.SemaphoreType.DMA((2,))],
        compiler_params=pltpu.CompilerParams(
            dimension_semantics=("arbitrary", "arbitrary", "arbitrary"),
            vmem_limit_bytes=VMEM_LIMIT),
        name="l0_diff_attention",
    )(lam_p, subln_col, qt, k, vt, *cast_f32)
    return out[0], out[1:]


def _l0_post_kernel(x_ref, attn_ref, conv_ref, p_ref, wo_ref, gffn_ref, w1_ref, w3_ref, w2_ref,
                    gple_ref, wg_ref, wp_ref, o_ref, acc_sc, hn_sc, h_sc):
    h = (x_ref[...] + _dot(attn_ref[...], wo_ref[0:DA_WIDTH, :])
         + _dot(conv_ref[...], wo_ref[DA_WIDTH:DA_WIDTH + SC_WIDTH, :]))
    hn_sc[...] = _rms(h, gffn_ref[...]).astype(BF16)
    h_sc[...] = h
    acc_sc[...] = jnp.zeros(acc_sc.shape, F32)
    _swiglu_chunks(hn_sc, w1_ref, w3_ref, w2_ref, acc_sc)
    o_ref[...] = _ple(h_sc[...] + acc_sc[...], p_ref, gple_ref, wg_ref, wp_ref)


def _l0_post(x2, attn, conv, p0, wo, gffn, w1, w3, w2, gple, wg, wp):
    n = x2.shape[0]
    tm = ROW_TILE
    return pl.pallas_call(
        _l0_post_kernel,
        grid=(n // tm,),
        in_specs=[_rows(tm, D_MODEL), _rows(tm, DA_WIDTH), _rows(tm, SC_WIDTH), _rows(tm, PLE_DIM),
                  _resident(wo.shape), _resident(gffn.shape), _resident(w1.shape),
                  _resident(w3.shape), _resident(w2.shape), _resident(gple.shape),
                  _resident(wg.shape), _resident(wp.shape)],
        out_specs=_rows(tm, D_MODEL),
        out_shape=jax.ShapeDtypeStruct((n, D_MODEL), F32),
        scratch_shapes=[pltpu.VMEM((tm, D_MODEL), F32), pltpu.VMEM((tm, D_MODEL), BF16),
                        pltpu.VMEM((tm, D_MODEL), F32)],
        compiler_params=pltpu.CompilerParams(dimension_semantics=("arbitrary",),
                                             vmem_limit_bytes=VMEM_LIMIT),
        name="l0_outproj_swiglu_ple",
    )(x2, attn, conv, p0, wo, gffn, w1, w3, w2, gple, wg, wp)


def _split_bf16(a):
    hi = a.astype(BF16)
    return hi, (a - hi.astype(F32)).astype(BF16)


def _l1_mix_kernel(h_ref, gmix_ref, win_ref, lng_ref, lnb_ref, ws_ref, bs_ref, wout_ref,
                   gffn_ref, wr_ref,
                   h1_ref, hn2_ref, ridx_ref, rgate_ref, hn_sc, u_sc, v_sc, vn_sc, gated_sc):
    tm = h_ref.shape[0]
    hn_sc[...] = _rms(h_ref[...], gmix_ref[...]).astype(BF16)

    def gelu(z):
        return 0.5 * z * (1.0 + lax.erf(z * (2.0 ** -0.5)))

    for dst, base in ((v_sc, GM_WIDTH), (u_sc, 0)):
        for c in range(GM_WIDTH // FF_CHUNK):
            cols = slice(c * FF_CHUNK, (c + 1) * FF_CHUNK)
            dst[:, cols] = gelu(_dot(hn_sc[...], win_ref[:, base + c * FF_CHUNK:base + (c + 1) * FF_CHUNK]))

    vv = v_sc[...]
    mu = jnp.mean(vv, axis=-1, keepdims=True)
    vc = vv - mu
    var = jnp.mean(vc * vc, axis=-1, keepdims=True)
    vn_sc[...] = (vc * lax.rsqrt(var + LN_EPS) * lng_ref[...] + lnb_ref[...]).astype(BF16)

    n_chunks = tm // GM_CHUNK
    row = lax.broadcasted_iota(jnp.int32, (GM_CHUNK, GM_CHUNK), 0)
    col = lax.broadcasted_iota(jnp.int32, (GM_CHUNK, GM_CHUNK), 1)
    for g in range(GM_GROUPS):
        gs = slice(g * LANES, (g + 1) * LANES)
        wm = jnp.where(col <= row, ws_ref[g], 0.0).astype(BF16)
        rhs = jnp.concatenate(
            [vn_sc[r * GM_CHUNK:(r + 1) * GM_CHUNK, gs] for r in range(n_chunks)], axis=1)
        vs = _dot(wm, rhs) + bs_ref[:, g:g + 1]
        for r in range(n_chunks):
            rs = slice(r * GM_CHUNK, (r + 1) * GM_CHUNK)
            gated_sc[rs, gs] = (u_sc[rs, gs] * vs[:, r * LANES:(r + 1) * LANES]).astype(BF16)

    h1 = h_ref[...] + _dot(gated_sc[...], wout_ref[...])
    h1_ref[...] = h1

    hn2 = _rms(h1, gffn_ref[...])
    hn2_ref[...] = hn2
    x_hi, x_lo = _split_bf16(hn2)
    a_hi = _dot(x_hi, wr_ref[...])
    logits = a_hi + pltpu.roll(a_hi, LANES - N_EXPERTS, 1) + _dot(x_lo, wr_ref[...])
    lane = lax.broadcasted_iota(jnp.int32, logits.shape, 1)
    logits = jnp.where(lane < N_EXPERTS, logits, -jnp.inf)
    v1 = jnp.max(logits, axis=-1, keepdims=True)
    i1 = jnp.min(jnp.where(logits == v1, lane, LANES), axis=-1, keepdims=True)
    rest = jnp.where(lane == i1, -jnp.inf, logits)
    v2 = jnp.max(rest, axis=-1, keepdims=True)
    i2 = jnp.min(jnp.where(rest == v2, lane, LANES), axis=-1, keepdims=True)
    e2 = jnp.exp(v2 - v1)
    ridx_ref[:, 0:1] = i1
    ridx_ref[:, 1:2] = i2
    rgate_ref[:, 0:1] = 1.0 / (1.0 + e2)
    rgate_ref[:, 1:2] = e2 / (1.0 + e2)


def _l1_mix(h, gmix, win, lng, lnb, ws, bs_t, wout, gffn, wr):
    n = h.shape[0]
    tm = ROW_TILE
    return pl.pallas_call(
        _l1_mix_kernel,
        grid=(n // tm,),
        in_specs=[_rows(tm, D_MODEL)] + [_resident(a.shape) for a in
                                         (gmix, win, lng, lnb, ws, bs_t, wout, gffn, wr)],
        out_specs=[_rows(tm, D_MODEL), _rows(tm, D_MODEL), _rows(tm, TOP_K), _rows(tm, TOP_K)],
        out_shape=[jax.ShapeDtypeStruct((n, D_MODEL), F32), jax.ShapeDtypeStruct((n, D_MODEL), F32),
                   jax.ShapeDtypeStruct((n, TOP_K), jnp.int32),
                   jax.ShapeDtypeStruct((n, TOP_K), F32)],
        scratch_shapes=[pltpu.VMEM((tm, D_MODEL), BF16), pltpu.VMEM((tm, GM_WIDTH), F32),
                        pltpu.VMEM((tm, GM_WIDTH), F32), pltpu.VMEM((tm, GM_WIDTH), BF16),
                        pltpu.VMEM((tm, GM_WIDTH), BF16)],
        compiler_params=pltpu.CompilerParams(dimension_semantics=("arbitrary",),
                                             vmem_limit_bytes=VMEM_LIMIT),
        name="l1_gmlp_router",
    )(h, gmix, win, lng, lnb, ws, bs_t, wout, gffn, wr)


def _moe_kernel(bexp_ref, nvalid_ref, tok_ref, dst_ref, hn_hbm, w1_ref, w3_ref, w2_ref, y_hbm,
                xbuf, xb_sc, acc_sc, ybuf, gsem, ssem, fence_sem, fence_sink):
    i = pl.program_id(0)
    n_valid = nvalid_ref[0]
    tm = MOE_TILE
    prev = jnp.maximum(i - 1, 0)

    def gather_row(blk, r):
        return pltpu.make_async_copy(hn_hbm.at[pl.ds(tok_ref[blk * tm + r], 1)],
                                     xbuf.at[pl.ds(r, 1)], gsem)

    def scatter_row(blk, r):
        return pltpu.make_async_copy(ybuf.at[pl.ds(r, 1)],
                                     y_hbm.at[pl.ds(dst_ref[blk * tm + r], 1)], ssem)

    def gather_wait():
        pltpu.make_async_copy(hn_hbm.at[pl.ds(0, tm)], xbuf, gsem).wait()

    def scatter_wait():
        pltpu.make_async_copy(ybuf, y_hbm.at[pl.ds(0, tm)], ssem).wait()

    @pl.when(i < n_valid)
    def _():
        @pl.when(i == 0)
        def _():
            def issue(r, carry):
                gather_row(0, r).start()
                return carry
            lax.fori_loop(0, tm, issue, 0)

        gather_wait()
        xb_sc[...] = xbuf[...].astype(BF16)
        acc_sc[...] = jnp.zeros(acc_sc.shape, F32)
        has_next = i + 1 < n_valid
        has_prev = i >= 1

        def copies(c):
            if c not in MOE_FENCE_CHUNKS:
                return
            g = MOE_FENCE_CHUNKS.index(c)
            if g > 0:
                fence_sink[0] = pl.semaphore_read(fence_sem)
            if g == len(MOE_FENCE_CHUNKS) - 1:
                return
            for r in range(MOE_COPY_ROWS[g], MOE_COPY_ROWS[g + 1]):
                @pl.when(has_next)
                def _():
                    gather_row(i + 1, r).start()

                @pl.when(has_prev)
                def _():
                    scatter_row(prev, r).start(priority=r % 2)

        _swiglu_chunks(xb_sc, w1_ref, w3_ref, w2_ref, acc_sc, per_chunk=copies)

        @pl.when(has_prev)
        def _():
            scatter_wait()

        ybuf[...] = acc_sc[...]

    @pl.when(i == n_valid)
    def _():
        def issue(r, carry):
            scatter_row(prev, r).start()
            return carry
        lax.fori_loop(0, tm, issue, 0)
        scatter_wait()
        ybuf[...] = jnp.zeros(ybuf.shape, F32)
        plane = y_hbm.shape[0] // TOP_K
        for b in range(TOP_K):
            spare = pltpu.make_async_copy(ybuf, y_hbm.at[pl.ds((b + 1) * plane - tm, tm)], ssem)
            spare.start()
            spare.wait()


def _moe_experts(block_expert, n_valid, row_token, row_dst, hn2, w1, w3, w2, n_out_rows):
    n_blocks = block_expert.shape[0]
    tm = MOE_TILE
    per_expert = lambda w: pl.BlockSpec((None,) + w.shape[1:], lambda i, be, *_: (be[i], 0, 0))
    return pl.pallas_call(
        _moe_kernel,
        grid_spec=pltpu.PrefetchScalarGridSpec(
            num_scalar_prefetch=4,
            grid=(n_blocks,),
            in_specs=[pl.BlockSpec(memory_space=pl.ANY), per_expert(w1), per_expert(w3),
                      per_expert(w2)],
            out_specs=pl.BlockSpec(memory_space=pl.ANY),
            scratch_shapes=[pltpu.VMEM((tm, D_MODEL), F32), pltpu.VMEM((tm, D_MODEL), BF16),
                            pltpu.VMEM((tm, D_MODEL), F32), pltpu.VMEM((tm, D_MODEL), F32),
                            pltpu.SemaphoreType.DMA(()), pltpu.SemaphoreType.DMA(()),
                            pltpu.SemaphoreType.REGULAR(()), pltpu.SMEM((1,), jnp.int32)],
        ),
        out_shape=jax.ShapeDtypeStruct((n_out_rows, D_MODEL), F32),
        compiler_params=pltpu.CompilerParams(dimension_semantics=("arbitrary",),
                                             vmem_limit_bytes=VMEM_LIMIT),
        name="l1_expert_swiglu",
    )(block_expert, n_valid, row_token, row_dst, hn2, w1, w3, w2)


def _combine_kernel(h_ref, y0_ref, y1_ref, gate_ref, p_ref, gple_ref, wg_ref, wp_ref, gfin_ref,
                    o_ref):
    gate = gate_ref[...]
    h = h_ref[...] + (y0_ref[...] * gate[:, 0:1] + y1_ref[...] * gate[:, 1:2])
    h = _ple(h, p_ref, gple_ref, wg_ref, wp_ref)
    o_ref[...] = _rms(h, gfin_ref[...])


def _combine(h1, y_rows, rgate, p1, gple, wg, wp, gfin):
    n = h1.shape[0]
    tm = ROW_TILE
    plane_tiles = y_rows.shape[0] // TOP_K // tm
    return pl.pallas_call(
        _combine_kernel,
        grid=(n // tm,),
        in_specs=[_rows(tm, D_MODEL), _rows(tm, D_MODEL),
                  pl.BlockSpec((tm, D_MODEL), lambda i: (plane_tiles + i, 0)), _rows(tm, TOP_K),
                  _rows(tm, PLE_DIM), _resident(gple.shape), _resident(wg.shape),
                  _resident(wp.shape), _resident(gfin.shape)],
        out_specs=_rows(tm, D_MODEL),
        out_shape=jax.ShapeDtypeStruct((n, D_MODEL), F32),
        compiler_params=pltpu.CompilerParams(dimension_semantics=("arbitrary",),
                                             vmem_limit_bytes=VMEM_LIMIT),
        name="l1_combine_ple_norm",
    )(h1, y_rows, y_rows, rgate, p1, gple, wg, wp, gfin)


def _routing_tables(ridx, n_blocks):
    n_assign = ridx.size
    e_flat = ridx.reshape(-1)
    experts = jnp.arange(N_EXPERTS, dtype=jnp.int32)[None, :]
    order = jnp.argsort(e_flat, stable=True).astype(jnp.int32)
    counts = jnp.sum((e_flat[:, None] == experts).astype(jnp.int32), axis=0)
    starts = jnp.cumsum(counts) - counts
    padded = (counts + MOE_TILE - 1) // MOE_TILE * MOE_TILE
    pad_ends = jnp.cumsum(padded)
    pad_starts = pad_ends - padded
    n_rows = (n_blocks + 2) * MOE_TILE
    r = jnp.arange(n_rows, dtype=jnp.int32)
    row_expert = jnp.minimum(jnp.sum((r[:, None] >= pad_ends[None, :]).astype(jnp.int32), axis=1),
                             N_EXPERTS - 1)
    of_row = lambda per_expert: jnp.sum(
        jnp.where(row_expert[:, None] == experts, per_expert[None, :], 0), axis=1)
    j = r - of_row(pad_starts)
    valid = (j < of_row(counts)) & (r < pad_ends[-1])
    src = jnp.clip(of_row(starts) + j, 0, n_assign - 1)
    assign = jnp.where(valid, order[src], -1)
    row_token = jnp.maximum(assign, 0) // TOP_K
    plane = n_assign // TOP_K + MOE_TILE
    spare =(r // MOE_TILE % 2) * plane + n_assign // TOP_K + r % MOE_TILE
    row_dst = jnp.where(assign >= 0, (assign % TOP_K) * plane + assign // TOP_K, spare)
    block_start = jnp.arange(n_blocks, dtype=jnp.int32) * MOE_TILE
    block_expert = jnp.minimum(
        jnp.sum((block_start[:, None] >= pad_ends[None, :]).astype(jnp.int32), axis=1),
        N_EXPERTS - 1).astype(jnp.int32)
    n_valid = (pad_ends[-1:] // MOE_TILE).astype(jnp.int32)
    return row_token, row_dst, block_expert, n_valid


def kernel(x, p, positions, ln_mix, ln_ffn, ln_ple, a_w_in, a_lambda, a_subln, a_conv_w, a_w_out,
           ffn_w1, ffn_w3, ffn_w2, c_w_in, c_ln_g, c_ln_b, c_w_s, c_b_s, c_w_out, router_w,
           moe_w1, moe_w3, moe_w2, ple_gate, ple_proj, final_norm):
    bsz, seq, _ = x.shape
    n = bsz * seq
    assert seq % ROW_TILE == 0 and seq % ATTN_TILE == 0 and ROW_TILE % GM_CHUNK == 0
    x2 = x.reshape(n, D_MODEL)
    pos3 = positions.reshape(n // ROW_TILE, 1, ROW_TILE)
    row = lambda a: a.reshape(1, -1)

    freq = (ROPE_THETA ** (-jnp.arange(0, ROT_DIM, 2, dtype=F32) / ROT_DIM)).reshape(-1, 1)
    qt, k, vt, conv = _inproj(x2, pos3, row(ln_mix[0]), freq, a_conv_w[0].T, a_w_in[0].astype(BF16), seq)
    lambda_init = 0.8 - 0.6 * math.exp(-0.3 * 0)
    moe_f32 = [w[0].reshape(-1, w.shape[-1]) for w in (moe_w1, moe_w3, moe_w2)]
    attn, moe_bf16 = _diff_attention(qt, k, vt, a_lambda[0], a_subln[0].reshape(-1, 1), lambda_init,
                                     bsz, seq, moe_f32)
    mw1, mw3, mw2 = (wb.reshape(w.shape[1:]) for wb, w in zip(moe_bf16, (moe_w1, moe_w3, moe_w2)))
    h = _l0_post(x2, attn, conv, p[0].reshape(n, PLE_DIM), a_w_out[0].astype(BF16), row(ln_ffn[0]),
                 ffn_w1[0].astype(BF16), ffn_w3[0].astype(BF16), ffn_w2[0].astype(BF16),
                 row(ln_ple[0]), ple_gate[0].astype(BF16), ple_proj[0].astype(BF16))

    wr = jnp.pad(jnp.concatenate(_split_bf16(router_w[0]), axis=1),
                 ((0, 0), (0, LANES - 2 * N_EXPERTS)))
    h1, hn2, ridx, rgate = _l1_mix(h, row(ln_mix[1]), c_w_in[0].astype(BF16), row(c_ln_g[0]),
                                   row(c_ln_b[0]), c_w_s[0], c_b_s[0].T, c_w_out[0].astype(BF16),
                                   row(ln_ffn[1]), wr)
    n_blocks = -(-(n * TOP_K + N_EXPERTS * (MOE_TILE - 1)) // MOE_TILE)
    row_token, row_dst, block_expert, n_valid = _routing_tables(ridx, n_blocks)
    y_rows = _moe_experts(block_expert, n_valid, row_token, row_dst, hn2, mw1, mw3, mw2,
                          TOP_K * (n + MOE_TILE))
    out = _combine(h1, y_rows, rgate, p[1].reshape(n, PLE_DIM), row(ln_ple[1]),
                   ple_gate[1].astype(BF16), ple_proj[1].astype(BF16), row(final_norm))
    return out.reshape(bsz, seq, D_MODEL)
```

```python
import functools
import math

import jax
import jax.numpy as jnp
from jax import lax
from jax.experimental import pallas as pl
from jax.experimental.pallas import tpu as pltpu

F32 = jnp.float32
BF16 = jnp.bfloat16

D_MODEL = 1024
PLE_DIM = 256
DA_HEADS = 4
DA_HEAD_DIM = 64
DA_V_DIM = 2 * DA_HEAD_DIM
DA_WIDTH = DA_HEADS * DA_V_DIM
ROPE_THETA = 500000.0
ROT_DIM = DA_HEAD_DIM // 4
SC_WIDTH = 512
GM_WIDTH = D_MODEL
GM_GROUPS = 8
GM_CHUNK = 128
D_FF = 2816
N_EXPERTS = 8
TOP_K = 2
RMS_EPS = 1e-6
LN_EPS = 1e-5

LANES = 128
FF_CHUNK = 256
N_FF_CHUNKS = D_FF // FF_CHUNK
ROW_TILE = 512
ATTN_TILE = 256
ATTN_Q_TILE = 512
Q_SCALE = DA_HEAD_DIM ** -0.5 * math.log2(math.e)
VT_ROWS = DA_V_DIM + 16
MOE_TILE = 512
MOE_FENCE_CHUNKS = (0, 5, 9, 10)
MOE_COPY_ROWS = (0, 256, 504, 512)
CONV_HALO = 8
VMEM_LIMIT = 56 * 2**20


def _resident(shape):
    nd = len(shape)
    return pl.BlockSpec(shape, lambda *_: (0,) * nd, pipeline_mode=pl.Buffered(1))


def _rows(tile, width):
    return pl.BlockSpec((tile, width), lambda i, *_: (i, 0))


def _rms(x, g):
    return x * lax.rsqrt(jnp.mean(x * x, axis=-1, keepdims=True) + RMS_EPS) * g


def _dot(a, b):
    return jnp.dot(a, b, preferred_element_type=F32)


def _swiglu_chunks(hn_ref, w1_ref, w3_ref, w2_ref, acc_ref, per_chunk=None):
    for c in range(N_FF_CHUNKS):
        cols = slice(c * FF_CHUNK, (c + 1) * FF_CHUNK)
        g = _dot(hn_ref[...], w1_ref[:, cols])
        u = _dot(hn_ref[...], w3_ref[:, cols])
        a = (g * jax.nn.sigmoid(g) * u).astype(BF16)
        if c == 0:
            acc_ref[...] = _dot(a, w2_ref[cols, :])
        else:
            acc_ref[...] += _dot(a, w2_ref[cols, :])
        if per_chunk is not None:
            per_chunk(c)


def _ple(h, p_ref, gple_ref, wg_ref, wp_ref):
    gate = jax.nn.sigmoid(_dot(_rms(h, gple_ref[...]).astype(BF16), wg_ref[...]))
    return h + gate * _dot(p_ref[...].astype(BF16), wp_ref[...])


def _inproj_kernel(x_ref, pos_ref, g_ref, freq_ref, cw_ref, w_ref,
                   qt_ref, k_ref, vt_ref, conv_ref, cbuf, *, tiles_per_seq):
    i = pl.program_id(0)
    tm = x_ref.shape[0]
    hn = _rms(x_ref[...], g_ref[...]).astype(BF16)

    half = ROT_DIM // 2
    ang = freq_ref[...] * pos_ref[...].astype(F32)
    cos_t = jnp.cos(ang)
    sin_t = jnp.sin(ang)
    zero_t = jnp.zeros_like(ang)
    rest = DA_HEAD_DIM - ROT_DIM
    pattern = lambda a, b, fill: jnp.concatenate(
        [a, b, jnp.full((rest, tm), fill, F32)] * (LANES // DA_HEAD_DIM), axis=0).T
    cos = pattern(cos_t, cos_t, 1.0)
    sin_up = pattern(-sin_t, zero_t, 0.0)
    sin_dn = pattern(zero_t, sin_t, 0.0)

    def rope(t):
        return (t * cos + pltpu.roll(t, LANES - half, 1) * sin_up
                + pltpu.roll(t, half, 1) * sin_dn)

    zq = _dot(hn, w_ref[:, 0:DA_WIDTH])
    zk = _dot(hn, w_ref[:, DA_WIDTH:2 * DA_WIDTH])
    zv = _dot(hn, w_ref[:, 2 * DA_WIDTH:3 * DA_WIDTH])
    ones = jnp.ones((VT_ROWS - DA_V_DIM, ATTN_TILE), BF16)
    for hd in range(DA_HEADS):
        sl = slice(hd * LANES, (hd + 1) * LANES)
        qt_ref[sl, :] = (rope(zq[:, sl]) * Q_SCALE).T.astype(BF16)
        k_ref[:, sl] = rope(zk[:, sl]).astype(BF16)
        for u in range(tm // ATTN_TILE):
            vt_ref[hd, u, 0:DA_V_DIM, :] = zv[u * ATTN_TILE:(u + 1) * ATTN_TILE, sl].T.astype(BF16)
            vt_ref[hd, u, DA_V_DIM:VT_ROWS, :] = ones

    off = 3 * DA_WIDTH
    b_gate = _dot(hn, w_ref[:, off:off + SC_WIDTH])
    c_gate = _dot(hn, w_ref[:, off + SC_WIDTH:off + 2 * SC_WIDTH])
    hc = _dot(hn, w_ref[:, off + 2 * SC_WIDTH:off + 3 * SC_WIDTH])

    @pl.when(i % tiles_per_seq == 0)
    def _():
        cbuf[0:CONV_HALO, :] = jnp.zeros((CONV_HALO, SC_WIDTH), F32)

    @pl.when(i % tiles_per_seq != 0)
    def _():
        cbuf[0:CONV_HALO, :] = cbuf[tm:tm + CONV_HALO, :]

    ch = c_gate * hc
    cbuf[CONV_HALO:CONV_HALO + tm, :] = ch
    ch1 = cbuf[CONV_HALO - 1:CONV_HALO - 1 + tm, :]
    ch2 = cbuf[CONV_HALO - 2:CONV_HALO - 2 + tm, :]
    conv = b_gate * (cw_ref[2:3, :] * ch + cw_ref[1:2, :] * ch1 + cw_ref[0:1, :] * ch2)
    conv_ref[...] = conv.astype(BF16)


def _inproj(x2, pos3, g, freq, cw, w, seq):
    n = x2.shape[0]
    tm = ROW_TILE
    out = jax.ShapeDtypeStruct((n, DA_WIDTH), BF16)
    sub = tm // ATTN_TILE
    return pl.pallas_call(
        functools.partial(_inproj_kernel, tiles_per_seq=seq // tm),
        grid=(n // tm,),
        in_specs=[_rows(tm, D_MODEL), pl.BlockSpec((None, 1, tm), lambda i: (i, 0, 0)),
                  _resident(g.shape), _resident(freq.shape),
                  _resident(cw.shape), _resident(w.shape)],
        out_specs=[pl.BlockSpec((DA_WIDTH, tm), lambda i: (0, i)), _rows(tm, DA_WIDTH),
                   pl.BlockSpec((DA_HEADS, sub, VT_ROWS, ATTN_TILE), lambda i: (0, i, 0, 0)),
                   _rows(tm, SC_WIDTH)],
        out_shape=[jax.ShapeDtypeStruct((DA_WIDTH, n), BF16), out,
                   jax.ShapeDtypeStruct((DA_HEADS, n // ATTN_TILE, VT_ROWS, ATTN_TILE), BF16), out],
        scratch_shapes=[pltpu.VMEM((tm + CONV_HALO, SC_WIDTH), F32)],
        compiler_params=pltpu.CompilerParams(dimension_semantics=("arbitrary",),
                                             vmem_limit_bytes=VMEM_LIMIT),
        name="l0_inproj",
    )(x2, pos3, g, freq, cw, w)


def _cast_slice(step, n_slices, grid_steps, src_refs, dst_refs, in_bufs, out_bufs, in_sem, out_sem):
    assert 2 <= n_slices <= grid_steps
    slot = step % 2
    other = 1 - slot

    def fetch(s, sl):
        return [pltpu.make_async_copy(src.at[pl.ds(s * buf.shape[1], buf.shape[1])], buf.at[sl],
                                      in_sem.at[sl]) for src, buf in zip(src_refs, in_bufs)]

    def write_back(s, sl):
        return [pltpu.make_async_copy(buf.at[sl], dst.at[pl.ds(s * buf.shape[1], buf.shape[1])],
                                      out_sem.at[sl]) for dst, buf in zip(dst_refs, out_bufs)]

    @pl.when(step == 0)
    def _():
        for c in fetch(0, 0):
            c.start()

    @pl.when(step < n_slices)
    def _():
        for c in fetch(step, slot):
            c.wait()

        @pl.when(step + 1 < n_slices)
        def _():
            for c in fetch(step + 1, other):
                c.start()

        @pl.when(step >= 2)
        def _():
            for c in write_back(step - 2, slot):
                c.wait()

        for src, dst in zip(in_bufs, out_bufs):
            dst[slot] = src[slot].astype(dst.dtype)
        for c in write_back(step, slot):
            c.start()

    @pl.when(step == min(n_slices, grid_steps - 1))
    def _():
        for s in (n_slices - 2, n_slices - 1):
            for c in write_back(s, s % 2):
                c.wait()


def _attn_kernel(lam_ref, sg_ref, qt_ref, k_ref, vt_ref, *rest, lambda_init, cast_groups, n_steps,
                 n_q):
    n_cast = sum(cnt for _, cnt in cast_groups)
    cast_src, rest = rest[:n_cast], rest[n_cast:]
    o_ref, rest = rest[0], rest[1:]
    cast_dst, rest = rest[:n_cast], rest[n_cast:]
    qc_sc, m_sc, alpha_sc, acc_sc, s_sc, p_sc = rest[:6]
    cast_in, cast_out = rest[6:6 + n_cast], rest[6 + n_cast:6 + 2 * n_cast]
    sems = rest[6 + 2 * n_cast:]
    step = ((pl.program_id(0) * pl.num_programs(1) + pl.program_id(1)) * pl.num_programs(2)
            + pl.program_id(2))
    first = 0
    for g, (n_slices, cnt) in enumerate(cast_groups):
        grp = slice(first, first + cnt)
        _cast_slice(step, n_slices, n_steps, cast_src[grp], cast_dst[grp], cast_in[grp],
                    cast_out[grp], sems[2 * g], sems[2 * g + 1])
        first += cnt

    qi = pl.program_id(2)
    tq = ATTN_Q_TILE
    tk = ATTN_TILE
    n_sub = tq // tk

    qt = qt_ref[...]
    dim = lax.broadcasted_iota(jnp.int32, (LANES, 1), 0)
    zero = jnp.zeros_like(qt)
    qc_sc[0] = jnp.where(dim < DA_HEAD_DIM, qt, zero)
    qc_sc[1] = jnp.where(dim >= DA_HEAD_DIM, qt, zero)
    m_sc[...] = jnp.full(m_sc.shape, -jnp.inf, F32)
    acc_sc[...] = jnp.zeros(acc_sc.shape, F32)

    def scores(j, diagonal):
        for c in range(2):
            for u in range(n_sub):
                ks = (j * n_sub + u) * tk
                su = _dot(k_ref[ks:ks + tk, :], qc_sc[c])
                if diagonal:
                    key = lax.broadcasted_iota(jnp.int32, (tk, tq), 0) + u * tk
                    qry = lax.broadcasted_iota(jnp.int32, (tk, tq), 1)
                    su = jnp.where(key <= qry, su, -jnp.inf)
                s_sc[c, u * tk:(u + 1) * tk, :] = su

    def softmax():
        for c in range(2):
            s = s_sc[c]
            m_prev = m_sc[c]
            m_new = jnp.maximum(m_prev, jnp.max(s, axis=0, keepdims=True))
            alpha_sc[c] = jnp.exp2(m_prev - m_new)
            m_sc[c] = m_new
            p_sc[c] = jnp.exp2(s - m_new).astype(BF16)

    def fold(j):
        for c in range(2):
            pv = None
            for u in range(n_sub):
                d = _dot(vt_ref[j * n_sub + u], p_sc[c, u * tk:(u + 1) * tk, :])
                pv = d if pv is None else pv + d
            acc_sc[c] = alpha_sc[c] * acc_sc[c] + pv

    def run(last):
        scores(0, diagonal=last == 0)
        for t in range(last + 1):
            if t >= 1:
                fold(t - 1)
            softmax()
            if t < last:
                scores(t + 1, diagonal=t + 1 == last)
        fold(last)

    for last in range(n_q):
        pl.when(qi == last)(functools.partial(run, last))

    lp = lam_ref[...]
    lam = (jnp.exp(jnp.sum(lp[0:1] * lp[1:2], axis=-1, keepdims=True))
           - jnp.exp(jnp.sum(lp[2:3] * lp[3:4], axis=-1, keepdims=True)) + lambda_init)
    a0 = acc_sc[0]
    a1 = acc_sc[1]
    o = (a0[0:DA_V_DIM] / a0[DA_V_DIM:DA_V_DIM + 1]
         - lam * (a1[0:DA_V_DIM] / a1[DA_V_DIM:DA_V_DIM + 1]))
    o = o * lax.rsqrt(jnp.mean(o * o, axis=0, keepdims=True) + RMS_EPS) * sg_ref[...]
    o_ref[...] = (o * (1.0 - lambda_init)).T.astype(o_ref.dtype)


def _diff_attention(qt, k, vt, lam_p, subln_col, lambda_init, bsz, seq, cast_f32):
    tq = ATTN_Q_TILE
    tk = ATTN_TILE
    nq = seq // tq
    nk = seq // tk
    n_steps = bsz * DA_HEADS * nq
    def n_slices(w):
        return max(c for c in range(1, n_steps + 1)
                   if w.shape[0] % c == 0 and (w.shape[0] // c) % 16 == 0)
    order = sorted(range(len(cast_f32)), key=lambda a: n_slices(cast_f32[a]))
    cast_sorted = [cast_f32[a] for a in order]
    counts = [n_slices(w) for w in cast_sorted]
    cast_groups = tuple((c, counts.count(c)) for c in sorted(set(counts)))
    n_cast = len(cast_f32)
    slices = [(w.shape[0] // c, w.shape[1]) for w, c in zip(cast_sorted, counts)]
    any_spec = pl.BlockSpec(memory_space=pl.ANY)
    out = pl.pallas_call(
        functools.partial(_attn_kernel, lambda_init=lambda_init, cast_groups=cast_groups,
                          n_steps=n_steps, n_q=nq),
        grid=(bsz, DA_HEADS, nq),
        in_specs=[pl.BlockSpec(lam_p.shape, lambda b, h, i: (0, 0)),
                  pl.BlockSpec(subln_col.shape, lambda b, h, i: (0, 0)),
                  pl.BlockSpec((LANES, tq), lambda b, h, i: (h, b * nq + i)),
                  pl.BlockSpec((seq, LANES), lambda b, h, i: (b, h)),
                  pl.BlockSpec((None, nk, VT_ROWS, tk), lambda b, h, i: (h, b, 0, 0))]
                 + [any_spec] * n_cast,
        out_specs=[pl.BlockSpec((tq, LANES), lambda b, h, i: (b * nq + i, h))] + [any_spec] * n_cast,
        out_shape=[jax.ShapeDtypeStruct(k.shape, BF16)]
                  + [jax.ShapeDtypeStruct(w.shape, BF16) for w in cast_sorted],
        scratch_shapes=[pltpu.VMEM((2, LANES, tq), BF16), pltpu.VMEM((2, 1, tq), F32),
                        pltpu.VMEM((2, 1, tq), F32), pltpu.VMEM((2, VT_ROWS, tq), F32),
                        pltpu.VMEM((2, tq, tq), F32), pltpu.VMEM((2, tq, tq), BF16)]
                       + [pltpu.VMEM((2,) + s, F32) for s in slices]
                       + [pltpu.VMEM((2,) + s, BF16) for s in slices]
                       + [pltpu.SemaphoreType.DMA((2,))] * (2 * len(cast_groups)),
        compiler_params=pltpu.CompilerParams(
            dimension_semantics=("arbitrary", "arbitrary", "arbitrary"),
            vmem_limit_bytes=VMEM_LIMIT),
        name="l0_diff_attention",
    )(lam_p, subln_col, qt, k, vt, *cast_sorted)
    cast_bf16 = [None] * n_cast
    for pos, a in enumerate(order):
        cast_bf16[a] = out[1 + pos]
    return out[0], cast_bf16


def _l0_post_kernel(x_ref, attn_ref, conv_ref, p_ref, wo_ref, gffn_ref, w1_ref, w3_ref, w2_ref,
                    gple_ref, wg_ref, wp_ref, o_ref, acc_sc, hn_sc, h_sc):
    h = (x_ref[...] + _dot(attn_ref[...], wo_ref[0:DA_WIDTH, :])
         + _dot(conv_ref[...], wo_ref[DA_WIDTH:DA_WIDTH + SC_WIDTH, :]))
    hn_sc[...] = _rms(h, gffn_ref[...]).astype(BF16)
    h_sc[...] = h
    _swiglu_chunks(hn_sc, w1_ref, w3_ref, w2_ref, acc_sc)
    o_ref[...] = _ple(h_sc[...] + acc_sc[...], p_ref, gple_ref, wg_ref, wp_ref)


def _l0_post(x2, attn, conv, p0, wo, gffn, w1, w3, w2, gple, wg, wp):
    n = x2.shape[0]
    tm = ROW_TILE
    return pl.pallas_call(
        _l0_post_kernel,
        grid=(n // tm,),
        in_specs=[_rows(tm, D_MODEL), _rows(tm, DA_WIDTH), _rows(tm, SC_WIDTH), _rows(tm, PLE_DIM),
                  _resident(wo.shape), _resident(gffn.shape), _resident(w1.shape),
                  _resident(w3.shape), _resident(w2.shape), _resident(gple.shape),
                  _resident(wg.shape), _resident(wp.shape)],
        out_specs=_rows(tm, D_MODEL),
        out_shape=jax.ShapeDtypeStruct((n, D_MODEL), F32),
        scratch_shapes=[pltpu.VMEM((tm, D_MODEL), F32), pltpu.VMEM((tm, D_MODEL), BF16),
                        pltpu.VMEM((tm, D_MODEL), F32)],
        compiler_params=pltpu.CompilerParams(dimension_semantics=("arbitrary",),
                                             vmem_limit_bytes=VMEM_LIMIT),
        name="l0_outproj_swiglu_ple",
    )(x2, attn, conv, p0, wo, gffn, w1, w3, w2, gple, wg, wp)


def _split_bf16(a):
    hi = a.astype(BF16)
    return hi, (a - hi.astype(F32)).astype(BF16)


def _l1_mix_kernel(h_ref, gmix_ref, win_ref, lng_ref, lnb_ref, ws_ref, bs_ref, wout_ref,
                   gffn_ref, wr_ref,
                   h1_ref, hn2_ref, ridx_ref, rgate_ref, hn_sc, u_sc, v_sc, vn_sc, gated_sc):
    tm = h_ref.shape[0]
    hn_sc[...] = _rms(h_ref[...], gmix_ref[...]).astype(BF16)

    def gelu(z):
        return 0.5 * z * (1.0 + lax.erf(z * (2.0 ** -0.5)))

    for dst, base in ((v_sc, GM_WIDTH), (u_sc, 0)):
        for c in range(GM_WIDTH // FF_CHUNK):
            cols = slice(c * FF_CHUNK, (c + 1) * FF_CHUNK)
            dst[:, cols] = gelu(_dot(hn_sc[...], win_ref[:, base + c * FF_CHUNK:base + (c + 1) * FF_CHUNK]))

    vv = v_sc[...]
    mu = jnp.mean(vv, axis=-1, keepdims=True)
    vc = vv - mu
    var = jnp.mean(vc * vc, axis=-1, keepdims=True)
    vn_sc[...] = (vc * lax.rsqrt(var + LN_EPS) * lng_ref[...] + lnb_ref[...]).astype(BF16)

    n_chunks = tm // GM_CHUNK
    row = lax.broadcasted_iota(jnp.int32, (GM_CHUNK, GM_CHUNK), 0)
    col = lax.broadcasted_iota(jnp.int32, (GM_CHUNK, GM_CHUNK), 1)
    for g in range(GM_GROUPS):
        gs = slice(g * LANES, (g + 1) * LANES)
        wm = jnp.where(col <= row, ws_ref[g], 0.0).astype(BF16)
        rhs = jnp.concatenate(
            [vn_sc[r * GM_CHUNK:(r + 1) * GM_CHUNK, gs] for r in range(n_chunks)], axis=1)
        vs = _dot(wm, rhs) + bs_ref[:, g:g + 1]
        for r in range(n_chunks):
            rs = slice(r * GM_CHUNK, (r + 1) * GM_CHUNK)
            gated_sc[rs, gs] = (u_sc[rs, gs] * vs[:, r * LANES:(r + 1) * LANES]).astype(BF16)

    h1 = h_ref[...] + _dot(gated_sc[...], wout_ref[...])
    h1_ref[...] = h1

    hn2 = _rms(h1, gffn_ref[...])
    hn2_ref[...] = hn2
    x_hi, x_lo = _split_bf16(hn2)
    a_hi = _dot(x_hi, wr_ref[...])
    logits = a_hi + pltpu.roll(a_hi, LANES - N_EXPERTS, 1) + _dot(x_lo, wr_ref[...])
    lane = lax.broadcasted_iota(jnp.int32, logits.shape, 1)
    logits = jnp.where(lane < N_EXPERTS, logits, -jnp.inf)
    v1 = jnp.max(logits, axis=-1, keepdims=True)
    i1 = jnp.min(jnp.where(logits == v1, lane, LANES), axis=-1, keepdims=True)
    rest = jnp.where(lane == i1, -jnp.inf, logits)
    v2 = jnp.max(rest, axis=-1, keepdims=True)
    i2 = jnp.min(jnp.where(rest == v2, lane, LANES), axis=-1, keepdims=True)
    e2 = jnp.exp(v2 - v1)
    ridx_ref[:, 0:1] = i1
    ridx_ref[:, 1:2] = i2
    rgate_ref[:, 0:1] = 1.0 / (1.0 + e2)
    rgate_ref[:, 1:2] = e2 / (1.0 + e2)


def _l1_mix(h, gmix, win, lng, lnb, ws, bs_t, wout, gffn, wr):
    n = h.shape[0]
    tm = ROW_TILE
    return pl.pallas_call(
        _l1_mix_kernel,
        grid=(n // tm,),
        in_specs=[_rows(tm, D_MODEL)] + [_resident(a.shape) for a in
                                         (gmix, win, lng, lnb, ws, bs_t, wout, gffn, wr)],
        out_specs=[_rows(tm, D_MODEL), _rows(tm, D_MODEL), _rows(tm, TOP_K), _rows(tm, TOP_K)],
        out_shape=[jax.ShapeDtypeStruct((n, D_MODEL), F32), jax.ShapeDtypeStruct((n, D_MODEL), F32),
                   jax.ShapeDtypeStruct((n, TOP_K), jnp.int32),
                   jax.ShapeDtypeStruct((n, TOP_K), F32)],
        scratch_shapes=[pltpu.VMEM((tm, D_MODEL), BF16), pltpu.VMEM((tm, GM_WIDTH), F32),
                        pltpu.VMEM((tm, GM_WIDTH), F32), pltpu.VMEM((tm, GM_WIDTH), BF16),
                        pltpu.VMEM((tm, GM_WIDTH), BF16)],
        compiler_params=pltpu.CompilerParams(dimension_semantics=("arbitrary",),
                                             vmem_limit_bytes=VMEM_LIMIT),
        name="l1_gmlp_router",
    )(h, gmix, win, lng, lnb, ws, bs_t, wout, gffn, wr)


def _moe_kernel(bexp_ref, nvalid_ref, tok_ref, dst_ref, hn_hbm, w1_ref, w3_ref, w2_ref, y_hbm,
                xbuf, xb_sc, acc_sc, ybuf, gsem, ssem, fence_sem, fence_sink):
    i = pl.program_id(0)
    n_valid = nvalid_ref[0]
    tm = MOE_TILE
    prev = jnp.maximum(i - 1, 0)

    def gather_row(blk, r):
        return pltpu.make_async_copy(hn_hbm.at[pl.ds(tok_ref[blk * tm + r], 1)],
                                     xbuf.at[pl.ds(r, 1)], gsem)

    def scatter_row(blk, r):
        return pltpu.make_async_copy(ybuf.at[pl.ds(r, 1)],
                                     y_hbm.at[pl.ds(dst_ref[blk * tm + r], 1)], ssem)

    def gather_wait():
        pltpu.make_async_copy(hn_hbm.at[pl.ds(0, tm)], xbuf, gsem).wait()

    def scatter_wait():
        pltpu.make_async_copy(ybuf, y_hbm.at[pl.ds(0, tm)], ssem).wait()

    @pl.when(i < n_valid)
    def _():
        @pl.when(i == 0)
        def _():
            def issue(r, carry):
                gather_row(0, r).start()
                return carry
            lax.fori_loop(0, tm, issue, 0)

        gather_wait()
        xb_sc[...] = xbuf[...].astype(BF16)
        has_next = i + 1 < n_valid
        has_prev = i >= 1

        def copies(c):
            if c not in MOE_FENCE_CHUNKS:
                return
            g = MOE_FENCE_CHUNKS.index(c)
            if g > 0:
                fence_sink[0] = pl.semaphore_read(fence_sem)
            if g == len(MOE_FENCE_CHUNKS) - 1:
                return
            for r in range(MOE_COPY_ROWS[g], MOE_COPY_ROWS[g + 1]):
                @pl.when(has_next)
                def _():
                    gather_row(i + 1, r).start(priority=1)

                @pl.when(has_prev)
                def _():
                    scatter_row(prev, r).start(priority=r % 2)

        _swiglu_chunks(xb_sc, w1_ref, w3_ref, w2_ref, acc_sc, per_chunk=copies)

        @pl.when(has_prev)
        def _():
            scatter_wait()

        ybuf[...] = acc_sc[...]

    @pl.when(i == n_valid)
    def _():
        def issue(r, carry):
            scatter_row(prev, r).start()
            return carry
        lax.fori_loop(0, tm, issue, 0)
        scatter_wait()
        ybuf[...] = jnp.zeros(ybuf.shape, F32)
        plane = y_hbm.shape[0] // TOP_K
        for b in range(TOP_K):
            spare = pltpu.make_async_copy(ybuf, y_hbm.at[pl.ds((b + 1) * plane - tm, tm)], ssem)
            spare.start()
            spare.wait()


def _moe_experts(block_expert, n_valid, row_token, row_dst, hn2, w1, w3, w2, n_out_rows):
    n_blocks = block_expert.shape[0]
    tm = MOE_TILE
    per_expert = lambda w: pl.BlockSpec((None,) + w.shape[1:], lambda i, be, *_: (be[i], 0, 0))
    return pl.pallas_call(
        _moe_kernel,
        grid_spec=pltpu.PrefetchScalarGridSpec(
            num_scalar_prefetch=4,
            grid=(n_blocks,),
            in_specs=[pl.BlockSpec(memory_space=pl.ANY), per_expert(w1), per_expert(w3),
                      per_expert(w2)],
            out_specs=pl.BlockSpec(memory_space=pl.ANY),
            scratch_shapes=[pltpu.VMEM((tm, D_MODEL), F32), pltpu.VMEM((tm, D_MODEL), BF16),
                            pltpu.VMEM((tm, D_MODEL), F32), pltpu.VMEM((tm, D_MODEL), F32),
                            pltpu.SemaphoreType.DMA(()), pltpu.SemaphoreType.DMA(()),
                            pltpu.SemaphoreType.REGULAR(()), pltpu.SMEM((1,), jnp.int32)],
        ),
        out_shape=jax.ShapeDtypeStruct((n_out_rows, D_MODEL), F32),
        compiler_params=pltpu.CompilerParams(dimension_semantics=("arbitrary",),
                                             vmem_limit_bytes=VMEM_LIMIT),
        name="l1_expert_swiglu",
    )(block_expert, n_valid, row_token, row_dst, hn2, w1, w3, w2)


def _combine_kernel(h_ref, y0_ref, y1_ref, gate_ref, p_ref, gple_ref, wg_ref, wp_ref, gfin_ref,
                    o_ref):
    gate = gate_ref[...]
    h = h_ref[...] + (y0_ref[...] * gate[:, 0:1] + y1_ref[...] * gate[:, 1:2])
    h = _ple(h, p_ref, gple_ref, wg_ref, wp_ref)
    o_ref[...] = _rms(h, gfin_ref[...])


def _combine(h1, y_rows, rgate, p1, gple, wg, wp, gfin):
    n = h1.shape[0]
    tm = ROW_TILE
    plane_tiles = y_rows.shape[0] // TOP_K // tm
    return pl.pallas_call(
        _combine_kernel,
        grid=(n // tm,),
        in_specs=[_rows(tm, D_MODEL), _rows(tm, D_MODEL),
                  pl.BlockSpec((tm, D_MODEL), lambda i: (plane_tiles + i, 0)), _rows(tm, TOP_K),
                  _rows(tm, PLE_DIM), _resident(gple.shape), _resident(wg.shape),
                  _resident(wp.shape), _resident(gfin.shape)],
        out_specs=_rows(tm, D_MODEL),
        out_shape=jax.ShapeDtypeStruct((n, D_MODEL), F32),
        compiler_params=pltpu.CompilerParams(dimension_semantics=("arbitrary",),
                                             vmem_limit_bytes=VMEM_LIMIT),
        name="l1_combine_ple_norm",
    )(h1, y_rows, y_rows, rgate, p1, gple, wg, wp, gfin)


def _routing_tables(ridx, n_blocks):
    n_assign = ridx.size
    e_flat = ridx.reshape(-1)
    experts = jnp.arange(N_EXPERTS, dtype=jnp.int32)[:, None]
    order = jnp.argsort(e_flat, stable=True).astype(jnp.int32)
    counts = jnp.sum((e_flat[None, :] == experts).astype(jnp.int32), axis=1)
    starts = jnp.cumsum(counts) - counts
    padded = (counts + MOE_TILE - 1) // MOE_TILE * MOE_TILE
    pad_ends = jnp.cumsum(padded)
    pad_starts = pad_ends - padded
    n_rows = (n_blocks + 2) * MOE_TILE
    r = jnp.arange(n_rows, dtype=jnp.int32)
    row_expert = jnp.minimum(jnp.sum((r[None, :] >= pad_ends[:, None]).astype(jnp.int32), axis=0),
                             N_EXPERTS - 1)
    of_row = lambda per_expert: jnp.sum(
        jnp.where(row_expert[None, :] == experts, per_expert[:, None], 0), axis=0)
    j = r - of_row(pad_starts)
    valid = (j < of_row(counts)) & (r < pad_ends[-1])
    src = jnp.clip(of_row(starts) + j, 0, n_assign - 1)
    assign = jnp.where(valid, order[src], -1)
    row_token = jnp.maximum(assign, 0) // TOP_K
    plane = n_assign // TOP_K + MOE_TILE
    spare = (r // MOE_TILE % 2) * plane + n_assign // TOP_K + r % MOE_TILE
    row_dst = jnp.where(assign >= 0, (assign % TOP_K) * plane + assign // TOP_K, spare)
    block_start = jnp.arange(n_blocks, dtype=jnp.int32) * MOE_TILE
    block_expert = jnp.minimum(
        jnp.sum((block_start[:, None] >= pad_ends[None, :]).astype(jnp.int32), axis=1),
        N_EXPERTS - 1).astype(jnp.int32)
    n_valid = (pad_ends[-1:] // MOE_TILE).astype(jnp.int32)
    return row_token, row_dst, block_expert, n_valid


def kernel(x, p, positions, ln_mix, ln_ffn, ln_ple, a_w_in, a_lambda, a_subln, a_conv_w, a_w_out,
           ffn_w1, ffn_w3, ffn_w2, c_w_in, c_ln_g, c_ln_b, c_w_s, c_b_s, c_w_out, router_w,
           moe_w1, moe_w3, moe_w2, ple_gate, ple_proj, final_norm):
    bsz, seq, _ = x.shape
    n = bsz * seq
    assert seq % ROW_TILE == 0 and seq % ATTN_TILE == 0 and ROW_TILE % GM_CHUNK == 0
    x2 = x.reshape(n, D_MODEL)
    pos3 = positions.reshape(n // ROW_TILE, 1, ROW_TILE)
    row = lambda a: a.reshape(1, -1)

    freq = (ROPE_THETA ** (-jnp.arange(0, ROT_DIM, 2, dtype=F32) / ROT_DIM)).reshape(-1, 1)
    qt, k, vt, conv = _inproj(x2, pos3, row(ln_mix[0]), freq, a_conv_w[0].T, a_w_in[0].astype(BF16), seq)
    lambda_init = 0.8 - 0.6 * math.exp(-0.3 * 0)
    moe_f32 = [w[0].reshape(-1, w.shape[-1]) for w in (moe_w1, moe_w3, moe_w2)]
    attn, moe_bf16 = _diff_attention(qt, k, vt, a_lambda[0], a_subln[0].reshape(-1, 1), lambda_init,
                                     bsz, seq, moe_f32)
    mw1, mw3, mw2 = (wb.reshape(w.shape[1:]) for wb, w in zip(moe_bf16, (moe_w1, moe_w3, moe_w2)))
    h = _l0_post(x2, attn, conv, p[0].reshape(n, PLE_DIM), a_w_out[0].astype(BF16), row(ln_ffn[0]),
                 ffn_w1[0].astype(BF16), ffn_w3[0].astype(BF16), ffn_w2[0].astype(BF16),
                 row(ln_ple[0]), ple_gate[0].astype(BF16), ple_proj[0].astype(BF16))

    wr = jnp.pad(jnp.concatenate(_split_bf16(router_w[0]), axis=1),
                 ((0, 0), (0, LANES - 2 * N_EXPERTS)))
    h1, hn2, ridx, rgate = _l1_mix(h, row(ln_mix[1]), c_w_in[0].astype(BF16), row(c_ln_g[0]),
                                   row(c_ln_b[0]), c_w_s[0], c_b_s[0].T, c_w_out[0].astype(BF16),
                                   row(ln_ffn[1]), wr)
    n_blocks = -(-(n * TOP_K + N_EXPERTS * (MOE_TILE - 1)) // MOE_TILE)
    row_token, row_dst, block_expert, n_valid = _routing_tables(ridx, n_blocks)
    y_rows = _moe_experts(block_expert, n_valid, row_token, row_dst, hn2, mw1, mw3, mw2,
                          TOP_K * (n + MOE_TILE))
    out = _combine(h1, y_rows, rgate, p[1].reshape(n, PLE_DIM), row(ln_ple[1]),
                   ple_gate[1].astype(BF16), ple_proj[1].astype(BF16), row(final_norm))
    return out.reshape(bsz, seq, D_MODEL)
```

```python
import functools
import math

import jax
import jax.numpy as jnp
from jax import lax
from jax.experimental import pallas as pl
from jax.experimental.pallas import tpu as pltpu

F32 = jnp.float32
BF16 = jnp.bfloat16

D_MODEL = 1024
PLE_DIM = 256
DA_HEADS = 4
DA_HEAD_DIM = 64
DA_V_DIM = 2 * DA_HEAD_DIM
DA_WIDTH = DA_HEADS * DA_V_DIM
ROPE_THETA = 500000.0
ROT_DIM = DA_HEAD_DIM // 4
SC_WIDTH = 512
GM_WIDTH = D_MODEL
GM_GROUPS = 8
GM_CHUNK = 128
D_FF = 2816
N_EXPERTS = 8
TOP_K = 2
RMS_EPS = 1e-6
LN_EPS = 1e-5

LANES = 128
FF_CHUNK = 256
N_FF_CHUNKS = D_FF // FF_CHUNK
ROW_TILE = 512
ATTN_TILE = 256
ATTN_Q_TILE = 512
Q_SCALE = DA_HEAD_DIM ** -0.5 * math.log2(math.e)
VT_ROWS = DA_V_DIM + 16
MOE_TILE = 512
MOE_FENCE_CHUNKS = (0, 5, 9, 10)
MOE_COPY_ROWS = (0, 256, 504, 512)
CONV_HALO = 8
VMEM_LIMIT = 56 * 2**20


def _resident(shape):
    nd = len(shape)
    return pl.BlockSpec(shape, lambda *_: (0,) * nd, pipeline_mode=pl.Buffered(1))


def _rows(tile, width):
    return pl.BlockSpec((tile, width), lambda i, *_: (i, 0))


def _rms(x, g):
    return x * lax.rsqrt(jnp.mean(x * x, axis=-1, keepdims=True) + RMS_EPS) * g


def _dot(a, b):
    return jnp.dot(a, b, preferred_element_type=F32)


def _swiglu_chunks(hn_ref, w1_ref, w3_ref, w2_ref, acc_ref, per_chunk=None):
    for c in range(N_FF_CHUNKS):
        cols = slice(c * FF_CHUNK, (c + 1) * FF_CHUNK)
        g = _dot(hn_ref[...], w1_ref[:, cols])
        u = _dot(hn_ref[...], w3_ref[:, cols])
        a = (g * jax.nn.sigmoid(g) * u).astype(BF16)
        if c == 0:
            acc_ref[...] = _dot(a, w2_ref[cols, :])
        else:
            acc_ref[...] += _dot(a, w2_ref[cols, :])
        if per_chunk is not None:
            per_chunk(c)


def _ple(h, p_ref, gple_ref, wg_ref, wp_ref):
    gate = jax.nn.sigmoid(_dot(_rms(h, gple_ref[...]).astype(BF16), wg_ref[...]))
    return h + gate * _dot(p_ref[...].astype(BF16), wp_ref[...])


def _inproj_kernel(x_ref, pos_ref, g_ref, freq_ref, cw_ref, w_ref,
                   qt_ref, k_ref, vt_ref, conv_ref, cbuf, *, tiles_per_seq):
    i = pl.program_id(0)
    tm = x_ref.shape[0]
    hn = _rms(x_ref[...], g_ref[...]).astype(BF16)

    half = ROT_DIM // 2
    ang = freq_ref[...] * pos_ref[...].astype(F32)
    cos_t = jnp.cos(ang)
    sin_t = jnp.sin(ang)
    zero_t = jnp.zeros_like(ang)
    rest = DA_HEAD_DIM - ROT_DIM
    pattern = lambda a, b, fill: jnp.concatenate(
        [a, b, jnp.full((rest, tm), fill, F32)] * (LANES // DA_HEAD_DIM), axis=0).T
    cos = pattern(cos_t, cos_t, 1.0)
    sin_up = pattern(-sin_t, zero_t, 0.0)
    sin_dn = pattern(zero_t, sin_t, 0.0)

    def rope(t):
        return (t * cos + pltpu.roll(t, LANES - half, 1) * sin_up
                + pltpu.roll(t, half, 1) * sin_dn)

    zq = _dot(hn, w_ref[:, 0:DA_WIDTH])
    zk = _dot(hn, w_ref[:, DA_WIDTH:2 * DA_WIDTH])
    zv = _dot(hn, w_ref[:, 2 * DA_WIDTH:3 * DA_WIDTH])
    ones = jnp.ones((VT_ROWS - DA_V_DIM, ATTN_TILE), BF16)
    for hd in range(DA_HEADS):
        sl = slice(hd * LANES, (hd + 1) * LANES)
        qt_ref[sl, :] = (rope(zq[:, sl]) * Q_SCALE).T.astype(BF16)
        k_ref[:, sl] = rope(zk[:, sl]).astype(BF16)
        for u in range(tm // ATTN_TILE):
            vt_ref[hd, u, 0:DA_V_DIM, :] = zv[u * ATTN_TILE:(u + 1) * ATTN_TILE, sl].T.astype(BF16)
            vt_ref[hd, u, DA_V_DIM:VT_ROWS, :] = ones

    off = 3 * DA_WIDTH
    b_gate = _dot(hn, w_ref[:, off:off + SC_WIDTH])
    c_gate = _dot(hn, w_ref[:, off + SC_WIDTH:off + 2 * SC_WIDTH])
    hc = _dot(hn, w_ref[:, off + 2 * SC_WIDTH:off + 3 * SC_WIDTH])

    @pl.when(i % tiles_per_seq == 0)
    def _():
        cbuf[0:CONV_HALO, :] = jnp.zeros((CONV_HALO, SC_WIDTH), F32)

    @pl.when(i % tiles_per_seq != 0)
    def _():
        cbuf[0:CONV_HALO, :] = cbuf[tm:tm + CONV_HALO, :]

    ch = c_gate * hc
    cbuf[CONV_HALO:CONV_HALO + tm, :] = ch
    ch1 = cbuf[CONV_HALO - 1:CONV_HALO - 1 + tm, :]
    ch2 = cbuf[CONV_HALO - 2:CONV_HALO - 2 + tm, :]
    conv = b_gate * (cw_ref[2:3, :] * ch + cw_ref[1:2, :] * ch1 + cw_ref[0:1, :] * ch2)
    conv_ref[...] = conv.astype(BF16)


def _inproj(x2, pos3, g, freq, cw, w, seq):
    n = x2.shape[0]
    tm = ROW_TILE
    out = jax.ShapeDtypeStruct((n, DA_WIDTH), BF16)
    sub = tm // ATTN_TILE
    return pl.pallas_call(
        functools.partial(_inproj_kernel, tiles_per_seq=seq // tm),
        grid=(n // tm,),
        in_specs=[_rows(tm, D_MODEL), pl.BlockSpec((None, 1, tm), lambda i: (i, 0, 0)),
                  _resident(g.shape), _resident(freq.shape),
                  _resident(cw.shape), _resident(w.shape)],
        out_specs=[pl.BlockSpec((DA_WIDTH, tm), lambda i: (0, i)), _rows(tm, DA_WIDTH),
                   pl.BlockSpec((DA_HEADS, sub, VT_ROWS, ATTN_TILE), lambda i: (0, i, 0, 0)),
                   _rows(tm, SC_WIDTH)],
        out_shape=[jax.ShapeDtypeStruct((DA_WIDTH, n), BF16), out,
                   jax.ShapeDtypeStruct((DA_HEADS, n // ATTN_TILE, VT_ROWS, ATTN_TILE), BF16), out],
        scratch_shapes=[pltpu.VMEM((tm + CONV_HALO, SC_WIDTH), F32)],
        compiler_params=pltpu.CompilerParams(dimension_semantics=("arbitrary",),
                                             vmem_limit_bytes=VMEM_LIMIT),
        name="l0_inproj",
    )(x2, pos3, g, freq, cw, w)


def _cast_slice(step, n_slices, grid_steps, src_refs, dst_refs, in_bufs, out_bufs, in_sem, out_sem):
    assert 2 <= n_slices <= grid_steps
    slot = step % 2
    other = 1 - slot

    def fetch(s, sl):
        return [pltpu.make_async_copy(src.at[pl.ds(s * buf.shape[1], buf.shape[1])], buf.at[sl],
                                      in_sem.at[sl]) for src, buf in zip(src_refs, in_bufs)]

    def write_back(s, sl):
        return [pltpu.make_async_copy(buf.at[sl], dst.at[pl.ds(s * buf.shape[1], buf.shape[1])],
                                      out_sem.at[sl]) for dst, buf in zip(dst_refs, out_bufs)]

    @pl.when(step == 0)
    def _():
        for c in fetch(0, 0):
            c.start()

    @pl.when(step < n_slices)
    def _():
        for c in fetch(step, slot):
            c.wait()

        @pl.when(step + 1 < n_slices)
        def _():
            for c in fetch(step + 1, other):
                c.start()

        @pl.when(step >= 2)
        def _():
            for c in write_back(step - 2, slot):
                c.wait()

        for src, dst in zip(in_bufs, out_bufs):
            dst[slot] = src[slot].astype(dst.dtype)
        for c in write_back(step, slot):
            c.start()

    @pl.when(step == min(n_slices, grid_steps - 1))
    def _():
        for s in (n_slices - 2, n_slices - 1):
            for c in write_back(s, s % 2):
                c.wait()


def _attn_kernel(lam_ref, sg_ref, qt_ref, k_ref, vt_ref, *rest, lambda_init, cast_groups, n_steps,
                 n_q):
    n_cast = sum(cnt for _, cnt in cast_groups)
    cast_src, rest = rest[:n_cast], rest[n_cast:]
    o_ref, rest = rest[0], rest[1:]
    cast_dst, rest = rest[:n_cast], rest[n_cast:]
    qc_sc, m_sc, alpha_sc, acc_sc, s_sc, p_sc = rest[:6]
    cast_in, cast_out = rest[6:6 + n_cast], rest[6 + n_cast:6 + 2 * n_cast]
    sems = rest[6 + 2 * n_cast:]
    step = ((pl.program_id(0) * pl.num_programs(1) + pl.program_id(1)) * pl.num_programs(2)
            + pl.program_id(2))
    first = 0
    for g, (n_slices, cnt) in enumerate(cast_groups):
        grp = slice(first, first + cnt)
        _cast_slice(step, n_slices, n_steps, cast_src[grp], cast_dst[grp], cast_in[grp],
                    cast_out[grp], sems[2 * g], sems[2 * g + 1])
        first += cnt

    qi = pl.program_id(2)
    tq = ATTN_Q_TILE
    tk = ATTN_TILE
    n_sub = tq // tk

    qt = qt_ref[...]
    dim = lax.broadcasted_iota(jnp.int32, (LANES, 1), 0)
    zero = jnp.zeros_like(qt)
    qc_sc[0] = jnp.where(dim < DA_HEAD_DIM, qt, zero)
    qc_sc[1] = jnp.where(dim >= DA_HEAD_DIM, qt, zero)
    m_sc[...] = jnp.full(m_sc.shape, -jnp.inf, F32)
    acc_sc[...] = jnp.zeros(acc_sc.shape, F32)

    def scores(j, diagonal):
        for c in range(2):
            for u in range(n_sub):
                ks = (j * n_sub + u) * tk
                su = _dot(k_ref[ks:ks + tk, :], qc_sc[c])
                if diagonal:
                    key = lax.broadcasted_iota(jnp.int32, (tk, tq), 0) + u * tk
                    qry = lax.broadcasted_iota(jnp.int32, (tk, tq), 1)
                    su = jnp.where(key <= qry, su, -jnp.inf)
                s_sc[c, u * tk:(u + 1) * tk, :] = su

    def softmax():
        for c in range(2):
            s = s_sc[c]
            m_prev = m_sc[c]
            m_new = jnp.maximum(m_prev, jnp.max(s, axis=0, keepdims=True))
            alpha_sc[c] = jnp.exp2(m_prev - m_new)
            m_sc[c] = m_new
            p_sc[c] = jnp.exp2(s - m_new).astype(BF16)

    def fold(j):
        for c in range(2):
            pv = None
            for u in range(n_sub):
                d = _dot(vt_ref[j * n_sub + u], p_sc[c, u * tk:(u + 1) * tk, :])
                pv = d if pv is None else pv + d
            acc_sc[c] = alpha_sc[c] * acc_sc[c] + pv

    def run(last):
        scores(0, diagonal=last == 0)
        for t in range(last + 1):
            if t >= 1:
                fold(t - 1)
            softmax()
            if t < last:
                scores(t + 1, diagonal=t + 1 == last)
        fold(last)

    for last in range(n_q):
        pl.when(qi == last)(functools.partial(run, last))

    lp = lam_ref[...]
    lam = (jnp.exp(jnp.sum(lp[0:1] * lp[1:2], axis=-1, keepdims=True))
           - jnp.exp(jnp.sum(lp[2:3] * lp[3:4], axis=-1, keepdims=True)) + lambda_init)
    a0 = acc_sc[0]
    a1 = acc_sc[1]
    o = (a0[0:DA_V_DIM] / a0[DA_V_DIM:DA_V_DIM + 1]
         - lam * (a1[0:DA_V_DIM] / a1[DA_V_DIM:DA_V_DIM + 1]))
    o = o * lax.rsqrt(jnp.mean(o * o, axis=0, keepdims=True) + RMS_EPS) * sg_ref[...]
    o_ref[...] = (o * (1.0 - lambda_init)).T.astype(o_ref.dtype)


def _diff_attention(qt, k, vt, lam_p, subln_col, lambda_init, bsz, seq, cast_f32):
    tq = ATTN_Q_TILE
    tk = ATTN_TILE
    nq = seq // tq
    nk = seq // tk
    n_steps = bsz * DA_HEADS * nq
    def n_slices(w):
        return max(c for c in range(1, n_steps + 1)
                   if w.shape[0] % c == 0 and (w.shape[0] // c) % 16 == 0)
    order = sorted(range(len(cast_f32)), key=lambda a: n_slices(cast_f32[a]))
    cast_sorted = [cast_f32[a] for a in order]
    counts = [n_slices(w) for w in cast_sorted]
    cast_groups = tuple((c, counts.count(c)) for c in sorted(set(counts)))
    n_cast = len(cast_f32)
    slices = [(w.shape[0] // c, w.shape[1]) for w, c in zip(cast_sorted, counts)]
    any_spec = pl.BlockSpec(memory_space=pl.ANY)
    out = pl.pallas_call(
        functools.partial(_attn_kernel, lambda_init=lambda_init, cast_groups=cast_groups,
                          n_steps=n_steps, n_q=nq),
        grid=(bsz, DA_HEADS, nq),
        in_specs=[pl.BlockSpec(lam_p.shape, lambda b, h, i: (0, 0)),
                  pl.BlockSpec(subln_col.shape, lambda b, h, i: (0, 0)),
                  pl.BlockSpec((LANES, tq), lambda b, h, i: (h, b * nq + i)),
                  pl.BlockSpec((seq, LANES), lambda b, h, i: (b, h)),
                  pl.BlockSpec((None, nk, VT_ROWS, tk), lambda b, h, i: (h, b, 0, 0))]
                 + [any_spec] * n_cast,
        out_specs=[pl.BlockSpec((tq, LANES), lambda b, h, i: (b * nq + i, h))] + [any_spec] * n_cast,
        out_shape=[jax.ShapeDtypeStruct(k.shape, BF16)]
                  + [jax.ShapeDtypeStruct(w.shape, BF16) for w in cast_sorted],
        scratch_shapes=[pltpu.VMEM((2, LANES, tq), BF16), pltpu.VMEM((2, 1, tq), F32),
                        pltpu.VMEM((2, 1, tq), F32), pltpu.VMEM((2, VT_ROWS, tq), F32),
                        pltpu.VMEM((2, tq, tq), F32), pltpu.VMEM((2, tq, tq), BF16)]
                       + [pltpu.VMEM((2,) + s, F32) for s in slices]
                       + [pltpu.VMEM((2,) + s, BF16) for s in slices]
                       + [pltpu.SemaphoreType.DMA((2,))] * (2 * len(cast_groups)),
        compiler_params=pltpu.CompilerParams(
            dimension_semantics=("arbitrary", "arbitrary", "arbitrary"),
            vmem_limit_bytes=VMEM_LIMIT),
        name="l0_diff_attention",
    )(lam_p, subln_col, qt, k, vt, *cast_sorted)
    cast_bf16 = [None] * n_cast
    for pos, a in enumerate(order):
        cast_bf16[a] = out[1 + pos]
    return out[0], cast_bf16


def _l0_post_kernel(x_ref, attn_ref, conv_ref, p_ref, wo_ref, gffn_ref, w1_ref, w3_ref, w2_ref,
                    gple_ref, wg_ref, wp_ref, o_ref, acc_sc, hn_sc, h_sc):
    h = (x_ref[...] + _dot(attn_ref[...], wo_ref[0:DA_WIDTH, :])
         + _dot(conv_ref[...], wo_ref[DA_WIDTH:DA_WIDTH + SC_WIDTH, :]))
    hn_sc[...] = _rms(h, gffn_ref[...]).astype(BF16)
    h_sc[...] = h
    _swiglu_chunks(hn_sc, w1_ref, w3_ref, w2_ref, acc_sc)
    o_ref[...] = _ple(h_sc[...] + acc_sc[...], p_ref, gple_ref, wg_ref, wp_ref)


def _l0_post(x2, attn, conv, p0, wo, gffn, w1, w3, w2, gple, wg, wp):
    n = x2.shape[0]
    tm = ROW_TILE
    return pl.pallas_call(
        _l0_post_kernel,
        grid=(n // tm,),
        in_specs=[_rows(tm, D_MODEL), _rows(tm, DA_WIDTH), _rows(tm, SC_WIDTH), _rows(tm, PLE_DIM),
                  _resident(wo.shape), _resident(gffn.shape), _resident(w1.shape),
                  _resident(w3.shape), _resident(w2.shape), _resident(gple.shape),
                  _resident(wg.shape), _resident(wp.shape)],
        out_specs=_rows(tm, D_MODEL),
        out_shape=jax.ShapeDtypeStruct((n, D_MODEL), F32),
        scratch_shapes=[pltpu.VMEM((tm, D_MODEL), F32), pltpu.VMEM((tm, D_MODEL), BF16),
                        pltpu.VMEM((tm, D_MODEL), F32)],
        compiler_params=pltpu.CompilerParams(dimension_semantics=("arbitrary",),
                                             vmem_limit_bytes=VMEM_LIMIT),
        name="l0_outproj_swiglu_ple",
    )(x2, attn, conv, p0, wo, gffn, w1, w3, w2, gple, wg, wp)


def _split_bf16(a):
    hi = a.astype(BF16)
    return hi, (a - hi.astype(F32)).astype(BF16)


def _l1_mix_kernel(h_ref, gmix_ref, win_ref, lng_ref, lnb_ref, ws_ref, bs_ref, wout_ref,
                   gffn_ref, wr_ref,
                   h1_ref, hn2_ref, ridx_ref, rgate_ref, hn_sc, u_sc, v_sc, vn_sc, gated_sc):
    tm = h_ref.shape[0]
    hn_sc[...] = _rms(h_ref[...], gmix_ref[...]).astype(BF16)

    def gelu(z):
        return 0.5 * z * (1.0 + lax.erf(z * (2.0 ** -0.5)))

    for dst, base in ((v_sc, GM_WIDTH), (u_sc, 0)):
        for c in range(GM_WIDTH // FF_CHUNK):
            cols = slice(c * FF_CHUNK, (c + 1) * FF_CHUNK)
            dst[:, cols] = gelu(_dot(hn_sc[...], win_ref[:, base + c * FF_CHUNK:base + (c + 1) * FF_CHUNK]))

    vv = v_sc[...]
    mu = jnp.mean(vv, axis=-1, keepdims=True)
    vc = vv - mu
    var = jnp.mean(vc * vc, axis=-1, keepdims=True)
    vn_sc[...] = (vc * lax.rsqrt(var + LN_EPS) * lng_ref[...] + lnb_ref[...]).astype(BF16)

    n_chunks = tm // GM_CHUNK
    row = lax.broadcasted_iota(jnp.int32, (GM_CHUNK, GM_CHUNK), 0)
    col = lax.broadcasted_iota(jnp.int32, (GM_CHUNK, GM_CHUNK), 1)
    for g in range(GM_GROUPS):
        gs = slice(g * LANES, (g + 1) * LANES)
        wm = jnp.where(col <= row, ws_ref[g], 0.0).astype(BF16)
        rhs = jnp.concatenate(
            [vn_sc[r * GM_CHUNK:(r + 1) * GM_CHUNK, gs] for r in range(n_chunks)], axis=1)
        vs = _dot(wm, rhs) + bs_ref[:, g:g + 1]
        for r in range(n_chunks):
            rs = slice(r * GM_CHUNK, (r + 1) * GM_CHUNK)
            gated_sc[rs, gs] = (u_sc[rs, gs] * vs[:, r * LANES:(r + 1) * LANES]).astype(BF16)

    h1 = h_ref[...] + _dot(gated_sc[...], wout_ref[...])
    h1_ref[...] = h1

    hn2 = _rms(h1, gffn_ref[...])
    hn2_ref[...] = hn2
    x_hi, x_lo = _split_bf16(hn2)
    a_hi = _dot(x_hi, wr_ref[...])
    logits = a_hi + pltpu.roll(a_hi, LANES - N_EXPERTS, 1) + _dot(x_lo, wr_ref[...])
    lane = lax.broadcasted_iota(jnp.int32, logits.shape, 1)
    logits = jnp.where(lane < N_EXPERTS, logits, -jnp.inf)
    v1 = jnp.max(logits, axis=-1, keepdims=True)
    i1 = jnp.min(jnp.where(logits == v1, lane, LANES), axis=-1, keepdims=True)
    rest = jnp.where(lane == i1, -jnp.inf, logits)
    v2 = jnp.max(rest, axis=-1, keepdims=True)
    i2 = jnp.min(jnp.where(rest == v2, lane, LANES), axis=-1, keepdims=True)
    e2 = jnp.exp(v2 - v1)
    ridx_ref[:, 0:1] = i1
    ridx_ref[:, 1:2] = i2
    rgate_ref[:, 0:1] = 1.0 / (1.0 + e2)
    rgate_ref[:, 1:2] = e2 / (1.0 + e2)


def _l1_mix(h, gmix, win, lng, lnb, ws, bs_t, wout, gffn, wr):
    n = h.shape[0]
    tm = ROW_TILE
    return pl.pallas_call(
        _l1_mix_kernel,
        grid=(n // tm,),
        in_specs=[_rows(tm, D_MODEL)] + [_resident(a.shape) for a in
                                         (gmix, win, lng, lnb, ws, bs_t, wout, gffn, wr)],
        out_specs=[_rows(tm, D_MODEL), _rows(tm, D_MODEL), _rows(tm, TOP_K), _rows(tm, TOP_K)],
        out_shape=[jax.ShapeDtypeStruct((n, D_MODEL), F32), jax.ShapeDtypeStruct((n, D_MODEL), F32),
                   jax.ShapeDtypeStruct((n, TOP_K), jnp.int32),
                   jax.ShapeDtypeStruct((n, TOP_K), F32)],
        scratch_shapes=[pltpu.VMEM((tm, D_MODEL), BF16), pltpu.VMEM((tm, GM_WIDTH), F32),
                        pltpu.VMEM((tm, GM_WIDTH), F32), pltpu.VMEM((tm, GM_WIDTH), BF16),
                        pltpu.VMEM((tm, GM_WIDTH), BF16)],
        compiler_params=pltpu.CompilerParams(dimension_semantics=("arbitrary",),
                                             vmem_limit_bytes=VMEM_LIMIT),
        name="l1_gmlp_router",
    )(h, gmix, win, lng, lnb, ws, bs_t, wout, gffn, wr)


def _moe_kernel(bexp_ref, nvalid_ref, tok_ref, dst_ref, hn_hbm, w1_ref, w3_ref, w2_ref, y_hbm,
                xbuf, xb_sc, acc_sc, ybuf, gsem, ssem, fence_sem, fence_sink):
    i = pl.program_id(0)
    n_valid = nvalid_ref[0]
    tm = MOE_TILE
    prev = jnp.maximum(i - 1, 0)

    def gather_row(blk, r):
        return pltpu.make_async_copy(hn_hbm.at[pl.ds(tok_ref[blk * tm + r], 1)],
                                     xbuf.at[pl.ds(r, 1)], gsem)

    def scatter_row(blk, r):
        return pltpu.make_async_copy(ybuf.at[pl.ds(r, 1)],
                                     y_hbm.at[pl.ds(dst_ref[blk * tm + r], 1)], ssem)

    def gather_wait():
        pltpu.make_async_copy(hn_hbm.at[pl.ds(0, tm)], xbuf, gsem).wait()

    def scatter_wait():
        pltpu.make_async_copy(ybuf, y_hbm.at[pl.ds(0, tm)], ssem).wait()

    @pl.when(i < n_valid)
    def _():
        @pl.when(i == 0)
        def _():
            def issue(r, carry):
                gather_row(0, r).start()
                return carry
            lax.fori_loop(0, tm, issue, 0)

        gather_wait()
        xb_sc[...] = xbuf[...].astype(BF16)
        has_next = i + 1 < n_valid
        has_prev = i >= 1

        def copies(c):
            if c not in MOE_FENCE_CHUNKS:
                return
            g = MOE_FENCE_CHUNKS.index(c)
            if g > 0:
                fence_sink[0] = pl.semaphore_read(fence_sem)
            if g == len(MOE_FENCE_CHUNKS) - 1:
                return
            for r in range(MOE_COPY_ROWS[g], MOE_COPY_ROWS[g + 1]):
                @pl.when(has_next)
                def _():
                    gather_row(i + 1, r).start(priority=1)

                @pl.when(has_prev)
                def _():
                    scatter_row(prev, r).start(priority=r % 2)

        _swiglu_chunks(xb_sc, w1_ref, w3_ref, w2_ref, acc_sc, per_chunk=copies)

        @pl.when(has_prev)
        def _():
            scatter_wait()

        ybuf[...] = acc_sc[...]

    @pl.when(i == n_valid)
    def _():
        def issue(r, carry):
            scatter_row(prev, r).start()
            return carry
        lax.fori_loop(0, tm, issue, 0)
        scatter_wait()
        ybuf[...] = jnp.zeros(ybuf.shape, F32)
        plane = y_hbm.shape[0] // TOP_K
        for b in range(TOP_K):
            spare = pltpu.make_async_copy(ybuf, y_hbm.at[pl.ds((b + 1) * plane - tm, tm)], ssem)
            spare.start()
            spare.wait()


def _moe_experts(block_expert, n_valid, row_token, row_dst, hn2, w1, w3, w2, n_out_rows):
    n_blocks = block_expert.shape[0]
    tm = MOE_TILE
    per_expert = lambda w: pl.BlockSpec((None,) + w.shape[1:], lambda i, be, *_: (be[i], 0, 0))
    return pl.pallas_call(
        _moe_kernel,
        grid_spec=pltpu.PrefetchScalarGridSpec(
            num_scalar_prefetch=4,
            grid=(n_blocks,),
            in_specs=[pl.BlockSpec(memory_space=pl.ANY), per_expert(w1), per_expert(w3),
                      per_expert(w2)],
            out_specs=pl.BlockSpec(memory_space=pl.ANY),
            scratch_shapes=[pltpu.VMEM((tm, D_MODEL), F32), pltpu.VMEM((tm, D_MODEL), BF16),
                            pltpu.VMEM((tm, D_MODEL), F32), pltpu.VMEM((tm, D_MODEL), F32),
                            pltpu.SemaphoreType.DMA(()), pltpu.SemaphoreType.DMA(()),
                            pltpu.SemaphoreType.REGULAR(()), pltpu.SMEM((1,), jnp.int32)],
        ),
        out_shape=jax.ShapeDtypeStruct((n_out_rows, D_MODEL), F32),
        compiler_params=pltpu.CompilerParams(dimension_semantics=("arbitrary",),
                                             vmem_limit_bytes=VMEM_LIMIT),
        name="l1_expert_swiglu",
    )(block_expert, n_valid, row_token, row_dst, hn2, w1, w3, w2)


def _combine_kernel(h_ref, y0_ref, y1_ref, gate_ref, p_ref, gple_ref, wg_ref, wp_ref, gfin_ref,
                    o_ref):
    gate = gate_ref[...]
    h = h_ref[...] + (y0_ref[...] * gate[:, 0:1] + y1_ref[...] * gate[:, 1:2])
    h = _ple(h, p_ref, gple_ref, wg_ref, wp_ref)
    o_ref[...] = _rms(h, gfin_ref[...])


def _combine(h1, y_rows, rgate, p1, gple, wg, wp, gfin):
    n = h1.shape[0]
    tm = ROW_TILE
    plane_tiles = y_rows.shape[0] // TOP_K // tm
    return pl.pallas_call(
        _combine_kernel,
        grid=(n // tm,),
        in_specs=[_rows(tm, D_MODEL), _rows(tm, D_MODEL),
                  pl.BlockSpec((tm, D_MODEL), lambda i: (plane_tiles + i, 0)), _rows(tm, TOP_K),
                  _rows(tm, PLE_DIM), _resident(gple.shape), _resident(wg.shape),
                  _resident(wp.shape), _resident(gfin.shape)],
        out_specs=_rows(tm, D_MODEL),
        out_shape=jax.ShapeDtypeStruct((n, D_MODEL), F32),
        compiler_params=pltpu.CompilerParams(dimension_semantics=("arbitrary",),
                                             vmem_limit_bytes=VMEM_LIMIT),
        name="l1_combine_ple_norm",
    )(h1, y_rows, y_rows, rgate, p1, gple, wg, wp, gfin)


def _routing_tables(ridx, n_blocks):
    n_assign = ridx.size
    experts = jnp.arange(N_EXPERTS, dtype=jnp.int32)
    order = jnp.argsort(ridx.reshape(-1), stable=True).astype(jnp.int32)
    counts = jnp.sum((ridx.reshape(1, -1, LANES) == experts[:, None, None]).astype(jnp.int32),
                     axis=(1, 2))
    starts = jnp.cumsum(counts) - counts
    padded = (counts + MOE_TILE - 1) // MOE_TILE * MOE_TILE
    pad_ends = jnp.cumsum(padded)
    pad_starts = pad_ends - padded
    blocks = jnp.arange(n_blocks + 2, dtype=jnp.int32)
    b_start = blocks * MOE_TILE
    b_expert = jnp.minimum(jnp.sum((b_start[:, None] >= pad_ends[None, :]).astype(jnp.int32), axis=1),
                           N_EXPERTS - 1)
    b_pos = b_start - pad_starts[b_expert]
    b_left = jnp.where(b_start < pad_ends[-1], counts[b_expert] - b_pos, 0)
    b_slot = starts[b_expert] + b_pos
    i = jnp.arange(MOE_TILE, dtype=jnp.int32)[None, :]
    src = jnp.clip(b_slot[:, None] + i, 0, n_assign - 1)
    assign = jnp.where(i < b_left[:, None], order[src], -1)
    row_token = jnp.maximum(assign, 0) // TOP_K
    plane = n_assign // TOP_K + MOE_TILE
    spare = ((blocks % 2) * plane + n_assign // TOP_K)[:, None] + i
    row_dst = jnp.where(assign >= 0, (assign % TOP_K) * plane + assign // TOP_K, spare)
    n_valid = (pad_ends[-1:] // MOE_TILE).astype(jnp.int32)
    return row_token.reshape(-1), row_dst.reshape(-1), b_expert[:n_blocks], n_valid


def kernel(x, p, positions, ln_mix, ln_ffn, ln_ple, a_w_in, a_lambda, a_subln, a_conv_w, a_w_out,
           ffn_w1, ffn_w3, ffn_w2, c_w_in, c_ln_g, c_ln_b, c_w_s, c_b_s, c_w_out, router_w,
           moe_w1, moe_w3, moe_w2, ple_gate, ple_proj, final_norm):
    bsz, seq, _ = x.shape
    n = bsz * seq
    assert seq % ROW_TILE == 0 and seq % ATTN_TILE == 0 and ROW_TILE % GM_CHUNK == 0
    x2 = x.reshape(n, D_MODEL)
    pos3 = positions.reshape(n // ROW_TILE, 1, ROW_TILE)
    row = lambda a: a.reshape(1, -1)

    freq = (ROPE_THETA ** (-jnp.arange(0, ROT_DIM, 2, dtype=F32) / ROT_DIM)).reshape(-1, 1)
    qt, k, vt, conv = _inproj(x2, pos3, row(ln_mix[0]), freq, a_conv_w[0].T, a_w_in[0].astype(BF16), seq)
    lambda_init = 0.8 - 0.6 * math.exp(-0.3 * 0)
    moe_f32 = [w[0].reshape(-1, w.shape[-1]) for w in (moe_w1, moe_w3, moe_w2)]
    attn, moe_bf16 = _diff_attention(qt, k, vt, a_lambda[0], a_subln[0].reshape(-1, 1), lambda_init,
                                     bsz, seq, moe_f32)
    mw1, mw3, mw2 = (wb.reshape(w.shape[1:]) for wb, w in zip(moe_bf16, (moe_w1, moe_w3, moe_w2)))
    h = _l0_post(x2, attn, conv, p[0].reshape(n, PLE_DIM), a_w_out[0].astype(BF16), row(ln_ffn[0]),
                 ffn_w1[0].astype(BF16), ffn_w3[0].astype(BF16), ffn_w2[0].astype(BF16),
                 row(ln_ple[0]), ple_gate[0].astype(BF16), ple_proj[0].astype(BF16))

    wr = jnp.pad(jnp.concatenate(_split_bf16(router_w[0]), axis=1),
                 ((0, 0), (0, LANES - 2 * N_EXPERTS)))
    h1, hn2, ridx, rgate = _l1_mix(h, row(ln_mix[1]), c_w_in[0].astype(BF16), row(c_ln_g[0]),
                                   row(c_ln_b[0]), c_w_s[0], c_b_s[0].T, c_w_out[0].astype(BF16),
                                   row(ln_ffn[1]), wr)
    n_blocks = -(-(n * TOP_K + N_EXPERTS * (MOE_TILE - 1)) // MOE_TILE)
    row_token, row_dst, block_expert, n_valid = _routing_tables(ridx, n_blocks)
    y_rows = _moe_experts(block_expert, n_valid, row_token, row_dst, hn2, mw1, mw3, mw2,
                          TOP_K * (n + MOE_TILE))
    out = _combine(h1, y_rows, rgate, p[1].reshape(n, PLE_DIM), row(ln_ple[1]),
                   ple_gate[1].astype(BF16), ple_proj[1].astype(BF16), row(final_norm))
    return out.reshape(bsz, seq, D_MODEL)
```

```python
import functools
import math

import jax
import jax.numpy as jnp
from jax import lax
from jax.experimental import pallas as pl
from jax.experimental.pallas import tpu as pltpu

F32 = jnp.float32
BF16 = jnp.bfloat16

D_MODEL = 1024
PLE_DIM = 256
DA_HEADS = 4
DA_HEAD_DIM = 64
DA_V_DIM = 2 * DA_HEAD_DIM
DA_WIDTH = DA_HEADS * DA_V_DIM
ROPE_THETA = 500000.0
ROT_DIM = DA_HEAD_DIM // 4
SC_WIDTH = 512
GM_WIDTH = D_MODEL
GM_GROUPS = 8
GM_CHUNK = 128
D_FF = 2816
N_EXPERTS = 8
TOP_K = 2
RMS_EPS = 1e-6
LN_EPS = 1e-5

LANES = 128
F32_SUBLANES = 8
FF_CHUNK = 256
N_FF_CHUNKS = D_FF // FF_CHUNK
ROW_TILE = 512
ATTN_TILE = 256
ATTN_Q_TILE = 512
Q_SCALE = DA_HEAD_DIM ** -0.5 * math.log2(math.e)
VT_ROWS = DA_V_DIM + 16
MOE_TILE = 512
MOE_FENCE_CHUNKS = (0, 5, 9, 10)
MOE_COPY_ROWS = (0, 256, 504, 512)
CONV_HALO = 8
VMEM_LIMIT = 56 * 2**20


def _resident(shape):
    nd = len(shape)
    return pl.BlockSpec(shape, lambda *_: (0,) * nd, pipeline_mode=pl.Buffered(1))


def _rows(tile, width):
    return pl.BlockSpec((tile, width), lambda i, *_: (i, 0))


def _layer_rows(tile, width, layer, tiles_per_layer):
    return pl.BlockSpec((tile, width), lambda i, *_: (layer * tiles_per_layer + i, 0))


def _rms(x, g):
    return x * lax.rsqrt(jnp.mean(x * x, axis=-1, keepdims=True) + RMS_EPS) * g


def _dot(a, b):
    return jnp.dot(a, b, preferred_element_type=F32)


def _swiglu_chunks(hn_ref, w1_ref, w3_ref, w2_ref, acc_ref, per_chunk=None):
    for c in range(N_FF_CHUNKS):
        cols = slice(c * FF_CHUNK, (c + 1) * FF_CHUNK)
        g = _dot(hn_ref[...], w1_ref[:, cols])
        u = _dot(hn_ref[...], w3_ref[:, cols])
        a = (g * jax.nn.sigmoid(g) * u).astype(BF16)
        if c == 0:
            acc_ref[...] = _dot(a, w2_ref[cols, :])
        else:
            acc_ref[...] += _dot(a, w2_ref[cols, :])
        if per_chunk is not None:
            per_chunk(c)


def _ple(h, p_ref, gple_ref, wg_ref, wp_ref):
    gate = jax.nn.sigmoid(_dot(_rms(h, gple_ref[...]).astype(BF16), wg_ref[...]))
    return h + gate * _dot(p_ref[...].astype(BF16), wp_ref[...])


def _inproj_kernel(x_ref, pos_ref, g_ref, freq_ref, cw_ref, w_ref,
                   qt_ref, k_ref, vt_ref, conv_ref, cbuf, *, tiles_per_seq):
    i = pl.program_id(0)
    tm = x_ref.shape[0]
    hn = _rms(x_ref[...], g_ref[...]).astype(BF16)

    half = ROT_DIM // 2
    ang = freq_ref[...] * pos_ref[...].astype(F32)
    cos_t = jnp.cos(ang)
    sin_t = jnp.sin(ang)
    zero_t = jnp.zeros_like(ang)
    rest = DA_HEAD_DIM - ROT_DIM
    pattern = lambda a, b, fill: jnp.concatenate(
        [a, b, jnp.full((rest, tm), fill, F32)] * (LANES // DA_HEAD_DIM), axis=0).T
    cos = pattern(cos_t, cos_t, 1.0)
    sin_up = pattern(-sin_t, zero_t, 0.0)
    sin_dn = pattern(zero_t, sin_t, 0.0)

    def rope(t):
        return (t * cos + pltpu.roll(t, LANES - half, 1) * sin_up
                + pltpu.roll(t, half, 1) * sin_dn)

    zq = _dot(hn, w_ref[:, 0:DA_WIDTH])
    zk = _dot(hn, w_ref[:, DA_WIDTH:2 * DA_WIDTH])
    zv = _dot(hn, w_ref[:, 2 * DA_WIDTH:3 * DA_WIDTH])
    ones = jnp.ones((VT_ROWS - DA_V_DIM, ATTN_TILE), BF16)
    for hd in range(DA_HEADS):
        sl = slice(hd * LANES, (hd + 1) * LANES)
        qt_ref[sl, :] = (rope(zq[:, sl]) * Q_SCALE).T.astype(BF16)
        k_ref[:, sl] = rope(zk[:, sl]).astype(BF16)
        for u in range(tm // ATTN_TILE):
            vt_ref[hd, u, 0:DA_V_DIM, :] = zv[u * ATTN_TILE:(u + 1) * ATTN_TILE, sl].T.astype(BF16)
            vt_ref[hd, u, DA_V_DIM:VT_ROWS, :] = ones

    off = 3 * DA_WIDTH
    b_gate = _dot(hn, w_ref[:, off:off + SC_WIDTH])
    c_gate = _dot(hn, w_ref[:, off + SC_WIDTH:off + 2 * SC_WIDTH])
    hc = _dot(hn, w_ref[:, off + 2 * SC_WIDTH:off + 3 * SC_WIDTH])

    @pl.when(i % tiles_per_seq == 0)
    def _():
        cbuf[0:CONV_HALO, :] = jnp.zeros((CONV_HALO, SC_WIDTH), F32)

    @pl.when(i % tiles_per_seq != 0)
    def _():
        cbuf[0:CONV_HALO, :] = cbuf[tm:tm + CONV_HALO, :]

    ch = c_gate * hc
    cbuf[CONV_HALO:CONV_HALO + tm, :] = ch
    ch1 = cbuf[CONV_HALO - 1:CONV_HALO - 1 + tm, :]
    ch2 = cbuf[CONV_HALO - 2:CONV_HALO - 2 + tm, :]
    conv = b_gate * (cw_ref[2:3, :] * ch + cw_ref[1:2, :] * ch1 + cw_ref[0:1, :] * ch2)
    conv_ref[...] = conv.astype(BF16)


def _inproj(x2, pos3, g, freq, cw, w, seq):
    n = x2.shape[0]
    tm = ROW_TILE
    out = jax.ShapeDtypeStruct((n, DA_WIDTH), BF16)
    sub = tm // ATTN_TILE
    return pl.pallas_call(
        functools.partial(_inproj_kernel, tiles_per_seq=seq // tm),
        grid=(n // tm,),
        in_specs=[_rows(tm, D_MODEL), pl.BlockSpec((None, 1, tm), lambda i: (i, 0, 0)),
                  _resident(g.shape), _resident(freq.shape),
                  _resident(cw.shape), _resident(w.shape)],
        out_specs=[pl.BlockSpec((DA_WIDTH, tm), lambda i: (0, i)), _rows(tm, DA_WIDTH),
                   pl.BlockSpec((DA_HEADS, sub, VT_ROWS, ATTN_TILE), lambda i: (0, i, 0, 0)),
                   _rows(tm, SC_WIDTH)],
        out_shape=[jax.ShapeDtypeStruct((DA_WIDTH, n), BF16), out,
                   jax.ShapeDtypeStruct((DA_HEADS, n // ATTN_TILE, VT_ROWS, ATTN_TILE), BF16), out],
        scratch_shapes=[pltpu.VMEM((tm + CONV_HALO, SC_WIDTH), F32)],
        compiler_params=pltpu.CompilerParams(dimension_semantics=("arbitrary",),
                                             vmem_limit_bytes=VMEM_LIMIT),
        name="l0_inproj",
    )(x2, pos3, g, freq, cw, w)


def _cast_slice(step, n_slices, grid_steps, src_refs, dst_refs, in_bufs, out_bufs, in_sem, out_sem):
    assert 2 <= n_slices <= grid_steps
    slot = step % 2
    other = 1 - slot

    def fetch(s, sl):
        return [pltpu.make_async_copy(src.at[pl.ds(s * buf.shape[1], buf.shape[1])], buf.at[sl],
                                      in_sem.at[sl]) for src, buf in zip(src_refs, in_bufs)]

    def write_back(s, sl):
        return [pltpu.make_async_copy(buf.at[sl], dst.at[pl.ds(s * buf.shape[1], buf.shape[1])],
                                      out_sem.at[sl]) for dst, buf in zip(dst_refs, out_bufs)]

    @pl.when(step == 0)
    def _():
        for c in fetch(0, 0):
            c.start()

    @pl.when(step < n_slices)
    def _():
        for c in fetch(step, slot):
            c.wait()

        @pl.when(step + 1 < n_slices)
        def _():
            for c in fetch(step + 1, other):
                c.start()

        @pl.when(step >= 2)
        def _():
            for c in write_back(step - 2, slot):
                c.wait()

        for src, dst in zip(in_bufs, out_bufs):
            dst[slot] = src[slot].astype(dst.dtype)
        for c in write_back(step, slot):
            c.start()

    @pl.when(step == min(n_slices, grid_steps - 1))
    def _():
        for s in (n_slices - 2, n_slices - 1):
            for c in write_back(s, s % 2):
                c.wait()


def _attn_kernel(lam_ref, sg_ref, qt_ref, k_ref, vt_ref, *rest, lambda_init, cast_groups, n_steps,
                 n_q):
    n_cast = sum(cnt for _, cnt in cast_groups)
    cast_src, rest = rest[:n_cast], rest[n_cast:]
    o_ref, rest = rest[0], rest[1:]
    cast_dst, rest = rest[:n_cast], rest[n_cast:]
    qc_sc, m_sc, alpha_sc, acc_sc, s_sc, p_sc = rest[:6]
    cast_in, cast_out = rest[6:6 + n_cast], rest[6 + n_cast:6 + 2 * n_cast]
    sems = rest[6 + 2 * n_cast:]
    step = ((pl.program_id(0) * pl.num_programs(1) + pl.program_id(1)) * pl.num_programs(2)
            + pl.program_id(2))
    first = 0
    for g, (n_slices, cnt) in enumerate(cast_groups):
        grp = slice(first, first + cnt)
        _cast_slice(step, n_slices, n_steps, cast_src[grp], cast_dst[grp], cast_in[grp],
                    cast_out[grp], sems[2 * g], sems[2 * g + 1])
        first += cnt

    qi = pl.program_id(2)
    tq = ATTN_Q_TILE
    tk = ATTN_TILE
    n_sub = tq // tk

    qt = qt_ref[...]
    dim = lax.broadcasted_iota(jnp.int32, (LANES, 1), 0)
    zero = jnp.zeros_like(qt)
    qc_sc[0] = jnp.where(dim < DA_HEAD_DIM, qt, zero)
    qc_sc[1] = jnp.where(dim >= DA_HEAD_DIM, qt, zero)
    m_sc[...] = jnp.full(m_sc.shape, -jnp.inf, F32)
    acc_sc[...] = jnp.zeros(acc_sc.shape, F32)

    def scores(j, diagonal):
        for c in range(2):
            for u in range(n_sub):
                ks = (j * n_sub + u) * tk
                su = _dot(k_ref[ks:ks + tk, :], qc_sc[c])
                if diagonal:
                    key = lax.broadcasted_iota(jnp.int32, (tk, tq), 0) + u * tk
                    qry = lax.broadcasted_iota(jnp.int32, (tk, tq), 1)
                    su = jnp.where(key <= qry, su, -jnp.inf)
                s_sc[c, u * tk:(u + 1) * tk, :] = su

    def softmax():
        for c in range(2):
            s = s_sc[c]
            m_prev = m_sc[c]
            m_new = jnp.maximum(m_prev, jnp.max(s, axis=0, keepdims=True))
            alpha_sc[c] = jnp.exp2(m_prev - m_new)
            m_sc[c] = m_new
            p_sc[c] = jnp.exp2(s - m_new).astype(BF16)

    def fold(j):
        for c in range(2):
            pv = None
            for u in range(n_sub):
                d = _dot(vt_ref[j * n_sub + u], p_sc[c, u * tk:(u + 1) * tk, :])
                pv = d if pv is None else pv + d
            acc_sc[c] = alpha_sc[c] * acc_sc[c] + pv

    def run(last):
        scores(0, diagonal=last == 0)
        for t in range(last + 1):
            if t >= 1:
                fold(t - 1)
            softmax()
            if t < last:
                scores(t + 1, diagonal=t + 1 == last)
        fold(last)

    for last in range(n_q):
        pl.when(qi == last)(functools.partial(run, last))

    lp = lam_ref[...]
    lam = (jnp.exp(jnp.sum(lp[0:1] * lp[1:2], axis=-1, keepdims=True))
           - jnp.exp(jnp.sum(lp[2:3] * lp[3:4], axis=-1, keepdims=True)) + lambda_init)
    a0 = acc_sc[0]
    a1 = acc_sc[1]
    o = (a0[0:DA_V_DIM] / a0[DA_V_DIM:DA_V_DIM + 1]
         - lam * (a1[0:DA_V_DIM] / a1[DA_V_DIM:DA_V_DIM + 1]))
    o = o * lax.rsqrt(jnp.mean(o * o, axis=0, keepdims=True) + RMS_EPS) * sg_ref[...]
    o_ref[...] = (o * (1.0 - lambda_init)).T.astype(o_ref.dtype)


def _diff_attention(qt, k, vt, lam_p, subln_col, lambda_init, bsz, seq, cast_f32):
    tq = ATTN_Q_TILE
    tk = ATTN_TILE
    nq = seq // tq
    nk = seq // tk
    n_steps = bsz * DA_HEADS * nq
    def n_slices(w):
        return max(c for c in range(1, n_steps + 1)
                   if w.shape[0] % c == 0 and (w.shape[0] // c) % 16 == 0)
    order = sorted(range(len(cast_f32)), key=lambda a: n_slices(cast_f32[a]))
    cast_sorted = [cast_f32[a] for a in order]
    counts = [n_slices(w) for w in cast_sorted]
    cast_groups = tuple((c, counts.count(c)) for c in sorted(set(counts)))
    n_cast = len(cast_f32)
    slices = [(w.shape[0] // c, w.shape[1]) for w, c in zip(cast_sorted, counts)]
    any_spec = pl.BlockSpec(memory_space=pl.ANY)
    out = pl.pallas_call(
        functools.partial(_attn_kernel, lambda_init=lambda_init, cast_groups=cast_groups,
                          n_steps=n_steps, n_q=nq),
        grid=(bsz, DA_HEADS, nq),
        in_specs=[pl.BlockSpec(lam_p.shape, lambda b, h, i: (0, 0)),
                  pl.BlockSpec(subln_col.shape, lambda b, h, i: (0, 0)),
                  pl.BlockSpec((LANES, tq), lambda b, h, i: (h, b * nq + i)),
                  pl.BlockSpec((seq, LANES), lambda b, h, i: (b, h)),
                  pl.BlockSpec((None, nk, VT_ROWS, tk), lambda b, h, i: (h, b, 0, 0))]
                 + [any_spec] * n_cast,
        out_specs=[pl.BlockSpec((tq, LANES), lambda b, h, i: (b * nq + i, h))] + [any_spec] * n_cast,
        out_shape=[jax.ShapeDtypeStruct(k.shape, BF16)]
                  + [jax.ShapeDtypeStruct(w.shape, BF16) for w in cast_sorted],
        scratch_shapes=[pltpu.VMEM((2, LANES, tq), BF16), pltpu.VMEM((2, 1, tq), F32),
                        pltpu.VMEM((2, 1, tq), F32), pltpu.VMEM((2, VT_ROWS, tq), F32),
                        pltpu.VMEM((2, tq, tq), F32), pltpu.VMEM((2, tq, tq), BF16)]
                       + [pltpu.VMEM((2,) + s, F32) for s in slices]
                       + [pltpu.VMEM((2,) + s, BF16) for s in slices]
                       + [pltpu.SemaphoreType.DMA((2,))] * (2 * len(cast_groups)),
        compiler_params=pltpu.CompilerParams(
            dimension_semantics=("arbitrary", "arbitrary", "arbitrary"),
            vmem_limit_bytes=VMEM_LIMIT),
        name="l0_diff_attention",
    )(lam_p, subln_col, qt, k, vt, *cast_sorted)
    cast_bf16 = [None] * n_cast
    for pos, a in enumerate(order):
        cast_bf16[a] = out[1 + pos]
    return out[0], cast_bf16


def _l0_post_kernel(x_ref, attn_ref, conv_ref, p_ref, wo_ref, gffn_ref, w1_ref, w3_ref, w2_ref,
                    gple_ref, wg_ref, wp_ref, o_ref, acc_sc, hn_sc, h_sc):
    h = (x_ref[...] + _dot(attn_ref[...], wo_ref[0:DA_WIDTH, :])
         + _dot(conv_ref[...], wo_ref[DA_WIDTH:DA_WIDTH + SC_WIDTH, :]))
    hn_sc[...] = _rms(h, gffn_ref[...]).astype(BF16)
    h_sc[...] = h
    _swiglu_chunks(hn_sc, w1_ref, w3_ref, w2_ref, acc_sc)
    o_ref[...] = _ple(h_sc[...] + acc_sc[...], p_ref, gple_ref, wg_ref, wp_ref)


def _l0_post(x2, attn, conv, p0, wo, gffn, w1, w3, w2, gple, wg, wp):
    n = x2.shape[0]
    tm = ROW_TILE
    return pl.pallas_call(
        _l0_post_kernel,
        grid=(n // tm,),
        in_specs=[_rows(tm, D_MODEL), _rows(tm, DA_WIDTH), _rows(tm, SC_WIDTH),
                  _layer_rows(tm, PLE_DIM, 0, n // tm),
                  _resident(wo.shape), _resident(gffn.shape), _resident(w1.shape),
                  _resident(w3.shape), _resident(w2.shape), _resident(gple.shape),
                  _resident(wg.shape), _resident(wp.shape)],
        out_specs=_rows(tm, D_MODEL),
        out_shape=jax.ShapeDtypeStruct((n, D_MODEL), F32),
        scratch_shapes=[pltpu.VMEM((tm, D_MODEL), F32), pltpu.VMEM((tm, D_MODEL), BF16),
                        pltpu.VMEM((tm, D_MODEL), F32)],
        compiler_params=pltpu.CompilerParams(dimension_semantics=("arbitrary",),
                                             vmem_limit_bytes=VMEM_LIMIT),
        name="l0_outproj_swiglu_ple",
    )(x2, attn, conv, p0, wo, gffn, w1, w3, w2, gple, wg, wp)


def _split_bf16(a):
    hi = a.astype(BF16)
    return hi, (a - hi.astype(F32)).astype(BF16)


def _l1_mix_kernel(h_ref, gmix_ref, win_ref, lng_ref, lnb_ref, ws_ref, bs_ref, wout_ref,
                   gffn_ref, wr_ref,
                   h1_ref, hn2_ref, ridx_ref, rgate_ref, hn_sc, u_sc, v_sc, vn_sc, gated_sc):
    tm = h_ref.shape[0]
    hn_sc[...] = _rms(h_ref[...], gmix_ref[...]).astype(BF16)

    def gelu(z):
        return 0.5 * z * (1.0 + lax.erf(z * (2.0 ** -0.5)))

    for dst, base in ((v_sc, GM_WIDTH), (u_sc, 0)):
        for c in range(GM_WIDTH // FF_CHUNK):
            cols = slice(c * FF_CHUNK, (c + 1) * FF_CHUNK)
            dst[:, cols] = gelu(_dot(hn_sc[...], win_ref[:, base + c * FF_CHUNK:base + (c + 1) * FF_CHUNK]))

    vv = v_sc[...]
    mu = jnp.mean(vv, axis=-1, keepdims=True)
    vc = vv - mu
    var = jnp.mean(vc * vc, axis=-1, keepdims=True)
    vn_sc[...] = (vc * lax.rsqrt(var + LN_EPS) * lng_ref[...] + lnb_ref[...]).astype(BF16)

    n_chunks = tm // GM_CHUNK
    row = lax.broadcasted_iota(jnp.int32, (GM_CHUNK, GM_CHUNK), 0)
    col = lax.broadcasted_iota(jnp.int32, (GM_CHUNK, GM_CHUNK), 1)
    for g in range(GM_GROUPS):
        gs = slice(g * LANES, (g + 1) * LANES)
        wm = jnp.where(col <= row, ws_ref[g], 0.0).astype(BF16)
        rhs = jnp.concatenate(
            [vn_sc[r * GM_CHUNK:(r + 1) * GM_CHUNK, gs] for r in range(n_chunks)], axis=1)
        vs = _dot(wm, rhs) + bs_ref[:, g:g + 1]
        for r in range(n_chunks):
            rs = slice(r * GM_CHUNK, (r + 1) * GM_CHUNK)
            gated_sc[rs, gs] = (u_sc[rs, gs] * vs[:, r * LANES:(r + 1) * LANES]).astype(BF16)

    h1 = h_ref[...] + _dot(gated_sc[...], wout_ref[...])
    h1_ref[...] = h1

    hn2 = _rms(h1, gffn_ref[...])
    hn2_ref[...] = hn2
    x_hi, x_lo = _split_bf16(hn2)
    a_hi = _dot(x_hi, wr_ref[...])
    logits = a_hi + pltpu.roll(a_hi, LANES - N_EXPERTS, 1) + _dot(x_lo, wr_ref[...])
    lane = lax.broadcasted_iota(jnp.int32, logits.shape, 1)
    logits = jnp.where(lane < N_EXPERTS, logits, -jnp.inf)
    v1 = jnp.max(logits, axis=-1, keepdims=True)
    i1 = jnp.min(jnp.where(logits == v1, lane, LANES), axis=-1, keepdims=True)
    rest = jnp.where(lane == i1, -jnp.inf, logits)
    v2 = jnp.max(rest, axis=-1, keepdims=True)
    i2 = jnp.min(jnp.where(rest == v2, lane, LANES), axis=-1, keepdims=True)
    e2 = jnp.exp(v2 - v1)
    choices = jnp.where(lane == 0, i1, jnp.where(lane == 1, i2, 0)).astype(F32)
    ridx_ref[...] = choices.T[0:ridx_ref.shape[0], :].astype(jnp.int32)
    rgate_ref[:, 0:1] = 1.0 / (1.0 + e2)
    rgate_ref[:, 1:2] = e2 / (1.0 + e2)


def _l1_mix(h, gmix, win, lng, lnb, ws, bs_t, wout, gffn, wr):
    n = h.shape[0]
    tm = ROW_TILE
    return pl.pallas_call(
        _l1_mix_kernel,
        grid=(n // tm,),
        in_specs=[_rows(tm, D_MODEL)] + [_resident(a.shape) for a in
                                         (gmix, win, lng, lnb, ws, bs_t, wout, gffn, wr)],
        out_specs=[_rows(tm, D_MODEL), _rows(tm, D_MODEL),
                   pl.BlockSpec((F32_SUBLANES, tm), lambda i: (0, i)), _rows(tm, TOP_K)],
        out_shape=[jax.ShapeDtypeStruct((n, D_MODEL), F32), jax.ShapeDtypeStruct((n, D_MODEL), F32),
                   jax.ShapeDtypeStruct((F32_SUBLANES, n), jnp.int32),
                   jax.ShapeDtypeStruct((n, TOP_K), F32)],
        scratch_shapes=[pltpu.VMEM((tm, D_MODEL), BF16), pltpu.VMEM((tm, GM_WIDTH), F32),
                        pltpu.VMEM((tm, GM_WIDTH), F32), pltpu.VMEM((tm, GM_WIDTH), BF16),
                        pltpu.VMEM((tm, GM_WIDTH), BF16)],
        compiler_params=pltpu.CompilerParams(dimension_semantics=("arbitrary",),
                                             vmem_limit_bytes=VMEM_LIMIT),
        name="l1_gmlp_router",
    )(h, gmix, win, lng, lnb, ws, bs_t, wout, gffn, wr)


def _moe_kernel(bexp_ref, nvalid_ref, tok_ref, dst_ref, hn_hbm, w1_ref, w3_ref, w2_ref, y_hbm,
                xbuf, xb_sc, acc_sc, ybuf, gsem, ssem, fence_sem, fence_sink):
    i = pl.program_id(0)
    n_valid = nvalid_ref[0]
    tm = MOE_TILE
    prev = jnp.maximum(i - 1, 0)

    def gather_row(blk, r):
        return pltpu.make_async_copy(hn_hbm.at[pl.ds(tok_ref[blk * tm + r], 1)],
                                     xbuf.at[pl.ds(r, 1)], gsem)

    def scatter_row(blk, r):
        return pltpu.make_async_copy(ybuf.at[pl.ds(r, 1)],
                                     y_hbm.at[pl.ds(dst_ref[blk * tm + r], 1)], ssem)

    def gather_wait():
        pltpu.make_async_copy(hn_hbm.at[pl.ds(0, tm)], xbuf, gsem).wait()

    def scatter_wait():
        pltpu.make_async_copy(ybuf, y_hbm.at[pl.ds(0, tm)], ssem).wait()

    @pl.when(i < n_valid)
    def _():
        @pl.when(i == 0)
        def _():
            def issue(r, carry):
                gather_row(0, r).start()
                return carry
            lax.fori_loop(0, tm, issue, 0)

        gather_wait()
        xb_sc[...] = xbuf[...].astype(BF16)
        has_next = i + 1 < n_valid
        has_prev = i >= 1

        def copies(c):
            if c not in MOE_FENCE_CHUNKS:
                return
            g = MOE_FENCE_CHUNKS.index(c)
            if g > 0:
                fence_sink[0] = pl.semaphore_read(fence_sem)
            if g == len(MOE_FENCE_CHUNKS) - 1:
                return
            for r in range(MOE_COPY_ROWS[g], MOE_COPY_ROWS[g + 1]):
                @pl.when(has_next)
                def _():
                    gather_row(i + 1, r).start(priority=1)

                @pl.when(has_prev)
                def _():
                    scatter_row(prev, r).start(priority=r % 2)

        _swiglu_chunks(xb_sc, w1_ref, w3_ref, w2_ref, acc_sc, per_chunk=copies)

        @pl.when(has_prev)
        def _():
            scatter_wait()

        ybuf[...] = acc_sc[...]

    @pl.when(i == n_valid)
    def _():
        def issue(r, carry):
            scatter_row(prev, r).start()
            return carry
        lax.fori_loop(0, tm, issue, 0)
        scatter_wait()
        ybuf[...] = jnp.zeros(ybuf.shape, F32)
        plane = y_hbm.shape[0] // TOP_K
        for b in range(TOP_K):
            spare = pltpu.make_async_copy(ybuf, y_hbm.at[pl.ds((b + 1) * plane - tm, tm)], ssem)
            spare.start()
            spare.wait()


def _moe_experts(block_expert, n_valid, row_token, row_dst, hn2, w1, w3, w2, n_out_rows):
    n_blocks = block_expert.shape[0]
    tm = MOE_TILE
    per_expert = lambda w: pl.BlockSpec((None,) + w.shape[1:], lambda i, be, *_: (be[i], 0, 0))
    return pl.pallas_call(
        _moe_kernel,
        grid_spec=pltpu.PrefetchScalarGridSpec(
            num_scalar_prefetch=4,
            grid=(n_blocks,),
            in_specs=[pl.BlockSpec(memory_space=pl.ANY), per_expert(w1), per_expert(w3),
                      per_expert(w2)],
            out_specs=pl.BlockSpec(memory_space=pl.ANY),
            scratch_shapes=[pltpu.VMEM((tm, D_MODEL), F32), pltpu.VMEM((tm, D_MODEL), BF16),
                            pltpu.VMEM((tm, D_MODEL), F32), pltpu.VMEM((tm, D_MODEL), F32),
                            pltpu.SemaphoreType.DMA(()), pltpu.SemaphoreType.DMA(()),
                            pltpu.SemaphoreType.REGULAR(()), pltpu.SMEM((1,), jnp.int32)],
        ),
        out_shape=jax.ShapeDtypeStruct((n_out_rows, D_MODEL), F32),
        compiler_params=pltpu.CompilerParams(dimension_semantics=("arbitrary",),
                                             vmem_limit_bytes=VMEM_LIMIT),
        name="l1_expert_swiglu",
    )(block_expert, n_valid, row_token, row_dst, hn2, w1, w3, w2)


def _combine_kernel(h_ref, y0_ref, y1_ref, gate_ref, p_ref, gple_ref, wg_ref, wp_ref, gfin_ref,
                    o_ref):
    gate = gate_ref[...]
    h = h_ref[...] + (y0_ref[...] * gate[:, 0:1] + y1_ref[...] * gate[:, 1:2])
    h = _ple(h, p_ref, gple_ref, wg_ref, wp_ref)
    o_ref[...] = _rms(h, gfin_ref[...])


def _combine(h1, y_rows, rgate, p1, gple, wg, wp, gfin):
    n = h1.shape[0]
    tm = ROW_TILE
    plane_tiles = y_rows.shape[0] // TOP_K // tm
    return pl.pallas_call(
        _combine_kernel,
        grid=(n // tm,),
        in_specs=[_rows(tm, D_MODEL), _rows(tm, D_MODEL),
                  pl.BlockSpec((tm, D_MODEL), lambda i: (plane_tiles + i, 0)), _rows(tm, TOP_K),
                  _layer_rows(tm, PLE_DIM, 1, n // tm), _resident(gple.shape), _resident(wg.shape),
                  _resident(wp.shape), _resident(gfin.shape)],
        out_specs=_rows(tm, D_MODEL),
        out_shape=jax.ShapeDtypeStruct((n, D_MODEL), F32),
        compiler_params=pltpu.CompilerParams(dimension_semantics=("arbitrary",),
                                             vmem_limit_bytes=VMEM_LIMIT),
        name="l1_combine_ple_norm",
    )(h1, y_rows, y_rows, rgate, p1, gple, wg, wp, gfin)


def _routing_tables(ridx, n_blocks):
    n_tok = ridx.shape[1]
    n_assign = ridx.size
    e_flat = ridx.reshape(-1)
    experts = jnp.arange(N_EXPERTS, dtype=jnp.int32)[:, None]
    order = jnp.argsort(e_flat, stable=True).astype(jnp.int32)
    counts = jnp.sum((e_flat[None, :] == experts).astype(jnp.int32), axis=1)
    starts = jnp.cumsum(counts) - counts
    padded = (counts + MOE_TILE - 1) // MOE_TILE * MOE_TILE
    pad_ends = jnp.cumsum(padded)
    pad_starts = pad_ends - padded
    n_rows = (n_blocks + 2) * MOE_TILE
    r = jnp.arange(n_rows, dtype=jnp.int32)
    row_expert = jnp.minimum(jnp.sum((r[None, :] >= pad_ends[:, None]).astype(jnp.int32), axis=0),
                             N_EXPERTS - 1)
    of_row = lambda per_expert: jnp.sum(
        jnp.where(row_expert[None, :] == experts, per_expert[:, None], 0), axis=0)
    j = r - of_row(pad_starts)
    valid = (j < of_row(counts)) & (r < pad_ends[-1])
    src = jnp.clip(of_row(starts) + j, 0, n_assign - 1)
    assign = jnp.where(valid, order[src], -1)
    row_token = jnp.maximum(assign, 0) % n_tok
    plane = n_tok + MOE_TILE
    spare = (r // MOE_TILE % 2) * plane + n_tok + r % MOE_TILE
    row_dst = jnp.where(assign >= 0, (assign // n_tok) * plane + assign % n_tok, spare)
    block_start = jnp.arange(n_blocks, dtype=jnp.int32) * MOE_TILE
    block_expert = jnp.minimum(
        jnp.sum((block_start[:, None] >= pad_ends[None, :]).astype(jnp.int32), axis=1),
        N_EXPERTS - 1).astype(jnp.int32)
    n_valid = (pad_ends[-1:] // MOE_TILE).astype(jnp.int32)
    return row_token, row_dst, block_expert, n_valid


def kernel(x, p, positions, ln_mix, ln_ffn, ln_ple, a_w_in, a_lambda, a_subln, a_conv_w, a_w_out,
           ffn_w1, ffn_w3, ffn_w2, c_w_in, c_ln_g, c_ln_b, c_w_s, c_b_s, c_w_out, router_w,
           moe_w1, moe_w3, moe_w2, ple_gate, ple_proj, final_norm):
    bsz, seq, _ = x.shape
    n = bsz * seq
    assert seq % ROW_TILE == 0 and seq % ATTN_TILE == 0 and ROW_TILE % GM_CHUNK == 0
    x2 = x.reshape(n, D_MODEL)
    pos3 = positions.reshape(n // ROW_TILE, 1, ROW_TILE)
    row = lambda a: a.reshape(1, -1)

    freq = (ROPE_THETA ** (-jnp.arange(0, ROT_DIM, 2, dtype=F32) / ROT_DIM)).reshape(-1, 1)
    qt, k, vt, conv = _inproj(x2, pos3, row(ln_mix[0]), freq, a_conv_w[0].T, a_w_in[0].astype(BF16), seq)
    lambda_init = 0.8 - 0.6 * math.exp(-0.3 * 0)
    moe_f32 = [w[0].reshape(-1, w.shape[-1]) for w in (moe_w1, moe_w3, moe_w2)]
    attn, moe_bf16 = _diff_attention(qt, k, vt, a_lambda[0], a_subln[0].reshape(-1, 1), lambda_init,
                                     bsz, seq, moe_f32)
    mw1, mw3, mw2 = (wb.reshape(w.shape[1:]) for wb, w in zip(moe_bf16, (moe_w1, moe_w3, moe_w2)))
    p_all = p.reshape(-1, PLE_DIM)
    h = _l0_post(x2, attn, conv, p_all, a_w_out[0].astype(BF16), row(ln_ffn[0]),
                 ffn_w1[0].astype(BF16), ffn_w3[0].astype(BF16), ffn_w2[0].astype(BF16),
                 row(ln_ple[0]), ple_gate[0].astype(BF16), ple_proj[0].astype(BF16))

    wr = jnp.pad(jnp.concatenate(_split_bf16(router_w[0]), axis=1),
                 ((0, 0), (0, LANES - 2 * N_EXPERTS)))
    h1, hn2, ridx, rgate = _l1_mix(h, row(ln_mix[1]), c_w_in[0].astype(BF16), row(c_ln_g[0]),
                                   row(c_ln_b[0]), c_w_s[0], c_b_s[0].T, c_w_out[0].astype(BF16),
                                   row(ln_ffn[1]), wr)
    n_blocks = -(-(n * TOP_K + N_EXPERTS * (MOE_TILE - 1)) // MOE_TILE)
    row_token, row_dst, block_expert, n_valid = _routing_tables(ridx[:TOP_K], n_blocks)
    y_rows = _moe_experts(block_expert, n_valid, row_token, row_dst, hn2, mw1, mw3, mw2,
                          TOP_K * (n + MOE_TILE))
    out = _combine(h1, y_rows, rgate, p_all, row(ln_ple[1]),
                   ple_gate[1].astype(BF16), ple_proj[1].astype(BF16), row(final_norm))
    return out.reshape(bsz, seq, D_MODEL)
```

```python
import functools
import math

import jax
import jax.numpy as jnp
from jax import lax
from jax.experimental import pallas as pl
from jax.experimental.pallas import tpu as pltpu

F32 = jnp.float32
BF16 = jnp.bfloat16

D_MODEL = 1024
PLE_DIM = 256
DA_HEADS = 4
DA_HEAD_DIM = 64
DA_V_DIM = 2 * DA_HEAD_DIM
DA_WIDTH = DA_HEADS * DA_V_DIM
ROPE_THETA = 500000.0
ROT_DIM = DA_HEAD_DIM // 4
SC_WIDTH = 512
GM_WIDTH = D_MODEL
GM_GROUPS = 8
GM_CHUNK = 128
D_FF = 2816
N_EXPERTS = 8
TOP_K = 2
RMS_EPS = 1e-6
LN_EPS = 1e-5

LANES = 128
F32_SUBLANES = 8
FF_CHUNK = 256
N_FF_CHUNKS = D_FF // FF_CHUNK
ROW_TILE = 512
ATTN_TILE = 256
ATTN_Q_TILE = 512
Q_SCALE = DA_HEAD_DIM ** -0.5 * math.log2(math.e)
VT_ROWS = DA_V_DIM + 16
MOE_TILE = 512
MOE_FENCE_CHUNKS = (0, 5, 9, 10)
MOE_COPY_ROWS = (0, 256, 504, 512)
CONV_HALO = 8
VMEM_LIMIT = 56 * 2**20


def _resident(shape):
    nd = len(shape)
    return pl.BlockSpec(shape, lambda *_: (0,) * nd, pipeline_mode=pl.Buffered(1))


def _rows(tile, width):
    return pl.BlockSpec((tile, width), lambda i, *_: (i, 0))


def _layer_rows(tile, width, layer, tiles_per_layer):
    return pl.BlockSpec((tile, width), lambda i, *_: (layer * tiles_per_layer + i, 0))


def _rms(x, g):
    return x * lax.rsqrt(jnp.mean(x * x, axis=-1, keepdims=True) + RMS_EPS) * g


def _dot(a, b):
    return jnp.dot(a, b, preferred_element_type=F32)


def _swiglu_chunks(hn_ref, w1_ref, w3_ref, w2_ref, acc_ref, per_chunk=None):
    for c in range(N_FF_CHUNKS):
        cols = slice(c * FF_CHUNK, (c + 1) * FF_CHUNK)
        g = _dot(hn_ref[...], w1_ref[:, cols])
        u = _dot(hn_ref[...], w3_ref[:, cols])
        a = (g * jax.nn.sigmoid(g) * u).astype(BF16)
        if c == 0:
            acc_ref[...] = _dot(a, w2_ref[cols, :])
        else:
            acc_ref[...] += _dot(a, w2_ref[cols, :])
        if per_chunk is not None:
            per_chunk(c)


def _ple(h, p_ref, gple_ref, wg_ref, wp_ref):
    gate = jax.nn.sigmoid(_dot(_rms(h, gple_ref[...]).astype(BF16), wg_ref[...]))
    return h + gate * _dot(p_ref[...].astype(BF16), wp_ref[...])


def _inproj_kernel(x_ref, pos_ref, g_ref, freq_ref, cw_ref, w_ref,
                   qt_ref, k_ref, vt_ref, conv_ref, cbuf, *, tiles_per_seq):
    i = pl.program_id(0)
    tm = x_ref.shape[0]
    hn = _rms(x_ref[...], g_ref[...]).astype(BF16)

    half = ROT_DIM // 2
    ang = freq_ref[...] * pos_ref[...].astype(F32)
    cos_t = jnp.cos(ang)
    sin_t = jnp.sin(ang)
    zero_t = jnp.zeros_like(ang)
    rest = DA_HEAD_DIM - ROT_DIM
    pattern = lambda a, b, fill: jnp.concatenate(
        [a, b, jnp.full((rest, tm), fill, F32)] * (LANES // DA_HEAD_DIM), axis=0).T
    cos = pattern(cos_t, cos_t, 1.0)
    sin_up = pattern(-sin_t, zero_t, 0.0)
    sin_dn = pattern(zero_t, sin_t, 0.0)

    def rope(t):
        return (t * cos + pltpu.roll(t, LANES - half, 1) * sin_up
                + pltpu.roll(t, half, 1) * sin_dn)

    zq = _dot(hn, w_ref[:, 0:DA_WIDTH])
    zk = _dot(hn, w_ref[:, DA_WIDTH:2 * DA_WIDTH])
    zv = _dot(hn, w_ref[:, 2 * DA_WIDTH:3 * DA_WIDTH])
    ones = jnp.ones((VT_ROWS - DA_V_DIM, ATTN_TILE), BF16)
    for hd in range(DA_HEADS):
        sl = slice(hd * LANES, (hd + 1) * LANES)
        qt_ref[sl, :] = (rope(zq[:, sl]) * Q_SCALE).T.astype(BF16)
        k_ref[:, sl] = rope(zk[:, sl]).astype(BF16)
        for u in range(tm // ATTN_TILE):
            vt_ref[hd, u, 0:DA_V_DIM, :] = zv[u * ATTN_TILE:(u + 1) * ATTN_TILE, sl].T.astype(BF16)
            vt_ref[hd, u, DA_V_DIM:VT_ROWS, :] = ones

    off = 3 * DA_WIDTH
    b_gate = _dot(hn, w_ref[:, off:off + SC_WIDTH])
    c_gate = _dot(hn, w_ref[:, off + SC_WIDTH:off + 2 * SC_WIDTH])
    hc = _dot(hn, w_ref[:, off + 2 * SC_WIDTH:off + 3 * SC_WIDTH])

    @pl.when(i % tiles_per_seq == 0)
    def _():
        cbuf[0:CONV_HALO, :] = jnp.zeros((CONV_HALO, SC_WIDTH), F32)

    @pl.when(i % tiles_per_seq != 0)
    def _():
        cbuf[0:CONV_HALO, :] = cbuf[tm:tm + CONV_HALO, :]

    ch = c_gate * hc
    cbuf[CONV_HALO:CONV_HALO + tm, :] = ch
    ch1 = cbuf[CONV_HALO - 1:CONV_HALO - 1 + tm, :]
    ch2 = cbuf[CONV_HALO - 2:CONV_HALO - 2 + tm, :]
    conv = b_gate * (cw_ref[2:3, :] * ch + cw_ref[1:2, :] * ch1 + cw_ref[0:1, :] * ch2)
    conv_ref[...] = conv.astype(BF16)


def _inproj(x2, pos3, g, freq, cw, w, seq):
    n = x2.shape[0]
    tm = ROW_TILE
    out = jax.ShapeDtypeStruct((n, DA_WIDTH), BF16)
    sub = tm // ATTN_TILE
    return pl.pallas_call(
        functools.partial(_inproj_kernel, tiles_per_seq=seq // tm),
        grid=(n // tm,),
        in_specs=[_rows(tm, D_MODEL), pl.BlockSpec((None, 1, tm), lambda i: (i, 0, 0)),
                  _resident(g.shape), _resident(freq.shape),
                  _resident(cw.shape), _resident(w.shape)],
        out_specs=[pl.BlockSpec((DA_WIDTH, tm), lambda i: (0, i)), _rows(tm, DA_WIDTH),
                   pl.BlockSpec((DA_HEADS, sub, VT_ROWS, ATTN_TILE), lambda i: (0, i, 0, 0)),
                   _rows(tm, SC_WIDTH)],
        out_shape=[jax.ShapeDtypeStruct((DA_WIDTH, n), BF16), out,
                   jax.ShapeDtypeStruct((DA_HEADS, n // ATTN_TILE, VT_ROWS, ATTN_TILE), BF16), out],
        scratch_shapes=[pltpu.VMEM((tm + CONV_HALO, SC_WIDTH), F32)],
        compiler_params=pltpu.CompilerParams(dimension_semantics=("arbitrary",),
                                             vmem_limit_bytes=VMEM_LIMIT),
        name="l0_inproj",
    )(x2, pos3, g, freq, cw, w)


def _cast_slice(step, n_slices, grid_steps, src_refs, dst_refs, in_bufs, out_bufs, in_sem, out_sem):
    assert 2 <= n_slices <= grid_steps
    slot = step % 2
    other = 1 - slot

    def fetch(s, sl):
        return [pltpu.make_async_copy(src.at[pl.ds(s * buf.shape[1], buf.shape[1])], buf.at[sl],
                                      in_sem.at[sl]) for src, buf in zip(src_refs, in_bufs)]

    def write_back(s, sl):
        return [pltpu.make_async_copy(buf.at[sl], dst.at[pl.ds(s * buf.shape[1], buf.shape[1])],
                                      out_sem.at[sl]) for dst, buf in zip(dst_refs, out_bufs)]

    @pl.when(step == 0)
    def _():
        for c in fetch(0, 0):
            c.start()

    @pl.when(step < n_slices)
    def _():
        for c in fetch(step, slot):
            c.wait()

        @pl.when(step + 1 < n_slices)
        def _():
            for c in fetch(step + 1, other):
                c.start()

        @pl.when(step >= 2)
        def _():
            for c in write_back(step - 2, slot):
                c.wait()

        for src, dst in zip(in_bufs, out_bufs):
            dst[slot] = src[slot].astype(dst.dtype)
        for c in write_back(step, slot):
            c.start()

    @pl.when(step == min(n_slices, grid_steps - 1))
    def _():
        for s in (n_slices - 2, n_slices - 1):
            for c in write_back(s, s % 2):
                c.wait()


def _attn_kernel(lam_ref, sg_ref, qt_ref, k_ref, vt_ref, *rest, lambda_init, cast_groups, n_steps,
                 n_q):
    n_cast = sum(cnt for _, cnt in cast_groups)
    cast_src, rest = rest[:n_cast], rest[n_cast:]
    o_ref, rest = rest[0], rest[1:]
    cast_dst, rest = rest[:n_cast], rest[n_cast:]
    qc_sc, m_sc, alpha_sc, acc_sc, s_sc, p_sc = rest[:6]
    cast_in, cast_out = rest[6:6 + n_cast], rest[6 + n_cast:6 + 2 * n_cast]
    sems = rest[6 + 2 * n_cast:]
    step = ((pl.program_id(0) * pl.num_programs(1) + pl.program_id(1)) * pl.num_programs(2)
            + pl.program_id(2))
    first = 0
    for g, (n_slices, cnt) in enumerate(cast_groups):
        grp = slice(first, first + cnt)
        _cast_slice(step, n_slices, n_steps, cast_src[grp], cast_dst[grp], cast_in[grp],
                    cast_out[grp], sems[2 * g], sems[2 * g + 1])
        first += cnt

    qi = pl.program_id(2)
    tq = ATTN_Q_TILE
    tk = ATTN_TILE
    n_sub = tq // tk

    qt = qt_ref[...]
    dim = lax.broadcasted_iota(jnp.int32, (LANES, 1), 0)
    zero = jnp.zeros_like(qt)
    qc_sc[0] = jnp.where(dim < DA_HEAD_DIM, qt, zero)
    qc_sc[1] = jnp.where(dim >= DA_HEAD_DIM, qt, zero)
    m_sc[...] = jnp.full(m_sc.shape, -jnp.inf, F32)
    acc_sc[...] = jnp.zeros(acc_sc.shape, F32)

    def scores(j, diagonal):
        for c in range(2):
            for u in range(n_sub):
                ks = (j * n_sub + u) * tk
                su = _dot(k_ref[ks:ks + tk, :], qc_sc[c])
                if diagonal:
                    key = lax.broadcasted_iota(jnp.int32, (tk, tq), 0) + u * tk
                    qry = lax.broadcasted_iota(jnp.int32, (tk, tq), 1)
                    su = jnp.where(key <= qry, su, -jnp.inf)
                s_sc[c, u * tk:(u + 1) * tk, :] = su

    def softmax():
        for c in range(2):
            s = s_sc[c]
            m_prev = m_sc[c]
            m_new = jnp.maximum(m_prev, jnp.max(s, axis=0, keepdims=True))
            alpha_sc[c] = jnp.exp2(m_prev - m_new)
            m_sc[c] = m_new
            p_sc[c] = jnp.exp2(s - m_new).astype(BF16)

    def fold(j):
        for c in range(2):
            pv = None
            for u in range(n_sub):
                d = _dot(vt_ref[j * n_sub + u], p_sc[c, u * tk:(u + 1) * tk, :])
                pv = d if pv is None else pv + d
            acc_sc[c] = alpha_sc[c] * acc_sc[c] + pv

    def run(last):
        scores(0, diagonal=last == 0)
        for t in range(last + 1):
            if t >= 1:
                fold(t - 1)
            softmax()
            if t < last:
                scores(t + 1, diagonal=t + 1 == last)
        fold(last)

    for last in range(n_q):
        pl.when(qi == last)(functools.partial(run, last))

    lp = lam_ref[...]
    lam = (jnp.exp(jnp.sum(lp[0:1] * lp[1:2], axis=-1, keepdims=True))
           - jnp.exp(jnp.sum(lp[2:3] * lp[3:4], axis=-1, keepdims=True)) + lambda_init)
    a0 = acc_sc[0]
    a1 = acc_sc[1]
    o = (a0[0:DA_V_DIM] / a0[DA_V_DIM:DA_V_DIM + 1]
         - lam * (a1[0:DA_V_DIM] / a1[DA_V_DIM:DA_V_DIM + 1]))
    o = o * lax.rsqrt(jnp.mean(o * o, axis=0, keepdims=True) + RMS_EPS) * sg_ref[...]
    o_ref[...] = (o * (1.0 - lambda_init)).T.astype(o_ref.dtype)


def _diff_attention(qt, k, vt, lam_p, subln_col, lambda_init, bsz, seq, cast_f32):
    tq = ATTN_Q_TILE
    tk = ATTN_TILE
    nq = seq // tq
    nk = seq // tk
    n_steps = bsz * DA_HEADS * nq
    def n_slices(w):
        return max(c for c in range(1, n_steps + 1)
                   if w.shape[0] % c == 0 and (w.shape[0] // c) % 16 == 0)
    order = sorted(range(len(cast_f32)), key=lambda a: n_slices(cast_f32[a]))
    cast_sorted = [cast_f32[a] for a in order]
    counts = [n_slices(w) for w in cast_sorted]
    cast_groups = tuple((c, counts.count(c)) for c in sorted(set(counts)))
    n_cast = len(cast_f32)
    slices = [(w.shape[0] // c, w.shape[1]) for w, c in zip(cast_sorted, counts)]
    any_spec = pl.BlockSpec(memory_space=pl.ANY)
    out = pl.pallas_call(
        functools.partial(_attn_kernel, lambda_init=lambda_init, cast_groups=cast_groups,
                          n_steps=n_steps, n_q=nq),
        grid=(bsz, DA_HEADS, nq),
        in_specs=[pl.BlockSpec(lam_p.shape, lambda b, h, i: (0, 0)),
                  pl.BlockSpec(subln_col.shape, lambda b, h, i: (0, 0)),
                  pl.BlockSpec((LANES, tq), lambda b, h, i: (h, b * nq + i)),
                  pl.BlockSpec((seq, LANES), lambda b, h, i: (b, h)),
                  pl.BlockSpec((None, nk, VT_ROWS, tk), lambda b, h, i: (h, b, 0, 0))]
                 + [any_spec] * n_cast,
        out_specs=[pl.BlockSpec((tq, LANES), lambda b, h, i: (b * nq + i, h))] + [any_spec] * n_cast,
        out_shape=[jax.ShapeDtypeStruct(k.shape, BF16)]
                  + [jax.ShapeDtypeStruct(w.shape, BF16) for w in cast_sorted],
        scratch_shapes=[pltpu.VMEM((2, LANES, tq), BF16), pltpu.VMEM((2, 1, tq), F32),
                        pltpu.VMEM((2, 1, tq), F32), pltpu.VMEM((2, VT_ROWS, tq), F32),
                        pltpu.VMEM((2, tq, tq), F32), pltpu.VMEM((2, tq, tq), BF16)]
                       + [pltpu.VMEM((2,) + s, F32) for s in slices]
                       + [pltpu.VMEM((2,) + s, BF16) for s in slices]
                       + [pltpu.SemaphoreType.DMA((2,))] * (2 * len(cast_groups)),
        compiler_params=pltpu.CompilerParams(
            dimension_semantics=("arbitrary", "arbitrary", "arbitrary"),
            vmem_limit_bytes=VMEM_LIMIT),
        name="l0_diff_attention",
    )(lam_p, subln_col, qt, k, vt, *cast_sorted)
    cast_bf16 = [None] * n_cast
    for pos, a in enumerate(order):
        cast_bf16[a] = out[1 + pos]
    return out[0], cast_bf16


def _l0_post_kernel(x_ref, attn_ref, conv_ref, p_ref, wo_ref, gffn_ref, w1_ref, w3_ref, w2_ref,
                    gple_ref, wg_ref, wp_ref, o_ref, acc_sc, hn_sc, h_sc):
    h = (x_ref[...] + _dot(attn_ref[...], wo_ref[0:DA_WIDTH, :])
         + _dot(conv_ref[...], wo_ref[DA_WIDTH:DA_WIDTH + SC_WIDTH, :]))
    hn_sc[...] = _rms(h, gffn_ref[...]).astype(BF16)
    h_sc[...] = h
    _swiglu_chunks(hn_sc, w1_ref, w3_ref, w2_ref, acc_sc)
    o_ref[...] = _ple(h_sc[...] + acc_sc[...], p_ref, gple_ref, wg_ref, wp_ref)


def _l0_post(x2, attn, conv, p0, wo, gffn, w1, w3, w2, gple, wg, wp):
    n = x2.shape[0]
    tm = ROW_TILE
    return pl.pallas_call(
        _l0_post_kernel,
        grid=(n // tm,),
        in_specs=[_rows(tm, D_MODEL), _rows(tm, DA_WIDTH), _rows(tm, SC_WIDTH),
                  _layer_rows(tm, PLE_DIM, 0, n // tm),
                  _resident(wo.shape), _resident(gffn.shape), _resident(w1.shape),
                  _resident(w3.shape), _resident(w2.shape), _resident(gple.shape),
                  _resident(wg.shape), _resident(wp.shape)],
        out_specs=_rows(tm, D_MODEL),
        out_shape=jax.ShapeDtypeStruct((n, D_MODEL), F32),
        scratch_shapes=[pltpu.VMEM((tm, D_MODEL), F32), pltpu.VMEM((tm, D_MODEL), BF16),
                        pltpu.VMEM((tm, D_MODEL), F32)],
        compiler_params=pltpu.CompilerParams(dimension_semantics=("arbitrary",),
                                             vmem_limit_bytes=VMEM_LIMIT),
        name="l0_outproj_swiglu_ple",
    )(x2, attn, conv, p0, wo, gffn, w1, w3, w2, gple, wg, wp)


def _split_bf16(a):
    hi = a.astype(BF16)
    return hi, (a - hi.astype(F32)).astype(BF16)


def _l1_mix_kernel(h_ref, gmix_ref, win_ref, lng_ref, lnb_ref, ws_ref, bs_ref, wout_ref,
                   gffn_ref, wr_ref,
                   h1_ref, hn2_ref, ridx_ref, rgate_ref, hn_sc, u_sc, v_sc, vn_sc, gated_sc):
    tm = h_ref.shape[0]
    hn_sc[...] = _rms(h_ref[...], gmix_ref[...]).astype(BF16)

    def gelu(z):
        return 0.5 * z * (1.0 + lax.erf(z * (2.0 ** -0.5)))

    for dst, base in ((v_sc, GM_WIDTH), (u_sc, 0)):
        for c in range(GM_WIDTH // FF_CHUNK):
            cols = slice(c * FF_CHUNK, (c + 1) * FF_CHUNK)
            dst[:, cols] = gelu(_dot(hn_sc[...], win_ref[:, base + c * FF_CHUNK:base + (c + 1) * FF_CHUNK]))

    vv = v_sc[...]
    mu = jnp.mean(vv, axis=-1, keepdims=True)
    vc = vv - mu
    var = jnp.mean(vc * vc, axis=-1, keepdims=True)
    vn_sc[...] = (vc * lax.rsqrt(var + LN_EPS) * lng_ref[...] + lnb_ref[...]).astype(BF16)

    n_chunks = tm // GM_CHUNK
    row = lax.broadcasted_iota(jnp.int32, (GM_CHUNK, GM_CHUNK), 0)
    col = lax.broadcasted_iota(jnp.int32, (GM_CHUNK, GM_CHUNK), 1)
    for g in range(GM_GROUPS):
        gs = slice(g * LANES, (g + 1) * LANES)
        wm = jnp.where(col <= row, ws_ref[g], 0.0).astype(BF16)
        rhs = jnp.concatenate(
            [vn_sc[r * GM_CHUNK:(r + 1) * GM_CHUNK, gs] for r in range(n_chunks)], axis=1)
        vs = _dot(wm, rhs) + bs_ref[:, g:g + 1]
        for r in range(n_chunks):
            rs = slice(r * GM_CHUNK, (r + 1) * GM_CHUNK)
            gated_sc[rs, gs] = (u_sc[rs, gs] * vs[:, r * LANES:(r + 1) * LANES]).astype(BF16)

    h1 = h_ref[...] + _dot(gated_sc[...], wout_ref[...])
    h1_ref[...] = h1

    hn2 = _rms(h1, gffn_ref[...])
    hn2_ref[...] = hn2
    x_hi, x_lo = _split_bf16(hn2)
    a_hi = _dot(x_hi, wr_ref[...])
    logits = a_hi + pltpu.roll(a_hi, LANES - N_EXPERTS, 1) + _dot(x_lo, wr_ref[...])
    lane = lax.broadcasted_iota(jnp.int32, logits.shape, 1)
    logits = jnp.where(lane < N_EXPERTS, logits, -jnp.inf)
    v1 = jnp.max(logits, axis=-1, keepdims=True)
    i1 = jnp.min(jnp.where(logits == v1, lane, LANES), axis=-1, keepdims=True)
    rest = jnp.where(lane == i1, -jnp.inf, logits)
    v2 = jnp.max(rest, axis=-1, keepdims=True)
    i2 = jnp.min(jnp.where(rest == v2, lane, LANES), axis=-1, keepdims=True)
    e2 = jnp.exp(v2 - v1)
    choices = jnp.where(lane == 0, i1, jnp.where(lane == 1, i2, 0)).astype(F32)
    ridx_ref[...] = choices.T[0:ridx_ref.shape[0], :].astype(jnp.int32)
    rgate_ref[:, 0:1] = 1.0 / (1.0 + e2)
    rgate_ref[:, 1:2] = e2 / (1.0 + e2)


def _l1_mix(h, gmix, win, lng, lnb, ws, bs_t, wout, gffn, wr):
    n = h.shape[0]
    tm = ROW_TILE
    return pl.pallas_call(
        _l1_mix_kernel,
        grid=(n // tm,),
        in_specs=[_rows(tm, D_MODEL)] + [_resident(a.shape) for a in
                                         (gmix, win, lng, lnb, ws, bs_t, wout, gffn, wr)],
        out_specs=[_rows(tm, D_MODEL), _rows(tm, D_MODEL),
                   pl.BlockSpec((F32_SUBLANES, tm), lambda i: (0, i)), _rows(tm, TOP_K)],
        out_shape=[jax.ShapeDtypeStruct((n, D_MODEL), F32), jax.ShapeDtypeStruct((n, D_MODEL), F32),
                   jax.ShapeDtypeStruct((F32_SUBLANES, n), jnp.int32),
                   jax.ShapeDtypeStruct((n, TOP_K), F32)],
        scratch_shapes=[pltpu.VMEM((tm, D_MODEL), BF16), pltpu.VMEM((tm, GM_WIDTH), F32),
                        pltpu.VMEM((tm, GM_WIDTH), F32), pltpu.VMEM((tm, GM_WIDTH), BF16),
                        pltpu.VMEM((tm, GM_WIDTH), BF16)],
        compiler_params=pltpu.CompilerParams(dimension_semantics=("arbitrary",),
                                             vmem_limit_bytes=VMEM_LIMIT),
        name="l1_gmlp_router",
    )(h, gmix, win, lng, lnb, ws, bs_t, wout, gffn, wr)


def _moe_kernel(bexp_ref, nvalid_ref, tok_ref, dst_ref, hn_hbm, w1_ref, w3_ref, w2_ref, y_hbm,
                xbuf, xb_sc, acc_sc, ybuf, gsem, ssem, fence_sem, fence_sink):
    i = pl.program_id(0)
    n_valid = nvalid_ref[0]
    tm = MOE_TILE
    prev = jnp.maximum(i - 1, 0)

    def gather_row(blk, r):
        return pltpu.make_async_copy(hn_hbm.at[pl.ds(tok_ref[blk * tm + r], 1)],
                                     xbuf.at[pl.ds(r, 1)], gsem)

    def scatter_row(blk, r):
        return pltpu.make_async_copy(ybuf.at[pl.ds(r, 1)],
                                     y_hbm.at[pl.ds(dst_ref[blk * tm + r], 1)], ssem)

    def gather_wait():
        pltpu.make_async_copy(hn_hbm.at[pl.ds(0, tm)], xbuf, gsem).wait()

    def scatter_wait():
        pltpu.make_async_copy(ybuf, y_hbm.at[pl.ds(0, tm)], ssem).wait()

    @pl.when(i < n_valid)
    def _():
        @pl.when(i == 0)
        def _():
            def issue(r, carry):
                gather_row(0, r).start()
                return carry
            lax.fori_loop(0, tm, issue, 0)

        gather_wait()
        xb_sc[...] = xbuf[...].astype(BF16)
        has_next = i + 1 < n_valid
        has_prev = i >= 1

        def copies(c):
            if c not in MOE_FENCE_CHUNKS:
                return
            g = MOE_FENCE_CHUNKS.index(c)
            if g > 0:
                fence_sink[0] = pl.semaphore_read(fence_sem)
            if g == len(MOE_FENCE_CHUNKS) - 1:
                return
            last_gather_group = len(MOE_COPY_ROWS) - 3
            for r in range(MOE_COPY_ROWS[g], tm if g == last_gather_group else MOE_COPY_ROWS[g + 1]):
                if g <= last_gather_group:
                    @pl.when(has_next)
                    def _():
                        gather_row(i + 1, r).start(priority=1)

                if r < MOE_COPY_ROWS[g + 1]:
                    @pl.when(has_prev)
                    def _():
                        scatter_row(prev, r).start(priority=r % 2)

        _swiglu_chunks(xb_sc, w1_ref, w3_ref, w2_ref, acc_sc, per_chunk=copies)

        @pl.when(has_prev)
        def _():
            scatter_wait()

        ybuf[...] = acc_sc[...]

    @pl.when(i == n_valid)
    def _():
        def issue(r, carry):
            scatter_row(prev, r).start()
            return carry
        lax.fori_loop(0, tm, issue, 0)
        scatter_wait()
        ybuf[...] = jnp.zeros(ybuf.shape, F32)
        plane = y_hbm.shape[0] // TOP_K
        for b in range(TOP_K):
            spare = pltpu.make_async_copy(ybuf, y_hbm.at[pl.ds((b + 1) * plane - tm, tm)], ssem)
            spare.start()
            spare.wait()


def _moe_experts(block_expert, n_valid, row_token, row_dst, hn2, w1, w3, w2, n_out_rows):
    n_blocks = block_expert.shape[0]
    tm = MOE_TILE
    per_expert = lambda w: pl.BlockSpec((None,) + w.shape[1:], lambda i, be, *_: (be[i], 0, 0))
    return pl.pallas_call(
        _moe_kernel,
        grid_spec=pltpu.PrefetchScalarGridSpec(
            num_scalar_prefetch=4,
            grid=(n_blocks,),
            in_specs=[pl.BlockSpec(memory_space=pl.ANY), per_expert(w1), per_expert(w3),
                      per_expert(w2)],
            out_specs=pl.BlockSpec(memory_space=pl.ANY),
            scratch_shapes=[pltpu.VMEM((tm, D_MODEL), F32), pltpu.VMEM((tm, D_MODEL), BF16),
                            pltpu.VMEM((tm, D_MODEL), F32), pltpu.VMEM((tm, D_MODEL), F32),
                            pltpu.SemaphoreType.DMA(()), pltpu.SemaphoreType.DMA(()),
                            pltpu.SemaphoreType.REGULAR(()), pltpu.SMEM((1,), jnp.int32)],
        ),
        out_shape=jax.ShapeDtypeStruct((n_out_rows, D_MODEL), F32),
        compiler_params=pltpu.CompilerParams(dimension_semantics=("arbitrary",),
                                             vmem_limit_bytes=VMEM_LIMIT),
        name="l1_expert_swiglu",
    )(block_expert, n_valid, row_token, row_dst, hn2, w1, w3, w2)


def _combine_kernel(h_ref, y0_ref, y1_ref, gate_ref, p_ref, gple_ref, wg_ref, wp_ref, gfin_ref,
                    o_ref):
    gate = gate_ref[...]
    h = h_ref[...] + (y0_ref[...] * gate[:, 0:1] + y1_ref[...] * gate[:, 1:2])
    h = _ple(h, p_ref, gple_ref, wg_ref, wp_ref)
    o_ref[...] = _rms(h, gfin_ref[...])


def _combine(h1, y_rows, rgate, p1, gple, wg, wp, gfin):
    n = h1.shape[0]
    tm = ROW_TILE
    plane_tiles = y_rows.shape[0] // TOP_K // tm
    return pl.pallas_call(
        _combine_kernel,
        grid=(n // tm,),
        in_specs=[_rows(tm, D_MODEL), _rows(tm, D_MODEL),
                  pl.BlockSpec((tm, D_MODEL), lambda i: (plane_tiles + i, 0)), _rows(tm, TOP_K),
                  _layer_rows(tm, PLE_DIM, 1, n // tm), _resident(gple.shape), _resident(wg.shape),
                  _resident(wp.shape), _resident(gfin.shape)],
        out_specs=_rows(tm, D_MODEL),
        out_shape=jax.ShapeDtypeStruct((n, D_MODEL), F32),
        compiler_params=pltpu.CompilerParams(dimension_semantics=("arbitrary",),
                                             vmem_limit_bytes=VMEM_LIMIT),
        name="l1_combine_ple_norm",
    )(h1, y_rows, y_rows, rgate, p1, gple, wg, wp, gfin)


def _routing_tables(ridx, n_blocks):
    n_tok = ridx.shape[1]
    n_assign = ridx.size
    e_flat = ridx.reshape(-1)
    experts = jnp.arange(N_EXPERTS, dtype=jnp.int32)[:, None]
    order = jnp.argsort(e_flat, stable=True).astype(jnp.int32)
    counts = jnp.sum((e_flat[None, :] == experts).astype(jnp.int32), axis=1)
    starts = jnp.cumsum(counts) - counts
    padded = (counts + MOE_TILE - 1) // MOE_TILE * MOE_TILE
    pad_ends = jnp.cumsum(padded)
    pad_starts = pad_ends - padded
    n_rows = (n_blocks + 2) * MOE_TILE
    r = jnp.arange(n_rows, dtype=jnp.int32)
    row_expert = jnp.minimum(jnp.sum((r[None, :] >= pad_ends[:, None]).astype(jnp.int32), axis=0),
                             N_EXPERTS - 1)
    of_row = lambda per_expert: jnp.sum(
        jnp.where(row_expert[None, :] == experts, per_expert[:, None], 0), axis=0)
    j = r - of_row(pad_starts)
    valid = (j < of_row(counts)) & (r < pad_ends[-1])
    src = jnp.clip(of_row(starts) + j, 0, n_assign - 1)
    assign = jnp.where(valid, order[src], -1)
    row_token = jnp.maximum(assign, 0) % n_tok
    plane = n_tok + MOE_TILE
    spare = (r // MOE_TILE % 2) * plane + n_tok + r % MOE_TILE
    row_dst = jnp.where(assign >= 0, (assign // n_tok) * plane + assign % n_tok, spare)
    block_start = jnp.arange(n_blocks, dtype=jnp.int32) * MOE_TILE
    block_expert = jnp.minimum(
        jnp.sum((block_start[:, None] >= pad_ends[None, :]).astype(jnp.int32), axis=1),
        N_EXPERTS - 1).astype(jnp.int32)
    n_valid = (pad_ends[-1:] // MOE_TILE).astype(jnp.int32)
    return row_token, row_dst, block_expert, n_valid


def kernel(x, p, positions, ln_mix, ln_ffn, ln_ple, a_w_in, a_lambda, a_subln, a_conv_w, a_w_out,
           ffn_w1, ffn_w3, ffn_w2, c_w_in, c_ln_g, c_ln_b, c_w_s, c_b_s, c_w_out, router_w,
           moe_w1, moe_w3, moe_w2, ple_gate, ple_proj, final_norm):
    bsz, seq, _ = x.shape
    n = bsz * seq
    assert seq % ROW_TILE == 0 and seq % ATTN_TILE == 0 and ROW_TILE % GM_CHUNK == 0
    x2 = x.reshape(n, D_MODEL)
    pos3 = positions.reshape(n // ROW_TILE, 1, ROW_TILE)
    row = lambda a: a.reshape(1, -1)

    freq = (ROPE_THETA ** (-jnp.arange(0, ROT_DIM, 2, dtype=F32) / ROT_DIM)).reshape(-1, 1)
    qt, k, vt, conv = _inproj(x2, pos3, row(ln_mix[0]), freq, a_conv_w[0].T, a_w_in[0].astype(BF16), seq)
    lambda_init = 0.8 - 0.6 * math.exp(-0.3 * 0)
    moe_f32 = [w[0].reshape(-1, w.shape[-1]) for w in (moe_w1, moe_w3, moe_w2)]
    attn, moe_bf16 = _diff_attention(qt, k, vt, a_lambda[0], a_subln[0].reshape(-1, 1), lambda_init,
                                     bsz, seq, moe_f32)
    mw1, mw3, mw2 = (wb.reshape(w.shape[1:]) for wb, w in zip(moe_bf16, (moe_w1, moe_w3, moe_w2)))
    p_all = p.reshape(-1, PLE_DIM)
    h = _l0_post(x2, attn, conv, p_all, a_w_out[0].astype(BF16), row(ln_ffn[0]),
                 ffn_w1[0].astype(BF16), ffn_w3[0].astype(BF16), ffn_w2[0].astype(BF16),
                 row(ln_ple[0]), ple_gate[0].astype(BF16), ple_proj[0].astype(BF16))

    wr = jnp.pad(jnp.concatenate(_split_bf16(router_w[0]), axis=1),
                 ((0, 0), (0, LANES - 2 * N_EXPERTS)))
    h1, hn2, ridx, rgate = _l1_mix(h, row(ln_mix[1]), c_w_in[0].astype(BF16), row(c_ln_g[0]),
                                   row(c_ln_b[0]), c_w_s[0], c_b_s[0].T, c_w_out[0].astype(BF16),
                                   row(ln_ffn[1]), wr)
    n_blocks = -(-(n * TOP_K + N_EXPERTS * (MOE_TILE - 1)) // MOE_TILE)
    row_token, row_dst, block_expert, n_valid = _routing_tables(ridx[:TOP_K], n_blocks)
    y_rows = _moe_experts(block_expert, n_valid, row_token, row_dst, hn2, mw1, mw3, mw2,
                          TOP_K * (n + MOE_TILE))
    out = _combine(h1, y_rows, rgate, p_all, row(ln_ple[1]),
                   ple_gate[1].astype(BF16), ple_proj[1].astype(BF16), row(final_norm))
    return out.reshape(bsz, seq, D_MODEL)
```

```python
import functools
import math

import jax
import jax.numpy as jnp
from jax import lax
from jax.experimental import pallas as pl
from jax.experimental.pallas import tpu as pltpu

F32 = jnp.float32
BF16 = jnp.bfloat16

D_MODEL = 1024
PLE_DIM = 256
DA_HEADS = 4
DA_HEAD_DIM = 64
DA_V_DIM = 2 * DA_HEAD_DIM
DA_WIDTH = DA_HEADS * DA_V_DIM
ROPE_THETA = 500000.0
ROT_DIM = DA_HEAD_DIM // 4
SC_WIDTH = 512
GM_WIDTH = D_MODEL
GM_GROUPS = 8
GM_CHUNK = 128
D_FF = 2816
N_EXPERTS = 8
TOP_K = 2
RMS_EPS = 1e-6
LN_EPS = 1e-5

LANES = 128
F32_SUBLANES = 8
FF_CHUNK = 256
N_FF_CHUNKS = D_FF // FF_CHUNK
ROW_TILE = 512
ATTN_TILE = 256
ATTN_Q_TILE = 512
Q_SCALE = DA_HEAD_DIM ** -0.5 * math.log2(math.e)
VT_ROWS = DA_V_DIM + 16
MOE_TILE = 512
MOE_FENCE_CHUNKS = (0, 5, 9, 10)
MOE_COPY_ROWS = (0, 256, 504, 512)
CONV_HALO = 8
VMEM_LIMIT = 56 * 2**20


def _resident(shape):
    nd = len(shape)
    return pl.BlockSpec(shape, lambda *_: (0,) * nd, pipeline_mode=pl.Buffered(1))


def _rows(tile, width):
    return pl.BlockSpec((tile, width), lambda i, *_: (i, 0))


def _layer_rows(tile, width, layer, tiles_per_layer):
    return pl.BlockSpec((tile, width), lambda i, *_: (layer * tiles_per_layer + i, 0))


def _rms(x, g):
    return x * lax.rsqrt(jnp.mean(x * x, axis=-1, keepdims=True) + RMS_EPS) * g


def _dot(a, b):
    return jnp.dot(a, b, preferred_element_type=F32)


def _swiglu_chunks(hn_ref, w1_ref, w3_ref, w2_ref, acc_ref, per_chunk=None):
    for c in range(N_FF_CHUNKS):
        cols = slice(c * FF_CHUNK, (c + 1) * FF_CHUNK)
        g = _dot(hn_ref[...], w1_ref[:, cols])
        u = _dot(hn_ref[...], w3_ref[:, cols])
        a = (g * jax.nn.sigmoid(g) * u).astype(BF16)
        if c == 0:
            acc_ref[...] = _dot(a, w2_ref[cols, :])
        else:
            acc_ref[...] += _dot(a, w2_ref[cols, :])
        if per_chunk is not None:
            per_chunk(c)


def _ple(h, p_ref, gple_ref, wg_ref, wp_ref):
    gate = jax.nn.sigmoid(_dot(_rms(h, gple_ref[...]).astype(BF16), wg_ref[...]))
    return h + gate * _dot(p_ref[...].astype(BF16), wp_ref[...])


def _inproj_kernel(x_ref, pos_ref, g_ref, freq_ref, cw_ref, w_ref,
                   qt_ref, k_ref, vt_ref, conv_ref, cbuf, *, tiles_per_seq):
    i = pl.program_id(0)
    tm = x_ref.shape[0]
    hn = _rms(x_ref[...], g_ref[...]).astype(BF16)

    half = ROT_DIM // 2
    ang = freq_ref[...] * pos_ref[...].astype(F32)
    cos_t = jnp.cos(ang)
    sin_t = jnp.sin(ang)
    zero_t = jnp.zeros_like(ang)
    rest = DA_HEAD_DIM - ROT_DIM
    pattern = lambda a, b, fill: jnp.concatenate(
        [a, b, jnp.full((rest, tm), fill, F32)] * (LANES // DA_HEAD_DIM), axis=0).T
    cos = pattern(cos_t, cos_t, 1.0)
    sin_up = pattern(-sin_t, zero_t, 0.0)
    sin_dn = pattern(zero_t, sin_t, 0.0)

    def rope(t):
        return (t * cos + pltpu.roll(t, LANES - half, 1) * sin_up
                + pltpu.roll(t, half, 1) * sin_dn)

    zq = _dot(hn, w_ref[:, 0:DA_WIDTH])
    zk = _dot(hn, w_ref[:, DA_WIDTH:2 * DA_WIDTH])
    zv = _dot(hn, w_ref[:, 2 * DA_WIDTH:3 * DA_WIDTH])
    ones = jnp.ones((VT_ROWS - DA_V_DIM, ATTN_TILE), BF16)
    for hd in range(DA_HEADS):
        sl = slice(hd * LANES, (hd + 1) * LANES)
        qt_ref[sl, :] = (rope(zq[:, sl]) * Q_SCALE).T.astype(BF16)
        k_ref[:, sl] = rope(zk[:, sl]).astype(BF16)
        for u in range(tm // ATTN_TILE):
            vt_ref[hd, u, 0:DA_V_DIM, :] = zv[u * ATTN_TILE:(u + 1) * ATTN_TILE, sl].T.astype(BF16)
            vt_ref[hd, u, DA_V_DIM:VT_ROWS, :] = ones

    off = 3 * DA_WIDTH
    b_gate = _dot(hn, w_ref[:, off:off + SC_WIDTH])
    c_gate = _dot(hn, w_ref[:, off + SC_WIDTH:off + 2 * SC_WIDTH])
    hc = _dot(hn, w_ref[:, off + 2 * SC_WIDTH:off + 3 * SC_WIDTH])

    @pl.when(i % tiles_per_seq == 0)
    def _():
        cbuf[0:CONV_HALO, :] = jnp.zeros((CONV_HALO, SC_WIDTH), F32)

    @pl.when(i % tiles_per_seq != 0)
    def _():
        cbuf[0:CONV_HALO, :] = cbuf[tm:tm + CONV_HALO, :]

    ch = c_gate * hc
    cbuf[CONV_HALO:CONV_HALO + tm, :] = ch
    ch1 = cbuf[CONV_HALO - 1:CONV_HALO - 1 + tm, :]
    ch2 = cbuf[CONV_HALO - 2:CONV_HALO - 2 + tm, :]
    conv = b_gate * (cw_ref[2:3, :] * ch + cw_ref[1:2, :] * ch1 + cw_ref[0:1, :] * ch2)
    conv_ref[...] = conv.astype(BF16)


def _inproj(x2, pos3, g, freq, cw, w, seq):
    n = x2.shape[0]
    tm = ROW_TILE
    out = jax.ShapeDtypeStruct((n, DA_WIDTH), BF16)
    sub = tm // ATTN_TILE
    return pl.pallas_call(
        functools.partial(_inproj_kernel, tiles_per_seq=seq // tm),
        grid=(n // tm,),
        in_specs=[_rows(tm, D_MODEL), pl.BlockSpec((None, 1, tm), lambda i: (i, 0, 0)),
                  _resident(g.shape), _resident(freq.shape),
                  _resident(cw.shape), _resident(w.shape)],
        out_specs=[pl.BlockSpec((DA_WIDTH, tm), lambda i: (0, i)), _rows(tm, DA_WIDTH),
                   pl.BlockSpec((DA_HEADS, sub, VT_ROWS, ATTN_TILE), lambda i: (0, i, 0, 0)),
                   _rows(tm, SC_WIDTH)],
        out_shape=[jax.ShapeDtypeStruct((DA_WIDTH, n), BF16), out,
                   jax.ShapeDtypeStruct((DA_HEADS, n // ATTN_TILE, VT_ROWS, ATTN_TILE), BF16), out],
        scratch_shapes=[pltpu.VMEM((tm + CONV_HALO, SC_WIDTH), F32)],
        compiler_params=pltpu.CompilerParams(dimension_semantics=("arbitrary",),
                                             vmem_limit_bytes=VMEM_LIMIT),
        name="l0_inproj",
    )(x2, pos3, g, freq, cw, w)


def _cast_slice(step, n_slices, grid_steps, src_refs, dst_refs, in_bufs, out_bufs, in_sem, out_sem):
    assert 2 <= n_slices <= grid_steps
    slot = step % 2
    other = 1 - slot

    def fetch(s, sl):
        return [pltpu.make_async_copy(src.at[pl.ds(s * buf.shape[1], buf.shape[1])], buf.at[sl],
                                      in_sem.at[sl]) for src, buf in zip(src_refs, in_bufs)]

    def write_back(s, sl):
        return [pltpu.make_async_copy(buf.at[sl], dst.at[pl.ds(s * buf.shape[1], buf.shape[1])],
                                      out_sem.at[sl]) for dst, buf in zip(dst_refs, out_bufs)]

    @pl.when(step == 0)
    def _():
        for c in fetch(0, 0):
            c.start()

    @pl.when(step < n_slices)
    def _():
        for c in fetch(step, slot):
            c.wait()

        @pl.when(step + 1 < n_slices)
        def _():
            for c in fetch(step + 1, other):
                c.start()

        @pl.when(step >= 2)
        def _():
            for c in write_back(step - 2, slot):
                c.wait()

        for src, dst in zip(in_bufs, out_bufs):
            dst[slot] = src[slot].astype(dst.dtype)
        for c in write_back(step, slot):
            c.start()

    @pl.when(step == min(n_slices, grid_steps - 1))
    def _():
        for s in (n_slices - 2, n_slices - 1):
            for c in write_back(s, s % 2):
                c.wait()


def _attn_kernel(lam_ref, sg_ref, qt_ref, k_ref, vt_ref, *rest, lambda_init, cast_groups, n_steps,
                 n_q):
    n_cast = sum(cnt for _, cnt in cast_groups)
    cast_src, rest = rest[:n_cast], rest[n_cast:]
    o_ref, rest = rest[0], rest[1:]
    cast_dst, rest = rest[:n_cast], rest[n_cast:]
    qc_sc, m_sc, alpha_sc, acc_sc, s_sc, p_sc = rest[:6]
    cast_in, cast_out = rest[6:6 + n_cast], rest[6 + n_cast:6 + 2 * n_cast]
    sems = rest[6 + 2 * n_cast:]
    step = ((pl.program_id(0) * pl.num_programs(1) + pl.program_id(1)) * pl.num_programs(2)
            + pl.program_id(2))
    first = 0
    for g, (n_slices, cnt) in enumerate(cast_groups):
        grp = slice(first, first + cnt)
        _cast_slice(step, n_slices, n_steps, cast_src[grp], cast_dst[grp], cast_in[grp],
                    cast_out[grp], sems[2 * g], sems[2 * g + 1])
        first += cnt

    qi = pl.program_id(2)
    tq = ATTN_Q_TILE
    tk = ATTN_TILE
    n_sub = tq // tk

    qt = qt_ref[...]
    dim = lax.broadcasted_iota(jnp.int32, (LANES, 1), 0)
    zero = jnp.zeros_like(qt)
    qc_sc[0] = jnp.where(dim < DA_HEAD_DIM, qt, zero)
    qc_sc[1] = jnp.where(dim >= DA_HEAD_DIM, qt, zero)
    m_sc[...] = jnp.full(m_sc.shape, -jnp.inf, F32)
    acc_sc[...] = jnp.zeros(acc_sc.shape, F32)

    def scores(j, diagonal):
        for c in range(2):
            for u in range(n_sub):
                ks = (j * n_sub + u) * tk
                su = _dot(k_ref[ks:ks + tk, :], qc_sc[c])
                if diagonal:
                    key = lax.broadcasted_iota(jnp.int32, (tk, tq), 0) + u * tk
                    qry = lax.broadcasted_iota(jnp.int32, (tk, tq), 1)
                    su = jnp.where(key <= qry, su, -jnp.inf)
                s_sc[c, u * tk:(u + 1) * tk, :] = su

    def softmax():
        for c in range(2):
            s = s_sc[c]
            m_prev = m_sc[c]
            m_new = jnp.maximum(m_prev, jnp.max(s, axis=0, keepdims=True))
            alpha_sc[c] = jnp.exp2(m_prev - m_new)
            m_sc[c] = m_new
            p_sc[c] = jnp.exp2(s - m_new).astype(BF16)

    def fold(j):
        for c in range(2):
            pv = None
            for u in range(n_sub):
                d = _dot(vt_ref[j * n_sub + u], p_sc[c, u * tk:(u + 1) * tk, :])
                pv = d if pv is None else pv + d
            acc_sc[c] = alpha_sc[c] * acc_sc[c] + pv

    def run(last):
        scores(0, diagonal=last == 0)
        for t in range(last + 1):
            if t >= 1:
                fold(t - 1)
            softmax()
            if t < last:
                scores(t + 1, diagonal=t + 1 == last)
        fold(last)

    for last in range(n_q):
        pl.when(qi == last)(functools.partial(run, last))

    lp = lam_ref[...]
    lam = (jnp.exp(jnp.sum(lp[0:1] * lp[1:2], axis=-1, keepdims=True))
           - jnp.exp(jnp.sum(lp[2:3] * lp[3:4], axis=-1, keepdims=True)) + lambda_init)
    a0 = acc_sc[0]
    a1 = acc_sc[1]
    o = (a0[0:DA_V_DIM] / a0[DA_V_DIM:DA_V_DIM + 1]
         - lam * (a1[0:DA_V_DIM] / a1[DA_V_DIM:DA_V_DIM + 1]))
    o = o * lax.rsqrt(jnp.mean(o * o, axis=0, keepdims=True) + RMS_EPS) * sg_ref[...]
    o_ref[...] = (o * (1.0 - lambda_init)).T.astype(o_ref.dtype)


def _diff_attention(qt, k, vt, lam_p, subln_col, lambda_init, bsz, seq, cast_f32):
    tq = ATTN_Q_TILE
    tk = ATTN_TILE
    nq = seq // tq
    nk = seq // tk
    n_steps = bsz * DA_HEADS * nq
    def n_slices(w):
        return max(c for c in range(1, n_steps + 1)
                   if w.shape[0] % c == 0 and (w.shape[0] // c) % 16 == 0)
    order = sorted(range(len(cast_f32)), key=lambda a: n_slices(cast_f32[a]))
    cast_sorted = [cast_f32[a] for a in order]
    counts = [n_slices(w) for w in cast_sorted]
    cast_groups = tuple((c, counts.count(c)) for c in sorted(set(counts)))
    n_cast = len(cast_f32)
    slices = [(w.shape[0] // c, w.shape[1]) for w, c in zip(cast_sorted, counts)]
    any_spec = pl.BlockSpec(memory_space=pl.ANY)
    out = pl.pallas_call(
        functools.partial(_attn_kernel, lambda_init=lambda_init, cast_groups=cast_groups,
                          n_steps=n_steps, n_q=nq),
        grid=(bsz, DA_HEADS, nq),
        in_specs=[pl.BlockSpec(lam_p.shape, lambda b, h, i: (0, 0)),
                  pl.BlockSpec(subln_col.shape, lambda b, h, i: (0, 0)),
                  pl.BlockSpec((LANES, tq), lambda b, h, i: (h, b * nq + i)),
                  pl.BlockSpec((seq, LANES), lambda b, h, i: (b, h)),
                  pl.BlockSpec((None, nk, VT_ROWS, tk), lambda b, h, i: (h, b, 0, 0))]
                 + [any_spec] * n_cast,
        out_specs=[pl.BlockSpec((tq, LANES), lambda b, h, i: (b * nq + i, h))] + [any_spec] * n_cast,
        out_shape=[jax.ShapeDtypeStruct(k.shape, BF16)]
                  + [jax.ShapeDtypeStruct(w.shape, BF16) for w in cast_sorted],
        scratch_shapes=[pltpu.VMEM((2, LANES, tq), BF16), pltpu.VMEM((2, 1, tq), F32),
                        pltpu.VMEM((2, 1, tq), F32), pltpu.VMEM((2, VT_ROWS, tq), F32),
                        pltpu.VMEM((2, tq, tq), F32), pltpu.VMEM((2, tq, tq), BF16)]
                       + [pltpu.VMEM((2,) + s, F32) for s in slices]
                       + [pltpu.VMEM((2,) + s, BF16) for s in slices]
                       + [pltpu.SemaphoreType.DMA((2,))] * (2 * len(cast_groups)),
        compiler_params=pltpu.CompilerParams(
            dimension_semantics=("arbitrary", "arbitrary", "arbitrary"),
            vmem_limit_bytes=VMEM_LIMIT),
        name="l0_diff_attention",
    )(lam_p, subln_col, qt, k, vt, *cast_sorted)
    cast_bf16 = [None] * n_cast
    for pos, a in enumerate(order):
        cast_bf16[a] = out[1 + pos]
    return out[0], cast_bf16


def _l0_post_kernel(x_ref, attn_ref, conv_ref, p_ref, wo_ref, gffn_ref, w1_ref, w3_ref, w2_ref,
                    gple_ref, wg_ref, wp_ref, o_ref, acc_sc, hn_sc, h_sc):
    h = (x_ref[...] + _dot(attn_ref[...], wo_ref[0:DA_WIDTH, :])
         + _dot(conv_ref[...], wo_ref[DA_WIDTH:DA_WIDTH + SC_WIDTH, :]))
    hn_sc[...] = _rms(h, gffn_ref[...]).astype(BF16)
    h_sc[...] = h
    _swiglu_chunks(hn_sc, w1_ref, w3_ref, w2_ref, acc_sc)
    o_ref[...] = _ple(h_sc[...] + acc_sc[...], p_ref, gple_ref, wg_ref, wp_ref)


def _l0_post(x2, attn, conv, p0, wo, gffn, w1, w3, w2, gple, wg, wp):
    n = x2.shape[0]
    tm = ROW_TILE
    return pl.pallas_call(
        _l0_post_kernel,
        grid=(n // tm,),
        in_specs=[_rows(tm, D_MODEL), _rows(tm, DA_WIDTH), _rows(tm, SC_WIDTH),
                  _layer_rows(tm, PLE_DIM, 0, n // tm),
                  _resident(wo.shape), _resident(gffn.shape), _resident(w1.shape),
                  _resident(w3.shape), _resident(w2.shape), _resident(gple.shape),
                  _resident(wg.shape), _resident(wp.shape)],
        out_specs=_rows(tm, D_MODEL),
        out_shape=jax.ShapeDtypeStruct((n, D_MODEL), F32),
        scratch_shapes=[pltpu.VMEM((tm, D_MODEL), F32), pltpu.VMEM((tm, D_MODEL), BF16),
                        pltpu.VMEM((tm, D_MODEL), F32)],
        compiler_params=pltpu.CompilerParams(dimension_semantics=("arbitrary",),
                                             vmem_limit_bytes=VMEM_LIMIT),
        name="l0_outproj_swiglu_ple",
    )(x2, attn, conv, p0, wo, gffn, w1, w3, w2, gple, wg, wp)


def _split_bf16(a):
    hi = a.astype(BF16)
    return hi, (a - hi.astype(F32)).astype(BF16)


def _l1_mix_kernel(h_ref, gmix_ref, win_ref, lng_ref, lnb_ref, ws_ref, bs_ref, wout_ref,
                   gffn_ref, wr_ref,
                   h1_ref, hn2_ref, ridx_ref, rgate_ref, hn_sc, u_sc, v_sc, vn_sc, gated_sc):
    tm = h_ref.shape[0]
    hn_sc[...] = _rms(h_ref[...], gmix_ref[...]).astype(BF16)

    def gelu(z):
        return 0.5 * z * (1.0 + lax.erf(z * (2.0 ** -0.5)))

    for dst, base in ((v_sc, GM_WIDTH), (u_sc, 0)):
        for c in range(GM_WIDTH // FF_CHUNK):
            cols = slice(c * FF_CHUNK, (c + 1) * FF_CHUNK)
            dst[:, cols] = gelu(_dot(hn_sc[...], win_ref[:, base + c * FF_CHUNK:base + (c + 1) * FF_CHUNK]))

    vv = v_sc[...]
    mu = jnp.mean(vv, axis=-1, keepdims=True)
    vc = vv - mu
    var = jnp.mean(vc * vc, axis=-1, keepdims=True)
    vn_sc[...] = (vc * lax.rsqrt(var + LN_EPS) * lng_ref[...] + lnb_ref[...]).astype(BF16)

    n_chunks = tm // GM_CHUNK
    row = lax.broadcasted_iota(jnp.int32, (GM_CHUNK, GM_CHUNK), 0)
    col = lax.broadcasted_iota(jnp.int32, (GM_CHUNK, GM_CHUNK), 1)
    for g in range(GM_GROUPS):
        gs = slice(g * LANES, (g + 1) * LANES)
        wm = jnp.where(col <= row, ws_ref[g], 0.0).astype(BF16)
        rhs = jnp.concatenate(
            [vn_sc[r * GM_CHUNK:(r + 1) * GM_CHUNK, gs] for r in range(n_chunks)], axis=1)
        vs = _dot(wm, rhs) + bs_ref[:, g:g + 1]
        for r in range(n_chunks):
            rs = slice(r * GM_CHUNK, (r + 1) * GM_CHUNK)
            gated_sc[rs, gs] = (u_sc[rs, gs] * vs[:, r * LANES:(r + 1) * LANES]).astype(BF16)

    h1 = h_ref[...] + _dot(gated_sc[...], wout_ref[...])
    h1_ref[...] = h1

    hn2 = _rms(h1, gffn_ref[...])
    hn2_ref[...] = hn2
    x_hi, x_lo = _split_bf16(hn2)
    a_hi = _dot(x_hi, wr_ref[...])
    logits = a_hi + pltpu.roll(a_hi, LANES - N_EXPERTS, 1) + _dot(x_lo, wr_ref[...])
    assert N_EXPERTS == F32_SUBLANES
    lt = logits.T[0:N_EXPERTS, :]
    expert = lax.broadcasted_iota(jnp.int32, lt.shape, 0)
    v1 = jnp.max(lt, axis=0, keepdims=True)
    i1 = jnp.min(jnp.where(lt == v1, expert, N_EXPERTS), axis=0, keepdims=True)
    rest = jnp.where(expert == i1, -jnp.inf, lt)
    v2 = jnp.max(rest, axis=0, keepdims=True)
    i2 = jnp.min(jnp.where(rest == v2, expert, N_EXPERTS), axis=0, keepdims=True)
    e2 = jnp.exp(v2 - v1)
    ridx_ref[...] = jnp.where(expert == 0, i1, jnp.where(expert == 1, i2, 0))
    rgate_ref[...] = jnp.where(expert == 0, 1.0 / (1.0 + e2), jnp.where(expert == 1, e2 / (1.0 + e2), 0.0))


def _l1_mix(h, gmix, win, lng, lnb, ws, bs_t, wout, gffn, wr):
    n = h.shape[0]
    tm = ROW_TILE
    return pl.pallas_call(
        _l1_mix_kernel,
        grid=(n // tm,),
        in_specs=[_rows(tm, D_MODEL)] + [_resident(a.shape) for a in
                                         (gmix, win, lng, lnb, ws, bs_t, wout, gffn, wr)],
        out_specs=[_rows(tm, D_MODEL), _rows(tm, D_MODEL),
                   pl.BlockSpec((F32_SUBLANES, tm), lambda i: (0, i)),
                   pl.BlockSpec((F32_SUBLANES, tm), lambda i: (0, i))],
        out_shape=[jax.ShapeDtypeStruct((n, D_MODEL), F32), jax.ShapeDtypeStruct((n, D_MODEL), F32),
                   jax.ShapeDtypeStruct((F32_SUBLANES, n), jnp.int32),
                   jax.ShapeDtypeStruct((F32_SUBLANES, n), F32)],
        scratch_shapes=[pltpu.VMEM((tm, D_MODEL), BF16), pltpu.VMEM((tm, GM_WIDTH), F32),
                        pltpu.VMEM((tm, GM_WIDTH), F32), pltpu.VMEM((tm, GM_WIDTH), BF16),
                        pltpu.VMEM((tm, GM_WIDTH), BF16)],
        compiler_params=pltpu.CompilerParams(dimension_semantics=("arbitrary",),
                                             vmem_limit_bytes=VMEM_LIMIT),
        name="l1_gmlp_router",
    )(h, gmix, win, lng, lnb, ws, bs_t, wout, gffn, wr)


def _moe_kernel(bexp_ref, nvalid_ref, tok_ref, dst_ref, hn_hbm, w1_ref, w3_ref, w2_ref, y_hbm,
                xbuf, xb_sc, acc_sc, ybuf, gsem, ssem, fence_sem, fence_sink):
    i = pl.program_id(0)
    n_valid = nvalid_ref[0]
    tm = MOE_TILE
    prev = jnp.maximum(i - 1, 0)

    def gather_row(blk, r):
        return pltpu.make_async_copy(hn_hbm.at[pl.ds(tok_ref[blk * tm + r], 1)],
                                     xbuf.at[pl.ds(r, 1)], gsem)

    def scatter_row(blk, r):
        return pltpu.make_async_copy(ybuf.at[pl.ds(r, 1)],
                                     y_hbm.at[pl.ds(dst_ref[blk * tm + r], 1)], ssem)

    def gather_wait():
        pltpu.make_async_copy(hn_hbm.at[pl.ds(0, tm)], xbuf, gsem).wait()

    def scatter_wait():
        pltpu.make_async_copy(ybuf, y_hbm.at[pl.ds(0, tm)], ssem).wait()

    @pl.when(i < n_valid)
    def _():
        @pl.when(i == 0)
        def _():
            def issue(r, carry):
                gather_row(0, r).start()
                return carry
            lax.fori_loop(0, tm, issue, 0)

        gather_wait()
        xb_sc[...] = xbuf[...].astype(BF16)
        has_next = i + 1 < n_valid
        has_prev = i >= 1

        def copies(c):
            if c not in MOE_FENCE_CHUNKS:
                return
            g = MOE_FENCE_CHUNKS.index(c)
            if g > 0:
                fence_sink[0] = pl.semaphore_read(fence_sem)
            if g == len(MOE_FENCE_CHUNKS) - 1:
                return
            last_gather_group = len(MOE_COPY_ROWS) - 3
            for r in range(MOE_COPY_ROWS[g], tm if g == last_gather_group else MOE_COPY_ROWS[g + 1]):
                if g <= last_gather_group:
                    @pl.when(has_next)
                    def _():
                        gather_row(i + 1, r).start(priority=1)

                if r < MOE_COPY_ROWS[g + 1]:
                    @pl.when(has_prev)
                    def _():
                        scatter_row(prev, r).start(priority=r % 2)

        _swiglu_chunks(xb_sc, w1_ref, w3_ref, w2_ref, acc_sc, per_chunk=copies)

        @pl.when(has_prev)
        def _():
            scatter_wait()

        ybuf[...] = acc_sc[...]

    @pl.when(i == n_valid)
    def _():
        def issue(r, carry):
            scatter_row(prev, r).start()
            return carry
        lax.fori_loop(0, tm, issue, 0)
        scatter_wait()
        ybuf[...] = jnp.zeros(ybuf.shape, F32)
        plane = y_hbm.shape[0] // TOP_K
        for b in range(TOP_K):
            spare = pltpu.make_async_copy(ybuf, y_hbm.at[pl.ds((b + 1) * plane - tm, tm)], ssem)
            spare.start()
            spare.wait()


def _moe_experts(block_expert, n_valid, row_token, row_dst, hn2, w1, w3, w2, n_out_rows):
    n_blocks = block_expert.shape[0]
    tm = MOE_TILE
    per_expert = lambda w: pl.BlockSpec((None,) + w.shape[1:], lambda i, be, *_: (be[i], 0, 0))
    return pl.pallas_call(
        _moe_kernel,
        grid_spec=pltpu.PrefetchScalarGridSpec(
            num_scalar_prefetch=4,
            grid=(n_blocks,),
            in_specs=[pl.BlockSpec(memory_space=pl.ANY), per_expert(w1), per_expert(w3),
                      per_expert(w2)],
            out_specs=pl.BlockSpec(memory_space=pl.ANY),
            scratch_shapes=[pltpu.VMEM((tm, D_MODEL), F32), pltpu.VMEM((tm, D_MODEL), BF16),
                            pltpu.VMEM((tm, D_MODEL), F32), pltpu.VMEM((tm, D_MODEL), F32),
                            pltpu.SemaphoreType.DMA(()), pltpu.SemaphoreType.DMA(()),
                            pltpu.SemaphoreType.REGULAR(()), pltpu.SMEM((1,), jnp.int32)],
        ),
        out_shape=jax.ShapeDtypeStruct((n_out_rows, D_MODEL), F32),
        compiler_params=pltpu.CompilerParams(dimension_semantics=("arbitrary",),
                                             vmem_limit_bytes=VMEM_LIMIT),
        name="l1_expert_swiglu",
    )(block_expert, n_valid, row_token, row_dst, hn2, w1, w3, w2)


def _combine_kernel(h_ref, y0_ref, y1_ref, gate_ref, p_ref, gple_ref, wg_ref, wp_ref, gfin_ref,
                    o_ref):
    tm = h_ref.shape[0]
    gate = jnp.concatenate([gate_ref[...], jnp.zeros((LANES - F32_SUBLANES, tm), F32)], axis=0).T
    h = h_ref[...] + (y0_ref[...] * gate[:, 0:1] + y1_ref[...] * gate[:, 1:2])
    h = _ple(h, p_ref, gple_ref, wg_ref, wp_ref)
    o_ref[...] = _rms(h, gfin_ref[...])


def _combine(h1, y_rows, rgate, p1, gple, wg, wp, gfin):
    n = h1.shape[0]
    tm = ROW_TILE
    plane_tiles = y_rows.shape[0] // TOP_K // tm
    return pl.pallas_call(
        _combine_kernel,
        grid=(n // tm,),
        in_specs=[_rows(tm, D_MODEL), _rows(tm, D_MODEL),
                  pl.BlockSpec((tm, D_MODEL), lambda i: (plane_tiles + i, 0)),
                  pl.BlockSpec((F32_SUBLANES, tm), lambda i: (0, i)),
                  _layer_rows(tm, PLE_DIM, 1, n // tm), _resident(gple.shape), _resident(wg.shape),
                  _resident(wp.shape), _resident(gfin.shape)],
        out_specs=_rows(tm, D_MODEL),
        out_shape=jax.ShapeDtypeStruct((n, D_MODEL), F32),
        compiler_params=pltpu.CompilerParams(dimension_semantics=("arbitrary",),
                                             vmem_limit_bytes=VMEM_LIMIT),
        name="l1_combine_ple_norm",
    )(h1, y_rows, y_rows, rgate, p1, gple, wg, wp, gfin)


def _routing_tables(ridx, n_blocks):
    n_tok = ridx.shape[1]
    n_assign = ridx.size
    e_flat = ridx.reshape(-1)
    experts = jnp.arange(N_EXPERTS, dtype=jnp.int32)[:, None]
    order = jnp.argsort(e_flat, stable=True).astype(jnp.int32)
    counts = jnp.sum((e_flat[None, :] == experts).astype(jnp.int32), axis=1)
    starts = jnp.cumsum(counts) - counts
    padded = (counts + MOE_TILE - 1) // MOE_TILE * MOE_TILE
    pad_ends = jnp.cumsum(padded)
    pad_starts = pad_ends - padded
    n_rows = (n_blocks + 2) * MOE_TILE
    r = jnp.arange(n_rows, dtype=jnp.int32)
    row_expert = jnp.minimum(jnp.sum((r[None, :] >= pad_ends[:, None]).astype(jnp.int32), axis=0),
                             N_EXPERTS - 1)
    of_row = lambda per_expert: jnp.sum(
        jnp.where(row_expert[None, :] == experts, per_expert[:, None], 0), axis=0)
    j = r - of_row(pad_starts)
    valid = (j < of_row(counts)) & (r < pad_ends[-1])
    src = jnp.clip(of_row(starts) + j, 0, n_assign - 1)
    assign = jnp.where(valid, order[src], -1)
    row_token = jnp.maximum(assign, 0) % n_tok
    plane = n_tok + MOE_TILE
    spare = (r // MOE_TILE % 2) * plane + n_tok + r % MOE_TILE
    row_dst = jnp.where(assign >= 0, (assign // n_tok) * plane + assign % n_tok, spare)
    block_start = jnp.arange(n_blocks, dtype=jnp.int32) * MOE_TILE
    block_expert = jnp.minimum(
        jnp.sum((block_start[:, None] >= pad_ends[None, :]).astype(jnp.int32), axis=1),
        N_EXPERTS - 1).astype(jnp.int32)
    n_valid = (pad_ends[-1:] // MOE_TILE).astype(jnp.int32)
    return row_token, row_dst, block_expert, n_valid


def kernel(x, p, positions, ln_mix, ln_ffn, ln_ple, a_w_in, a_lambda, a_subln, a_conv_w, a_w_out,
           ffn_w1, ffn_w3, ffn_w2, c_w_in, c_ln_g, c_ln_b, c_w_s, c_b_s, c_w_out, router_w,
           moe_w1, moe_w3, moe_w2, ple_gate, ple_proj, final_norm):
    bsz, seq, _ = x.shape
    n = bsz * seq
    assert seq % ROW_TILE == 0 and seq % ATTN_TILE == 0 and ROW_TILE % GM_CHUNK == 0
    x2 = x.reshape(n, D_MODEL)
    pos3 = positions.reshape(n // ROW_TILE, 1, ROW_TILE)
    row = lambda a: a.reshape(1, -1)

    freq = (ROPE_THETA ** (-jnp.arange(0, ROT_DIM, 2, dtype=F32) / ROT_DIM)).reshape(-1, 1)
    qt, k, vt, conv = _inproj(x2, pos3, row(ln_mix[0]), freq, a_conv_w[0].T, a_w_in[0].astype(BF16), seq)
    lambda_init = 0.8 - 0.6 * math.exp(-0.3 * 0)
    moe_f32 = [w[0].reshape(-1, w.shape[-1]) for w in (moe_w1, moe_w3, moe_w2)]
    attn, moe_bf16 = _diff_attention(qt, k, vt, a_lambda[0], a_subln[0].reshape(-1, 1), lambda_init,
                                     bsz, seq, moe_f32)
    mw1, mw3, mw2 = (wb.reshape(w.shape[1:]) for wb, w in zip(moe_bf16, (moe_w1, moe_w3, moe_w2)))
    p_all = p.reshape(-1, PLE_DIM)
    h = _l0_post(x2, attn, conv, p_all, a_w_out[0].astype(BF16), row(ln_ffn[0]),
                 ffn_w1[0].astype(BF16), ffn_w3[0].astype(BF16), ffn_w2[0].astype(BF16),
                 row(ln_ple[0]), ple_gate[0].astype(BF16), ple_proj[0].astype(BF16))

    wr = jnp.pad(jnp.concatenate(_split_bf16(router_w[0]), axis=1),
                 ((0, 0), (0, LANES - 2 * N_EXPERTS)))
    h1, hn2, ridx, rgate = _l1_mix(h, row(ln_mix[1]), c_w_in[0].astype(BF16), row(c_ln_g[0]),
                                   row(c_ln_b[0]), c_w_s[0], c_b_s[0].T, c_w_out[0].astype(BF16),
                                   row(ln_ffn[1]), wr)
    n_blocks = -(-(n * TOP_K + N_EXPERTS * (MOE_TILE - 1)) // MOE_TILE)
    row_token, row_dst, block_expert, n_valid = _routing_tables(ridx[:TOP_K], n_blocks)
    y_rows = _moe_experts(block_expert, n_valid, row_token, row_dst, hn2, mw1, mw3, mw2,
                          TOP_K * (n + MOE_TILE))
    out = _combine(h1, y_rows, rgate, p_all, row(ln_ple[1]),
                   ple_gate[1].astype(BF16), ple_proj[1].astype(BF16), row(final_norm))
    return out.reshape(bsz, seq, D_MODEL)
```

```python
import functools
import math

import jax
import jax.numpy as jnp
from jax import lax
from jax.experimental import pallas as pl
from jax.experimental.pallas import tpu as pltpu

F32 = jnp.float32
BF16 = jnp.bfloat16

D_MODEL = 1024
PLE_DIM = 256
DA_HEADS = 4
DA_HEAD_DIM = 64
DA_V_DIM = 2 * DA_HEAD_DIM
DA_WIDTH = DA_HEADS * DA_V_DIM
ROPE_THETA = 500000.0
ROT_DIM = DA_HEAD_DIM // 4
SC_WIDTH = 512
GM_WIDTH = D_MODEL
GM_GROUPS = 8
GM_CHUNK = 128
D_FF = 2816
N_EXPERTS = 8
TOP_K = 2
RMS_EPS = 1e-6
LN_EPS = 1e-5

LANES = 128
F32_SUBLANES = 8
FF_CHUNK = 256
N_FF_CHUNKS = D_FF // FF_CHUNK
ROW_TILE = 512
ATTN_TILE = 256
ATTN_Q_TILE = 512
Q_SCALE = DA_HEAD_DIM ** -0.5 * math.log2(math.e)
VT_ROWS = DA_V_DIM + 16
MOE_TILE = 512
MOE_FENCE_CHUNKS = (0, 5, 9, 10)
MOE_COPY_ROWS = (0, 256, 504, 512)
CONV_HALO = 8
VMEM_LIMIT = 56 * 2**20


def _resident(shape):
    nd = len(shape)
    return pl.BlockSpec(shape, lambda *_: (0,) * nd, pipeline_mode=pl.Buffered(1))


def _rows(tile, width):
    return pl.BlockSpec((tile, width), lambda i, *_: (i, 0))


def _layer_rows(tile, width, layer, tiles_per_layer):
    return pl.BlockSpec((tile, width), lambda i, *_: (layer * tiles_per_layer + i, 0))


def _rms(x, g):
    return x * lax.rsqrt(jnp.mean(x * x, axis=-1, keepdims=True) + RMS_EPS) * g


def _dot(a, b):
    return jnp.dot(a, b, preferred_element_type=F32)


def _swiglu_chunks(hn_ref, w1_ref, w3_ref, w2_ref, acc_ref, per_chunk=None):
    for c in range(N_FF_CHUNKS):
        cols = slice(c * FF_CHUNK, (c + 1) * FF_CHUNK)
        g = _dot(hn_ref[...], w1_ref[:, cols])
        u = _dot(hn_ref[...], w3_ref[:, cols])
        a = (g * jax.nn.sigmoid(g) * u).astype(BF16)
        if c == 0:
            acc_ref[...] = _dot(a, w2_ref[cols, :])
        else:
            acc_ref[...] += _dot(a, w2_ref[cols, :])
        if per_chunk is not None:
            per_chunk(c)


def _ple(h, p_ref, gple_ref, wg_ref, wp_ref):
    gate = jax.nn.sigmoid(_dot(_rms(h, gple_ref[...]).astype(BF16), wg_ref[...]))
    return h + gate * _dot(p_ref[...].astype(BF16), wp_ref[...])


def _inproj_kernel(x_ref, pos_ref, g_ref, freq_ref, cw_ref, w_ref,
                   qt_ref, k_ref, vt_ref, conv_ref, cbuf, *, tiles_per_seq):
    i = pl.program_id(0)
    tm = x_ref.shape[0]
    hn = _rms(x_ref[...], g_ref[...]).astype(BF16)

    half = ROT_DIM // 2
    ang = freq_ref[...] * pos_ref[...].astype(F32)
    cos_t = jnp.cos(ang)
    sin_t = jnp.sin(ang)
    zero_t = jnp.zeros_like(ang)
    rest = DA_HEAD_DIM - ROT_DIM
    pattern = lambda a, b, fill: jnp.concatenate(
        [a, b, jnp.full((rest, tm), fill, F32)] * (LANES // DA_HEAD_DIM), axis=0).T
    cos = pattern(cos_t, cos_t, 1.0)
    sin_up = pattern(-sin_t, zero_t, 0.0)
    sin_dn = pattern(zero_t, sin_t, 0.0)

    def rope(t):
        return (t * cos + pltpu.roll(t, LANES - half, 1) * sin_up
                + pltpu.roll(t, half, 1) * sin_dn)

    zq = _dot(hn, w_ref[:, 0:DA_WIDTH])
    zk = _dot(hn, w_ref[:, DA_WIDTH:2 * DA_WIDTH])
    zv = _dot(hn, w_ref[:, 2 * DA_WIDTH:3 * DA_WIDTH])
    ones = jnp.ones((VT_ROWS - DA_V_DIM, ATTN_TILE), BF16)
    for hd in range(DA_HEADS):
        sl = slice(hd * LANES, (hd + 1) * LANES)
        qt_ref[sl, :] = (rope(zq[:, sl]) * Q_SCALE).T.astype(BF16)
        k_ref[:, sl] = rope(zk[:, sl]).astype(BF16)
        for u in range(tm // ATTN_TILE):
            vt_ref[hd, u, 0:DA_V_DIM, :] = zv[u * ATTN_TILE:(u + 1) * ATTN_TILE, sl].T.astype(BF16)
            vt_ref[hd, u, DA_V_DIM:VT_ROWS, :] = ones

    off = 3 * DA_WIDTH
    b_gate = _dot(hn, w_ref[:, off:off + SC_WIDTH])
    c_gate = _dot(hn, w_ref[:, off + SC_WIDTH:off + 2 * SC_WIDTH])
    hc = _dot(hn, w_ref[:, off + 2 * SC_WIDTH:off + 3 * SC_WIDTH])

    @pl.when(i % tiles_per_seq == 0)
    def _():
        cbuf[0:CONV_HALO, :] = jnp.zeros((CONV_HALO, SC_WIDTH), F32)

    @pl.when(i % tiles_per_seq != 0)
    def _():
        cbuf[0:CONV_HALO, :] = cbuf[tm:tm + CONV_HALO, :]

    ch = c_gate * hc
    cbuf[CONV_HALO:CONV_HALO + tm, :] = ch
    ch1 = cbuf[CONV_HALO - 1:CONV_HALO - 1 + tm, :]
    ch2 = cbuf[CONV_HALO - 2:CONV_HALO - 2 + tm, :]
    conv = b_gate * (cw_ref[2:3, :] * ch + cw_ref[1:2, :] * ch1 + cw_ref[0:1, :] * ch2)
    conv_ref[...] = conv.astype(BF16)


def _inproj(x2, pos3, g, freq, cw, w, seq):
    n = x2.shape[0]
    tm = ROW_TILE
    out = jax.ShapeDtypeStruct((n, DA_WIDTH), BF16)
    sub = tm // ATTN_TILE
    return pl.pallas_call(
        functools.partial(_inproj_kernel, tiles_per_seq=seq // tm),
        grid=(n // tm,),
        in_specs=[_rows(tm, D_MODEL), pl.BlockSpec((None, 1, tm), lambda i: (i, 0, 0)),
                  _resident(g.shape), _resident(freq.shape),
                  _resident(cw.shape), _resident(w.shape)],
        out_specs=[pl.BlockSpec((DA_WIDTH, tm), lambda i: (0, i)), _rows(tm, DA_WIDTH),
                   pl.BlockSpec((DA_HEADS, sub, VT_ROWS, ATTN_TILE), lambda i: (0, i, 0, 0)),
                   _rows(tm, SC_WIDTH)],
        out_shape=[jax.ShapeDtypeStruct((DA_WIDTH, n), BF16), out,
                   jax.ShapeDtypeStruct((DA_HEADS, n // ATTN_TILE, VT_ROWS, ATTN_TILE), BF16), out],
        scratch_shapes=[pltpu.VMEM((tm + CONV_HALO, SC_WIDTH), F32)],
        compiler_params=pltpu.CompilerParams(dimension_semantics=("arbitrary",),
                                             vmem_limit_bytes=VMEM_LIMIT),
        name="l0_inproj",
    )(x2, pos3, g, freq, cw, w)


def _cast_slice(step, n_slices, grid_steps, src_refs, dst_refs, in_bufs, out_bufs, in_sem, out_sem):
    assert 2 <= n_slices <= grid_steps
    slot = step % 2
    other = 1 - slot

    def fetch(s, sl):
        return [pltpu.make_async_copy(src.at[pl.ds(s * buf.shape[1], buf.shape[1])], buf.at[sl],
                                      in_sem.at[sl]) for src, buf in zip(src_refs, in_bufs)]

    def write_back(s, sl):
        return [pltpu.make_async_copy(buf.at[sl], dst.at[pl.ds(s * buf.shape[1], buf.shape[1])],
                                      out_sem.at[sl]) for dst, buf in zip(dst_refs, out_bufs)]

    @pl.when(step == 0)
    def _():
        for c in fetch(0, 0):
            c.start()

    @pl.when(step < n_slices)
    def _():
        for c in fetch(step, slot):
            c.wait()

        @pl.when(step + 1 < n_slices)
        def _():
            for c in fetch(step + 1, other):
                c.start()

        @pl.when(step >= 2)
        def _():
            for c in write_back(step - 2, slot):
                c.wait()

        for src, dst in zip(in_bufs, out_bufs):
            dst[slot] = src[slot].astype(dst.dtype)
        for c in write_back(step, slot):
            c.start()

    @pl.when(step == min(n_slices, grid_steps - 1))
    def _():
        for s in (n_slices - 2, n_slices - 1):
            for c in write_back(s, s % 2):
                c.wait()


def _attn_kernel(lam_ref, sg_ref, qt_ref, k_ref, vt_ref, *rest, lambda_init, cast_groups, n_steps,
                 n_q):
    n_cast = sum(cnt for _, cnt in cast_groups)
    cast_src, rest = rest[:n_cast], rest[n_cast:]
    o_ref, rest = rest[0], rest[1:]
    cast_dst, rest = rest[:n_cast], rest[n_cast:]
    qc_sc, m_sc, alpha_sc, acc_sc, s_sc, p_sc = rest[:6]
    cast_in, cast_out = rest[6:6 + n_cast], rest[6 + n_cast:6 + 2 * n_cast]
    sems = rest[6 + 2 * n_cast:]
    step = ((pl.program_id(0) * pl.num_programs(1) + pl.program_id(1)) * pl.num_programs(2)
            + pl.program_id(2))
    first = 0
    for g, (n_slices, cnt) in enumerate(cast_groups):
        grp = slice(first, first + cnt)
        _cast_slice(step, n_slices, n_steps, cast_src[grp], cast_dst[grp], cast_in[grp],
                    cast_out[grp], sems[2 * g], sems[2 * g + 1])
        first += cnt

    qi = pl.program_id(2)
    tq = ATTN_Q_TILE
    tk = ATTN_TILE
    n_sub = tq // tk

    qt = qt_ref[...]
    dim = lax.broadcasted_iota(jnp.int32, (LANES, 1), 0)
    zero = jnp.zeros_like(qt)
    qc_sc[0] = jnp.where(dim < DA_HEAD_DIM, qt, zero)
    qc_sc[1] = jnp.where(dim >= DA_HEAD_DIM, qt, zero)
    m_sc[...] = jnp.full(m_sc.shape, -jnp.inf, F32)
    acc_sc[...] = jnp.zeros(acc_sc.shape, F32)

    def scores(j, diagonal):
        for c in range(2):
            for u in range(n_sub):
                ks = (j * n_sub + u) * tk
                su = _dot(k_ref[ks:ks + tk, :], qc_sc[c])
                if diagonal:
                    key = lax.broadcasted_iota(jnp.int32, (tk, tq), 0) + u * tk
                    qry = lax.broadcasted_iota(jnp.int32, (tk, tq), 1)
                    su = jnp.where(key <= qry, su, -jnp.inf)
                s_sc[c, u * tk:(u + 1) * tk, :] = su

    def softmax():
        for c in range(2):
            s = s_sc[c]
            m_prev = m_sc[c]
            m_new = jnp.maximum(m_prev, jnp.max(s, axis=0, keepdims=True))
            alpha_sc[c] = jnp.exp2(m_prev - m_new)
            m_sc[c] = m_new
            p_sc[c] = jnp.exp2(s - m_new).astype(BF16)

    def fold(j):
        for c in range(2):
            pv = None
            for u in range(n_sub):
                d = _dot(vt_ref[j * n_sub + u], p_sc[c, u * tk:(u + 1) * tk, :])
                pv = d if pv is None else pv + d
            acc_sc[c] = alpha_sc[c] * acc_sc[c] + pv

    def run(last):
        scores(0, diagonal=last == 0)
        for t in range(last + 1):
            if t >= 1:
                fold(t - 1)
            softmax()
            if t < last:
                scores(t + 1, diagonal=t + 1 == last)
        fold(last)

    for last in range(n_q):
        pl.when(qi == last)(functools.partial(run, last))

    lp = lam_ref[...]
    lam = (jnp.exp(jnp.sum(lp[0:1] * lp[1:2], axis=-1, keepdims=True))
           - jnp.exp(jnp.sum(lp[2:3] * lp[3:4], axis=-1, keepdims=True)) + lambda_init)
    a0 = acc_sc[0]
    a1 = acc_sc[1]
    o = (a0[0:DA_V_DIM] / a0[DA_V_DIM:DA_V_DIM + 1]
         - lam * (a1[0:DA_V_DIM] / a1[DA_V_DIM:DA_V_DIM + 1]))
    o = o * lax.rsqrt(jnp.mean(o * o, axis=0, keepdims=True) + RMS_EPS) * sg_ref[...]
    o_ref[...] = (o * (1.0 - lambda_init)).T.astype(o_ref.dtype)


def _diff_attention(qt, k, vt, lam_p, subln_col, lambda_init, bsz, seq, cast_f32):
    tq = ATTN_Q_TILE
    tk = ATTN_TILE
    nq = seq // tq
    nk = seq // tk
    n_steps = bsz * DA_HEADS * nq
    def n_slices(w):
        return max(c for c in range(1, n_steps + 1)
                   if w.shape[0] % c == 0 and (w.shape[0] // c) % 16 == 0)
    order = sorted(range(len(cast_f32)), key=lambda a: n_slices(cast_f32[a]))
    cast_sorted = [cast_f32[a] for a in order]
    counts = [n_slices(w) for w in cast_sorted]
    cast_groups = tuple((c, counts.count(c)) for c in sorted(set(counts)))
    n_cast = len(cast_f32)
    slices = [(w.shape[0] // c, w.shape[1]) for w, c in zip(cast_sorted, counts)]
    any_spec = pl.BlockSpec(memory_space=pl.ANY)
    out = pl.pallas_call(
        functools.partial(_attn_kernel, lambda_init=lambda_init, cast_groups=cast_groups,
                          n_steps=n_steps, n_q=nq),
        grid=(bsz, DA_HEADS, nq),
        in_specs=[pl.BlockSpec(lam_p.shape, lambda b, h, i: (0, 0)),
                  pl.BlockSpec(subln_col.shape, lambda b, h, i: (0, 0)),
                  pl.BlockSpec((LANES, tq), lambda b, h, i: (h, b * nq + i)),
                  pl.BlockSpec((seq, LANES), lambda b, h, i: (b, h)),
                  pl.BlockSpec((None, nk, VT_ROWS, tk), lambda b, h, i: (h, b, 0, 0))]
                 + [any_spec] * n_cast,
        out_specs=[pl.BlockSpec((tq, LANES), lambda b, h, i: (b * nq + i, h))] + [any_spec] * n_cast,
        out_shape=[jax.ShapeDtypeStruct(k.shape, BF16)]
                  + [jax.ShapeDtypeStruct(w.shape, BF16) for w in cast_sorted],
        scratch_shapes=[pltpu.VMEM((2, LANES, tq), BF16), pltpu.VMEM((2, 1, tq), F32),
                        pltpu.VMEM((2, 1, tq), F32), pltpu.VMEM((2, VT_ROWS, tq), F32),
                        pltpu.VMEM((2, tq, tq), F32), pltpu.VMEM((2, tq, tq), BF16)]
                       + [pltpu.VMEM((2,) + s, F32) for s in slices]
                       + [pltpu.VMEM((2,) + s, BF16) for s in slices]
                       + [pltpu.SemaphoreType.DMA((2,))] * (2 * len(cast_groups)),
        compiler_params=pltpu.CompilerParams(
            dimension_semantics=("arbitrary", "arbitrary", "arbitrary"),
            vmem_limit_bytes=VMEM_LIMIT),
        name="l0_diff_attention",
    )(lam_p, subln_col, qt, k, vt, *cast_sorted)
    cast_bf16 = [None] * n_cast
    for pos, a in enumerate(order):
        cast_bf16[a] = out[1 + pos]
    return out[0], cast_bf16


def _l0_post_kernel(x_ref, attn_ref, conv_ref, p_ref, wo_ref, gffn_ref, w1_ref, w3_ref, w2_ref,
                    gple_ref, wg_ref, wp_ref, o_ref, acc_sc, hn_sc, h_sc):
    h = (x_ref[...] + _dot(attn_ref[...], wo_ref[0:DA_WIDTH, :])
         + _dot(conv_ref[...], wo_ref[DA_WIDTH:DA_WIDTH + SC_WIDTH, :]))
    hn_sc[...] = _rms(h, gffn_ref[...]).astype(BF16)
    h_sc[...] = h
    _swiglu_chunks(hn_sc, w1_ref, w3_ref, w2_ref, acc_sc)
    o_ref[...] = _ple(h_sc[...] + acc_sc[...], p_ref, gple_ref, wg_ref, wp_ref)


def _l0_post(x2, attn, conv, p0, wo, gffn, w1, w3, w2, gple, wg, wp):
    n = x2.shape[0]
    tm = ROW_TILE
    return pl.pallas_call(
        _l0_post_kernel,
        grid=(n // tm,),
        in_specs=[_rows(tm, D_MODEL), _rows(tm, DA_WIDTH), _rows(tm, SC_WIDTH),
                  _layer_rows(tm, PLE_DIM, 0, n // tm),
                  _resident(wo.shape), _resident(gffn.shape), _resident(w1.shape),
                  _resident(w3.shape), _resident(w2.shape), _resident(gple.shape),
                  _resident(wg.shape), _resident(wp.shape)],
        out_specs=_rows(tm, D_MODEL),
        out_shape=jax.ShapeDtypeStruct((n, D_MODEL), F32),
        scratch_shapes=[pltpu.VMEM((tm, D_MODEL), F32), pltpu.VMEM((tm, D_MODEL), BF16),
                        pltpu.VMEM((tm, D_MODEL), F32)],
        compiler_params=pltpu.CompilerParams(dimension_semantics=("arbitrary",),
                                             vmem_limit_bytes=VMEM_LIMIT),
        name="l0_outproj_swiglu_ple",
    )(x2, attn, conv, p0, wo, gffn, w1, w3, w2, gple, wg, wp)


def _split_bf16(a):
    hi = a.astype(BF16)
    return hi, (a - hi.astype(F32)).astype(BF16)


def _l1_mix_kernel(h_ref, gmix_ref, win_ref, lng_ref, lnb_ref, ws_ref, bs_ref, wout_ref,
                   gffn_ref, wr_ref,
                   h1_ref, hn2_ref, ridx_ref, rgate_ref, hn_sc, u_sc, v_sc, vn_sc, gated_sc):
    tm = h_ref.shape[0]
    hn_sc[...] = _rms(h_ref[...], gmix_ref[...]).astype(BF16)

    def gelu(z):
        return 0.5 * z * (1.0 + lax.erf(z * (2.0 ** -0.5)))

    for dst, base in ((v_sc, GM_WIDTH), (u_sc, 0)):
        for c in range(GM_WIDTH // FF_CHUNK):
            cols = slice(c * FF_CHUNK, (c + 1) * FF_CHUNK)
            dst[:, cols] = gelu(_dot(hn_sc[...], win_ref[:, base + c * FF_CHUNK:base + (c + 1) * FF_CHUNK]))

    vv = v_sc[...]
    mu = jnp.mean(vv, axis=-1, keepdims=True)
    vc = vv - mu
    var = jnp.mean(vc * vc, axis=-1, keepdims=True)
    vn_sc[...] = (vc * lax.rsqrt(var + LN_EPS) * lng_ref[...] + lnb_ref[...]).astype(BF16)

    n_chunks = tm // GM_CHUNK
    row = lax.broadcasted_iota(jnp.int32, (GM_CHUNK, GM_CHUNK), 0)
    col = lax.broadcasted_iota(jnp.int32, (GM_CHUNK, GM_CHUNK), 1)
    for g in range(GM_GROUPS):
        gs = slice(g * LANES, (g + 1) * LANES)
        wm = jnp.where(col <= row, ws_ref[g], 0.0).astype(BF16)
        rhs = jnp.concatenate(
            [vn_sc[r * GM_CHUNK:(r + 1) * GM_CHUNK, gs] for r in range(n_chunks)], axis=1)
        vs = _dot(wm, rhs) + bs_ref[:, g:g + 1]
        for r in range(n_chunks):
            rs = slice(r * GM_CHUNK, (r + 1) * GM_CHUNK)
            gated_sc[rs, gs] = (u_sc[rs, gs] * vs[:, r * LANES:(r + 1) * LANES]).astype(BF16)

    h1 = h_ref[...] + _dot(gated_sc[...], wout_ref[...])
    h1_ref[...] = h1

    hn2 = _rms(h1, gffn_ref[...])
    hn2_ref[...] = hn2
    x_hi, x_lo = _split_bf16(hn2)
    a_hi = _dot(x_hi, wr_ref[...])
    logits = a_hi + pltpu.roll(a_hi, LANES - N_EXPERTS, 1) + _dot(x_lo, wr_ref[...])
    assert N_EXPERTS == F32_SUBLANES
    lt = logits.T[0:N_EXPERTS, :]
    expert = lax.broadcasted_iota(jnp.int32, lt.shape, 0)
    v1 = jnp.max(lt, axis=0, keepdims=True)
    i1 = jnp.min(jnp.where(lt == v1, expert, N_EXPERTS), axis=0, keepdims=True)
    rest = jnp.where(expert == i1, -jnp.inf, lt)
    v2 = jnp.max(rest, axis=0, keepdims=True)
    i2 = jnp.min(jnp.where(rest == v2, expert, N_EXPERTS), axis=0, keepdims=True)
    e2 = jnp.exp(v2 - v1)
    ridx_ref[...] = jnp.where(expert == 0, i1, jnp.where(expert == 1, i2, 0))
    rgate_ref[...] = jnp.where(expert == 0, 1.0 / (1.0 + e2), jnp.where(expert == 1, e2 / (1.0 + e2), 0.0))


def _l1_mix(h, gmix, win, lng, lnb, ws, bs_t, wout, gffn, wr):
    n = h.shape[0]
    tm = ROW_TILE
    return pl.pallas_call(
        _l1_mix_kernel,
        grid=(n // tm,),
        in_specs=[_rows(tm, D_MODEL)] + [_resident(a.shape) for a in
                                         (gmix, win, lng, lnb, ws, bs_t, wout, gffn, wr)],
        out_specs=[_rows(tm, D_MODEL), _rows(tm, D_MODEL),
                   pl.BlockSpec((F32_SUBLANES, tm), lambda i: (0, i)),
                   pl.BlockSpec((F32_SUBLANES, tm), lambda i: (0, i))],
        out_shape=[jax.ShapeDtypeStruct((n, D_MODEL), F32), jax.ShapeDtypeStruct((n, D_MODEL), F32),
                   jax.ShapeDtypeStruct((F32_SUBLANES, n), jnp.int32),
                   jax.ShapeDtypeStruct((F32_SUBLANES, n), F32)],
        scratch_shapes=[pltpu.VMEM((tm, D_MODEL), BF16), pltpu.VMEM((tm, GM_WIDTH), F32),
                        pltpu.VMEM((tm, GM_WIDTH), F32), pltpu.VMEM((tm, GM_WIDTH), BF16),
                        pltpu.VMEM((tm, GM_WIDTH), BF16)],
        compiler_params=pltpu.CompilerParams(dimension_semantics=("arbitrary",),
                                             vmem_limit_bytes=VMEM_LIMIT),
        name="l1_gmlp_router",
    )(h, gmix, win, lng, lnb, ws, bs_t, wout, gffn, wr)


def _moe_kernel(bexp_ref, nvalid_ref, tok_ref, dst_ref, hn_hbm, w1_ref, w3_ref, w2_ref, y_hbm,
                xbuf, xb_sc, acc_sc, ybuf, gsem, ssem, fence_sem, fence_sink):
    i = pl.program_id(0)
    n_valid = nvalid_ref[0]
    tm = MOE_TILE
    prev = jnp.maximum(i - 1, 0)

    def gather_row(blk, r):
        return pltpu.make_async_copy(hn_hbm.at[pl.ds(tok_ref[blk * tm + r], 1)],
                                     xbuf.at[pl.ds(r, 1)], gsem)

    def scatter_row(blk, r):
        return pltpu.make_async_copy(ybuf.at[pl.ds(r, 1)],
                                     y_hbm.at[pl.ds(dst_ref[blk * tm + r], 1)], ssem)

    def gather_wait():
        pltpu.make_async_copy(hn_hbm.at[pl.ds(0, tm)], xbuf, gsem).wait()

    def scatter_wait():
        pltpu.make_async_copy(ybuf, y_hbm.at[pl.ds(0, tm)], ssem).wait()

    @pl.when(i < n_valid)
    def _():
        @pl.when(i == 0)
        def _():
            def issue(r, carry):
                gather_row(0, r).start()
                return carry
            lax.fori_loop(0, tm, issue, 0)

        gather_wait()
        xb_sc[...] = xbuf[...].astype(BF16)
        has_next = i + 1 < n_valid
        has_prev = i >= 1

        def copies(c):
            if c not in MOE_FENCE_CHUNKS:
                return
            g = MOE_FENCE_CHUNKS.index(c)
            if g > 0:
                fence_sink[0] = pl.semaphore_read(fence_sem)
            if g == len(MOE_FENCE_CHUNKS) - 1:
                return
            last_gather_group = len(MOE_COPY_ROWS) - 3
            for r in range(MOE_COPY_ROWS[g], tm if g == last_gather_group else MOE_COPY_ROWS[g + 1]):
                if g <= last_gather_group:
                    @pl.when(has_next)
                    def _():
                        gather_row(i + 1, r).start(priority=1)

                if r < MOE_COPY_ROWS[g + 1]:
                    @pl.when(has_prev)
                    def _():
                        scatter_row(prev, r).start(priority=r % 2)

        _swiglu_chunks(xb_sc, w1_ref, w3_ref, w2_ref, acc_sc, per_chunk=copies)

        @pl.when(has_prev)
        def _():
            scatter_wait()

        ybuf[...] = acc_sc[...]

    @pl.when(i == n_valid)
    def _():
        def issue(r, carry):
            scatter_row(prev, r).start()
            return carry
        lax.fori_loop(0, tm, issue, 0)
        scatter_wait()
        ybuf[...] = jnp.zeros(ybuf.shape, F32)
        plane = y_hbm.shape[0] // TOP_K
        for b in range(TOP_K):
            spare = pltpu.make_async_copy(ybuf, y_hbm.at[pl.ds((b + 1) * plane - tm, tm)], ssem)
            spare.start()
            spare.wait()


def _moe_experts(block_expert, n_valid, row_token, row_dst, hn2, w1, w3, w2, n_out_rows):
    n_blocks = block_expert.shape[0]
    tm = MOE_TILE
    per_expert = lambda w: pl.BlockSpec((None,) + w.shape[1:], lambda i, be, *_: (be[i], 0, 0))
    return pl.pallas_call(
        _moe_kernel,
        grid_spec=pltpu.PrefetchScalarGridSpec(
            num_scalar_prefetch=4,
            grid=(n_blocks,),
            in_specs=[pl.BlockSpec(memory_space=pl.ANY), per_expert(w1), per_expert(w3),
                      per_expert(w2)],
            out_specs=pl.BlockSpec(memory_space=pl.ANY),
            scratch_shapes=[pltpu.VMEM((tm, D_MODEL), F32), pltpu.VMEM((tm, D_MODEL), BF16),
                            pltpu.VMEM((tm, D_MODEL), F32), pltpu.VMEM((tm, D_MODEL), F32),
                            pltpu.SemaphoreType.DMA(()), pltpu.SemaphoreType.DMA(()),
                            pltpu.SemaphoreType.REGULAR(()), pltpu.SMEM((1,), jnp.int32)],
        ),
        out_shape=jax.ShapeDtypeStruct((n_out_rows, D_MODEL), F32),
        compiler_params=pltpu.CompilerParams(dimension_semantics=("arbitrary",),
                                             vmem_limit_bytes=VMEM_LIMIT),
        name="l1_expert_swiglu",
    )(block_expert, n_valid, row_token, row_dst, hn2, w1, w3, w2)


def _combine_kernel(h_ref, y0_ref, y1_ref, gate_ref, p_ref, gple_ref, wg_ref, wp_ref, gfin_ref,
                    o_ref):
    tm = h_ref.shape[0]
    gate = jnp.concatenate([gate_ref[...], jnp.zeros((LANES - F32_SUBLANES, tm), F32)], axis=0).T
    h = h_ref[...] + (y0_ref[...] * gate[:, 0:1] + y1_ref[...] * gate[:, 1:2])
    h = _ple(h, p_ref, gple_ref, wg_ref, wp_ref)
    o_ref[...] = _rms(h, gfin_ref[...])


def _combine(h1, y_rows, rgate, p1, gple, wg, wp, gfin):
    n = h1.shape[0]
    tm = ROW_TILE
    plane_tiles = y_rows.shape[0] // TOP_K // tm
    return pl.pallas_call(
        _combine_kernel,
        grid=(n // tm,),
        in_specs=[_rows(tm, D_MODEL), _rows(tm, D_MODEL),
                  pl.BlockSpec((tm, D_MODEL), lambda i: (plane_tiles + i, 0)),
                  pl.BlockSpec((F32_SUBLANES, tm), lambda i: (0, i)),
                  _layer_rows(tm, PLE_DIM, 1, n // tm), _resident(gple.shape), _resident(wg.shape),
                  _resident(wp.shape), _resident(gfin.shape)],
        out_specs=_rows(tm, D_MODEL),
        out_shape=jax.ShapeDtypeStruct((n, D_MODEL), F32),
        compiler_params=pltpu.CompilerParams(dimension_semantics=("arbitrary",),
                                             vmem_limit_bytes=VMEM_LIMIT),
        name="l1_combine_ple_norm",
    )(h1, y_rows, y_rows, rgate, p1, gple, wg, wp, gfin)


def _routing_tables(ridx, n_blocks):
    n_tok = ridx.shape[1]
    n_assign = ridx.size
    e_flat = ridx.reshape(-1)
    experts = jnp.arange(N_EXPERTS, dtype=jnp.int32)[:, None]
    counts = jnp.sum((e_flat[None, :] == experts).astype(jnp.int32), axis=1)
    padded = (counts + MOE_TILE - 1) // MOE_TILE * MOE_TILE
    pad_ends = jnp.cumsum(padded)
    n_rows = (n_blocks + 2) * MOE_TILE
    r = jnp.arange(n_rows, dtype=jnp.int32)
    filler = jnp.arange(n_rows - n_assign, dtype=jnp.int32)
    filler_ends = jnp.cumsum(padded - counts)
    filler_expert = jnp.sum((filler[None, :] >= filler_ends[:, None]).astype(jnp.int32), axis=0)
    keys = jnp.concatenate([2 * e_flat, 2 * filler_expert + 1])
    ids = jnp.concatenate([jnp.arange(n_assign, dtype=jnp.int32),
                           jnp.full((n_rows - n_assign,), -1, jnp.int32)])
    _, assign = lax.sort((keys, ids), num_keys=1, is_stable=True)
    row_token = jnp.maximum(assign, 0) % n_tok
    plane = n_tok + MOE_TILE
    spare = (r // MOE_TILE % 2) * plane + n_tok + r % MOE_TILE
    row_dst = jnp.where(assign >= 0, (assign // n_tok) * plane + assign % n_tok, spare)
    block_start = jnp.arange(n_blocks, dtype=jnp.int32) * MOE_TILE
    block_expert = jnp.minimum(
        jnp.sum((block_start[:, None] >= pad_ends[None, :]).astype(jnp.int32), axis=1),
        N_EXPERTS - 1).astype(jnp.int32)
    n_valid = (pad_ends[-1:] // MOE_TILE).astype(jnp.int32)
    return row_token, row_dst, block_expert, n_valid


def kernel(x, p, positions, ln_mix, ln_ffn, ln_ple, a_w_in, a_lambda, a_subln, a_conv_w, a_w_out,
           ffn_w1, ffn_w3, ffn_w2, c_w_in, c_ln_g, c_ln_b, c_w_s, c_b_s, c_w_out, router_w,
           moe_w1, moe_w3, moe_w2, ple_gate, ple_proj, final_norm):
    bsz, seq, _ = x.shape
    n = bsz * seq
    assert seq % ROW_TILE == 0 and seq % ATTN_TILE == 0 and ROW_TILE % GM_CHUNK == 0
    x2 = x.reshape(n, D_MODEL)
    pos3 = positions.reshape(n // ROW_TILE, 1, ROW_TILE)
    row = lambda a: a.reshape(1, -1)

    freq = (ROPE_THETA ** (-jnp.arange(0, ROT_DIM, 2, dtype=F32) / ROT_DIM)).reshape(-1, 1)
    qt, k, vt, conv = _inproj(x2, pos3, row(ln_mix[0]), freq, a_conv_w[0].T, a_w_in[0].astype(BF16), seq)
    lambda_init = 0.8 - 0.6 * math.exp(-0.3 * 0)
    moe_f32 = [w[0].reshape(-1, w.shape[-1]) for w in (moe_w1, moe_w3, moe_w2)]
    attn, moe_bf16 = _diff_attention(qt, k, vt, a_lambda[0], a_subln[0].reshape(-1, 1), lambda_init,
                                     bsz, seq, moe_f32)
    mw1, mw3, mw2 = (wb.reshape(w.shape[1:]) for wb, w in zip(moe_bf16, (moe_w1, moe_w3, moe_w2)))
    p_all = p.reshape(-1, PLE_DIM)
    h = _l0_post(x2, attn, conv, p_all, a_w_out[0].astype(BF16), row(ln_ffn[0]),
                 ffn_w1[0].astype(BF16), ffn_w3[0].astype(BF16), ffn_w2[0].astype(BF16),
                 row(ln_ple[0]), ple_gate[0].astype(BF16), ple_proj[0].astype(BF16))

    wr = jnp.pad(jnp.concatenate(_split_bf16(router_w[0]), axis=1),
                 ((0, 0), (0, LANES - 2 * N_EXPERTS)))
    h1, hn2, ridx, rgate = _l1_mix(h, row(ln_mix[1]), c_w_in[0].astype(BF16), row(c_ln_g[0]),
                                   row(c_ln_b[0]), c_w_s[0], c_b_s[0].T, c_w_out[0].astype(BF16),
                                   row(ln_ffn[1]), wr)
    n_blocks = -(-(n * TOP_K + N_EXPERTS * (MOE_TILE - 1)) // MOE_TILE)
    row_token, row_dst, block_expert, n_valid = _routing_tables(ridx[:TOP_K], n_blocks)
    y_rows = _moe_experts(block_expert, n_valid, row_token, row_dst, hn2, mw1, mw3, mw2,
                          TOP_K * (n + MOE_TILE))
    out = _combine(h1, y_rows, rgate, p_all, row(ln_ple[1]),
                   ple_gate[1].astype(BF16), ple_proj[1].astype(BF16), row(final_norm))
    return out.reshape(bsz, seq, D_MODEL)
```

```python
import functools
import math

import jax
import jax.numpy as jnp
from jax import lax
from jax.experimental import pallas as pl
from jax.experimental.pallas import tpu as pltpu

F32 = jnp.float32
BF16 = jnp.bfloat16

D_MODEL = 1024
PLE_DIM = 256
DA_HEADS = 4
DA_HEAD_DIM = 64
DA_V_DIM = 2 * DA_HEAD_DIM
DA_WIDTH = DA_HEADS * DA_V_DIM
ROPE_THETA = 500000.0
ROT_DIM = DA_HEAD_DIM // 4
SC_WIDTH = 512
GM_WIDTH = D_MODEL
GM_GROUPS = 8
GM_CHUNK = 128
D_FF = 2816
N_EXPERTS = 8
TOP_K = 2
RMS_EPS = 1e-6
LN_EPS = 1e-5

LANES = 128
F32_SUBLANES = 8
FF_CHUNK = 256
N_FF_CHUNKS = D_FF // FF_CHUNK
ROW_TILE = 512
ATTN_TILE = 256
ATTN_Q_TILE = 512
Q_SCALE = DA_HEAD_DIM ** -0.5 * math.log2(math.e)
VT_ROWS = DA_V_DIM + 16
MOE_TILE = 512
MOE_FENCE_CHUNKS = (0, 5, 9, 10)
MOE_COPY_ROWS = (0, 256, 504, 512)
CONV_HALO = 8
VMEM_LIMIT = 56 * 2**20


def _resident(shape):
    nd = len(shape)
    return pl.BlockSpec(shape, lambda *_: (0,) * nd, pipeline_mode=pl.Buffered(1))


def _rows(tile, width):
    return pl.BlockSpec((tile, width), lambda i, *_: (i, 0))


def _layer_rows(tile, width, layer, tiles_per_layer):
    return pl.BlockSpec((tile, width), lambda i, *_: (layer * tiles_per_layer + i, 0))


def _rms(x, g):
    return x * lax.rsqrt(jnp.mean(x * x, axis=-1, keepdims=True) + RMS_EPS) * g


def _dot(a, b):
    return jnp.dot(a, b, preferred_element_type=F32)


def _swiglu_chunks(hn_ref, w1_ref, w3_ref, w2_ref, acc_ref, per_chunk=None):
    for c in range(N_FF_CHUNKS):
        cols = slice(c * FF_CHUNK, (c + 1) * FF_CHUNK)
        g = _dot(hn_ref[...], w1_ref[:, cols])
        u = _dot(hn_ref[...], w3_ref[:, cols])
        a = (g * jax.nn.sigmoid(g) * u).astype(BF16)
        if c == 0:
            acc_ref[...] = _dot(a, w2_ref[cols, :])
        else:
            acc_ref[...] += _dot(a, w2_ref[cols, :])
        if per_chunk is not None:
            per_chunk(c)


def _ple(h, p_ref, gple_ref, wg_ref, wp_ref):
    gate = jax.nn.sigmoid(_dot(_rms(h, gple_ref[...]).astype(BF16), wg_ref[...]))
    return h + gate * _dot(p_ref[...].astype(BF16), wp_ref[...])


def _inproj_kernel(x_ref, pos_ref, g_ref, freq_ref, cw_ref, w_ref,
                   qt_ref, k_ref, vt_ref, conv_ref, cbuf, *, tiles_per_seq):
    i = pl.program_id(0)
    tm = x_ref.shape[0]
    hn = _rms(x_ref[...], g_ref[...]).astype(BF16)

    half = ROT_DIM // 2
    ang = freq_ref[...] * pos_ref[...].astype(F32)
    cos_t = jnp.cos(ang)
    sin_t = jnp.sin(ang)
    zero_t = jnp.zeros_like(ang)
    rest = DA_HEAD_DIM - ROT_DIM
    pattern = lambda a, b, fill: jnp.concatenate(
        [a, b, jnp.full((rest, tm), fill, F32)] * (LANES // DA_HEAD_DIM), axis=0).T
    cos = pattern(cos_t, cos_t, 1.0)
    sin_up = pattern(-sin_t, zero_t, 0.0)
    sin_dn = pattern(zero_t, sin_t, 0.0)

    def rope(t):
        return (t * cos + pltpu.roll(t, LANES - half, 1) * sin_up
                + pltpu.roll(t, half, 1) * sin_dn)

    zq = _dot(hn, w_ref[:, 0:DA_WIDTH])
    zk = _dot(hn, w_ref[:, DA_WIDTH:2 * DA_WIDTH])
    zv = _dot(hn, w_ref[:, 2 * DA_WIDTH:3 * DA_WIDTH])
    ones = jnp.ones((VT_ROWS - DA_V_DIM, ATTN_TILE), BF16)
    for hd in range(DA_HEADS):
        sl = slice(hd * LANES, (hd + 1) * LANES)
        qt_ref[sl, :] = (rope(zq[:, sl]) * Q_SCALE).T.astype(BF16)
        k_ref[:, sl] = rope(zk[:, sl]).astype(BF16)
        for u in range(tm // ATTN_TILE):
            vt_ref[hd, u, 0:DA_V_DIM, :] = zv[u * ATTN_TILE:(u + 1) * ATTN_TILE, sl].T.astype(BF16)
            vt_ref[hd, u, DA_V_DIM:VT_ROWS, :] = ones

    off = 3 * DA_WIDTH
    b_gate = _dot(hn, w_ref[:, off:off + SC_WIDTH])
    c_gate = _dot(hn, w_ref[:, off + SC_WIDTH:off + 2 * SC_WIDTH])
    hc = _dot(hn, w_ref[:, off + 2 * SC_WIDTH:off + 3 * SC_WIDTH])

    @pl.when(i % tiles_per_seq == 0)
    def _():
        cbuf[0:CONV_HALO, :] = jnp.zeros((CONV_HALO, SC_WIDTH), F32)

    @pl.when(i % tiles_per_seq != 0)
    def _():
        cbuf[0:CONV_HALO, :] = cbuf[tm:tm + CONV_HALO, :]

    ch = c_gate * hc
    cbuf[CONV_HALO:CONV_HALO + tm, :] = ch
    ch1 = cbuf[CONV_HALO - 1:CONV_HALO - 1 + tm, :]
    ch2 = cbuf[CONV_HALO - 2:CONV_HALO - 2 + tm, :]
    conv = b_gate * (cw_ref[2:3, :] * ch + cw_ref[1:2, :] * ch1 + cw_ref[0:1, :] * ch2)
    conv_ref[...] = conv.astype(BF16)


def _inproj(x2, pos3, g, freq, cw, w, seq):
    n = x2.shape[0]
    tm = ROW_TILE
    out = jax.ShapeDtypeStruct((n, DA_WIDTH), BF16)
    sub = tm // ATTN_TILE
    return pl.pallas_call(
        functools.partial(_inproj_kernel, tiles_per_seq=seq // tm),
        grid=(n // tm,),
        in_specs=[_rows(tm, D_MODEL), pl.BlockSpec((None, 1, tm), lambda i: (i, 0, 0)),
                  _resident(g.shape), _resident(freq.shape),
                  _resident(cw.shape), _resident(w.shape)],
        out_specs=[pl.BlockSpec((DA_WIDTH, tm), lambda i: (0, i)), _rows(tm, DA_WIDTH),
                   pl.BlockSpec((DA_HEADS, sub, VT_ROWS, ATTN_TILE), lambda i: (0, i, 0, 0)),
                   _rows(tm, SC_WIDTH)],
        out_shape=[jax.ShapeDtypeStruct((DA_WIDTH, n), BF16), out,
                   jax.ShapeDtypeStruct((DA_HEADS, n // ATTN_TILE, VT_ROWS, ATTN_TILE), BF16), out],
        scratch_shapes=[pltpu.VMEM((tm + CONV_HALO, SC_WIDTH), F32)],
        compiler_params=pltpu.CompilerParams(dimension_semantics=("arbitrary",),
                                             vmem_limit_bytes=VMEM_LIMIT),
        name="l0_inproj",
    )(x2, pos3, g, freq, cw, w)


def _cast_slice(step, n_slices, grid_steps, src_refs, dst_refs, in_bufs, out_bufs, in_sem, out_sem):
    assert 2 <= n_slices <= grid_steps
    slot = step % 2
    other = 1 - slot

    def fetch(s, sl):
        return [pltpu.make_async_copy(src.at[pl.ds(s * buf.shape[1], buf.shape[1])], buf.at[sl],
                                      in_sem.at[sl]) for src, buf in zip(src_refs, in_bufs)]

    def write_back(s, sl):
        return [pltpu.make_async_copy(buf.at[sl], dst.at[pl.ds(s * buf.shape[1], buf.shape[1])],
                                      out_sem.at[sl]) for dst, buf in zip(dst_refs, out_bufs)]

    @pl.when(step == 0)
    def _():
        for c in fetch(0, 0):
            c.start()

    @pl.when(step < n_slices)
    def _():
        for c in fetch(step, slot):
            c.wait()

        @pl.when(step + 1 < n_slices)
        def _():
            for c in fetch(step + 1, other):
                c.start()

        @pl.when(step >= 2)
        def _():
            for c in write_back(step - 2, slot):
                c.wait()

        for src, dst in zip(in_bufs, out_bufs):
            dst[slot] = src[slot].astype(dst.dtype)
        for c in write_back(step, slot):
            c.start()

    @pl.when(step == min(n_slices, grid_steps - 1))
    def _():
        for s in (n_slices - 2, n_slices - 1):
            for c in write_back(s, s % 2):
                c.wait()


def _attn_kernel(lam_ref, sg_ref, qt_ref, k_ref, vt_ref, *rest, lambda_init, cast_groups, n_steps,
                 n_q):
    n_cast = sum(cnt for _, cnt in cast_groups)
    cast_src, rest = rest[:n_cast], rest[n_cast:]
    o_ref, rest = rest[0], rest[1:]
    cast_dst, rest = rest[:n_cast], rest[n_cast:]
    qc_sc, m_sc, alpha_sc, acc_sc, s_sc, p_sc = rest[:6]
    cast_in, cast_out = rest[6:6 + n_cast], rest[6 + n_cast:6 + 2 * n_cast]
    sems = rest[6 + 2 * n_cast:]
    step = ((pl.program_id(0) * pl.num_programs(1) + pl.program_id(1)) * pl.num_programs(2)
            + pl.program_id(2))
    first = 0
    for g, (n_slices, cnt) in enumerate(cast_groups):
        grp = slice(first, first + cnt)
        _cast_slice(step, n_slices, n_steps, cast_src[grp], cast_dst[grp], cast_in[grp],
                    cast_out[grp], sems[2 * g], sems[2 * g + 1])
        first += cnt

    qi = pl.program_id(2)
    tq = ATTN_Q_TILE
    tk = ATTN_TILE
    n_sub = tq // tk

    qt = qt_ref[...]
    dim = lax.broadcasted_iota(jnp.int32, (LANES, 1), 0)
    zero = jnp.zeros_like(qt)
    qc_sc[0] = jnp.where(dim < DA_HEAD_DIM, qt, zero)
    qc_sc[1] = jnp.where(dim >= DA_HEAD_DIM, qt, zero)
    m_sc[...] = jnp.full(m_sc.shape, -jnp.inf, F32)
    acc_sc[...] = jnp.zeros(acc_sc.shape, F32)

    def scores(j, diagonal):
        for c in range(2):
            for u in range(n_sub):
                ks = (j * n_sub + u) * tk
                su = _dot(k_ref[ks:ks + tk, :], qc_sc[c])
                if diagonal:
                    key = lax.broadcasted_iota(jnp.int32, (tk, tq), 0) + u * tk
                    qry = lax.broadcasted_iota(jnp.int32, (tk, tq), 1)
                    su = jnp.where(key <= qry, su, -jnp.inf)
                s_sc[c, u * tk:(u + 1) * tk, :] = su

    def softmax():
        for c in range(2):
            s = s_sc[c]
            m_prev = m_sc[c]
            m_new = jnp.maximum(m_prev, jnp.max(s, axis=0, keepdims=True))
            alpha_sc[c] = jnp.exp2(m_prev - m_new)
            m_sc[c] = m_new
            p_sc[c] = jnp.exp2(s - m_new).astype(BF16)

    def fold(j):
        for c in range(2):
            pv = None
            for u in range(n_sub):
                d = _dot(vt_ref[j * n_sub + u], p_sc[c, u * tk:(u + 1) * tk, :])
                pv = d if pv is None else pv + d
            acc_sc[c] = alpha_sc[c] * acc_sc[c] + pv

    def run(last):
        scores(0, diagonal=last == 0)
        for t in range(last + 1):
            if t >= 1:
                fold(t - 1)
            softmax()
            if t < last:
                scores(t + 1, diagonal=t + 1 == last)
        fold(last)

    for last in range(n_q):
        pl.when(qi == last)(functools.partial(run, last))

    lp = lam_ref[...]
    lam = (jnp.exp(jnp.sum(lp[0:1] * lp[1:2], axis=-1, keepdims=True))
           - jnp.exp(jnp.sum(lp[2:3] * lp[3:4], axis=-1, keepdims=True)) + lambda_init)
    a0 = acc_sc[0]
    a1 = acc_sc[1]
    o = (a0[0:DA_V_DIM] / a0[DA_V_DIM:DA_V_DIM + 1]
         - lam * (a1[0:DA_V_DIM] / a1[DA_V_DIM:DA_V_DIM + 1]))
    o = o * lax.rsqrt(jnp.mean(o * o, axis=0, keepdims=True) + RMS_EPS) * sg_ref[...]
    o_ref[...] = (o * (1.0 - lambda_init)).T.astype(o_ref.dtype)


def _diff_attention(qt, k, vt, lam_p, subln_col, lambda_init, bsz, seq, cast_f32):
    tq = ATTN_Q_TILE
    tk = ATTN_TILE
    nq = seq // tq
    nk = seq // tk
    n_steps = bsz * DA_HEADS * nq
    def n_slices(w):
        return max(c for c in range(1, n_steps + 1)
                   if w.shape[0] % c == 0 and (w.shape[0] // c) % 16 == 0)
    order = sorted(range(len(cast_f32)), key=lambda a: n_slices(cast_f32[a]))
    cast_sorted = [cast_f32[a] for a in order]
    counts = [n_slices(w) for w in cast_sorted]
    cast_groups = tuple((c, counts.count(c)) for c in sorted(set(counts)))
    n_cast = len(cast_f32)
    slices = [(w.shape[0] // c, w.shape[1]) for w, c in zip(cast_sorted, counts)]
    any_spec = pl.BlockSpec(memory_space=pl.ANY)
    out = pl.pallas_call(
        functools.partial(_attn_kernel, lambda_init=lambda_init, cast_groups=cast_groups,
                          n_steps=n_steps, n_q=nq),
        grid=(bsz, DA_HEADS, nq),
        in_specs=[pl.BlockSpec(lam_p.shape, lambda b, h, i: (0, 0)),
                  pl.BlockSpec(subln_col.shape, lambda b, h, i: (0, 0)),
                  pl.BlockSpec((LANES, tq), lambda b, h, i: (h, b * nq + i)),
                  pl.BlockSpec((seq, LANES), lambda b, h, i: (b, h)),
                  pl.BlockSpec((None, nk, VT_ROWS, tk), lambda b, h, i: (h, b, 0, 0))]
                 + [any_spec] * n_cast,
        out_specs=[pl.BlockSpec((tq, LANES), lambda b, h, i: (b * nq + i, h))] + [any_spec] * n_cast,
        out_shape=[jax.ShapeDtypeStruct(k.shape, BF16)]
                  + [jax.ShapeDtypeStruct(w.shape, BF16) for w in cast_sorted],
        scratch_shapes=[pltpu.VMEM((2, LANES, tq), BF16), pltpu.VMEM((2, 1, tq), F32),
                        pltpu.VMEM((2, 1, tq), F32), pltpu.VMEM((2, VT_ROWS, tq), F32),
                        pltpu.VMEM((2, tq, tq), F32), pltpu.VMEM((2, tq, tq), BF16)]
                       + [pltpu.VMEM((2,) + s, F32) for s in slices]
                       + [pltpu.VMEM((2,) + s, BF16) for s in slices]
                       + [pltpu.SemaphoreType.DMA((2,))] * (2 * len(cast_groups)),
        compiler_params=pltpu.CompilerParams(
            dimension_semantics=("arbitrary", "arbitrary", "arbitrary"),
            vmem_limit_bytes=VMEM_LIMIT),
        name="l0_diff_attention",
    )(lam_p, subln_col, qt, k, vt, *cast_sorted)
    cast_bf16 = [None] * n_cast
    for pos, a in enumerate(order):
        cast_bf16[a] = out[1 + pos]
    return out[0], cast_bf16


def _l0_post_kernel(x_ref, attn_ref, conv_ref, p_ref, wo_ref, gffn_ref, w1_ref, w3_ref, w2_ref,
                    gple_ref, wg_ref, wp_ref, o_ref, acc_sc, hn_sc, h_sc):
    h = (x_ref[...] + _dot(attn_ref[...], wo_ref[0:DA_WIDTH, :])
         + _dot(conv_ref[...], wo_ref[DA_WIDTH:DA_WIDTH + SC_WIDTH, :]))
    hn_sc[...] = _rms(h, gffn_ref[...]).astype(BF16)
    h_sc[...] = h
    _swiglu_chunks(hn_sc, w1_ref, w3_ref, w2_ref, acc_sc)
    o_ref[...] = _ple(h_sc[...] + acc_sc[...], p_ref, gple_ref, wg_ref, wp_ref)


def _l0_post(x2, attn, conv, p0, wo, gffn, w1, w3, w2, gple, wg, wp):
    n = x2.shape[0]
    tm = ROW_TILE
    return pl.pallas_call(
        _l0_post_kernel,
        grid=(n // tm,),
        in_specs=[_rows(tm, D_MODEL), _rows(tm, DA_WIDTH), _rows(tm, SC_WIDTH),
                  _layer_rows(tm, PLE_DIM, 0, n // tm),
                  _resident(wo.shape), _resident(gffn.shape), _resident(w1.shape),
                  _resident(w3.shape), _resident(w2.shape), _resident(gple.shape),
                  _resident(wg.shape), _resident(wp.shape)],
        out_specs=_rows(tm, D_MODEL),
        out_shape=jax.ShapeDtypeStruct((n, D_MODEL), F32),
        scratch_shapes=[pltpu.VMEM((tm, D_MODEL), F32), pltpu.VMEM((tm, D_MODEL), BF16),
                        pltpu.VMEM((tm, D_MODEL), F32)],
        compiler_params=pltpu.CompilerParams(dimension_semantics=("arbitrary",),
                                             vmem_limit_bytes=VMEM_LIMIT),
        name="l0_outproj_swiglu_ple",
    )(x2, attn, conv, p0, wo, gffn, w1, w3, w2, gple, wg, wp)


def _split_bf16(a):
    hi = a.astype(BF16)
    return hi, (a - hi.astype(F32)).astype(BF16)


def _l1_mix_kernel(h_ref, gmix_ref, win_ref, lng_ref, lnb_ref, ws_ref, bs_ref, wout_ref,
                   gffn_ref, wr_ref,
                   h1_ref, hn2_ref, ridx_ref, rgate_ref, hn_sc, u_sc, v_sc, vn_sc, gated_sc):
    tm = h_ref.shape[0]
    hn_sc[...] = _rms(h_ref[...], gmix_ref[...]).astype(BF16)

    def gelu(z):
        return 0.5 * z * (1.0 + lax.erf(z * (2.0 ** -0.5)))

    for dst, base in ((v_sc, GM_WIDTH), (u_sc, 0)):
        for c in range(GM_WIDTH // FF_CHUNK):
            cols = slice(c * FF_CHUNK, (c + 1) * FF_CHUNK)
            dst[:, cols] = gelu(_dot(hn_sc[...], win_ref[:, base + c * FF_CHUNK:base + (c + 1) * FF_CHUNK]))

    vv = v_sc[...]
    mu = jnp.mean(vv, axis=-1, keepdims=True)
    vc = vv - mu
    var = jnp.mean(vc * vc, axis=-1, keepdims=True)
    vn_sc[...] = (vc * lax.rsqrt(var + LN_EPS) * lng_ref[...] + lnb_ref[...]).astype(BF16)

    n_chunks = tm // GM_CHUNK
    row = lax.broadcasted_iota(jnp.int32, (GM_CHUNK, GM_CHUNK), 0)
    col = lax.broadcasted_iota(jnp.int32, (GM_CHUNK, GM_CHUNK), 1)
    for g in range(GM_GROUPS):
        gs = slice(g * LANES, (g + 1) * LANES)
        wm = jnp.where(col <= row, ws_ref[g], 0.0).astype(BF16)
        rhs = jnp.concatenate(
            [vn_sc[r * GM_CHUNK:(r + 1) * GM_CHUNK, gs] for r in range(n_chunks)], axis=1)
        vs = _dot(wm, rhs) + bs_ref[:, g:g + 1]
        for r in range(n_chunks):
            rs = slice(r * GM_CHUNK, (r + 1) * GM_CHUNK)
            gated_sc[rs, gs] = (u_sc[rs, gs] * vs[:, r * LANES:(r + 1) * LANES]).astype(BF16)

    h1 = h_ref[...] + _dot(gated_sc[...], wout_ref[...])
    h1_ref[...] = h1

    hn2 = _rms(h1, gffn_ref[...])
    hn2_ref[...] = hn2
    x_hi, x_lo = _split_bf16(hn2)
    a_hi = _dot(x_hi, wr_ref[...])
    logits = a_hi + pltpu.roll(a_hi, LANES - N_EXPERTS, 1) + _dot(x_lo, wr_ref[...])
    assert N_EXPERTS == F32_SUBLANES
    lt = logits.T[0:N_EXPERTS, :]
    expert = lax.broadcasted_iota(jnp.int32, lt.shape, 0)
    v1 = jnp.max(lt, axis=0, keepdims=True)
    i1 = jnp.min(jnp.where(lt == v1, expert, N_EXPERTS), axis=0, keepdims=True)
    rest = jnp.where(expert == i1, -jnp.inf, lt)
    v2 = jnp.max(rest, axis=0, keepdims=True)
    i2 = jnp.min(jnp.where(rest == v2, expert, N_EXPERTS), axis=0, keepdims=True)
    e2 = jnp.exp(v2 - v1)
    ridx_ref[...] = jnp.where(expert == 0, i1, jnp.where(expert == 1, i2, 0))
    rgate_ref[...] = jnp.where(expert == 0, 1.0 / (1.0 + e2), jnp.where(expert == 1, e2 / (1.0 + e2), 0.0))


def _l1_mix(h, gmix, win, lng, lnb, ws, bs_t, wout, gffn, wr):
    n = h.shape[0]
    tm = ROW_TILE
    return pl.pallas_call(
        _l1_mix_kernel,
        grid=(n // tm,),
        in_specs=[_rows(tm, D_MODEL)] + [_resident(a.shape) for a in
                                         (gmix, win, lng, lnb, ws, bs_t, wout, gffn, wr)],
        out_specs=[_rows(tm, D_MODEL), _rows(tm, D_MODEL),
                   pl.BlockSpec((F32_SUBLANES, tm), lambda i: (0, i)),
                   pl.BlockSpec((F32_SUBLANES, tm), lambda i: (0, i))],
        out_shape=[jax.ShapeDtypeStruct((n, D_MODEL), F32), jax.ShapeDtypeStruct((n, D_MODEL), F32),
                   jax.ShapeDtypeStruct((F32_SUBLANES, n), jnp.int32),
                   jax.ShapeDtypeStruct((F32_SUBLANES, n), F32)],
        scratch_shapes=[pltpu.VMEM((tm, D_MODEL), BF16), pltpu.VMEM((tm, GM_WIDTH), F32),
                        pltpu.VMEM((tm, GM_WIDTH), F32), pltpu.VMEM((tm, GM_WIDTH), BF16),
                        pltpu.VMEM((tm, GM_WIDTH), BF16)],
        compiler_params=pltpu.CompilerParams(dimension_semantics=("arbitrary",),
                                             vmem_limit_bytes=VMEM_LIMIT),
        name="l1_gmlp_router",
    )(h, gmix, win, lng, lnb, ws, bs_t, wout, gffn, wr)


def _moe_kernel(bexp_ref, nvalid_ref, tok_ref, dst_ref, hn_hbm, w1_ref, w3_ref, w2_ref, y_hbm,
                xbuf, xb_sc, acc_sc, ybuf, gsem, ssem, fence_sem, fence_sink):
    i = pl.program_id(0)
    n_valid = nvalid_ref[0]
    tm = MOE_TILE
    prev = jnp.maximum(i - 1, 0)

    def gather_row(blk, r):
        return pltpu.make_async_copy(hn_hbm.at[pl.ds(tok_ref[blk * tm + r], 1)],
                                     xbuf.at[pl.ds(r, 1)], gsem)

    def scatter_row(blk, r):
        return pltpu.make_async_copy(ybuf.at[pl.ds(r, 1)],
                                     y_hbm.at[pl.ds(dst_ref[blk * tm + r], 1)], ssem)

    def gather_wait():
        pltpu.make_async_copy(hn_hbm.at[pl.ds(0, tm)], xbuf, gsem).wait()

    def scatter_wait():
        pltpu.make_async_copy(ybuf, y_hbm.at[pl.ds(0, tm)], ssem).wait()

    @pl.when(i < n_valid)
    def _():
        @pl.when(i == 0)
        def _():
            def issue(r, carry):
                gather_row(0, r).start()
                return carry
            lax.fori_loop(0, tm, issue, 0)

        gather_wait()
        xb_sc[...] = xbuf[...].astype(BF16)
        has_next = i + 1 < n_valid
        has_prev = i >= 1

        def copies(c):
            if c not in MOE_FENCE_CHUNKS:
                return
            g = MOE_FENCE_CHUNKS.index(c)
            if g > 0:
                fence_sink[0] = pl.semaphore_read(fence_sem)
            if g == len(MOE_FENCE_CHUNKS) - 1:
                return
            last_gather_group = len(MOE_COPY_ROWS) - 3
            for r in range(MOE_COPY_ROWS[g], tm if g == last_gather_group else MOE_COPY_ROWS[g + 1]):
                if g <= last_gather_group:
                    @pl.when(has_next)
                    def _():
                        gather_row(i + 1, r).start(priority=1)

                if r < MOE_COPY_ROWS[g + 1]:
                    @pl.when(has_prev)
                    def _():
                        scatter_row(prev, r).start(priority=r % 2)

        _swiglu_chunks(xb_sc, w1_ref, w3_ref, w2_ref, acc_sc, per_chunk=copies)

        @pl.when(has_prev)
        def _():
            scatter_wait()

        ybuf[...] = acc_sc[...]

    @pl.when(i == n_valid)
    def _():
        def issue(r, carry):
            scatter_row(prev, r).start()
            return carry
        lax.fori_loop(0, tm, issue, 0)
        scatter_wait()
        ybuf[...] = jnp.zeros(ybuf.shape, F32)
        plane = y_hbm.shape[0] // TOP_K
        for b in range(TOP_K):
            spare = pltpu.make_async_copy(ybuf, y_hbm.at[pl.ds((b + 1) * plane - tm, tm)], ssem)
            spare.start()
            spare.wait()


def _moe_experts(block_expert, n_valid, row_token, row_dst, hn2, w1, w3, w2, n_out_rows):
    n_blocks = block_expert.shape[0]
    tm = MOE_TILE
    per_expert = lambda w: pl.BlockSpec((None,) + w.shape[1:], lambda i, be, *_: (be[i], 0, 0))
    return pl.pallas_call(
        _moe_kernel,
        grid_spec=pltpu.PrefetchScalarGridSpec(
            num_scalar_prefetch=4,
            grid=(n_blocks,),
            in_specs=[pl.BlockSpec(memory_space=pl.ANY), per_expert(w1), per_expert(w3),
                      per_expert(w2)],
            out_specs=pl.BlockSpec(memory_space=pl.ANY),
            scratch_shapes=[pltpu.VMEM((tm, D_MODEL), F32), pltpu.VMEM((tm, D_MODEL), BF16),
                            pltpu.VMEM((tm, D_MODEL), F32), pltpu.VMEM((tm, D_MODEL), F32),
                            pltpu.SemaphoreType.DMA(()), pltpu.SemaphoreType.DMA(()),
                            pltpu.SemaphoreType.REGULAR(()), pltpu.SMEM((1,), jnp.int32)],
        ),
        out_shape=jax.ShapeDtypeStruct((n_out_rows, D_MODEL), F32),
        compiler_params=pltpu.CompilerParams(dimension_semantics=("arbitrary",),
                                             vmem_limit_bytes=VMEM_LIMIT),
        name="l1_expert_swiglu",
    )(block_expert, n_valid, row_token, row_dst, hn2, w1, w3, w2)


def _combine_kernel(h_ref, y0_ref, y1_ref, gate_ref, p_ref, gple_ref, wg_ref, wp_ref, gfin_ref,
                    o_ref):
    tm = h_ref.shape[0]
    gate = jnp.concatenate([gate_ref[...], jnp.zeros((LANES - F32_SUBLANES, tm), F32)], axis=0).T
    h = h_ref[...] + (y0_ref[...] * gate[:, 0:1] + y1_ref[...] * gate[:, 1:2])
    h = _ple(h, p_ref, gple_ref, wg_ref, wp_ref)
    o_ref[...] = _rms(h, gfin_ref[...])


def _combine(h1, y_rows, rgate, p1, gple, wg, wp, gfin):
    n = h1.shape[0]
    tm = ROW_TILE
    plane_tiles = y_rows.shape[0] // TOP_K // tm
    return pl.pallas_call(
        _combine_kernel,
        grid=(n // tm,),
        in_specs=[_rows(tm, D_MODEL), _rows(tm, D_MODEL),
                  pl.BlockSpec((tm, D_MODEL), lambda i: (plane_tiles + i, 0)),
                  pl.BlockSpec((F32_SUBLANES, tm), lambda i: (0, i)),
                  _layer_rows(tm, PLE_DIM, 1, n // tm), _resident(gple.shape), _resident(wg.shape),
                  _resident(wp.shape), _resident(gfin.shape)],
        out_specs=_rows(tm, D_MODEL),
        out_shape=jax.ShapeDtypeStruct((n, D_MODEL), F32),
        compiler_params=pltpu.CompilerParams(dimension_semantics=("arbitrary",),
                                             vmem_limit_bytes=VMEM_LIMIT),
        name="l1_combine_ple_norm",
    )(h1, y_rows, y_rows, rgate, p1, gple, wg, wp, gfin)


def _routing_tables(ridx, n_blocks):
    n_tok = ridx.shape[1]
    n_assign = ridx.size
    e_flat = ridx.reshape(-1)
    experts = jnp.arange(N_EXPERTS, dtype=jnp.int32)[:, None]
    counts = jnp.sum((e_flat[None, :] == experts).astype(jnp.int32), axis=1)
    padded = (counts + MOE_TILE - 1) // MOE_TILE * MOE_TILE
    pad_ends = jnp.cumsum(padded)
    n_rows = (n_blocks + 2) * MOE_TILE
    r = jnp.arange(n_rows, dtype=jnp.int32)
    filler = jnp.arange(n_rows - n_assign, dtype=jnp.int32)
    filler_ends = jnp.cumsum(padded - counts)
    filler_expert = jnp.sum((filler[None, :] >= filler_ends[:, None]).astype(jnp.int32), axis=0)
    keys = jnp.concatenate([2 * e_flat, 2 * filler_expert + 1])
    assert n_assign < 0xFFFF
    ids = jnp.concatenate([jnp.arange(n_assign, dtype=jnp.int32),
                           jnp.full((n_rows - n_assign,), 0xFFFF, jnp.int32)])
    assign = jnp.sort(keys * 0x10000 + ids) & 0xFFFF
    assign = jnp.where(assign == 0xFFFF, -1, assign)
    row_token = jnp.maximum(assign, 0) % n_tok
    plane = n_tok + MOE_TILE
    spare = (r // MOE_TILE % 2) * plane + n_tok + r % MOE_TILE
    row_dst = jnp.where(assign >= 0, (assign // n_tok) * plane + assign % n_tok, spare)
    block_start = jnp.arange(n_blocks, dtype=jnp.int32) * MOE_TILE
    block_expert = jnp.minimum(
        jnp.sum((block_start[:, None] >= pad_ends[None, :]).astype(jnp.int32), axis=1),
        N_EXPERTS - 1).astype(jnp.int32)
    n_valid = (pad_ends[-1:] // MOE_TILE).astype(jnp.int32)
    return row_token, row_dst, block_expert, n_valid


def kernel(x, p, positions, ln_mix, ln_ffn, ln_ple, a_w_in, a_lambda, a_subln, a_conv_w, a_w_out,
           ffn_w1, ffn_w3, ffn_w2, c_w_in, c_ln_g, c_ln_b, c_w_s, c_b_s, c_w_out, router_w,
           moe_w1, moe_w3, moe_w2, ple_gate, ple_proj, final_norm):
    bsz, seq, _ = x.shape
    n = bsz * seq
    assert seq % ROW_TILE == 0 and seq % ATTN_TILE == 0 and ROW_TILE % GM_CHUNK == 0
    x2 = x.reshape(n, D_MODEL)
    pos3 = positions.reshape(n // ROW_TILE, 1, ROW_TILE)
    row = lambda a: a.reshape(1, -1)

    freq = (ROPE_THETA ** (-jnp.arange(0, ROT_DIM, 2, dtype=F32) / ROT_DIM)).reshape(-1, 1)
    qt, k, vt, conv = _inproj(x2, pos3, row(ln_mix[0]), freq, a_conv_w[0].T, a_w_in[0].astype(BF16), seq)
    lambda_init = 0.8 - 0.6 * math.exp(-0.3 * 0)
    moe_f32 = [w[0].reshape(-1, w.shape[-1]) for w in (moe_w1, moe_w3, moe_w2)]
    attn, moe_bf16 = _diff_attention(qt, k, vt, a_lambda[0], a_subln[0].reshape(-1, 1), lambda_init,
                                     bsz, seq, moe_f32)
    mw1, mw3, mw2 = (wb.reshape(w.shape[1:]) for wb, w in zip(moe_bf16, (moe_w1, moe_w3, moe_w2)))
    p_all = p.reshape(-1, PLE_DIM)
    h = _l0_post(x2, attn, conv, p_all, a_w_out[0].astype(BF16), row(ln_ffn[0]),
                 ffn_w1[0].astype(BF16), ffn_w3[0].astype(BF16), ffn_w2[0].astype(BF16),
                 row(ln_ple[0]), ple_gate[0].astype(BF16), ple_proj[0].astype(BF16))

    wr = jnp.pad(jnp.concatenate(_split_bf16(router_w[0]), axis=1),
                 ((0, 0), (0, LANES - 2 * N_EXPERTS)))
    h1, hn2, ridx, rgate = _l1_mix(h, row(ln_mix[1]), c_w_in[0].astype(BF16), row(c_ln_g[0]),
                                   row(c_ln_b[0]), c_w_s[0], c_b_s[0].T, c_w_out[0].astype(BF16),
                                   row(ln_ffn[1]), wr)
    n_blocks = -(-(n * TOP_K + N_EXPERTS * (MOE_TILE - 1)) // MOE_TILE)
    row_token, row_dst, block_expert, n_valid = _routing_tables(ridx[:TOP_K], n_blocks)
    y_rows = _moe_experts(block_expert, n_valid, row_token, row_dst, hn2, mw1, mw3, mw2,
                          TOP_K * (n + MOE_TILE))
    out = _combine(h1, y_rows, rgate, p_all, row(ln_ple[1]),
                   ple_gate[1].astype(BF16), ple_proj[1].astype(BF16), row(final_norm))
    return out.reshape(bsz, seq, D_MODEL)
```

```python
import functools
import math

import jax
import jax.numpy as jnp
from jax import lax
from jax.experimental import pallas as pl
from jax.experimental.pallas import tpu as pltpu

F32 = jnp.float32
BF16 = jnp.bfloat16

D_MODEL = 1024
PLE_DIM = 256
DA_HEADS = 4
DA_HEAD_DIM = 64
DA_V_DIM = 2 * DA_HEAD_DIM
DA_WIDTH = DA_HEADS * DA_V_DIM
ROPE_THETA = 500000.0
ROT_DIM = DA_HEAD_DIM // 4
SC_WIDTH = 512
GM_WIDTH = D_MODEL
GM_GROUPS = 8
GM_CHUNK = 128
D_FF = 2816
N_EXPERTS = 8
TOP_K = 2
RMS_EPS = 1e-6
LN_EPS = 1e-5

LANES = 128
F32_SUBLANES = 8
FF_CHUNK = 256
N_FF_CHUNKS = D_FF // FF_CHUNK
ROW_TILE = 512
ATTN_TILE = 256
ATTN_Q_TILE = 512
Q_SCALE = DA_HEAD_DIM ** -0.5 * math.log2(math.e)
VT_ROWS = DA_V_DIM + 16
MOE_TILE = 512
MOE_FENCE_CHUNKS = (0, 8, 10)
MOE_COPY_ROWS = (0, 504, 512)
CONV_HALO = 8
VMEM_LIMIT = 56 * 2**20


def _resident(shape):
    nd = len(shape)
    return pl.BlockSpec(shape, lambda *_: (0,) * nd, pipeline_mode=pl.Buffered(1))


def _rows(tile, width):
    return pl.BlockSpec((tile, width), lambda i, *_: (i, 0))


def _layer_rows(tile, width, layer, tiles_per_layer):
    return pl.BlockSpec((tile, width), lambda i, *_: (layer * tiles_per_layer + i, 0))


def _rms(x, g):
    return x * lax.rsqrt(jnp.mean(x * x, axis=-1, keepdims=True) + RMS_EPS) * g


def _dot(a, b):
    return jnp.dot(a, b, preferred_element_type=F32)


def _swiglu_chunks(hn_ref, w1_ref, w3_ref, w2_ref, acc_ref, per_chunk=None):
    for c in range(N_FF_CHUNKS):
        cols = slice(c * FF_CHUNK, (c + 1) * FF_CHUNK)
        g = _dot(hn_ref[...], w1_ref[:, cols])
        u = _dot(hn_ref[...], w3_ref[:, cols])
        a = (g * jax.nn.sigmoid(g) * u).astype(BF16)
        if c == 0:
            acc_ref[...] = _dot(a, w2_ref[cols, :])
        else:
            acc_ref[...] += _dot(a, w2_ref[cols, :])
        if per_chunk is not None:
            per_chunk(c)


def _ple(h, p_ref, gple_ref, wg_ref, wp_ref):
    gate = jax.nn.sigmoid(_dot(_rms(h, gple_ref[...]).astype(BF16), wg_ref[...]))
    return h + gate * _dot(p_ref[...].astype(BF16), wp_ref[...])


def _inproj_kernel(x_ref, pos_ref, g_ref, freq_ref, cw_ref, w_ref,
                   qt_ref, k_ref, vt_ref, conv_ref, cbuf, *, tiles_per_seq):
    i = pl.program_id(0)
    tm = x_ref.shape[0]
    hn = _rms(x_ref[...], g_ref[...]).astype(BF16)

    half = ROT_DIM // 2
    ang = freq_ref[...] * pos_ref[...].astype(F32)
    cos_t = jnp.cos(ang)
    sin_t = jnp.sin(ang)
    zero_t = jnp.zeros_like(ang)
    rest = DA_HEAD_DIM - ROT_DIM
    pattern = lambda a, b, fill: jnp.concatenate(
        [a, b, jnp.full((rest, tm), fill, F32)] * (LANES // DA_HEAD_DIM), axis=0).T
    cos = pattern(cos_t, cos_t, 1.0)
    sin_up = pattern(-sin_t, zero_t, 0.0)
    sin_dn = pattern(zero_t, sin_t, 0.0)

    def rope(t):
        return (t * cos + pltpu.roll(t, LANES - half, 1) * sin_up
                + pltpu.roll(t, half, 1) * sin_dn)

    zq = _dot(hn, w_ref[:, 0:DA_WIDTH])
    zk = _dot(hn, w_ref[:, DA_WIDTH:2 * DA_WIDTH])
    zv = _dot(hn, w_ref[:, 2 * DA_WIDTH:3 * DA_WIDTH])
    ones = jnp.ones((VT_ROWS - DA_V_DIM, ATTN_TILE), BF16)
    for hd in range(DA_HEADS):
        sl = slice(hd * LANES, (hd + 1) * LANES)
        qt_ref[sl, :] = (rope(zq[:, sl]) * Q_SCALE).T.astype(BF16)
        k_ref[:, sl] = rope(zk[:, sl]).astype(BF16)
        for u in range(tm // ATTN_TILE):
            vt_ref[hd, u, 0:DA_V_DIM, :] = zv[u * ATTN_TILE:(u + 1) * ATTN_TILE, sl].T.astype(BF16)
            vt_ref[hd, u, DA_V_DIM:VT_ROWS, :] = ones

    off = 3 * DA_WIDTH
    b_gate = _dot(hn, w_ref[:, off:off + SC_WIDTH])
    c_gate = _dot(hn, w_ref[:, off + SC_WIDTH:off + 2 * SC_WIDTH])
    hc = _dot(hn, w_ref[:, off + 2 * SC_WIDTH:off + 3 * SC_WIDTH])

    @pl.when(i % tiles_per_seq == 0)
    def _():
        cbuf[0:CONV_HALO, :] = jnp.zeros((CONV_HALO, SC_WIDTH), F32)

    @pl.when(i % tiles_per_seq != 0)
    def _():
        cbuf[0:CONV_HALO, :] = cbuf[tm:tm + CONV_HALO, :]

    ch = c_gate * hc
    cbuf[CONV_HALO:CONV_HALO + tm, :] = ch
    ch1 = cbuf[CONV_HALO - 1:CONV_HALO - 1 + tm, :]
    ch2 = cbuf[CONV_HALO - 2:CONV_HALO - 2 + tm, :]
    conv = b_gate * (cw_ref[2:3, :] * ch + cw_ref[1:2, :] * ch1 + cw_ref[0:1, :] * ch2)
    conv_ref[...] = conv.astype(BF16)


def _inproj(x2, pos3, g, freq, cw, w, seq):
    n = x2.shape[0]
    tm = ROW_TILE
    out = jax.ShapeDtypeStruct((n, DA_WIDTH), BF16)
    sub = tm // ATTN_TILE
    return pl.pallas_call(
        functools.partial(_inproj_kernel, tiles_per_seq=seq // tm),
        grid=(n // tm,),
        in_specs=[_rows(tm, D_MODEL), pl.BlockSpec((None, 1, tm), lambda i: (i, 0, 0)),
                  _resident(g.shape), _resident(freq.shape),
                  _resident(cw.shape), _resident(w.shape)],
        out_specs=[pl.BlockSpec((DA_WIDTH, tm), lambda i: (0, i)), _rows(tm, DA_WIDTH),
                   pl.BlockSpec((DA_HEADS, sub, VT_ROWS, ATTN_TILE), lambda i: (0, i, 0, 0)),
                   _rows(tm, SC_WIDTH)],
        out_shape=[jax.ShapeDtypeStruct((DA_WIDTH, n), BF16), out,
                   jax.ShapeDtypeStruct((DA_HEADS, n // ATTN_TILE, VT_ROWS, ATTN_TILE), BF16), out],
        scratch_shapes=[pltpu.VMEM((tm + CONV_HALO, SC_WIDTH), F32)],
        compiler_params=pltpu.CompilerParams(dimension_semantics=("arbitrary",),
                                             vmem_limit_bytes=VMEM_LIMIT),
        name="l0_inproj",
    )(x2, pos3, g, freq, cw, w)


def _cast_slice(step, n_slices, grid_steps, src_refs, dst_refs, in_bufs, out_bufs, in_sem, out_sem):
    assert 2 <= n_slices <= grid_steps
    slot = step % 2
    other = 1 - slot

    def fetch(s, sl):
        return [pltpu.make_async_copy(src.at[pl.ds(s * buf.shape[1], buf.shape[1])], buf.at[sl],
                                      in_sem.at[sl]) for src, buf in zip(src_refs, in_bufs)]

    def write_back(s, sl):
        return [pltpu.make_async_copy(buf.at[sl], dst.at[pl.ds(s * buf.shape[1], buf.shape[1])],
                                      out_sem.at[sl]) for dst, buf in zip(dst_refs, out_bufs)]

    @pl.when(step == 0)
    def _():
        for c in fetch(0, 0):
            c.start()

    @pl.when(step < n_slices)
    def _():
        for c in fetch(step, slot):
            c.wait()

        @pl.when(step + 1 < n_slices)
        def _():
            for c in fetch(step + 1, other):
                c.start()

        @pl.when(step >= 2)
        def _():
            for c in write_back(step - 2, slot):
                c.wait()

        for src, dst in zip(in_bufs, out_bufs):
            dst[slot] = src[slot].astype(dst.dtype)
        for c in write_back(step, slot):
            c.start()

    @pl.when(step == min(n_slices, grid_steps - 1))
    def _():
        for s in (n_slices - 2, n_slices - 1):
            for c in write_back(s, s % 2):
                c.wait()


def _attn_kernel(lam_ref, sg_ref, qt_ref, k_ref, vt_ref, *rest, lambda_init, cast_groups, n_steps,
                 n_q):
    n_cast = sum(cnt for _, cnt in cast_groups)
    cast_src, rest = rest[:n_cast], rest[n_cast:]
    o_ref, rest = rest[0], rest[1:]
    cast_dst, rest = rest[:n_cast], rest[n_cast:]
    qc_sc, m_sc, alpha_sc, acc_sc, s_sc, p_sc = rest[:6]
    cast_in, cast_out = rest[6:6 + n_cast], rest[6 + n_cast:6 + 2 * n_cast]
    sems = rest[6 + 2 * n_cast:]
    step = ((pl.program_id(0) * pl.num_programs(1) + pl.program_id(1)) * pl.num_programs(2)
            + pl.program_id(2))
    first = 0
    for g, (n_slices, cnt) in enumerate(cast_groups):
        grp = slice(first, first + cnt)
        _cast_slice(step, n_slices, n_steps, cast_src[grp], cast_dst[grp], cast_in[grp],
                    cast_out[grp], sems[2 * g], sems[2 * g + 1])
        first += cnt

    qi = pl.program_id(2)
    tq = ATTN_Q_TILE
    tk = ATTN_TILE
    n_sub = tq // tk

    qt = qt_ref[...]
    dim = lax.broadcasted_iota(jnp.int32, (LANES, 1), 0)
    zero = jnp.zeros_like(qt)
    qc_sc[0] = jnp.where(dim < DA_HEAD_DIM, qt, zero)
    qc_sc[1] = jnp.where(dim >= DA_HEAD_DIM, qt, zero)
    m_sc[...] = jnp.full(m_sc.shape, -jnp.inf, F32)
    acc_sc[...] = jnp.zeros(acc_sc.shape, F32)

    def scores(j, diagonal):
        for c in range(2):
            for u in range(n_sub):
                ks = (j * n_sub + u) * tk
                su = _dot(k_ref[ks:ks + tk, :], qc_sc[c])
                if diagonal:
                    key = lax.broadcasted_iota(jnp.int32, (tk, tq), 0) + u * tk
                    qry = lax.broadcasted_iota(jnp.int32, (tk, tq), 1)
                    su = jnp.where(key <= qry, su, -jnp.inf)
                s_sc[c, u * tk:(u + 1) * tk, :] = su

    def softmax():
        for c in range(2):
            s = s_sc[c]
            m_prev = m_sc[c]
            m_new = jnp.maximum(m_prev, jnp.max(s, axis=0, keepdims=True))
            alpha_sc[c] = jnp.exp2(m_prev - m_new)
            m_sc[c] = m_new
            p_sc[c] = jnp.exp2(s - m_new).astype(BF16)

    def fold(j):
        for c in range(2):
            pv = None
            for u in range(n_sub):
                d = _dot(vt_ref[j * n_sub + u], p_sc[c, u * tk:(u + 1) * tk, :])
                pv = d if pv is None else pv + d
            acc_sc[c] = alpha_sc[c] * acc_sc[c] + pv

    def run(last):
        scores(0, diagonal=last == 0)
        for t in range(last + 1):
            if t >= 1:
                fold(t - 1)
            softmax()
            if t < last:
                scores(t + 1, diagonal=t + 1 == last)
        fold(last)

    for last in range(n_q):
        pl.when(qi == last)(functools.partial(run, last))

    lp = lam_ref[...]
    lam = (jnp.exp(jnp.sum(lp[0:1] * lp[1:2], axis=-1, keepdims=True))
           - jnp.exp(jnp.sum(lp[2:3] * lp[3:4], axis=-1, keepdims=True)) + lambda_init)
    a0 = acc_sc[0]
    a1 = acc_sc[1]
    o = (a0[0:DA_V_DIM] / a0[DA_V_DIM:DA_V_DIM + 1]
         - lam * (a1[0:DA_V_DIM] / a1[DA_V_DIM:DA_V_DIM + 1]))
    o = o * lax.rsqrt(jnp.mean(o * o, axis=0, keepdims=True) + RMS_EPS) * sg_ref[...]
    o_ref[...] = (o * (1.0 - lambda_init)).T.astype(o_ref.dtype)


def _diff_attention(qt, k, vt, lam_p, subln_col, lambda_init, bsz, seq, cast_f32):
    tq = ATTN_Q_TILE
    tk = ATTN_TILE
    nq = seq // tq
    nk = seq // tk
    n_steps = bsz * DA_HEADS * nq
    def n_slices(w):
        return max(c for c in range(1, n_steps + 1)
                   if w.shape[0] % c == 0 and (w.shape[0] // c) % 16 == 0)
    order = sorted(range(len(cast_f32)), key=lambda a: n_slices(cast_f32[a]))
    cast_sorted = [cast_f32[a] for a in order]
    counts = [n_slices(w) for w in cast_sorted]
    cast_groups = tuple((c, counts.count(c)) for c in sorted(set(counts)))
    n_cast = len(cast_f32)
    slices = [(w.shape[0] // c, w.shape[1]) for w, c in zip(cast_sorted, counts)]
    any_spec = pl.BlockSpec(memory_space=pl.ANY)
    out = pl.pallas_call(
        functools.partial(_attn_kernel, lambda_init=lambda_init, cast_groups=cast_groups,
                          n_steps=n_steps, n_q=nq),
        grid=(bsz, DA_HEADS, nq),
        in_specs=[pl.BlockSpec(lam_p.shape, lambda b, h, i: (0, 0)),
                  pl.BlockSpec(subln_col.shape, lambda b, h, i: (0, 0)),
                  pl.BlockSpec((LANES, tq), lambda b, h, i: (h, b * nq + i)),
                  pl.BlockSpec((seq, LANES), lambda b, h, i: (b, h)),
                  pl.BlockSpec((None, nk, VT_ROWS, tk), lambda b, h, i: (h, b, 0, 0))]
                 + [any_spec] * n_cast,
        out_specs=[pl.BlockSpec((tq, LANES), lambda b, h, i: (b * nq + i, h))] + [any_spec] * n_cast,
        out_shape=[jax.ShapeDtypeStruct(k.shape, BF16)]
                  + [jax.ShapeDtypeStruct(w.shape, BF16) for w in cast_sorted],
        scratch_shapes=[pltpu.VMEM((2, LANES, tq), BF16), pltpu.VMEM((2, 1, tq), F32),
                        pltpu.VMEM((2, 1, tq), F32), pltpu.VMEM((2, VT_ROWS, tq), F32),
                        pltpu.VMEM((2, tq, tq), F32), pltpu.VMEM((2, tq, tq), BF16)]
                       + [pltpu.VMEM((2,) + s, F32) for s in slices]
                       + [pltpu.VMEM((2,) + s, BF16) for s in slices]
                       + [pltpu.SemaphoreType.DMA((2,))] * (2 * len(cast_groups)),
        compiler_params=pltpu.CompilerParams(
            dimension_semantics=("arbitrary", "arbitrary", "arbitrary"),
            vmem_limit_bytes=VMEM_LIMIT),
        name="l0_diff_attention",
    )(lam_p, subln_col, qt, k, vt, *cast_sorted)
    cast_bf16 = [None] * n_cast
    for pos, a in enumerate(order):
        cast_bf16[a] = out[1 + pos]
    return out[0], cast_bf16


def _l0_post_kernel(x_ref, attn_ref, conv_ref, p_ref, wo_ref, gffn_ref, w1_ref, w3_ref, w2_ref,
                    gple_ref, wg_ref, wp_ref, o_ref, acc_sc, hn_sc, h_sc):
    h = (x_ref[...] + _dot(attn_ref[...], wo_ref[0:DA_WIDTH, :])
         + _dot(conv_ref[...], wo_ref[DA_WIDTH:DA_WIDTH + SC_WIDTH, :]))
    hn_sc[...] = _rms(h, gffn_ref[...]).astype(BF16)
    h_sc[...] = h
    _swiglu_chunks(hn_sc, w1_ref, w3_ref, w2_ref, acc_sc)
    o_ref[...] = _ple(h_sc[...] + acc_sc[...], p_ref, gple_ref, wg_ref, wp_ref)


def _l0_post(x2, attn, conv, p0, wo, gffn, w1, w3, w2, gple, wg, wp):
    n = x2.shape[0]
    tm = ROW_TILE
    return pl.pallas_call(
        _l0_post_kernel,
        grid=(n // tm,),
        in_specs=[_rows(tm, D_MODEL), _rows(tm, DA_WIDTH), _rows(tm, SC_WIDTH),
                  _layer_rows(tm, PLE_DIM, 0, n // tm),
                  _resident(wo.shape), _resident(gffn.shape), _resident(w1.shape),
                  _resident(w3.shape), _resident(w2.shape), _resident(gple.shape),
                  _resident(wg.shape), _resident(wp.shape)],
        out_specs=_rows(tm, D_MODEL),
        out_shape=jax.ShapeDtypeStruct((n, D_MODEL), F32),
        scratch_shapes=[pltpu.VMEM((tm, D_MODEL), F32), pltpu.VMEM((tm, D_MODEL), BF16),
                        pltpu.VMEM((tm, D_MODEL), F32)],
        compiler_params=pltpu.CompilerParams(dimension_semantics=("arbitrary",),
                                             vmem_limit_bytes=VMEM_LIMIT),
        name="l0_outproj_swiglu_ple",
    )(x2, attn, conv, p0, wo, gffn, w1, w3, w2, gple, wg, wp)


def _split_bf16(a):
    hi = a.astype(BF16)
    return hi, (a - hi.astype(F32)).astype(BF16)


def _l1_mix_kernel(h_ref, gmix_ref, win_ref, lng_ref, lnb_ref, ws_ref, bs_ref, wout_ref,
                   gffn_ref, wr_ref,
                   h1_ref, hn2_ref, ridx_ref, rgate_ref, hn_sc, u_sc, v_sc, vn_sc, gated_sc):
    tm = h_ref.shape[0]
    hn_sc[...] = _rms(h_ref[...], gmix_ref[...]).astype(BF16)

    def gelu(z):
        return 0.5 * z * (1.0 + lax.erf(z * (2.0 ** -0.5)))

    for dst, base in ((v_sc, GM_WIDTH), (u_sc, 0)):
        for c in range(GM_WIDTH // FF_CHUNK):
            cols = slice(c * FF_CHUNK, (c + 1) * FF_CHUNK)
            dst[:, cols] = gelu(_dot(hn_sc[...], win_ref[:, base + c * FF_CHUNK:base + (c + 1) * FF_CHUNK]))

    vv = v_sc[...]
    mu = jnp.mean(vv, axis=-1, keepdims=True)
    vc = vv - mu
    var = jnp.mean(vc * vc, axis=-1, keepdims=True)
    vn_sc[...] = (vc * lax.rsqrt(var + LN_EPS) * lng_ref[...] + lnb_ref[...]).astype(BF16)

    n_chunks = tm // GM_CHUNK
    row = lax.broadcasted_iota(jnp.int32, (GM_CHUNK, GM_CHUNK), 0)
    col = lax.broadcasted_iota(jnp.int32, (GM_CHUNK, GM_CHUNK), 1)
    for g in range(GM_GROUPS):
        gs = slice(g * LANES, (g + 1) * LANES)
        wm = jnp.where(col <= row, ws_ref[g], 0.0).astype(BF16)
        rhs = jnp.concatenate(
            [vn_sc[r * GM_CHUNK:(r + 1) * GM_CHUNK, gs] for r in range(n_chunks)], axis=1)
        vs = _dot(wm, rhs) + bs_ref[:, g:g + 1]
        for r in range(n_chunks):
            rs = slice(r * GM_CHUNK, (r + 1) * GM_CHUNK)
            gated_sc[rs, gs] = (u_sc[rs, gs] * vs[:, r * LANES:(r + 1) * LANES]).astype(BF16)

    h1 = h_ref[...] + _dot(gated_sc[...], wout_ref[...])
    h1_ref[...] = h1

    hn2 = _rms(h1, gffn_ref[...])
    hn2_ref[...] = hn2
    x_hi, x_lo = _split_bf16(hn2)
    a_hi = _dot(x_hi, wr_ref[...])
    logits = a_hi + pltpu.roll(a_hi, LANES - N_EXPERTS, 1) + _dot(x_lo, wr_ref[...])
    assert N_EXPERTS == F32_SUBLANES
    lt = logits.T[0:N_EXPERTS, :]
    expert = lax.broadcasted_iota(jnp.int32, lt.shape, 0)
    v1 = jnp.max(lt, axis=0, keepdims=True)
    i1 = jnp.min(jnp.where(lt == v1, expert, N_EXPERTS), axis=0, keepdims=True)
    rest = jnp.where(expert == i1, -jnp.inf, lt)
    v2 = jnp.max(rest, axis=0, keepdims=True)
    i2 = jnp.min(jnp.where(rest == v2, expert, N_EXPERTS), axis=0, keepdims=True)
    e2 = jnp.exp(v2 - v1)
    ridx_ref[...] = jnp.where(expert == 0, i1, jnp.where(expert == 1, i2, 0))
    rgate_ref[...] = jnp.where(expert == 0, 1.0 / (1.0 + e2), jnp.where(expert == 1, e2 / (1.0 + e2), 0.0))


def _l1_mix(h, gmix, win, lng, lnb, ws, bs_t, wout, gffn, wr):
    n = h.shape[0]
    tm = ROW_TILE
    return pl.pallas_call(
        _l1_mix_kernel,
        grid=(n // tm,),
        in_specs=[_rows(tm, D_MODEL)] + [_resident(a.shape) for a in
                                         (gmix, win, lng, lnb, ws, bs_t, wout, gffn, wr)],
        out_specs=[_rows(tm, D_MODEL), _rows(tm, D_MODEL),
                   pl.BlockSpec((F32_SUBLANES, tm), lambda i: (0, i)),
                   pl.BlockSpec((F32_SUBLANES, tm), lambda i: (0, i))],
        out_shape=[jax.ShapeDtypeStruct((n, D_MODEL), F32), jax.ShapeDtypeStruct((n, D_MODEL), F32),
                   jax.ShapeDtypeStruct((F32_SUBLANES, n), jnp.int32),
                   jax.ShapeDtypeStruct((F32_SUBLANES, n), F32)],
        scratch_shapes=[pltpu.VMEM((tm, D_MODEL), BF16), pltpu.VMEM((tm, GM_WIDTH), F32),
                        pltpu.VMEM((tm, GM_WIDTH), F32), pltpu.VMEM((tm, GM_WIDTH), BF16),
                        pltpu.VMEM((tm, GM_WIDTH), BF16)],
        compiler_params=pltpu.CompilerParams(dimension_semantics=("arbitrary",),
                                             vmem_limit_bytes=VMEM_LIMIT),
        name="l1_gmlp_router",
    )(h, gmix, win, lng, lnb, ws, bs_t, wout, gffn, wr)


def _moe_kernel(bexp_ref, nvalid_ref, tok_ref, dst_ref, hn_hbm, w1_ref, w3_ref, w2_ref, y_hbm,
                xbuf, xb_sc, acc_sc, ybuf, gsem, ssem, fence_sem, fence_sink):
    i = pl.program_id(0)
    n_valid = nvalid_ref[0]
    tm = MOE_TILE
    prev = jnp.maximum(i - 1, 0)

    def gather_row(blk, r):
        return pltpu.make_async_copy(hn_hbm.at[pl.ds(tok_ref[blk * tm + r], 1)],
                                     xbuf.at[pl.ds(r, 1)], gsem)

    def scatter_row(blk, r):
        return pltpu.make_async_copy(ybuf.at[pl.ds(r, 1)],
                                     y_hbm.at[pl.ds(dst_ref[blk * tm + r], 1)], ssem)

    def gather_wait():
        pltpu.make_async_copy(hn_hbm.at[pl.ds(0, tm)], xbuf, gsem).wait()

    def scatter_wait():
        pltpu.make_async_copy(ybuf, y_hbm.at[pl.ds(0, tm)], ssem).wait()

    @pl.when(i < n_valid)
    def _():
        @pl.when(i == 0)
        def _():
            def issue(r, carry):
                gather_row(0, r).start()
                return carry
            lax.fori_loop(0, tm, issue, 0)

        gather_wait()
        xb_sc[...] = xbuf[...].astype(BF16)
        has_next = i + 1 < n_valid
        has_prev = i >= 1

        def copies(c):
            if c not in MOE_FENCE_CHUNKS:
                return
            g = MOE_FENCE_CHUNKS.index(c)
            if g > 0:
                fence_sink[0] = pl.semaphore_read(fence_sem)
            if g == len(MOE_FENCE_CHUNKS) - 1:
                return
            last_gather_group = len(MOE_COPY_ROWS) - 3
            for r in range(MOE_COPY_ROWS[g], tm if g == last_gather_group else MOE_COPY_ROWS[g + 1]):
                if g <= last_gather_group:
                    @pl.when(has_next)
                    def _():
                        gather_row(i + 1, r).start(priority=1)

                if r < MOE_COPY_ROWS[g + 1]:
                    @pl.when(has_prev)
                    def _():
                        scatter_row(prev, r).start(priority=r % 2)

        _swiglu_chunks(xb_sc, w1_ref, w3_ref, w2_ref, acc_sc, per_chunk=copies)

        @pl.when(has_prev)
        def _():
            scatter_wait()

        ybuf[...] = acc_sc[...]

    @pl.when(i == n_valid)
    def _():
        def issue(r, carry):
            scatter_row(prev, r).start()
            return carry
        lax.fori_loop(0, tm, issue, 0)
        scatter_wait()
        ybuf[...] = jnp.zeros(ybuf.shape, F32)
        plane = y_hbm.shape[0] // TOP_K
        for b in range(TOP_K):
            spare = pltpu.make_async_copy(ybuf, y_hbm.at[pl.ds((b + 1) * plane - tm, tm)], ssem)
            spare.start()
            spare.wait()


def _moe_experts(block_expert, n_valid, row_token, row_dst, hn2, w1, w3, w2, n_out_rows):
    n_blocks = block_expert.shape[0]
    tm = MOE_TILE
    per_expert = lambda w: pl.BlockSpec((None,) + w.shape[1:], lambda i, be, *_: (be[i], 0, 0))
    return pl.pallas_call(
        _moe_kernel,
        grid_spec=pltpu.PrefetchScalarGridSpec(
            num_scalar_prefetch=4,
            grid=(n_blocks,),
            in_specs=[pl.BlockSpec(memory_space=pl.ANY), per_expert(w1), per_expert(w3),
                      per_expert(w2)],
            out_specs=pl.BlockSpec(memory_space=pl.ANY),
            scratch_shapes=[pltpu.VMEM((tm, D_MODEL), F32), pltpu.VMEM((tm, D_MODEL), BF16),
                            pltpu.VMEM((tm, D_MODEL), F32), pltpu.VMEM((tm, D_MODEL), F32),
                            pltpu.SemaphoreType.DMA(()), pltpu.SemaphoreType.DMA(()),
                            pltpu.SemaphoreType.REGULAR(()), pltpu.SMEM((1,), jnp.int32)],
        ),
        out_shape=jax.ShapeDtypeStruct((n_out_rows, D_MODEL), F32),
        compiler_params=pltpu.CompilerParams(dimension_semantics=("arbitrary",),
                                             vmem_limit_bytes=VMEM_LIMIT),
        name="l1_expert_swiglu",
    )(block_expert, n_valid, row_token, row_dst, hn2, w1, w3, w2)


def _combine_kernel(h_ref, y0_ref, y1_ref, gate_ref, p_ref, gple_ref, wg_ref, wp_ref, gfin_ref,
                    o_ref):
    tm = h_ref.shape[0]
    gate = jnp.concatenate([gate_ref[...], jnp.zeros((LANES - F32_SUBLANES, tm), F32)], axis=0).T
    h = h_ref[...] + (y0_ref[...] * gate[:, 0:1] + y1_ref[...] * gate[:, 1:2])
    h = _ple(h, p_ref, gple_ref, wg_ref, wp_ref)
    o_ref[...] = _rms(h, gfin_ref[...])


def _combine(h1, y_rows, rgate, p1, gple, wg, wp, gfin):
    n = h1.shape[0]
    tm = ROW_TILE
    plane_tiles = y_rows.shape[0] // TOP_K // tm
    return pl.pallas_call(
        _combine_kernel,
        grid=(n // tm,),
        in_specs=[_rows(tm, D_MODEL), _rows(tm, D_MODEL),
                  pl.BlockSpec((tm, D_MODEL), lambda i: (plane_tiles + i, 0)),
                  pl.BlockSpec((F32_SUBLANES, tm), lambda i: (0, i)),
                  _layer_rows(tm, PLE_DIM, 1, n // tm), _resident(gple.shape), _resident(wg.shape),
                  _resident(wp.shape), _resident(gfin.shape)],
        out_specs=_rows(tm, D_MODEL),
        out_shape=jax.ShapeDtypeStruct((n, D_MODEL), F32),
        compiler_params=pltpu.CompilerParams(dimension_semantics=("arbitrary",),
                                             vmem_limit_bytes=VMEM_LIMIT),
        name="l1_combine_ple_norm",
    )(h1, y_rows, y_rows, rgate, p1, gple, wg, wp, gfin)


def _routing_tables(ridx, n_blocks):
    n_tok = ridx.shape[1]
    n_assign = ridx.size
    e_flat = ridx.reshape(-1)
    experts = jnp.arange(N_EXPERTS, dtype=jnp.int32)[:, None]
    counts = jnp.sum((e_flat[None, :] == experts).astype(jnp.int32), axis=1)
    padded = (counts + MOE_TILE - 1) // MOE_TILE * MOE_TILE
    pad_ends = jnp.cumsum(padded)
    n_rows = (n_blocks + 2) * MOE_TILE
    r = jnp.arange(n_rows, dtype=jnp.int32)
    filler = jnp.arange(n_rows - n_assign, dtype=jnp.int32)
    filler_ends = jnp.cumsum(padded - counts)
    filler_expert = jnp.sum((filler[None, :] >= filler_ends[:, None]).astype(jnp.int32), axis=0)
    keys = jnp.concatenate([2 * e_flat, 2 * filler_expert + 1])
    assert n_assign < 0xFFFF
    ids = jnp.concatenate([jnp.arange(n_assign, dtype=jnp.int32),
                           jnp.full((n_rows - n_assign,), 0xFFFF, jnp.int32)])
    assign = jnp.sort(keys * 0x10000 + ids) & 0xFFFF
    assign = jnp.where(assign == 0xFFFF, -1, assign)
    row_token = jnp.maximum(assign, 0) % n_tok
    plane = n_tok + MOE_TILE
    spare = (r // MOE_TILE % 2) * plane + n_tok + r % MOE_TILE
    row_dst = jnp.where(assign >= 0, (assign // n_tok) * plane + assign % n_tok, spare)
    block_start = jnp.arange(n_blocks, dtype=jnp.int32) * MOE_TILE
    block_expert = jnp.minimum(
        jnp.sum((block_start[:, None] >= pad_ends[None, :]).astype(jnp.int32), axis=1),
        N_EXPERTS - 1).astype(jnp.int32)
    n_valid = (pad_ends[-1:] // MOE_TILE).astype(jnp.int32)
    return row_token, row_dst, block_expert, n_valid


def kernel(x, p, positions, ln_mix, ln_ffn, ln_ple, a_w_in, a_lambda, a_subln, a_conv_w, a_w_out,
           ffn_w1, ffn_w3, ffn_w2, c_w_in, c_ln_g, c_ln_b, c_w_s, c_b_s, c_w_out, router_w,
           moe_w1, moe_w3, moe_w2, ple_gate, ple_proj, final_norm):
    bsz, seq, _ = x.shape
    n = bsz * seq
    assert seq % ROW_TILE == 0 and seq % ATTN_TILE == 0 and ROW_TILE % GM_CHUNK == 0
    x2 = x.reshape(n, D_MODEL)
    pos3 = positions.reshape(n // ROW_TILE, 1, ROW_TILE)
    row = lambda a: a.reshape(1, -1)

    freq = (ROPE_THETA ** (-jnp.arange(0, ROT_DIM, 2, dtype=F32) / ROT_DIM)).reshape(-1, 1)
    qt, k, vt, conv = _inproj(x2, pos3, row(ln_mix[0]), freq, a_conv_w[0].T, a_w_in[0].astype(BF16), seq)
    lambda_init = 0.8 - 0.6 * math.exp(-0.3 * 0)
    moe_f32 = [w[0].reshape(-1, w.shape[-1]) for w in (moe_w1, moe_w3, moe_w2)]
    attn, moe_bf16 = _diff_attention(qt, k, vt, a_lambda[0], a_subln[0].reshape(-1, 1), lambda_init,
                                     bsz, seq, moe_f32)
    mw1, mw3, mw2 = (wb.reshape(w.shape[1:]) for wb, w in zip(moe_bf16, (moe_w1, moe_w3, moe_w2)))
    p_all = p.reshape(-1, PLE_DIM)
    h = _l0_post(x2, attn, conv, p_all, a_w_out[0].astype(BF16), row(ln_ffn[0]),
                 ffn_w1[0].astype(BF16), ffn_w3[0].astype(BF16), ffn_w2[0].astype(BF16),
                 row(ln_ple[0]), ple_gate[0].astype(BF16), ple_proj[0].astype(BF16))

    wr = jnp.pad(jnp.concatenate(_split_bf16(router_w[0]), axis=1),
                 ((0, 0), (0, LANES - 2 * N_EXPERTS)))
    h1, hn2, ridx, rgate = _l1_mix(h, row(ln_mix[1]), c_w_in[0].astype(BF16), row(c_ln_g[0]),
                                   row(c_ln_b[0]), c_w_s[0], c_b_s[0].T, c_w_out[0].astype(BF16),
                                   row(ln_ffn[1]), wr)
    n_blocks = -(-(n * TOP_K + N_EXPERTS * (MOE_TILE - 1)) // MOE_TILE)
    row_token, row_dst, block_expert, n_valid = _routing_tables(ridx[:TOP_K], n_blocks)
    y_rows = _moe_experts(block_expert, n_valid, row_token, row_dst, hn2, mw1, mw3, mw2,
                          TOP_K * (n + MOE_TILE))
    out = _combine(h1, y_rows, rgate, p_all, row(ln_ple[1]),
                   ple_gate[1].astype(BF16), ple_proj[1].astype(BF16), row(final_norm))
    return out.reshape(bsz, seq, D_MODEL)
```

```python
import functools
import math

import jax
import jax.numpy as jnp
from jax import lax
from jax.experimental import pallas as pl
from jax.experimental.pallas import tpu as pltpu

F32 = jnp.float32
BF16 = jnp.bfloat16

D_MODEL = 1024
PLE_DIM = 256
DA_HEADS = 4
DA_HEAD_DIM = 64
DA_V_DIM = 2 * DA_HEAD_DIM
DA_WIDTH = DA_HEADS * DA_V_DIM
ROPE_THETA = 500000.0
ROT_DIM = DA_HEAD_DIM // 4
SC_WIDTH = 512
GM_WIDTH = D_MODEL
GM_GROUPS = 8
GM_CHUNK = 128
D_FF = 2816
N_EXPERTS = 8
TOP_K = 2
RMS_EPS = 1e-6
LN_EPS = 1e-5

LANES = 128
F32_SUBLANES = 8
FF_CHUNK = 256
N_FF_CHUNKS = D_FF // FF_CHUNK
ROW_TILE = 512
ATTN_TILE = 256
ATTN_Q_TILE = 512
Q_SCALE = DA_HEAD_DIM ** -0.5 * math.log2(math.e)
VT_ROWS = DA_V_DIM + 16
MOE_TILE = 512
MOE_FENCE_CHUNKS = (0, 8, 10)
MOE_COPY_ROWS = (0, 504, 512)
CONV_HALO = 8
VMEM_LIMIT = 56 * 2**20


def _resident(shape):
    nd = len(shape)
    return pl.BlockSpec(shape, lambda *_: (0,) * nd, pipeline_mode=pl.Buffered(1))


def _rows(tile, width):
    return pl.BlockSpec((tile, width), lambda i, *_: (i, 0))


def _layer_rows(tile, width, layer, tiles_per_layer):
    return pl.BlockSpec((tile, width), lambda i, *_: (layer * tiles_per_layer + i, 0))


def _rms(x, g):
    return x * lax.rsqrt(jnp.mean(x * x, axis=-1, keepdims=True) + RMS_EPS) * g


def _dot(a, b):
    return jnp.dot(a, b, preferred_element_type=F32)


def _swiglu_chunks(hn_ref, w1_ref, w3_ref, w2_ref, acc_ref, per_chunk=None):
    for c in range(N_FF_CHUNKS):
        cols = slice(c * FF_CHUNK, (c + 1) * FF_CHUNK)
        g = _dot(hn_ref[...], w1_ref[:, cols])
        u = _dot(hn_ref[...], w3_ref[:, cols])
        a = (g * jax.nn.sigmoid(g) * u).astype(BF16)
        if c == 0:
            acc_ref[...] = _dot(a, w2_ref[cols, :])
        else:
            acc_ref[...] += _dot(a, w2_ref[cols, :])
        if per_chunk is not None:
            per_chunk(c)


def _ple(h, p_ref, gple_ref, wg_ref, wp_ref):
    gate = jax.nn.sigmoid(_dot(_rms(h, gple_ref[...]).astype(BF16), wg_ref[...]))
    return h + gate * _dot(p_ref[...].astype(BF16), wp_ref[...])


def _inproj_kernel(x_ref, pos_ref, g_ref, freq_ref, cw_ref, w_ref,
                   qt_ref, k_ref, vt_ref, conv_ref, cbuf, *, tiles_per_seq):
    i = pl.program_id(0)
    tm = x_ref.shape[0]
    hn = _rms(x_ref[...], g_ref[...]).astype(BF16)

    half = ROT_DIM // 2
    ang = freq_ref[...] * pos_ref[...].astype(F32)
    cos_t = jnp.cos(ang)
    sin_t = jnp.sin(ang)
    zero_t = jnp.zeros_like(ang)
    rest = DA_HEAD_DIM - ROT_DIM
    pattern = lambda a, b, fill: jnp.concatenate(
        [a, b, jnp.full((rest, tm), fill, F32)] * (LANES // DA_HEAD_DIM), axis=0).T
    cos = pattern(cos_t, cos_t, 1.0)
    sin_up = pattern(-sin_t, zero_t, 0.0)
    sin_dn = pattern(zero_t, sin_t, 0.0)

    def rope(t):
        return (t * cos + pltpu.roll(t, LANES - half, 1) * sin_up
                + pltpu.roll(t, half, 1) * sin_dn)

    zq = _dot(hn, w_ref[:, 0:DA_WIDTH])
    zk = _dot(hn, w_ref[:, DA_WIDTH:2 * DA_WIDTH])
    zv = _dot(hn, w_ref[:, 2 * DA_WIDTH:3 * DA_WIDTH])
    ones = jnp.ones((VT_ROWS - DA_V_DIM, ATTN_TILE), BF16)
    for hd in range(DA_HEADS):
        sl = slice(hd * LANES, (hd + 1) * LANES)
        qt_ref[sl, :] = (rope(zq[:, sl]) * Q_SCALE).T.astype(BF16)
        k_ref[:, sl] = rope(zk[:, sl]).astype(BF16)
        for u in range(tm // ATTN_TILE):
            vt_ref[hd, u, 0:DA_V_DIM, :] = zv[u * ATTN_TILE:(u + 1) * ATTN_TILE, sl].T.astype(BF16)
            vt_ref[hd, u, DA_V_DIM:VT_ROWS, :] = ones

    off = 3 * DA_WIDTH
    b_gate = _dot(hn, w_ref[:, off:off + SC_WIDTH])
    c_gate = _dot(hn, w_ref[:, off + SC_WIDTH:off + 2 * SC_WIDTH])
    hc = _dot(hn, w_ref[:, off + 2 * SC_WIDTH:off + 3 * SC_WIDTH])

    @pl.when(i % tiles_per_seq == 0)
    def _():
        cbuf[0:CONV_HALO, :] = jnp.zeros((CONV_HALO, SC_WIDTH), F32)

    @pl.when(i % tiles_per_seq != 0)
    def _():
        cbuf[0:CONV_HALO, :] = cbuf[tm:tm + CONV_HALO, :]

    ch = c_gate * hc
    cbuf[CONV_HALO:CONV_HALO + tm, :] = ch
    ch1 = cbuf[CONV_HALO - 1:CONV_HALO - 1 + tm, :]
    ch2 = cbuf[CONV_HALO - 2:CONV_HALO - 2 + tm, :]
    conv = b_gate * (cw_ref[2:3, :] * ch + cw_ref[1:2, :] * ch1 + cw_ref[0:1, :] * ch2)
    conv_ref[...] = conv.astype(BF16)


def _inproj(x2, pos3, g, freq, cw, w, seq):
    n = x2.shape[0]
    tm = ROW_TILE
    out = jax.ShapeDtypeStruct((n, DA_WIDTH), BF16)
    sub = tm // ATTN_TILE
    return pl.pallas_call(
        functools.partial(_inproj_kernel, tiles_per_seq=seq // tm),
        grid=(n // tm,),
        in_specs=[_rows(tm, D_MODEL), pl.BlockSpec((None, 1, tm), lambda i: (i, 0, 0)),
                  _resident(g.shape), _resident(freq.shape),
                  _resident(cw.shape), _resident(w.shape)],
        out_specs=[pl.BlockSpec((DA_WIDTH, tm), lambda i: (0, i)), _rows(tm, DA_WIDTH),
                   pl.BlockSpec((DA_HEADS, sub, VT_ROWS, ATTN_TILE), lambda i: (0, i, 0, 0)),
                   _rows(tm, SC_WIDTH)],
        out_shape=[jax.ShapeDtypeStruct((DA_WIDTH, n), BF16), out,
                   jax.ShapeDtypeStruct((DA_HEADS, n // ATTN_TILE, VT_ROWS, ATTN_TILE), BF16), out],
        scratch_shapes=[pltpu.VMEM((tm + CONV_HALO, SC_WIDTH), F32)],
        compiler_params=pltpu.CompilerParams(dimension_semantics=("arbitrary",),
                                             vmem_limit_bytes=VMEM_LIMIT),
        name="l0_inproj",
    )(x2, pos3, g, freq, cw, w)


def _cast_slice(step, n_slices, grid_steps, src_refs, dst_refs, in_bufs, out_bufs, in_sem, out_sem):
    assert 2 <= n_slices <= grid_steps
    slot = step % 2
    other = 1 - slot

    def fetch(s, sl):
        return [pltpu.make_async_copy(src.at[pl.ds(s * buf.shape[1], buf.shape[1])], buf.at[sl],
                                      in_sem.at[sl]) for src, buf in zip(src_refs, in_bufs)]

    def write_back(s, sl):
        return [pltpu.make_async_copy(buf.at[sl], dst.at[pl.ds(s * buf.shape[1], buf.shape[1])],
                                      out_sem.at[sl]) for dst, buf in zip(dst_refs, out_bufs)]

    @pl.when(step == 0)
    def _():
        for c in fetch(0, 0):
            c.start()

    @pl.when(step < n_slices)
    def _():
        for c in fetch(step, slot):
            c.wait()

        @pl.when(step + 1 < n_slices)
        def _():
            for c in fetch(step + 1, other):
                c.start()

        @pl.when(step >= 2)
        def _():
            for c in write_back(step - 2, slot):
                c.wait()

        for src, dst in zip(in_bufs, out_bufs):
            dst[slot] = src[slot].astype(dst.dtype)
        for c in write_back(step, slot):
            c.start()

    @pl.when(step == min(n_slices, grid_steps - 1))
    def _():
        for s in (n_slices - 2, n_slices - 1):
            for c in write_back(s, s % 2):
                c.wait()


def _attn_kernel(lam_ref, sg_ref, qt_ref, k_ref, vt_ref, *rest, lambda_init, cast_groups, n_steps,
                 n_q):
    n_cast = sum(cnt for _, cnt in cast_groups)
    cast_src, rest = rest[:n_cast], rest[n_cast:]
    o_ref, rest = rest[0], rest[1:]
    cast_dst, rest = rest[:n_cast], rest[n_cast:]
    qc_sc, m_sc, alpha_sc, acc_sc, s_sc, p_sc = rest[:6]
    cast_in, cast_out = rest[6:6 + n_cast], rest[6 + n_cast:6 + 2 * n_cast]
    sems = rest[6 + 2 * n_cast:]
    step = ((pl.program_id(0) * pl.num_programs(1) + pl.program_id(1)) * pl.num_programs(2)
            + pl.program_id(2))
    first = 0
    for g, (n_slices, cnt) in enumerate(cast_groups):
        grp = slice(first, first + cnt)
        _cast_slice(step, n_slices, n_steps, cast_src[grp], cast_dst[grp], cast_in[grp],
                    cast_out[grp], sems[2 * g], sems[2 * g + 1])
        first += cnt

    qi = pl.program_id(2)
    tq = ATTN_Q_TILE
    tk = ATTN_TILE
    n_sub = tq // tk

    qt = qt_ref[...]
    dim = lax.broadcasted_iota(jnp.int32, (LANES, 1), 0)
    zero = jnp.zeros_like(qt)
    qc_sc[0] = jnp.where(dim < DA_HEAD_DIM, qt, zero)
    qc_sc[1] = jnp.where(dim >= DA_HEAD_DIM, qt, zero)
    m_sc[...] = jnp.full(m_sc.shape, -jnp.inf, F32)
    acc_sc[...] = jnp.zeros(acc_sc.shape, F32)

    def scores(j, diagonal):
        for c in range(2):
            for u in range(n_sub):
                ks = (j * n_sub + u) * tk
                su = _dot(k_ref[ks:ks + tk, :], qc_sc[c])
                if diagonal:
                    key = lax.broadcasted_iota(jnp.int32, (tk, tq), 0) + u * tk
                    qry = lax.broadcasted_iota(jnp.int32, (tk, tq), 1)
                    su = jnp.where(key <= qry, su, -jnp.inf)
                s_sc[c, u * tk:(u + 1) * tk, :] = su

    def softmax():
        for c in range(2):
            s = s_sc[c]
            m_prev = m_sc[c]
            m_new = jnp.maximum(m_prev, jnp.max(s, axis=0, keepdims=True))
            alpha_sc[c] = jnp.exp2(m_prev - m_new)
            m_sc[c] = m_new
            p_sc[c] = jnp.exp2(s - m_new).astype(BF16)

    def fold(j):
        for c in range(2):
            pv = None
            for u in range(n_sub):
                d = _dot(vt_ref[j * n_sub + u], p_sc[c, u * tk:(u + 1) * tk, :])
                pv = d if pv is None else pv + d
            acc_sc[c] = alpha_sc[c] * acc_sc[c] + pv

    def run(last):
        scores(0, diagonal=last == 0)
        for t in range(last + 1):
            if t >= 1:
                fold(t - 1)
            softmax()
            if t < last:
                scores(t + 1, diagonal=t + 1 == last)
        fold(last)

    for last in range(n_q):
        pl.when(qi == last)(functools.partial(run, last))

    lp = lam_ref[...]
    lam = (jnp.exp(jnp.sum(lp[0:1] * lp[1:2], axis=-1, keepdims=True))
           - jnp.exp(jnp.sum(lp[2:3] * lp[3:4], axis=-1, keepdims=True)) + lambda_init)
    a0 = acc_sc[0]
    a1 = acc_sc[1]
    o = (a0[0:DA_V_DIM] / a0[DA_V_DIM:DA_V_DIM + 1]
         - lam * (a1[0:DA_V_DIM] / a1[DA_V_DIM:DA_V_DIM + 1]))
    o = o * lax.rsqrt(jnp.mean(o * o, axis=0, keepdims=True) + RMS_EPS) * sg_ref[...]
    o_ref[...] = (o * (1.0 - lambda_init)).T.astype(o_ref.dtype)


def _diff_attention(qt, k, vt, lam_p, subln_col, lambda_init, bsz, seq, cast_f32):
    tq = ATTN_Q_TILE
    tk = ATTN_TILE
    nq = seq // tq
    nk = seq // tk
    n_steps = bsz * DA_HEADS * nq
    def n_slices(w):
        return max(c for c in range(1, n_steps + 1)
                   if w.shape[0] % c == 0 and (w.shape[0] // c) % 16 == 0)
    order = sorted(range(len(cast_f32)), key=lambda a: n_slices(cast_f32[a]))
    cast_sorted = [cast_f32[a] for a in order]
    counts = [n_slices(w) for w in cast_sorted]
    cast_groups = tuple((c, counts.count(c)) for c in sorted(set(counts)))
    n_cast = len(cast_f32)
    slices = [(w.shape[0] // c, w.shape[1]) for w, c in zip(cast_sorted, counts)]
    any_spec = pl.BlockSpec(memory_space=pl.ANY)
    out = pl.pallas_call(
        functools.partial(_attn_kernel, lambda_init=lambda_init, cast_groups=cast_groups,
                          n_steps=n_steps, n_q=nq),
        grid=(bsz, DA_HEADS, nq),
        in_specs=[pl.BlockSpec(lam_p.shape, lambda b, h, i: (0, 0)),
                  pl.BlockSpec(subln_col.shape, lambda b, h, i: (0, 0)),
                  pl.BlockSpec((LANES, tq), lambda b, h, i: (h, b * nq + i)),
                  pl.BlockSpec((seq, LANES), lambda b, h, i: (b, h)),
                  pl.BlockSpec((None, nk, VT_ROWS, tk), lambda b, h, i: (h, b, 0, 0))]
                 + [any_spec] * n_cast,
        out_specs=[pl.BlockSpec((tq, LANES), lambda b, h, i: (b * nq + i, h))] + [any_spec] * n_cast,
        out_shape=[jax.ShapeDtypeStruct(k.shape, BF16)]
                  + [jax.ShapeDtypeStruct(w.shape, BF16) for w in cast_sorted],
        scratch_shapes=[pltpu.VMEM((2, LANES, tq), BF16), pltpu.VMEM((2, 1, tq), F32),
                        pltpu.VMEM((2, 1, tq), F32), pltpu.VMEM((2, VT_ROWS, tq), F32),
                        pltpu.VMEM((2, tq, tq), F32), pltpu.VMEM((2, tq, tq), BF16)]
                       + [pltpu.VMEM((2,) + s, F32) for s in slices]
                       + [pltpu.VMEM((2,) + s, BF16) for s in slices]
                       + [pltpu.SemaphoreType.DMA((2,))] * (2 * len(cast_groups)),
        compiler_params=pltpu.CompilerParams(
            dimension_semantics=("arbitrary", "arbitrary", "arbitrary"),
            vmem_limit_bytes=VMEM_LIMIT),
        name="l0_diff_attention",
    )(lam_p, subln_col, qt, k, vt, *cast_sorted)
    cast_bf16 = [None] * n_cast
    for pos, a in enumerate(order):
        cast_bf16[a] = out[1 + pos]
    return out[0], cast_bf16


def _l0_post_kernel(x_ref, attn_ref, conv_ref, p_ref, wo_ref, gffn_ref, w1_ref, w3_ref, w2_ref,
                    gple_ref, wg_ref, wp_ref, o_ref, acc_sc, hn_sc, h_sc):
    h = (x_ref[...] + _dot(attn_ref[...], wo_ref[0:DA_WIDTH, :])
         + _dot(conv_ref[...], wo_ref[DA_WIDTH:DA_WIDTH + SC_WIDTH, :]))
    hn_sc[...] = _rms(h, gffn_ref[...]).astype(BF16)
    h_sc[...] = h
    _swiglu_chunks(hn_sc, w1_ref, w3_ref, w2_ref, acc_sc)
    o_ref[...] = _ple(h_sc[...] + acc_sc[...], p_ref, gple_ref, wg_ref, wp_ref)


def _l0_post(x2, attn, conv, p0, wo, gffn, w1, w3, w2, gple, wg, wp):
    n = x2.shape[0]
    tm = ROW_TILE
    return pl.pallas_call(
        _l0_post_kernel,
        grid=(n // tm,),
        in_specs=[_rows(tm, D_MODEL), _rows(tm, DA_WIDTH), _rows(tm, SC_WIDTH),
                  _layer_rows(tm, PLE_DIM, 0, n // tm),
                  _resident(wo.shape), _resident(gffn.shape), _resident(w1.shape),
                  _resident(w3.shape), _resident(w2.shape), _resident(gple.shape),
                  _resident(wg.shape), _resident(wp.shape)],
        out_specs=_rows(tm, D_MODEL),
        out_shape=jax.ShapeDtypeStruct((n, D_MODEL), F32),
        scratch_shapes=[pltpu.VMEM((tm, D_MODEL), F32), pltpu.VMEM((tm, D_MODEL), BF16),
                        pltpu.VMEM((tm, D_MODEL), F32)],
        compiler_params=pltpu.CompilerParams(dimension_semantics=("arbitrary",),
                                             vmem_limit_bytes=VMEM_LIMIT),
        name="l0_outproj_swiglu_ple",
    )(x2, attn, conv, p0, wo, gffn, w1, w3, w2, gple, wg, wp)


def _split_bf16(a):
    hi = a.astype(BF16)
    return hi, (a - hi.astype(F32)).astype(BF16)


def _l1_mix_kernel(h_ref, gmix_ref, win_ref, lng_ref, lnb_ref, ws_ref, bs_ref, wout_ref,
                   gffn_ref, wr_ref,
                   h1_ref, hn2_ref, ridx_ref, rgate_ref, hn_sc, u_sc, v_sc, vn_sc, gated_sc):
    tm = h_ref.shape[0]
    hn_sc[...] = _rms(h_ref[...], gmix_ref[...]).astype(BF16)

    def gelu(z):
        return 0.5 * z * (1.0 + lax.erf(z * (2.0 ** -0.5)))

    for dst, base in ((v_sc, GM_WIDTH), (u_sc, 0)):
        for c in range(GM_WIDTH // FF_CHUNK):
            cols = slice(c * FF_CHUNK, (c + 1) * FF_CHUNK)
            dst[:, cols] = gelu(_dot(hn_sc[...], win_ref[:, base + c * FF_CHUNK:base + (c + 1) * FF_CHUNK]))

    vv = v_sc[...]
    mu = jnp.mean(vv, axis=-1, keepdims=True)
    vc = vv - mu
    var = jnp.mean(vc * vc, axis=-1, keepdims=True)
    vn_sc[...] = (vc * lax.rsqrt(var + LN_EPS) * lng_ref[...] + lnb_ref[...]).astype(BF16)

    n_chunks = tm // GM_CHUNK
    row = lax.broadcasted_iota(jnp.int32, (GM_CHUNK, GM_CHUNK), 0)
    col = lax.broadcasted_iota(jnp.int32, (GM_CHUNK, GM_CHUNK), 1)
    for g in range(GM_GROUPS):
        gs = slice(g * LANES, (g + 1) * LANES)
        wm = jnp.where(col <= row, ws_ref[g], 0.0).astype(BF16)
        rhs = jnp.concatenate(
            [vn_sc[r * GM_CHUNK:(r + 1) * GM_CHUNK, gs] for r in range(n_chunks)], axis=1)
        vs = _dot(wm, rhs) + bs_ref[:, g:g + 1]
        for r in range(n_chunks):
            rs = slice(r * GM_CHUNK, (r + 1) * GM_CHUNK)
            gated_sc[rs, gs] = (u_sc[rs, gs] * vs[:, r * LANES:(r + 1) * LANES]).astype(BF16)

    h1 = h_ref[...] + _dot(gated_sc[...], wout_ref[...])
    h1_ref[...] = h1

    hn2 = _rms(h1, gffn_ref[...])
    hn2_ref[...] = hn2
    x_hi, x_lo = _split_bf16(hn2)
    a_hi = _dot(x_hi, wr_ref[...])
    logits = a_hi + pltpu.roll(a_hi, LANES - N_EXPERTS, 1) + _dot(x_lo, wr_ref[...])
    assert N_EXPERTS == F32_SUBLANES
    lt = logits.T[0:N_EXPERTS, :]
    expert = lax.broadcasted_iota(jnp.int32, lt.shape, 0)
    v1 = jnp.max(lt, axis=0, keepdims=True)
    i1 = jnp.min(jnp.where(lt == v1, expert, N_EXPERTS), axis=0, keepdims=True)
    rest = jnp.where(expert == i1, -jnp.inf, lt)
    v2 = jnp.max(rest, axis=0, keepdims=True)
    i2 = jnp.min(jnp.where(rest == v2, expert, N_EXPERTS), axis=0, keepdims=True)
    e2 = jnp.exp(v2 - v1)
    ridx_ref[...] = jnp.where(expert == 0, i1, jnp.where(expert == 1, i2, 0))
    rgate_ref[...] = jnp.where(expert == 0, 1.0 / (1.0 + e2), jnp.where(expert == 1, e2 / (1.0 + e2), 0.0))


def _l1_mix(h, gmix, win, lng, lnb, ws, bs_t, wout, gffn, wr):
    n = h.shape[0]
    tm = ROW_TILE
    return pl.pallas_call(
        _l1_mix_kernel,
        grid=(n // tm,),
        in_specs=[_rows(tm, D_MODEL)] + [_resident(a.shape) for a in
                                         (gmix, win, lng, lnb, ws, bs_t, wout, gffn, wr)],
        out_specs=[_rows(tm, D_MODEL), _rows(tm, D_MODEL),
                   pl.BlockSpec((F32_SUBLANES, tm), lambda i: (0, i)),
                   pl.BlockSpec((F32_SUBLANES, tm), lambda i: (0, i))],
        out_shape=[jax.ShapeDtypeStruct((n, D_MODEL), F32), jax.ShapeDtypeStruct((n, D_MODEL), F32),
                   jax.ShapeDtypeStruct((F32_SUBLANES, n), jnp.int32),
                   jax.ShapeDtypeStruct((F32_SUBLANES, n), F32)],
        scratch_shapes=[pltpu.VMEM((tm, D_MODEL), BF16), pltpu.VMEM((tm, GM_WIDTH), F32),
                        pltpu.VMEM((tm, GM_WIDTH), F32), pltpu.VMEM((tm, GM_WIDTH), BF16),
                        pltpu.VMEM((tm, GM_WIDTH), BF16)],
        compiler_params=pltpu.CompilerParams(dimension_semantics=("arbitrary",),
                                             vmem_limit_bytes=VMEM_LIMIT),
        name="l1_gmlp_router",
    )(h, gmix, win, lng, lnb, ws, bs_t, wout, gffn, wr)


def _moe_kernel(bexp_ref, nvalid_ref, tok_ref, dst_ref, hn_hbm, w1_ref, w3_ref, w2_ref, y_hbm,
                xbuf, xb_sc, acc_sc, ybuf, gsem, ssem, fence_sem, fence_sink):
    i = pl.program_id(0)
    n_valid = nvalid_ref[0]
    tm = MOE_TILE
    prev = jnp.maximum(i - 1, 0)
    slot = i % 2
    other = 1 - slot

    def gather_row(blk, r):
        return pltpu.make_async_copy(hn_hbm.at[pl.ds(tok_ref[blk * tm + r], 1)],
                                     xbuf.at[pl.ds(r, 1)], gsem)

    def scatter_row(blk, r, sl):
        return pltpu.make_async_copy(ybuf.at[sl, pl.ds(r, 1)],
                                     y_hbm.at[pl.ds(dst_ref[blk * tm + r], 1)], ssem.at[sl])

    def gather_wait():
        pltpu.make_async_copy(hn_hbm.at[pl.ds(0, tm)], xbuf, gsem).wait()

    def scatter_wait(sl):
        pltpu.make_async_copy(ybuf.at[sl], y_hbm.at[pl.ds(0, tm)], ssem.at[sl]).wait()

    @pl.when(i < n_valid)
    def _():
        @pl.when(i == 0)
        def _():
            def issue(r, carry):
                gather_row(0, r).start()
                return carry
            lax.fori_loop(0, tm, issue, 0)

        gather_wait()
        xb_sc[...] = xbuf[...].astype(BF16)
        has_next = i + 1 < n_valid
        has_prev = i >= 1

        def copies(c):
            if c not in MOE_FENCE_CHUNKS:
                return
            g = MOE_FENCE_CHUNKS.index(c)
            if g > 0:
                fence_sink[0] = pl.semaphore_read(fence_sem)
            if g == len(MOE_FENCE_CHUNKS) - 1:
                return
            last_gather_group = len(MOE_COPY_ROWS) - 3
            for r in range(MOE_COPY_ROWS[g], tm if g == last_gather_group else MOE_COPY_ROWS[g + 1]):
                if g <= last_gather_group:
                    @pl.when(has_next)
                    def _():
                        gather_row(i + 1, r).start(priority=1)

                if r < MOE_COPY_ROWS[g + 1]:
                    @pl.when(has_prev)
                    def _():
                        scatter_row(prev, r, other).start(priority=r % 2)

        _swiglu_chunks(xb_sc, w1_ref, w3_ref, w2_ref, acc_sc, per_chunk=copies)

        @pl.when(i >= 2)
        def _():
            scatter_wait(slot)

        ybuf[slot] = acc_sc[...]

    @pl.when(i == n_valid)
    def _():
        def issue(r, carry):
            scatter_row(prev, r, other).start()
            return carry
        lax.fori_loop(0, tm, issue, 0)
        scatter_wait(other)

        @pl.when(i >= 2)
        def _():
            scatter_wait(slot)

        ybuf[0] = jnp.zeros(ybuf.shape[1:], F32)
        plane = y_hbm.shape[0] // TOP_K
        for b in range(TOP_K):
            spare = pltpu.make_async_copy(ybuf.at[0], y_hbm.at[pl.ds((b + 1) * plane - tm, tm)],
                                          ssem.at[0])
            spare.start()
            spare.wait()


def _moe_experts(block_expert, n_valid, row_token, row_dst, hn2, w1, w3, w2, n_out_rows):
    n_blocks = block_expert.shape[0]
    tm = MOE_TILE
    per_expert = lambda w: pl.BlockSpec((None,) + w.shape[1:], lambda i, be, *_: (be[i], 0, 0))
    return pl.pallas_call(
        _moe_kernel,
        grid_spec=pltpu.PrefetchScalarGridSpec(
            num_scalar_prefetch=4,
            grid=(n_blocks,),
            in_specs=[pl.BlockSpec(memory_space=pl.ANY), per_expert(w1), per_expert(w3),
                      per_expert(w2)],
            out_specs=pl.BlockSpec(memory_space=pl.ANY),
            scratch_shapes=[pltpu.VMEM((tm, D_MODEL), F32), pltpu.VMEM((tm, D_MODEL), BF16),
                            pltpu.VMEM((tm, D_MODEL), F32), pltpu.VMEM((2, tm, D_MODEL), F32),
                            pltpu.SemaphoreType.DMA(()), pltpu.SemaphoreType.DMA((2,)),
                            pltpu.SemaphoreType.REGULAR(()), pltpu.SMEM((1,), jnp.int32)],
        ),
        out_shape=jax.ShapeDtypeStruct((n_out_rows, D_MODEL), F32),
        compiler_params=pltpu.CompilerParams(dimension_semantics=("arbitrary",),
                                             vmem_limit_bytes=VMEM_LIMIT),
        name="l1_expert_swiglu",
    )(block_expert, n_valid, row_token, row_dst, hn2, w1, w3, w2)


def _combine_kernel(h_ref, y0_ref, y1_ref, gate_ref, p_ref, gple_ref, wg_ref, wp_ref, gfin_ref,
                    o_ref):
    tm = h_ref.shape[0]
    gate = jnp.concatenate([gate_ref[...], jnp.zeros((LANES - F32_SUBLANES, tm), F32)], axis=0).T
    h = h_ref[...] + (y0_ref[...] * gate[:, 0:1] + y1_ref[...] * gate[:, 1:2])
    h = _ple(h, p_ref, gple_ref, wg_ref, wp_ref)
    o_ref[...] = _rms(h, gfin_ref[...])


def _combine(h1, y_rows, rgate, p1, gple, wg, wp, gfin):
    n = h1.shape[0]
    tm = ROW_TILE
    plane_tiles = y_rows.shape[0] // TOP_K // tm
    return pl.pallas_call(
        _combine_kernel,
        grid=(n // tm,),
        in_specs=[_rows(tm, D_MODEL), _rows(tm, D_MODEL),
                  pl.BlockSpec((tm, D_MODEL), lambda i: (plane_tiles + i, 0)),
                  pl.BlockSpec((F32_SUBLANES, tm), lambda i: (0, i)),
                  _layer_rows(tm, PLE_DIM, 1, n // tm), _resident(gple.shape), _resident(wg.shape),
                  _resident(wp.shape), _resident(gfin.shape)],
        out_specs=_rows(tm, D_MODEL),
        out_shape=jax.ShapeDtypeStruct((n, D_MODEL), F32),
        compiler_params=pltpu.CompilerParams(dimension_semantics=("arbitrary",),
                                             vmem_limit_bytes=VMEM_LIMIT),
        name="l1_combine_ple_norm",
    )(h1, y_rows, y_rows, rgate, p1, gple, wg, wp, gfin)


def _routing_tables(ridx, n_blocks):
    n_tok = ridx.shape[1]
    n_assign = ridx.size
    e_flat = ridx.reshape(-1)
    experts = jnp.arange(N_EXPERTS, dtype=jnp.int32)[:, None]
    counts = jnp.sum((e_flat[None, :] == experts).astype(jnp.int32), axis=1)
    padded = (counts + MOE_TILE - 1) // MOE_TILE * MOE_TILE
    pad_ends = jnp.cumsum(padded)
    n_rows = (n_blocks + 2) * MOE_TILE
    r = jnp.arange(n_rows, dtype=jnp.int32)
    filler = jnp.arange(n_rows - n_assign, dtype=jnp.int32)
    filler_ends = jnp.cumsum(padded - counts)
    filler_expert = jnp.sum((filler[None, :] >= filler_ends[:, None]).astype(jnp.int32), axis=0)
    keys = jnp.concatenate([2 * e_flat, 2 * filler_expert + 1])
    assert n_assign < 0xFFFF
    ids = jnp.concatenate([jnp.arange(n_assign, dtype=jnp.int32),
                           jnp.full((n_rows - n_assign,), 0xFFFF, jnp.int32)])
    assign = jnp.sort(keys * 0x10000 + ids) & 0xFFFF
    assign = jnp.where(assign == 0xFFFF, -1, assign)
    row_token = jnp.maximum(assign, 0) % n_tok
    plane = n_tok + MOE_TILE
    spare = (r // MOE_TILE % 2) * plane + n_tok + r % MOE_TILE
    row_dst = jnp.where(assign >= 0, (assign // n_tok) * plane + assign % n_tok, spare)
    block_start = jnp.arange(n_blocks, dtype=jnp.int32) * MOE_TILE
    block_expert = jnp.minimum(
        jnp.sum((block_start[:, None] >= pad_ends[None, :]).astype(jnp.int32), axis=1),
        N_EXPERTS - 1).astype(jnp.int32)
    n_valid = (pad_ends[-1:] // MOE_TILE).astype(jnp.int32)
    return row_token, row_dst, block_expert, n_valid


def kernel(x, p, positions, ln_mix, ln_ffn, ln_ple, a_w_in, a_lambda, a_subln, a_conv_w, a_w_out,
           ffn_w1, ffn_w3, ffn_w2, c_w_in, c_ln_g, c_ln_b, c_w_s, c_b_s, c_w_out, router_w,
           moe_w1, moe_w3, moe_w2, ple_gate, ple_proj, final_norm):
    bsz, seq, _ = x.shape
    n = bsz * seq
    assert seq % ROW_TILE == 0 and seq % ATTN_TILE == 0 and ROW_TILE % GM_CHUNK == 0
    x2 = x.reshape(n, D_MODEL)
    pos3 = positions.reshape(n // ROW_TILE, 1, ROW_TILE)
    row = lambda a: a.reshape(1, -1)

    freq = (ROPE_THETA ** (-jnp.arange(0, ROT_DIM, 2, dtype=F32) / ROT_DIM)).reshape(-1, 1)
    qt, k, vt, conv = _inproj(x2, pos3, row(ln_mix[0]), freq, a_conv_w[0].T, a_w_in[0].astype(BF16), seq)
    lambda_init = 0.8 - 0.6 * math.exp(-0.3 * 0)
    moe_f32 = [w[0].reshape(-1, w.shape[-1]) for w in (moe_w1, moe_w3, moe_w2)]
    attn, moe_bf16 = _diff_attention(qt, k, vt, a_lambda[0], a_subln[0].reshape(-1, 1), lambda_init,
                                     bsz, seq, moe_f32)
    mw1, mw3, mw2 = (wb.reshape(w.shape[1:]) for wb, w in zip(moe_bf16, (moe_w1, moe_w3, moe_w2)))
    p_all = p.reshape(-1, PLE_DIM)
    h = _l0_post(x2, attn, conv, p_all, a_w_out[0].astype(BF16), row(ln_ffn[0]),
                 ffn_w1[0].astype(BF16), ffn_w3[0].astype(BF16), ffn_w2[0].astype(BF16),
                 row(ln_ple[0]), ple_gate[0].astype(BF16), ple_proj[0].astype(BF16))

    wr = jnp.pad(jnp.concatenate(_split_bf16(router_w[0]), axis=1),
                 ((0, 0), (0, LANES - 2 * N_EXPERTS)))
    h1, hn2, ridx, rgate = _l1_mix(h, row(ln_mix[1]), c_w_in[0].astype(BF16), row(c_ln_g[0]),
                                   row(c_ln_b[0]), c_w_s[0], c_b_s[0].T, c_w_out[0].astype(BF16),
                                   row(ln_ffn[1]), wr)
    n_blocks = -(-(n * TOP_K + N_EXPERTS * (MOE_TILE - 1)) // MOE_TILE)
    row_token, row_dst, block_expert, n_valid = _routing_tables(ridx[:TOP_K], n_blocks)
    y_rows = _moe_experts(block_expert, n_valid, row_token, row_dst, hn2, mw1, mw3, mw2,
                          TOP_K * (n + MOE_TILE))
    out = _combine(h1, y_rows, rgate, p_all, row(ln_ple[1]),
                   ple_gate[1].astype(BF16), ple_proj[1].astype(BF16), row(final_norm))
    return out.reshape(bsz, seq, D_MODEL)
```
